```python
import math
import jax, jax.numpy as jnp
from jax import lax
import numpy as np

D_MODEL = 1024
BATCH = 2
SEQ = 8192
DEPTH = 2
DEC_BATCH = 128
DEC_SEQ = 4
PAST_LEN = 16384
PAGE_SIZE = 128

N_A_LAYERS = DEPTH // 2
N_B_LAYERS = DEPTH - N_A_LAYERS
SSM_GROUP = 16
SSM_GROUPS = D_MODEL // SSM_GROUP
SSM_STATE = 64
SSM_CHUNK = 512
N_HEADS = 16
HEAD_DIM = 64
N_KV_HEADS = 4
Q_PER_KV = N_HEADS // N_KV_HEADS
WINDOW = 128
ATTN_BLOCK = WINDOW
CACHE_WIN = min(WINDOW, PAST_LEN)
N_EXPERTS = 32
TOP_K = 4
D_FF = D_MODEL
SWIGLU_LIMIT = 7.0
SWIGLU_ALPHA = 1.702
RMS_EPS = 1e-5
NEG_INF = -1e30

kernel_name = "yoco_s5_swa_sink_moe_step"


def _rmsnorm(x, g):
    x32 = x.astype(jnp.float32)
    y = x32 * lax.rsqrt(jnp.mean(x32 * x32, axis=-1, keepdims=True) + RMS_EPS) * g.astype(jnp.float32)
    return y.astype(x.dtype)


def _ada(c, w, b, n):
    m = jax.nn.silu(c) @ w + b
    return jnp.split(m[:, None, :], n, axis=-1)


def _modulate(h, shift, scale):
    return h * (1.0 + scale) + shift


def _alibi_slopes():
    return jnp.exp2(-8.0 * jnp.arange(1, N_HEADS + 1, dtype=jnp.float32) / N_HEADS)


def _ssm_combine(e1, e2):
    a1, b1 = e1
    a2, b2 = e2
    return a1 * a2, a2 * b1 + b2


def _s5_mixer(u, h0, a_re, a_im, log_dt, b_re, b_im, c_re, c_im, d_skip, w_glu_a, w_glu_b):
    f32 = jnp.float32
    bt, t, _ = u.shape
    lam = lax.complex(a_re.astype(f32), a_im.astype(f32))
    lam_dt = lam * jnp.exp(log_dt.astype(f32))[:, None]
    abar = jnp.exp(lam_dt)
    bbar = ((abar - 1.0) / lam)[..., None] * lax.complex(b_re.astype(f32), b_im.astype(f32))
    cmat = lax.complex(c_re.astype(f32), c_im.astype(f32))
    chunk = SSM_CHUNK if t % SSM_CHUNK == 0 else t
    n_chunks = t // chunk
    apow = jnp.exp(lam_dt[None] * jnp.arange(1, chunk + 1, dtype=f32)[:, None, None])
    ug = u.astype(f32).reshape(bt, n_chunks, chunk, SSM_GROUPS, SSM_GROUP).transpose(1, 0, 2, 3, 4)

    def step(h, u_c):
        bu = jnp.einsum('btgc,gpc->btgp', u_c.astype(bbar.dtype), bbar)
        a = jnp.broadcast_to(abar, bu.shape)
        _, hl = lax.associative_scan(_ssm_combine, (a, bu), axis=1)
        hs = hl + apow[None] * h[:, None]
        y = jnp.einsum('btgp,gcp->btgc', hs, cmat).real
        return hs[:, -1], y

    h_last, ys = lax.scan(step, h0, ug)
    y = ys.transpose(1, 0, 2, 3, 4).reshape(bt, t, D_MODEL) + d_skip.astype(f32) * u.astype(f32)
    z = jax.nn.gelu(y).astype(u.dtype)
    out = (z @ w_glu_a) * jax.nn.sigmoid(z @ w_glu_b)
    return out, h_last


def _sink_attend(q, k, v, dist, valid, slopes, sinks):
    f32 = jnp.float32
    s = jnp.einsum('...qhgd,...khd->...hgqk', q, k, preferred_element_type=f32) * (HEAD_DIM ** -0.5)
    s = s - slopes.reshape(N_KV_HEADS, Q_PER_KV, 1, 1) * dist[..., None, None, :, :]
    s = jnp.where(valid[..., None, None, :, :], s, NEG_INF)
    sink = sinks.astype(f32).reshape(N_KV_HEADS, Q_PER_KV, 1)
    m = jnp.maximum(s.max(-1), sink)
    e = jnp.exp(s - m[..., None])
    p = e / (e.sum(-1) + jnp.exp(sink - m))[..., None]
    return jnp.einsum('...hgqk,...khd->...qhgd', p.astype(v.dtype), v)


def _banded_attention(q, k, v, slopes, sinks):
    b, l = q.shape[:2]
    nb = l // ATTN_BLOCK
    qb = q.reshape(b, nb, ATTN_BLOCK, N_KV_HEADS, Q_PER_KV, HEAD_DIM)
    kb = k.reshape(b, nb, ATTN_BLOCK, N_KV_HEADS, HEAD_DIM)
    vb = v.reshape(b, nb, ATTN_BLOCK, N_KV_HEADS, HEAD_DIM)
    pad = ((0, 0), (1, 0), (0, 0), (0, 0), (0, 0))
    kk = jnp.concatenate([jnp.pad(kb, pad)[:, :-1], kb], axis=2)
    vv = jnp.concatenate([jnp.pad(vb, pad)[:, :-1], vb], axis=2)
    r = jnp.arange(ATTN_BLOCK)[:, None]
    j = jnp.arange(2 * ATTN_BLOCK)[None, :]
    dist_i = r + ATTN_BLOCK - j
    first = (jnp.arange(nb) == 0)[:, None, None] & (j < ATTN_BLOCK)[None]
    valid = (dist_i >= 0) & (dist_i < WINDOW) & jnp.logical_not(first)
    o = _sink_attend(qb, kk, vv, dist_i.astype(jnp.float32), valid, slopes, sinks)
    return o.reshape(b, l, N_HEADS * HEAD_DIM)


def _window_cache_attention(q, kk, vv, slopes, sinks):
    n, t = q.shape[:2]
    q_pos = PAST_LEN + jnp.arange(t)
    k_pos = PAST_LEN - CACHE_WIN + jnp.arange(kk.shape[1])
    dist_i = q_pos[:, None] - k_pos[None, :]
    valid = (dist_i >= 0) & (dist_i < WINDOW)
    o = _sink_attend(q.reshape(n, t, N_KV_HEADS, Q_PER_KV, HEAD_DIM), kk, vv,
                     dist_i.astype(jnp.float32), valid, slopes, sinks)
    return o.reshape(n, t, N_HEADS * HEAD_DIM)


def _moe(h, w_router, b_router, w_gu, b_gu, w_down, b_down):
    f32 = jnp.float32
    logits = (h @ w_router + b_router).astype(f32)
    vals, idx = lax.top_k(logits, TOP_K)
    wts = jax.nn.softmax(vals, axis=-1)
    gates = jnp.einsum('nk,nke->ne', wts, jax.nn.one_hot(idx, N_EXPERTS, dtype=f32))

    def body(acc, xs):
        wgu, bgu, wd, bd, g = xs
        gu = (h @ wgu + bgu).astype(f32)
        gate = jnp.minimum(gu[:, :D_FF], SWIGLU_LIMIT)
        up = jnp.clip(gu[:, D_FF:], -SWIGLU_LIMIT, SWIGLU_LIMIT)
        act = ((up + 1.0) * gate * jax.nn.sigmoid(SWIGLU_ALPHA * gate)).astype(wd.dtype)
        y = (act @ wd + bd).astype(f32)
        return acc + g[:, None] * y, None

    acc0 = jnp.zeros((h.shape[0], D_MODEL), f32)
    acc, _ = lax.scan(body, acc0, (w_gu, b_gu, w_down, b_down, gates.T))
    return acc.astype(h.dtype)


def _trunk(x, c, ssm_re, ssm_im, cache_k, cache_v, p):
    prompt = cache_k is None
    bt, t = x.shape[:2]
    slopes = _alibi_slopes()
    new_re, new_im = [], []
    kk = vv = new_k = new_v = None
    for l in range(DEPTH):
        sh1, sc1, g1, sh2, sc2, g2 = _ada(c, p['w_ada'][l], p['b_ada'][l], 6)
        if l == N_A_LAYERS:
            sh_kv, sc_kv = _ada(c, p['w_ada_kv'], p['b_ada_kv'], 2)
            hkv = _modulate(_rmsnorm(x, p['g_kv']), sh_kv, sc_kv)
            kv = (hkv @ p['w_kv'] + p['b_kv']).reshape(bt, t, 2 * N_KV_HEADS, HEAD_DIM)
            k_sh, v_sh = jnp.split(kv, 2, axis=2)
            if prompt:
                kk, vv = k_sh, v_sh
                new_k, new_v = k_sh[:, -CACHE_WIN:], v_sh[:, -CACHE_WIN:]
            else:
                kk = jnp.concatenate([cache_k.astype(k_sh.dtype), k_sh], axis=1)
                vv = jnp.concatenate([cache_v.astype(v_sh.dtype), v_sh], axis=1)
                new_k, new_v = kk[:, -CACHE_WIN:], vv[:, -CACHE_WIN:]
        h = _modulate(_rmsnorm(x, p['g_mix'][l]), sh1, sc1)
        if l < N_A_LAYERS:
            if prompt:
                h0 = jnp.zeros((bt, SSM_GROUPS, SSM_STATE), jnp.complex64)
            else:
                h0 = lax.complex(ssm_re[l].astype(jnp.float32), ssm_im[l].astype(jnp.float32))
            mix, h_last = _s5_mixer(h, h0, p['ssm_a_re'][l], p['ssm_a_im'][l], p['ssm_log_dt'][l],
                                    p['ssm_b_re'][l], p['ssm_b_im'][l], p['ssm_c_re'][l], p['ssm_c_im'][l],
                                    p['ssm_d'][l], p['w_glu_a'][l], p['w_glu_b'][l])
            new_re.append(h_last.real)
            new_im.append(h_last.imag)
        else:
            lb = l - N_A_LAYERS
            q = (h @ p['w_q'][lb] + p['b_q'][lb]).reshape(bt, t, N_HEADS, HEAD_DIM)
            if prompt:
                o = _banded_attention(q, kk, vv, slopes, p['attn_sinks'][lb])
            else:
                o = _window_cache_attention(q, kk, vv, slopes, p['attn_sinks'][lb])
            mix = o @ p['w_o'][lb]
        x = x + g1 * mix
        h = _modulate(_rmsnorm(x, p['g_ffn'][l]), sh2, sc2)
        x = x + g2 * _moe(h.reshape(-1, D_MODEL), p['w_router'][l], p['b_router'][l], p['w_gu'][l],
                          p['b_gu'][l], p['w_down'][l], p['b_down'][l]).reshape(x.shape)
    y = _rmsnorm(x, p['g_final'])
    return y, jnp.stack(new_re), jnp.stack(new_im), new_k, new_v


def setup_inputs(seed: int = 0) -> dict:
    key = jax.random.key(seed)
    ks = iter(jax.random.split(key, 48))
    f32 = jnp.float32

    def nrm(shape, scale):
        return scale * jax.random.normal(next(ks), shape, f32)

    D, G, P, CH = D_MODEL, SSM_GROUPS, SSM_STATE, SSM_GROUP
    QD, KVD = N_HEADS * HEAD_DIM, N_KV_HEADS * HEAD_DIM
    NA, NB, E, F = N_A_LAYERS, N_B_LAYERS, N_EXPERTS, D_FF
    inp = {}
    inp['x_prompt'] = nrm((BATCH, SEQ, D), 1.0)
    inp['x_sample'] = nrm((DEC_BATCH, DEC_SEQ, D), 1.0)
    inp['state_ssm_re'] = nrm((NA, DEC_BATCH, G, P), 0.1)
    inp['state_ssm_im'] = nrm((NA, DEC_BATCH, G, P), 0.1)
    inp['cache_k'] = nrm((DEC_BATCH, CACHE_WIN, N_KV_HEADS, HEAD_DIM), 1.0)
    inp['cache_v'] = nrm((DEC_BATCH, CACHE_WIN, N_KV_HEADS, HEAD_DIM), 1.0)
    inp['c_prompt'] = nrm((BATCH, D), 1.0)
    inp['c_sample'] = nrm((DEC_BATCH, D), 1.0)
    inp['g_mix'] = 1.0 + nrm((DEPTH, D), 0.02)
    inp['g_ffn'] = 1.0 + nrm((DEPTH, D), 0.02)
    inp['w_ada'] = nrm((DEPTH, D, 6 * D), 0.5 * D ** -0.5)
    inp['b_ada'] = nrm((DEPTH, 6 * D), 0.01)
    inp['ssm_a_re'] = -0.5 + nrm((NA, G, P), 0.01)
    inp['ssm_a_im'] = math.pi * jnp.arange(P, dtype=f32)[None, None, :] + nrm((NA, G, P), 0.01)
    inp['ssm_log_dt'] = jax.random.uniform(next(ks), (NA, G), f32, math.log(1e-3), math.log(1e-1))
    inp['ssm_b_re'] = nrm((NA, G, P, CH), (2 * CH) ** -0.5)
    inp['ssm_b_im'] = nrm((NA, G, P, CH), (2 * CH) ** -0.5)
    inp['ssm_c_re'] = nrm((NA, G, CH, P), (2 * P) ** -0.5)
    inp['ssm_c_im'] = nrm((NA, G, CH, P), (2 * P) ** -0.5)
    inp['ssm_d'] = nrm((NA, D), 1.0)
    inp['w_glu_a'] = nrm((NA, D, D), D ** -0.5)
    inp['w_glu_b'] = nrm((NA, D, D), D ** -0.5)
    inp['g_kv'] = 1.0 + nrm((D,), 0.02)
    inp['w_ada_kv'] = nrm((D, 2 * D), 0.5 * D ** -0.5)
    inp['b_ada_kv'] = nrm((2 * D,), 0.01)
    inp['w_kv'] = nrm((D, 2 * KVD), D ** -0.5)
    inp['b_kv'] = nrm((2 * KVD,), 0.01)
    inp['w_q'] = nrm((NB, D, QD), D ** -0.5)
    inp['b_q'] = nrm((NB, QD), 0.01)
    inp['w_o'] = nrm((NB, QD, D), QD ** -0.5)
    inp['attn_sinks'] = nrm((NB, N_HEADS), 0.5)
    inp['w_router'] = nrm((DEPTH, D, E), D ** -0.5)
    inp['b_router'] = nrm((DEPTH, E), 0.01)
    inp['w_gu'] = nrm((DEPTH, E, D, 2 * F), D ** -0.5)
    inp['b_gu'] = nrm((DEPTH, E, 2 * F), 0.01)
    inp['w_down'] = nrm((DEPTH, E, F, D), F ** -0.5)
    inp['b_down'] = nrm((DEPTH, E, D), 0.01)
    inp['g_final'] = 1.0 + nrm((D,), 0.02)
    return inp


def reference(x_prompt, x_sample, state_ssm_re, state_ssm_im, cache_k, cache_v, c_prompt, c_sample,
              g_mix, g_ffn, w_ada, b_ada, ssm_a_re, ssm_a_im, ssm_log_dt, ssm_b_re, ssm_b_im,
              ssm_c_re, ssm_c_im, ssm_d, w_glu_a, w_glu_b, g_kv, w_ada_kv, b_ada_kv, w_kv, b_kv,
              w_q, b_q, w_o, attn_sinks, w_router, b_router, w_gu, b_gu, w_down, b_down, g_final):
    p = dict(g_mix=g_mix, g_ffn=g_ffn, w_ada=w_ada, b_ada=b_ada, ssm_a_re=ssm_a_re, ssm_a_im=ssm_a_im,
             ssm_log_dt=ssm_log_dt, ssm_b_re=ssm_b_re, ssm_b_im=ssm_b_im, ssm_c_re=ssm_c_re,
             ssm_c_im=ssm_c_im, ssm_d=ssm_d, w_glu_a=w_glu_a, w_glu_b=w_glu_b, g_kv=g_kv,
             w_ada_kv=w_ada_kv, b_ada_kv=b_ada_kv, w_kv=w_kv, b_kv=b_kv, w_q=w_q, b_q=b_q, w_o=w_o,
             attn_sinks=attn_sinks, w_router=w_router, b_router=b_router, w_gu=w_gu, b_gu=b_gu,
             w_down=w_down, b_down=b_down, g_final=g_final)
    y_prompt, ssm_re_p, ssm_im_p, k_p, v_p = _trunk(x_prompt, c_prompt, None, None, None, None, p)
    y_sample, ssm_re_s, ssm_im_s, k_s, v_s = _trunk(x_sample, c_sample, state_ssm_re, state_ssm_im,
                                                    cache_k, cache_v, p)
    return (y_prompt, y_sample, ssm_re_p, ssm_im_p, k_p, v_p, ssm_re_s, ssm_im_s, k_s, v_s)
```

```python
import functools
import math

import numpy as np
import jax
import jax.numpy as jnp
from jax import lax
from jax.experimental import pallas as pl
from jax.experimental.pallas import tpu as pltpu

F32 = jnp.float32
BF16 = jnp.bfloat16
I32 = jnp.int32

D_MODEL = 1024
SSM_GROUP = 16
SSM_GROUPS = D_MODEL // SSM_GROUP
SSM_STATE = 64
SSM_COLS = SSM_GROUPS * SSM_STATE
N_HEADS = 16
HEAD_DIM = 64
N_KV_HEADS = 4
Q_PER_KV = N_HEADS // N_KV_HEADS
KV_DIM = N_KV_HEADS * HEAD_DIM
WINDOW = 128
N_EXPERTS = 32
TOP_K = 4
D_FF = D_MODEL
SWIGLU_LIMIT = 7.0
SWIGLU_ALPHA = 1.702
RMS_EPS = 1e-5
NEG_INF = -1e30

LANES = 128
SUBLANES = 8
ROW_TILES = D_MODEL // LANES
TM = 256
S5_CHUNK = 256
S5_COLS = 512
MXU_K = 256
GROUPS_PER_K = MXU_K // SSM_GROUP
N_KT = D_MODEL // MXU_K
VMEM_LIMIT = 56 * 1024 * 1024


def _cparams(sem=None):
    return pltpu.CompilerParams(dimension_semantics=sem, vmem_limit_bytes=VMEM_LIMIT)


def _sigmoid(x):
    return 1.0 / (1.0 + jnp.exp(-x))


def _rms(x, g):
    return x * lax.rsqrt(jnp.mean(x * x, axis=-1, keepdims=True) + RMS_EPS) * g


def _bdot(a, b):
    return jnp.dot(a.astype(BF16), b.astype(BF16), preferred_element_type=F32)


def _gelu_tanh(x):
    return 0.5 * x * (1.0 + jnp.tanh(math.sqrt(2.0 / math.pi) * (x + 0.044715 * (x * x * x))))


def _split_bf16(w):
    hi = w.astype(BF16)
    lo = (w - hi.astype(F32)).astype(BF16)
    return hi, lo


def _dot3(a, b_hi, b_lo):
    a_hi = a.astype(BF16)
    a_lo = (a - a_hi.astype(F32)).astype(BF16)
    return (jnp.dot(a_hi, b_hi, preferred_element_type=F32)
            + jnp.dot(a_lo, b_hi, preferred_element_type=F32)
            + jnp.dot(a_hi, b_lo, preferred_element_type=F32))


def _ada_kernel(c_ref, w_ref, b_ref, o_ref):
    c = c_ref[...]
    o_ref[...] = _bdot(c * _sigmoid(c), w_ref[...]) + b_ref[...]


def _ada(c, w, b):
    rows, cols = c.shape[0], w.shape[1]
    tn = 1024
    return pl.pallas_call(
        _ada_kernel,
        grid=(cols // tn,),
        in_specs=[pl.BlockSpec((rows, D_MODEL), lambda j: (0, 0)),
                  pl.BlockSpec((D_MODEL, tn), lambda j: (0, j)),
                  pl.BlockSpec((1, tn), lambda j: (0, j))],
        out_specs=pl.BlockSpec((rows, tn), lambda j: (0, j)),
        out_shape=jax.ShapeDtypeStruct((rows, cols), F32),
        compiler_params=_cparams(("arbitrary",)),
    )(c, w, b.reshape(1, cols))


def _ssm_prep_kernel(are_ref, aim_ref, ldt_ref, bre_ref, bim_ref,
                     abr_ref, abi_ref, bbr_ref, bbi_ref, apr_ref, api_ref, *, sub_len):
    a_re, a_im = are_ref[...], aim_ref[...]
    dt = jnp.exp(ldt_ref[...])
    mag = jnp.exp(a_re * dt)
    ab_re = mag * jnp.cos(a_im * dt)
    ab_im = mag * jnp.sin(a_im * dt)
    abr_ref[...] = ab_re
    abi_ref[...] = ab_im
    den = a_re * a_re + a_im * a_im
    x, y = ab_re - 1.0, ab_im
    f_re = (x * a_re + y * a_im) / den
    f_im = (y * a_re - x * a_im) / den
    b_re, b_im = bre_ref[...], bim_ref[...]
    bbr_ref[...] = f_re * b_re - f_im * b_im
    bbi_ref[...] = f_re * b_im + f_im * b_re
    pr, pi = ab_re, ab_im
    for _ in range(int(math.log2(sub_len))):
        pr, pi = pr * pr - pi * pi, 2.0 * pr * pi
    cr, ci = jnp.ones_like(pr), jnp.zeros_like(pi)
    for j in range(SUBLANES + 1):
        apr_ref[j:j + 1, :] = cr
        api_ref[j:j + 1, :] = ci
        cr, ci = cr * pr - ci * pi, cr * pi + ci * pr


def _ssm_prep(a_re, a_im, log_dt, b_re, b_im, sub_len):
    gp = SSM_COLS
    ldt = jnp.broadcast_to(log_dt[:, None], (SSM_GROUPS, SSM_STATE)).reshape(1, gp)
    b_re_t = b_re.reshape(gp, SSM_GROUP).T
    b_im_t = b_im.reshape(gp, SSM_GROUP).T
    row = jax.ShapeDtypeStruct((1, gp), F32)
    mat = jax.ShapeDtypeStruct((SSM_GROUP, gp), F32)
    pw = jax.ShapeDtypeStruct((SUBLANES + 1, gp), F32)
    return pl.pallas_call(
        functools.partial(_ssm_prep_kernel, sub_len=sub_len),
        out_shape=(row, row, mat, mat, pw, pw),
        compiler_params=_cparams(),
    )(a_re.reshape(1, gp), a_im.reshape(1, gp), ldt, b_re_t, b_im_t)


def _expand_b(bbar):
    eye = jnp.eye(GROUPS_PER_K, dtype=F32)
    b = bbar.reshape(SSM_GROUP, N_KT, GROUPS_PER_K, SSM_STATE).transpose(1, 0, 2, 3)
    e = b[:, None, :, :, :] * eye[None, :, None, :, None]
    return e.reshape(N_KT, MXU_K, GROUPS_PER_K * SSM_STATE)


def _expand_c(c):
    eye = jnp.eye(GROUPS_PER_K, dtype=F32)
    cc = c.reshape(N_KT, GROUPS_PER_K, SSM_GROUP, SSM_STATE).transpose(0, 1, 3, 2)
    e = cc[:, :, :, None, :] * eye[None, :, None, :, None]
    return e.reshape(N_KT, GROUPS_PER_K * SSM_STATE, MXU_K)


def _cmul(ar, ai, br, bi):
    return ar * br - ai * bi, ar * bi + ai * br


def _s5_prompt_kernel(x_ref, sh_ref, sc_ref, g1_ref, gmix_ref, abr_ref, abi_ref, apr_ref, api_ref,
                      bre_ref, bim_ref, cre_ref, cim_ref, dsk_ref, wa_ref, wb_ref,
                      o_ref, sre_ref, sim_ref,
                      u_ref, hr_ref, hi_ref, hmr_ref, hmi_ref, car_ref, cai_ref, *, L):
    ls = L // SUBLANES
    kcols = GROUPS_PER_K * SSM_STATE

    @pl.when(pl.program_id(1) == 0)
    def _():
        car_ref[...] = jnp.zeros_like(car_ref)
        cai_ref[...] = jnp.zeros_like(cai_ref)

    x = x_ref[...]
    u = _rms(x, gmix_ref[...]) * (1.0 + sc_ref[...]) + sh_ref[...]
    u_ref[...] = u
    r = lax.broadcasted_iota(I32, (L, L), 0)
    c = lax.broadcasted_iota(I32, (L, L), 1)
    perm = jnp.where(c == (r % SUBLANES) * ls + r // SUBLANES, 1.0, 0.0).astype(BF16)
    unperm = jnp.where(r == (c % SUBLANES) * ls + c // SUBLANES, 1.0, 0.0).astype(BF16)
    ub = jnp.dot(perm, u.astype(BF16), preferred_element_type=F32).astype(BF16)
    for kt in range(N_KT):
        uk = ub[:, kt * MXU_K:(kt + 1) * MXU_K]
        hr_ref[:, kt * kcols:(kt + 1) * kcols] = jnp.dot(uk, bre_ref[kt], preferred_element_type=F32)
        hi_ref[:, kt * kcols:(kt + 1) * kcols] = jnp.dot(uk, bim_ref[kt], preferred_element_type=F32)

    unroll = 4

    for cb in range(SSM_COLS // S5_COLS):
        cols = slice(cb * S5_COLS, (cb + 1) * S5_COLS)
        ar = jnp.broadcast_to(abr_ref[:, cols], (SUBLANES, S5_COLS))
        ai = jnp.broadcast_to(abi_ref[:, cols], (SUBLANES, S5_COLS))

        def scan_body(it, carry, cols=cols, ar=ar, ai=ai):
            sr, si = carry
            for k in range(unroll):
                r0 = pl.multiple_of((it * unroll + k) * SUBLANES, SUBLANES)
                tr, ti = _cmul(ar, ai, sr, si)
                sr = tr + hr_ref[pl.ds(r0, SUBLANES), cols]
                si = ti + hi_ref[pl.ds(r0, SUBLANES), cols]
                hr_ref[pl.ds(r0, SUBLANES), cols] = sr
                hi_ref[pl.ds(r0, SUBLANES), cols] = si
            return sr, si

        zero = jnp.zeros((SUBLANES, S5_COLS), F32)
        lax.fori_loop(0, ls // unroll, scan_body, (zero, zero))

    fr = hr_ref[L - SUBLANES:L, :]
    fi = hi_ref[L - SUBLANES:L, :]
    row = lax.broadcasted_iota(I32, (SUBLANES, SSM_COLS), 0)
    gr, gi = fr, fi
    pr, pi = apr_ref[1:2, :], api_ref[1:2, :]
    for s in (1, 2, 4):
        sr = jnp.where(row >= s, pltpu.roll(gr, s, axis=0), 0.0)
        si = jnp.where(row >= s, pltpu.roll(gi, s, axis=0), 0.0)
        tr, ti = _cmul(pr, pi, sr, si)
        gr, gi = gr + tr, gi + ti
        pr, pi = _cmul(pr, pi, pr, pi)
    c0r, c0i = car_ref[...], cai_ref[...]
    tr, ti = _cmul(apr_ref[0:SUBLANES, :], api_ref[0:SUBLANES, :], c0r, c0i)
    hmr = tr + jnp.where(row >= 1, pltpu.roll(gr, 1, axis=0), 0.0)
    hmi = ti + jnp.where(row >= 1, pltpu.roll(gi, 1, axis=0), 0.0)
    hmr_ref[...] = hmr
    hmi_ref[...] = hmi
    tr, ti = _cmul(apr_ref[SUBLANES:SUBLANES + 1, :], api_ref[SUBLANES:SUBLANES + 1, :], c0r, c0i)
    ncr = tr + gr[SUBLANES - 1:SUBLANES, :]
    nci = ti + gi[SUBLANES - 1:SUBLANES, :]
    car_ref[...] = ncr
    cai_ref[...] = nci
    sre_ref[...] = ncr
    sim_ref[...] = nci

    for cb in range(SSM_COLS // S5_COLS):
        cols = slice(cb * S5_COLS, (cb + 1) * S5_COLS)
        ar = jnp.broadcast_to(abr_ref[:, cols], (SUBLANES, S5_COLS))
        ai = jnp.broadcast_to(abi_ref[:, cols], (SUBLANES, S5_COLS))

        def fix_body(it, carry, cols=cols, ar=ar, ai=ai):
            dr, di = carry
            for k in range(unroll):
                r0 = pl.multiple_of((it * unroll + k) * SUBLANES, SUBLANES)
                dr, di = _cmul(ar, ai, dr, di)
                hr_ref[pl.ds(r0, SUBLANES), cols] = hr_ref[pl.ds(r0, SUBLANES), cols] + dr
                hi_ref[pl.ds(r0, SUBLANES), cols] = hi_ref[pl.ds(r0, SUBLANES), cols] + di
            return dr, di

        lax.fori_loop(0, ls // unroll, fix_body, (hmr_ref[:, cols], hmi_ref[:, cols]))

    ys = []
    for nt in range(N_KT):
        hr = hr_ref[:, nt * kcols:(nt + 1) * kcols].astype(BF16)
        hi = hi_ref[:, nt * kcols:(nt + 1) * kcols].astype(BF16)
        ys.append(jnp.dot(hr, cre_ref[nt], preferred_element_type=F32)
                  + jnp.dot(hi, cim_ref[nt], preferred_element_type=F32))
    yp = jnp.concatenate(ys, axis=-1)
    yp_hi = yp.astype(BF16)
    yp_lo = (yp - yp_hi.astype(F32)).astype(BF16)
    y = (jnp.dot(unperm, yp_hi, preferred_element_type=F32)
         + jnp.dot(unperm, yp_lo, preferred_element_type=F32)) + dsk_ref[...] * u_ref[...]
    z = _gelu_tanh(y).astype(BF16)
    mix = jnp.dot(z, wa_ref[...], preferred_element_type=F32) * _sigmoid(
        jnp.dot(z, wb_ref[...], preferred_element_type=F32))
    o_ref[...] = x_ref[...] + g1_ref[...] * mix


def _s5_prompt(x, sh, sc, g1, gmix, prep, bexp, cexp, dsk, wa, wb):
    b, t, _ = x.shape
    L = S5_CHUNK
    abr, abi, apr, api = prep
    bre, bim = bexp
    cre, cim = cexp
    kcols = GROUPS_PER_K * SSM_STATE
    full = lambda shape: pl.BlockSpec(shape, lambda bi, ci: (0,) * len(shape))
    mod = pl.BlockSpec((None, 1, D_MODEL), lambda bi, ci: (bi, 0, 0))
    return pl.pallas_call(
        functools.partial(_s5_prompt_kernel, L=L),
        grid=(b, t // L),
        in_specs=[pl.BlockSpec((None, L, D_MODEL), lambda bi, ci: (bi, ci, 0)), mod, mod, mod,
                  full((1, D_MODEL)), full((1, SSM_COLS)), full((1, SSM_COLS)),
                  full((SUBLANES + 1, SSM_COLS)), full((SUBLANES + 1, SSM_COLS)),
                  full((N_KT, MXU_K, kcols)), full((N_KT, MXU_K, kcols)),
                  full((N_KT, kcols, MXU_K)), full((N_KT, kcols, MXU_K)),
                  full((1, D_MODEL)), full((D_MODEL, D_MODEL)), full((D_MODEL, D_MODEL))],
        out_specs=[pl.BlockSpec((None, L, D_MODEL), lambda bi, ci: (bi, ci, 0)),
                   pl.BlockSpec((None, 1, SSM_COLS), lambda bi, ci: (bi, 0, 0)),
                   pl.BlockSpec((None, 1, SSM_COLS), lambda bi, ci: (bi, 0, 0))],
        out_shape=(jax.ShapeDtypeStruct((b, t, D_MODEL), F32),
                   jax.ShapeDtypeStruct((b, 1, SSM_COLS), F32),
                   jax.ShapeDtypeStruct((b, 1, SSM_COLS), F32)),
        scratch_shapes=[pltpu.VMEM((L, D_MODEL), F32),
                        pltpu.VMEM((L, SSM_COLS), F32), pltpu.VMEM((L, SSM_COLS), F32),
                        pltpu.VMEM((SUBLANES, SSM_COLS), F32), pltpu.VMEM((SUBLANES, SSM_COLS), F32),
                        pltpu.VMEM((1, SSM_COLS), F32), pltpu.VMEM((1, SSM_COLS), F32)],
        compiler_params=_cparams(("arbitrary", "arbitrary")),
    )(x, sh, sc, g1, gmix, abr, abi, apr, api, bre, bim, cre, cim, dsk, wa, wb)


def _s5_sample_kernel(x_ref, sh_ref, sc_ref, g1_ref, gmix_ref, h0r_ref, h0i_ref, abr_ref, abi_ref,
                      brh_ref, brl_ref, bih_ref, bil_ref, cre_ref, cim_ref, dsk_ref, wa_ref, wb_ref,
                      o_ref, sre_ref, sim_ref, *, steps):
    kcols = GROUPS_PER_K * SSM_STATE
    ar, ai = abr_ref[...], abi_ref[...]
    sr, si = h0r_ref[...], h0i_ref[...]
    for t in range(steps):
        x = x_ref[t]
        u = _rms(x, gmix_ref[...]) * (1.0 + sc_ref[...]) + sh_ref[...]
        bur, bui = [], []
        for kt in range(N_KT):
            uk = u[:, kt * MXU_K:(kt + 1) * MXU_K]
            bur.append(_dot3(uk, brh_ref[kt], brl_ref[kt]))
            bui.append(_dot3(uk, bih_ref[kt], bil_ref[kt]))
        tr, ti = _cmul(ar, ai, sr, si)
        sr = tr + jnp.concatenate(bur, axis=-1)
        si = ti + jnp.concatenate(bui, axis=-1)
        ys = []
        for nt in range(N_KT):
            ys.append(_bdot(sr[:, nt * kcols:(nt + 1) * kcols], cre_ref[nt])
                      + _bdot(si[:, nt * kcols:(nt + 1) * kcols], cim_ref[nt]))
        y = jnp.concatenate(ys, axis=-1) + dsk_ref[...] * u
        z = _gelu_tanh(y).astype(BF16)
        mix = jnp.dot(z, wa_ref[...], preferred_element_type=F32) * _sigmoid(
            jnp.dot(z, wb_ref[...], preferred_element_type=F32))
        o_ref[t] = x + g1_ref[...] * mix
    sre_ref[...] = sr
    sim_ref[...] = si


def _s5_sample(x_t, sh, sc, g1, gmix, h0r, h0i, abr, abi, bsplit, cexp, dsk, wa, wb):
    steps, n, _ = x_t.shape
    (brh, brl), (bih, bil) = bsplit
    cre, cim = cexp
    return pl.pallas_call(
        functools.partial(_s5_sample_kernel, steps=steps),
        out_shape=(jax.ShapeDtypeStruct((steps, n, D_MODEL), F32),
                   jax.ShapeDtypeStruct((n, SSM_COLS), F32),
                   jax.ShapeDtypeStruct((n, SSM_COLS), F32)),
        compiler_params=_cparams(),
    )(x_t, sh, sc, g1, gmix, h0r, h0i, abr, abi, brh, brl, bih, bil, cre, cim, dsk, wa, wb)


class _Tiles:
    def __init__(self, n_prompt, n_sample, seq_len):
        self.ntp = n_prompt // TM
        self.nts = n_sample // TM
        self.nt = self.ntp + self.nts
        self.tiles_per_seq = seq_len // TM
        self.n_batch = n_prompt // seq_len
        self.n_tok = n_prompt + n_sample

    def p_rows(self, width=D_MODEL):
        return pl.BlockSpec((TM, width), lambda t, *_: (jnp.minimum(t, self.ntp - 1), 0))

    def s_rows(self, width=D_MODEL):
        return pl.BlockSpec((TM, width), lambda t, *_: (jnp.maximum(t - self.ntp, 0), 0))

    def p_mod(self):
        return pl.BlockSpec((None, 1, D_MODEL),
                            lambda t, *_: (jnp.minimum(t // self.tiles_per_seq, self.n_batch - 1), 0, 0))

    def all_rows(self, width):
        return pl.BlockSpec((TM, width), lambda t, *_: (t, 0))


def _const_spec(shape):
    return pl.BlockSpec(shape, lambda t, *_: (0,) * len(shape))


def _pick(is_sample, s_ref, p_ref):
    return jnp.where(is_sample, s_ref[...], p_ref[...])


def _moe_input(is_sample, xp_ref, xs_ref, shp_ref, shs_ref, scp_ref, scs_ref, g_ref):
    x = _pick(is_sample, xs_ref, xp_ref)
    sh = _pick(is_sample, shs_ref, shp_ref)
    sc = _pick(is_sample, scs_ref, scp_ref)
    return _rms(x, g_ref[...]) * (1.0 + sc) + sh


def _router_kernel(xp_ref, xs_ref, shp_ref, shs_ref, scp_ref, scs_ref, g_ref, wh_ref, wl_ref, b_ref,
                   mi_ref, mw_ref, cnt_ref, run_ref, *, tiles):
    t = pl.program_id(0)

    @pl.when(t == 0)
    def _():
        run_ref[...] = jnp.zeros_like(run_ref)

    h = _moe_input(t >= tiles.ntp, xp_ref, xs_ref, shp_ref, shs_ref, scp_ref, scs_ref, g_ref)
    logits = _dot3(h, wh_ref[...], wl_ref[...]) + b_ref[...]
    lane = lax.broadcasted_iota(I32, (TM, LANES), 1)
    vals, hots = [], []
    work = logits
    for _ in range(TOP_K):
        m = jnp.max(work, axis=-1, keepdims=True)
        first = jnp.min(jnp.where(work == m, lane, LANES), axis=-1, keepdims=True)
        hot = lane == first
        vals.append(m)
        hots.append(hot)
        work = jnp.where(hot, -jnp.inf, work)
    exps = [jnp.exp(v - vals[0]) for v in vals]
    den = exps[0] + exps[1] + exps[2] + exps[3]

    chosen = jnp.zeros((TM, LANES), F32)
    for hot in hots:
        chosen = jnp.where(hot, 1.0, chosen)
    r = lax.broadcasted_iota(I32, (TM, TM), 0)
    c = lax.broadcasted_iota(I32, (TM, TM), 1)
    before = jnp.where(c < r, 1.0, 0.0).astype(BF16)
    rank_all = jnp.dot(before, chosen.astype(BF16), preferred_element_type=F32) + run_ref[...]
    run_ref[...] = run_ref[...] + jnp.sum(chosen, axis=0, keepdims=True)
    cnt_ref[...] = run_ref[...]

    mi = jnp.zeros((TM, LANES), I32)
    mw = jnp.zeros((TM, LANES), F32)
    for k in range(TOP_K):
        e_k = jnp.min(jnp.where(hots[k], lane, LANES), axis=-1, keepdims=True)
        r_k = jnp.sum(jnp.where(hots[k], rank_all, 0.0), axis=-1, keepdims=True).astype(I32)
        mi = jnp.where(lane == k, e_k, mi)
        mi = jnp.where(lane == TOP_K + k, r_k, mi)
        mw = jnp.where(lane == k, exps[k] / den, mw)
    mi_ref[...] = mi
    mw_ref[...] = mw


def _router(tiles, xp, xs, shp, shs, scp, scs, g, w_hi, w_lo, b_pad):
    return pl.pallas_call(
        functools.partial(_router_kernel, tiles=tiles),
        grid=(tiles.nt,),
        in_specs=[tiles.p_rows(), tiles.s_rows(), tiles.p_mod(), tiles.s_rows(), tiles.p_mod(),
                  tiles.s_rows(), _const_spec((1, D_MODEL)), _const_spec((D_MODEL, LANES)),
                  _const_spec((D_MODEL, LANES)), _const_spec((1, LANES))],
        out_specs=[tiles.all_rows(LANES), tiles.all_rows(LANES), _const_spec((1, LANES))],
        out_shape=(jax.ShapeDtypeStruct((tiles.n_tok, LANES), I32),
                   jax.ShapeDtypeStruct((tiles.n_tok, LANES), F32),
                   jax.ShapeDtypeStruct((1, LANES), F32)),
        scratch_shapes=[pltpu.VMEM((1, LANES), F32)],
        compiler_params=_cparams(("arbitrary",)),
    )(xp, xs, shp, shs, scp, scs, g, w_hi, w_lo, b_pad)


PAD_ARMS = tuple(1 << i for i in reversed(range(int(math.log2(TM)))))


def _dispatch_kernel(slot_ref, pstart_ref, pcnt_ref,
                     xp_ref, xs_ref, shp_ref, shs_ref, scp_ref, scs_ref, g_ref,
                     o_hbm, rows_ref, zero_ref, sems, zsem, *, tiles):
    t = pl.program_id(0)
    buf = t % 2
    tile_rows = TM * ROW_TILES
    base = pl.multiple_of(buf * tile_rows, tile_rows)

    def tile_wait(b):
        b0 = pl.multiple_of(b * tile_rows, tile_rows)
        for _ in range(TOP_K):
            pltpu.make_async_copy(rows_ref.at[pl.ds(b0, tile_rows), :],
                                  o_hbm.at[pl.ds(0, tile_rows), :], sems.at[b]).wait()

    def pad_copies(act):
        def body(e, carry):
            start, cnt = pstart_ref[e], pcnt_ref[e]
            for p in PAD_ARMS:
                @pl.when((cnt & p) != 0)
                def _(start=start, p=p):
                    cp = pltpu.make_async_copy(
                        zero_ref.at[pl.ds(0, p * ROW_TILES), :],
                        o_hbm.at[pl.ds(pl.multiple_of(start * ROW_TILES, ROW_TILES), p * ROW_TILES), :],
                        zsem)
                    cp.start() if act == "start" else cp.wait()
                start = start + jnp.where((cnt & p) != 0, p, 0)
            return carry
        lax.fori_loop(0, N_EXPERTS, body, 0)

    @pl.when(t == 0)
    def _():
        zero_ref[...] = jnp.zeros_like(zero_ref)
        pad_copies("start")

    @pl.when(t >= 2)
    def _():
        tile_wait(buf)

    h = _moe_input(t >= tiles.ntp, xp_ref, xs_ref, shp_ref, shs_ref, scp_ref, scs_ref, g_ref)
    for c in range(ROW_TILES):
        rows_ref[pl.ds(base + c, TM, stride=ROW_TILES), :] = h[:, c * LANES:(c + 1) * LANES]

    def issue(n, carry):
        src = rows_ref.at[pl.ds(pl.multiple_of(base + n * ROW_TILES, ROW_TILES), ROW_TILES), :]
        for k in range(TOP_K):
            slot = slot_ref[(t * TM + n) * TOP_K + k]
            dst = o_hbm.at[pl.ds(pl.multiple_of(slot * ROW_TILES, ROW_TILES), ROW_TILES), :]
            pltpu.make_async_copy(src, dst, sems.at[buf]).start()
        return carry
    lax.fori_loop(0, TM, issue, 0)

    @pl.when(t == tiles.nt - 1)
    def _():
        if tiles.nt >= 2:
            tile_wait(1 - buf)
        tile_wait(buf)
        pad_copies("wait")


def _dispatch(tiles, n_slots, slots, pad_start, pad_cnt, xp, xs, shp, shs, scp, scs, g):
    grid_spec = pltpu.PrefetchScalarGridSpec(
        num_scalar_prefetch=3,
        grid=(tiles.nt,),
        in_specs=[tiles.p_rows(), tiles.s_rows(), tiles.p_mod(), tiles.s_rows(), tiles.p_mod(),
                  tiles.s_rows(), _const_spec((1, D_MODEL))],
        out_specs=pl.BlockSpec(memory_space=pl.ANY),
        scratch_shapes=[pltpu.VMEM((2 * TM * ROW_TILES, LANES), F32),
                        pltpu.VMEM((PAD_ARMS[0] * ROW_TILES, LANES), F32),
                        pltpu.SemaphoreType.DMA((2,)), pltpu.SemaphoreType.DMA(())],
    )
    return pl.pallas_call(
        functools.partial(_dispatch_kernel, tiles=tiles),
        grid_spec=grid_spec,
        out_shape=jax.ShapeDtypeStruct((n_slots * ROW_TILES, LANES), F32),
        compiler_params=_cparams(("arbitrary",)),
    )(slots, pad_start, pad_cnt, xp, xs, shp, shs, scp, scs, g)


def _expert_kernel(te_ref, nt_ref, hs_ref, wgu_ref, bgu_ref, wd_ref, bd_ref, o_ref,
                   wgu_bf, wd_bf, lhs_ref):
    i = pl.program_id(0)

    @pl.when(i < nt_ref[0])
    def _():
        prev = te_ref[jnp.maximum(i - 1, 0)]

        @pl.when((i == 0) | (te_ref[i] != prev))
        def _():
            wgu_bf[...] = wgu_ref[...].astype(BF16)
            wd_bf[...] = wd_ref[...].astype(BF16)

        for c in range(ROW_TILES):
            lhs_ref[:, c * LANES:(c + 1) * LANES] = hs_ref[pl.ds(c, TM, stride=ROW_TILES), :].astype(BF16)
        gu = jnp.dot(lhs_ref[...], wgu_bf[...], preferred_element_type=F32) + bgu_ref[...]
        gate = jnp.minimum(gu[:, :D_FF], SWIGLU_LIMIT)
        up = jnp.clip(gu[:, D_FF:], -SWIGLU_LIMIT, SWIGLU_LIMIT)
        act = ((up + 1.0) * gate * _sigmoid(SWIGLU_ALPHA * gate)).astype(BF16)
        y = jnp.dot(act, wd_bf[...], preferred_element_type=F32) + bd_ref[...]
        for c in range(ROW_TILES):
            o_ref[pl.ds(c, TM, stride=ROW_TILES), :] = y[:, c * LANES:(c + 1) * LANES]


def _experts(max_tiles, tile_expert, n_tiles, hs, w_gu, b_gu, w_down, b_down):
    tile_rows = TM * ROW_TILES
    row_map = lambda i, te, nt: (jnp.minimum(i, nt[0] - 1), 0)
    w_map = lambda i, te, nt: (te[i], 0, 0)
    grid_spec = pltpu.PrefetchScalarGridSpec(
        num_scalar_prefetch=2,
        grid=(max_tiles,),
        in_specs=[pl.BlockSpec((tile_rows, LANES), row_map),
                  pl.BlockSpec((None, D_MODEL, 2 * D_FF), w_map),
                  pl.BlockSpec((None, 1, 2 * D_FF), w_map),
                  pl.BlockSpec((None, D_FF, D_MODEL), w_map),
                  pl.BlockSpec((None, 1, D_MODEL), w_map)],
        out_specs=pl.BlockSpec((tile_rows, LANES), row_map),
        scratch_shapes=[pltpu.VMEM((D_MODEL, 2 * D_FF), BF16), pltpu.VMEM((D_FF, D_MODEL), BF16),
                        pltpu.VMEM((TM, D_MODEL), BF16)],
    )
    return pl.pallas_call(
        _expert_kernel,
        grid_spec=grid_spec,
        out_shape=jax.ShapeDtypeStruct(hs.shape, F32),
        compiler_params=_cparams(("arbitrary",)),
    )(tile_expert, n_tiles, hs, w_gu, b_gu.reshape(N_EXPERTS, 1, 2 * D_FF), w_down,
      b_down.reshape(N_EXPERTS, 1, D_MODEL))


def _combine_kernel(slot_ref, y_hbm, mw_ref, xp_ref, xs_ref, g2p_ref, g2s_ref, gfin_ref,
                    op_ref, os_ref, gbuf, acc_ref, sems, *, tiles, final_norm):
    t = pl.program_id(0)
    tile_rows = TM * ROW_TILES
    buf_rows = TOP_K * tile_rows

    def issue_tile(tt, b):
        b0 = pl.multiple_of(b * buf_rows, buf_rows)

        def issue(n, carry):
            for k in range(TOP_K):
                slot = slot_ref[(tt * TM + n) * TOP_K + k]
                src = y_hbm.at[pl.ds(pl.multiple_of(slot * ROW_TILES, ROW_TILES), ROW_TILES), :]
                dst = gbuf.at[pl.ds(pl.multiple_of(b0 + k * tile_rows + n * ROW_TILES, ROW_TILES),
                                    ROW_TILES), :]
                pltpu.make_async_copy(src, dst, sems.at[b]).start()
            return carry
        lax.fori_loop(0, TM, issue, 0)

    @pl.when(t == 0)
    def _():
        issue_tile(0, 0)

    @pl.when(t + 1 < tiles.nt)
    def _():
        issue_tile(t + 1, (t + 1) % 2)

    buf = t % 2
    b0 = pl.multiple_of(buf * buf_rows, buf_rows)
    for _ in range(TOP_K):
        pltpu.make_async_copy(y_hbm.at[pl.ds(0, tile_rows), :],
                              gbuf.at[pl.ds(b0, tile_rows), :], sems.at[buf]).wait()

    is_s = t >= tiles.ntp
    w = mw_ref[...]
    for c in range(ROW_TILES):
        acc = None
        for k in range(TOP_K):
            rows = gbuf[pl.ds(b0 + k * tile_rows + c, TM, stride=ROW_TILES), :]
            term = w[:, k:k + 1] * rows
            acc = term if acc is None else acc + term
        acc_ref[:, c * LANES:(c + 1) * LANES] = acc
    x = _pick(is_s, xs_ref, xp_ref)
    g2 = _pick(is_s, g2s_ref, g2p_ref)
    res = x + g2 * acc_ref[...]
    if final_norm:
        res = _rms(res, gfin_ref[...])

    @pl.when(is_s)
    def _():
        os_ref[...] = res

    @pl.when(jnp.logical_not(is_s))
    def _():
        op_ref[...] = res


def _combine(tiles, slots, y_sorted, mw, xp, xs, g2p, g2s, gfin, final_norm):
    grid_spec = pltpu.PrefetchScalarGridSpec(
        num_scalar_prefetch=1,
        grid=(tiles.nt,),
        in_specs=[pl.BlockSpec(memory_space=pl.ANY), tiles.all_rows(LANES), tiles.p_rows(), tiles.s_rows(),
                  tiles.p_mod(), tiles.s_rows(), _const_spec((1, D_MODEL))],
        out_specs=[tiles.p_rows(), tiles.s_rows()],
        scratch_shapes=[pltpu.VMEM((2 * TOP_K * TM * ROW_TILES, LANES), F32),
                        pltpu.VMEM((TM, D_MODEL), F32), pltpu.SemaphoreType.DMA((2,))],
    )
    return pl.pallas_call(
        functools.partial(_combine_kernel, tiles=tiles, final_norm=final_norm),
        grid_spec=grid_spec,
        out_shape=(jax.ShapeDtypeStruct(xp.shape, F32), jax.ShapeDtypeStruct(xs.shape, F32)),
        compiler_params=_cparams(("arbitrary",)),
    )(slots, y_sorted, mw, xp, xs, g2p, g2s, gfin)


def _moe_layer(tiles, xp, xs, shp, shs, scp, scs, g2p, g2s, g_ffn, w_router, b_router, w_gu, b_gu,
               w_down, b_down, gfin, final_norm):
    n_tok = tiles.n_tok
    w_pad = jnp.pad(w_router, ((0, 0), (0, LANES - N_EXPERTS)))
    w_hi, w_lo = _split_bf16(w_pad)
    b_pad = jnp.pad(b_router, (0, LANES - N_EXPERTS), constant_values=NEG_INF).reshape(1, LANES)
    mi, mw, counts = _router(tiles, xp, xs, shp, shs, scp, scs, g_ffn, w_hi, w_lo, b_pad)

    cnt = counts[0, :N_EXPERTS].astype(I32)
    tiles_e = (cnt + TM - 1) // TM
    tile_end = jnp.cumsum(tiles_e)
    tile_start = tile_end - tiles_e
    idx, rank = mi[:, :TOP_K], mi[:, TOP_K:2 * TOP_K]
    offs = jnp.sum(jnp.where(idx[..., None] == jnp.arange(N_EXPERTS, dtype=I32),
                             (tile_start * TM)[None, None, :], 0), axis=-1)
    slots = (offs + rank).reshape(n_tok * TOP_K)
    max_tiles = (n_tok * TOP_K) // TM + N_EXPERTS
    tile_expert = jnp.minimum(
        jnp.sum(jnp.arange(max_tiles, dtype=I32)[:, None] >= tile_end[None, :], axis=-1), N_EXPERTS - 1
    ).astype(I32)
    n_tiles = tile_end[-1:].astype(I32)
    pad_start = (tile_start * TM + cnt).astype(I32)
    pad_cnt = (tiles_e * TM - cnt).astype(I32)

    hs = _dispatch(tiles, max_tiles * TM, slots, pad_start, pad_cnt, xp, xs, shp, shs, scp, scs, g_ffn)
    ys = _experts(max_tiles, tile_expert, n_tiles, hs, w_gu, b_gu, w_down, b_down)
    return _combine(tiles, slots, ys, mw, xp, xs, g2p, g2s, gfin, final_norm)


def _slopes():
    return [2.0 ** (-8.0 * (h + 1) / N_HEADS) for h in range(N_HEADS)]


def _attn_prompt_kernel(sink_ref, x_ref, shkv_ref, sckv_ref, sh_ref, sc_ref, g1_ref, gkv_ref, gmix_ref,
                        wkv_ref, bkv_ref, wq_ref, bq_ref, wo_ref,
                        o_ref, kl_ref, vl_ref, kk_ref, vv_ref, oh_ref):
    j = pl.program_id(1)
    blk = WINDOW

    @pl.when(j == 0)
    def _():
        kk_ref[...] = jnp.zeros_like(kk_ref)
        vv_ref[...] = jnp.zeros_like(vv_ref)

    @pl.when(j > 0)
    def _():
        kk_ref[0:blk, :] = kk_ref[blk:2 * blk, :]
        vv_ref[0:blk, :] = vv_ref[blk:2 * blk, :]

    x = x_ref[...]
    hkv = _rms(x, gkv_ref[...]) * (1.0 + sckv_ref[...]) + shkv_ref[...]
    kv = _bdot(hkv, wkv_ref[...]) + bkv_ref[...]
    kl_ref[...] = kv[:, :KV_DIM]
    vl_ref[...] = kv[:, KV_DIM:]
    kk_ref[blk:2 * blk, :] = kv[:, :KV_DIM].astype(BF16)
    vv_ref[blk:2 * blk, :] = kv[:, KV_DIM:].astype(BF16)

    h = _rms(x, gmix_ref[...]) * (1.0 + sc_ref[...]) + sh_ref[...]
    q = _bdot(h, wq_ref[...]) + bq_ref[...]

    r = lax.broadcasted_iota(I32, (blk, 2 * blk), 0)
    c = lax.broadcasted_iota(I32, (blk, 2 * blk), 1)
    dist_i = r + blk - c
    valid = (dist_i >= 0) & (dist_i < WINDOW) & jnp.logical_not((j == 0) & (c < blk))
    dist = dist_i.astype(F32)
    slopes = _slopes()
    for g in range(N_KV_HEADS):
        kg = kk_ref[:, g * HEAD_DIM:(g + 1) * HEAD_DIM]
        vg = vv_ref[:, g * HEAD_DIM:(g + 1) * HEAD_DIM]
        for qh in range(Q_PER_KV):
            hd = g * Q_PER_KV + qh
            qd = q[:, hd * HEAD_DIM:(hd + 1) * HEAD_DIM].astype(BF16)
            s = lax.dot_general(qd, kg, (((1,), (1,)), ((), ())), preferred_element_type=F32)
            s = s * (HEAD_DIM ** -0.5) - slopes[hd] * dist
            s = jnp.where(valid, s, NEG_INF)
            sink = sink_ref[hd]
            m = jnp.maximum(jnp.max(s, axis=-1, keepdims=True), sink)
            e = jnp.exp(s - m)
            den = jnp.sum(e, axis=-1, keepdims=True) + jnp.exp(sink - m)
            p = (e / den).astype(BF16)
            oh_ref[:, hd * HEAD_DIM:(hd + 1) * HEAD_DIM] = jnp.dot(p, vg, preferred_element_type=F32)
    mix = _bdot(oh_ref[...], wo_ref[...])
    o_ref[...] = x + g1_ref[...] * mix


def _attn_prompt(x, shkv, sckv, sh, sc, g1, gkv, gmix, wkv, bkv, wq, bq, wo, sinks):
    b, t, _ = x.shape
    blk = WINDOW
    qd = N_HEADS * HEAD_DIM
    full = lambda shape: pl.BlockSpec(shape, lambda bi, ji, *_: (0,) * len(shape))
    mod = pl.BlockSpec((None, 1, D_MODEL), lambda bi, ji, *_: (bi, 0, 0))
    rows = pl.BlockSpec((None, blk, D_MODEL), lambda bi, ji, *_: (bi, ji, 0))
    last = pl.BlockSpec((None, blk, KV_DIM), lambda bi, ji, *_: (bi, 0, 0))
    grid_spec = pltpu.PrefetchScalarGridSpec(
        num_scalar_prefetch=1,
        grid=(b, t // blk),
        in_specs=[rows, mod, mod, mod, mod, mod, full((1, D_MODEL)), full((1, D_MODEL)),
                  full((D_MODEL, 2 * KV_DIM)), full((1, 2 * KV_DIM)), full((D_MODEL, qd)), full((1, qd)),
                  full((qd, D_MODEL))],
        out_specs=[rows, last, last],
        scratch_shapes=[pltpu.VMEM((2 * blk, KV_DIM), BF16), pltpu.VMEM((2 * blk, KV_DIM), BF16),
                        pltpu.VMEM((blk, qd), F32)],
    )
    return pl.pallas_call(
        _attn_prompt_kernel,
        grid_spec=grid_spec,
        out_shape=(jax.ShapeDtypeStruct((b, t, D_MODEL), F32),
                   jax.ShapeDtypeStruct((b, blk, KV_DIM), F32),
                   jax.ShapeDtypeStruct((b, blk, KV_DIM), F32)),
        compiler_params=_cparams(("arbitrary", "arbitrary")),
    )(sinks, x, shkv, sckv, sh, sc, g1, gkv, gmix, wkv, bkv, wq, bq, wo)


def _qkv_sample_kernel(x_ref, shkv_ref, sckv_ref, sh_ref, sc_ref, gkv_ref, gmix_ref,
                       wkv_ref, bkv_ref, wq_ref, bq_ref, q_ref, k_ref, v_ref):
    x = x_ref[...]
    hkv = _rms(x, gkv_ref[...]) * (1.0 + sckv_ref[...]) + shkv_ref[...]
    kv = _bdot(hkv, wkv_ref[...]) + bkv_ref[...]
    k_ref[...] = kv[:, :KV_DIM]
    v_ref[...] = kv[:, KV_DIM:]
    h = _rms(x, gmix_ref[...]) * (1.0 + sc_ref[...]) + sh_ref[...]
    q_ref[...] = _bdot(h, wq_ref[...]) + bq_ref[...]


def _qkv_sample(x, shkv, sckv, sh, sc, gkv, gmix, wkv, bkv, wq, bq):
    n = x.shape[0]
    return pl.pallas_call(
        _qkv_sample_kernel,
        out_shape=(jax.ShapeDtypeStruct((n, N_HEADS * HEAD_DIM), F32),
                   jax.ShapeDtypeStruct((n, KV_DIM), F32), jax.ShapeDtypeStruct((n, KV_DIM), F32)),
        compiler_params=_cparams(),
    )(x, shkv, sckv, sh, sc, gkv, gmix, wkv, bkv, wq, bq)


def _attn_sample_kernel(q_ref, ck_ref, cv_ref, nk_ref, nv_ref, slope_ref, sink_ref, o_ref, *, steps):
    rows = steps * Q_PER_KV
    sb = q_ref.shape[0]
    r_c = lax.broadcasted_iota(I32, (sb, rows, WINDOW), 1) // Q_PER_KV
    j_c = lax.broadcasted_iota(I32, (sb, rows, WINDOW), 2)
    dist_c = r_c + WINDOW - j_c
    valid_c = (dist_c >= 0) & (dist_c < WINDOW)
    r_n = lax.broadcasted_iota(I32, (sb, rows, steps), 1) // Q_PER_KV
    j_n = lax.broadcasted_iota(I32, (sb, rows, steps), 2)
    dist_n = r_n - j_n
    valid_n = (dist_n >= 0) & (dist_n < WINDOW)
    scale = HEAD_DIM ** -0.5
    for g in range(N_KV_HEADS):
        lanes = slice(g * HEAD_DIM, (g + 1) * HEAD_DIM)
        qg = q_ref[:, g].astype(BF16)
        slope = slope_ref[g][None]
        sink = sink_ref[g][None]
        kc, vc = ck_ref[:, :, lanes].astype(BF16), cv_ref[:, :, lanes].astype(BF16)
        kn, vn = nk_ref[:, :, lanes].astype(BF16), nv_ref[:, :, lanes].astype(BF16)
        s_c = jnp.einsum('nrd,njd->nrj', qg, kc, preferred_element_type=F32) * scale
        s_n = jnp.einsum('nrd,njd->nrj', qg, kn, preferred_element_type=F32) * scale
        s_c = jnp.where(valid_c, s_c - slope * dist_c.astype(F32), NEG_INF)
        s_n = jnp.where(valid_n, s_n - slope * dist_n.astype(F32), NEG_INF)
        m = jnp.maximum(jnp.maximum(jnp.max(s_c, axis=-1, keepdims=True),
                                    jnp.max(s_n, axis=-1, keepdims=True)), sink)
        e_c = jnp.exp(s_c - m)
        e_n = jnp.exp(s_n - m)
        den = (jnp.sum(e_c, axis=-1, keepdims=True) + jnp.sum(e_n, axis=-1, keepdims=True)
               + jnp.exp(sink - m))
        o = (jnp.einsum('nrj,njd->nrd', (e_c / den).astype(BF16), vc, preferred_element_type=F32)
             + jnp.einsum('nrj,njd->nrd', (e_n / den).astype(BF16), vn, preferred_element_type=F32))
        o_ref[:, g] = o


def _attn_sample(qg, cache_k, cache_v, k_new, v_new, slope_rows, sink_rows):
    n, _, rows, _ = qg.shape
    steps = rows // Q_PER_KV
    sb = 8
    blk = lambda shape: pl.BlockSpec((sb,) + shape, lambda i: (i,) + (0,) * len(shape))
    full = lambda shape: pl.BlockSpec(shape, lambda i: (0,) * len(shape))
    return pl.pallas_call(
        functools.partial(_attn_sample_kernel, steps=steps),
        grid=(n // sb,),
        in_specs=[blk((N_KV_HEADS, rows, HEAD_DIM)), blk((WINDOW, KV_DIM)), blk((WINDOW, KV_DIM)),
                  blk((steps, KV_DIM)), blk((steps, KV_DIM)),
                  full((N_KV_HEADS, rows, 1)), full((N_KV_HEADS, rows, 1))],
        out_specs=blk((N_KV_HEADS, rows, HEAD_DIM)),
        out_shape=jax.ShapeDtypeStruct(qg.shape, F32),
        compiler_params=_cparams(("arbitrary",)),
    )(qg, cache_k, cache_v, k_new, v_new, slope_rows, sink_rows)


def _oproj_kernel(o_ref, x_ref, g1_ref, wo_ref, y_ref):
    y_ref[...] = x_ref[...] + g1_ref[...] * _bdot(o_ref[...], wo_ref[...])


def _oproj(o, x, g1, wo):
    return pl.pallas_call(
        _oproj_kernel,
        out_shape=jax.ShapeDtypeStruct(x.shape, F32),
        compiler_params=_cparams(),
    )(o, x, g1, wo)


def kernel(x_prompt, x_sample, state_ssm_re, state_ssm_im, cache_k, cache_v, c_prompt, c_sample, g_mix, g_ffn, w_ada, b_ada, ssm_a_re, ssm_a_im, ssm_log_dt, ssm_b_re, ssm_b_im, ssm_c_re, ssm_c_im, ssm_d, w_glu_a, w_glu_b, g_kv, w_ada_kv, b_ada_kv, w_kv, b_kv, w_q, b_q, w_o, attn_sinks, w_router, b_router, w_gu, b_gu, w_down, b_down, g_final):
    bsz, seq, d = x_prompt.shape
    n_seq, steps, _ = x_sample.shape
    assert d == D_MODEL and seq % S5_CHUNK == 0 and seq % TM == 0 and (n_seq * steps) % TM == 0
    assert g_mix.shape[0] == 2 and ssm_a_re.shape[0] == 1 and w_q.shape[0] == 1
    n_p, n_s = bsz * seq, n_seq * steps
    tiles = _Tiles(n_p, n_s, seq)
    row = lambda v: v.reshape(1, -1)

    c_all = jnp.concatenate([c_prompt, c_sample], axis=0)
    c_rows = -(-c_all.shape[0] // SUBLANES) * SUBLANES
    c_all = jnp.pad(c_all, ((0, c_rows - c_all.shape[0]), (0, 0)))
    mods = [_ada(c_all, w_ada[l], b_ada[l]) for l in range(2)]
    mod_kv = _ada(c_all, w_ada_kv, b_ada_kv)

    def split(m, n):
        out = []
        for i in range(n):
            col = m[:, i * D_MODEL:(i + 1) * D_MODEL]
            out.append((col[:bsz, None, :], jnp.repeat(col[bsz:bsz + n_seq], steps, axis=0)))
        return out

    (sh1p, sh1s), (sc1p, sc1s), (g1p, g1s), (sh2p, sh2s), (sc2p, sc2s), (g2p, g2s) = split(mods[0], 6)
    abr, abi, bbr, bbi, apr, api = _ssm_prep(ssm_a_re[0], ssm_a_im[0], ssm_log_dt[0], ssm_b_re[0],
                                             ssm_b_im[0], S5_CHUNK // SUBLANES)
    b_re_x, b_im_x = _expand_b(bbr), _expand_b(bbi)
    cexp = (_expand_c(ssm_c_re[0]).astype(BF16), _expand_c(-ssm_c_im[0]).astype(BF16))
    wa, wb = w_glu_a[0].astype(BF16), w_glu_b[0].astype(BF16)
    xp, sre_p, sim_p = _s5_prompt(x_prompt, sh1p, sc1p, g1p, row(g_mix[0]), (abr, abi, apr, api),
                                  (b_re_x.astype(BF16), b_im_x.astype(BF16)), cexp, row(ssm_d[0]), wa, wb)
    per_seq = lambda v: v.reshape(n_seq, steps, D_MODEL)[:, 0, :]
    xs_t, sre_s, sim_s = _s5_sample(
        x_sample.transpose(1, 0, 2), per_seq(sh1s), per_seq(sc1s), per_seq(g1s), row(g_mix[0]),
        state_ssm_re[0].reshape(n_seq, SSM_COLS), state_ssm_im[0].reshape(n_seq, SSM_COLS), abr, abi,
        (_split_bf16(b_re_x), _split_bf16(b_im_x)), cexp, row(ssm_d[0]), wa, wb)
    xp = xp.reshape(n_p, D_MODEL)
    xs = xs_t.transpose(1, 0, 2).reshape(n_s, D_MODEL)
    xp, xs = _moe_layer(tiles, xp, xs, sh2p, sh2s, sc2p, sc2s, g2p, g2s, row(g_ffn[0]), w_router[0],
                        b_router[0], w_gu[0], b_gu[0], w_down[0], b_down[0], row(g_final), False)

    (sh1p, sh1s), (sc1p, sc1s), (g1p, g1s), (sh2p, sh2s), (sc2p, sc2s), (g2p, g2s) = split(mods[1], 6)
    (shkp, shks), (sckp, scks) = split(mod_kv, 2)
    wkv, wq, wo = w_kv.astype(BF16), w_q[0].astype(BF16), w_o[0].astype(BF16)
    xp3, k_p, v_p = _attn_prompt(xp.reshape(bsz, seq, D_MODEL), shkp, sckp, sh1p, sc1p, g1p, row(g_kv),
                                 row(g_mix[1]), wkv, row(b_kv), wq, row(b_q[0]), wo, attn_sinks[0])
    xp = xp3.reshape(n_p, D_MODEL)

    q_s, k_s, v_s = _qkv_sample(xs, shks, scks, sh1s, sc1s, row(g_kv), row(g_mix[1]), wkv, row(b_kv), wq,
                                row(b_q[0]))
    rows = steps * Q_PER_KV
    qg = q_s.reshape(n_seq, steps, N_KV_HEADS, Q_PER_KV, HEAD_DIM).transpose(0, 2, 1, 3, 4).reshape(
        n_seq, N_KV_HEADS, rows, HEAD_DIM)
    head_of_row = (np.arange(N_KV_HEADS)[:, None] * Q_PER_KV + np.arange(rows)[None, :] % Q_PER_KV)
    slope_rows = jnp.asarray(np.asarray(_slopes(), np.float32)[head_of_row][..., None])
    sink_rows = attn_sinks[0][head_of_row][..., None]
    k_new = k_s.reshape(n_seq, steps, KV_DIM)
    v_new = v_s.reshape(n_seq, steps, KV_DIM)
    ck = cache_k.reshape(n_seq, WINDOW, KV_DIM)
    cv = cache_v.reshape(n_seq, WINDOW, KV_DIM)
    og = _attn_sample(qg, ck, cv, k_new, v_new, slope_rows, sink_rows)
    o_s = og.reshape(n_seq, N_KV_HEADS, steps, Q_PER_KV, HEAD_DIM).transpose(0, 2, 1, 3, 4).reshape(
        n_s, N_HEADS * HEAD_DIM)
    xs = _oproj(o_s, xs, g1s, wo)

    yp, ys = _moe_layer(tiles, xp, xs, sh2p, sh2s, sc2p, sc2s, g2p, g2s, row(g_ffn[1]), w_router[1],
                        b_router[1], w_gu[1], b_gu[1], w_down[1], b_down[1], row(g_final), True)

    shape4 = lambda a: a.reshape(a.shape[0], a.shape[1], N_KV_HEADS, HEAD_DIM)
    state = lambda a, n: a.reshape(1, n, SSM_GROUPS, SSM_STATE)
    return (yp.reshape(bsz, seq, D_MODEL), ys.reshape(n_seq, steps, D_MODEL),
            state(sre_p, bsz), state(sim_p, bsz), shape4(k_p), shape4(v_p),
            state(sre_s, n_seq), state(sim_s, n_seq),
            shape4(jnp.concatenate([ck[:, steps:], k_new], axis=1)),
            shape4(jnp.concatenate([cv[:, steps:], v_new], axis=1)))
```

```python
import functools
import math

import numpy as np
import jax
import jax.numpy as jnp
from jax import lax
from jax.experimental import pallas as pl
from jax.experimental.pallas import tpu as pltpu

F32 = jnp.float32
BF16 = jnp.bfloat16
I32 = jnp.int32

D_MODEL = 1024
SSM_GROUP = 16
SSM_GROUPS = D_MODEL // SSM_GROUP
SSM_STATE = 64
SSM_COLS = SSM_GROUPS * SSM_STATE
N_HEADS = 16
HEAD_DIM = 64
N_KV_HEADS = 4
Q_PER_KV = N_HEADS // N_KV_HEADS
KV_DIM = N_KV_HEADS * HEAD_DIM
WINDOW = 128
N_EXPERTS = 32
TOP_K = 4
D_FF = D_MODEL
SWIGLU_LIMIT = 7.0
SWIGLU_ALPHA = 1.702
RMS_EPS = 1e-5
NEG_INF = -1e30

LANES = 128
SUBLANES = 8
ROW_TILES = D_MODEL // LANES
TM = 256
S5_CHUNK = 256
S5_COLS = 512
MXU_K = 256
GROUPS_PER_K = MXU_K // SSM_GROUP
N_KT = D_MODEL // MXU_K
VMEM_LIMIT = 56 * 1024 * 1024


def _cparams(sem=None):
    return pltpu.CompilerParams(dimension_semantics=sem, vmem_limit_bytes=VMEM_LIMIT)


def _sigmoid(x):
    return 1.0 / (1.0 + jnp.exp(-x))


def _rms(x, g):
    return x * lax.rsqrt(jnp.mean(x * x, axis=-1, keepdims=True) + RMS_EPS) * g


def _bdot(a, b):
    return jnp.dot(a.astype(BF16), b.astype(BF16), preferred_element_type=F32)


def _gelu_tanh(x):
    return 0.5 * x * (1.0 + jnp.tanh(math.sqrt(2.0 / math.pi) * (x + 0.044715 * (x * x * x))))


def _split_bf16(w):
    hi = w.astype(BF16)
    lo = (w - hi.astype(F32)).astype(BF16)
    return hi, lo


def _dot3(a, b_hi, b_lo):
    a_hi = a.astype(BF16)
    a_lo = (a - a_hi.astype(F32)).astype(BF16)
    return (jnp.dot(a_hi, b_hi, preferred_element_type=F32)
            + jnp.dot(a_lo, b_hi, preferred_element_type=F32)
            + jnp.dot(a_hi, b_lo, preferred_element_type=F32))


def _ada_kernel(c_ref, w_ref, b_ref, o_ref):
    c = c_ref[...]
    o_ref[...] = _bdot(c * _sigmoid(c), w_ref[...]) + b_ref[...]


def _ada(c, w, b, layer):
    rows, cols = c.shape[0], w.shape[2]
    tn = 1024
    return pl.pallas_call(
        _ada_kernel,
        grid=(cols // tn,),
        in_specs=[pl.BlockSpec((rows, D_MODEL), lambda j: (0, 0)),
                  pl.BlockSpec((None, D_MODEL, tn), lambda j: (layer, 0, j)),
                  pl.BlockSpec((None, 1, tn), lambda j: (layer, 0, j))],
        out_specs=pl.BlockSpec((rows, tn), lambda j: (0, j)),
        out_shape=jax.ShapeDtypeStruct((rows, cols), F32),
        compiler_params=_cparams(("arbitrary",)),
        name="ada",
    )(c, w, b.reshape(b.shape[0], 1, cols))


def _ssm_prep_kernel(are_ref, aim_ref, ldt_ref, bre_ref, bim_ref,
                     abr_ref, abi_ref, bbr_ref, bbi_ref, apr_ref, api_ref, *, sub_len):
    a_re, a_im = are_ref[...], aim_ref[...]
    dt = jnp.exp(ldt_ref[...])
    mag = jnp.exp(a_re * dt)
    ab_re = mag * jnp.cos(a_im * dt)
    ab_im = mag * jnp.sin(a_im * dt)
    abr_ref[...] = ab_re
    abi_ref[...] = ab_im
    den = a_re * a_re + a_im * a_im
    x, y = ab_re - 1.0, ab_im
    f_re = (x * a_re + y * a_im) / den
    f_im = (y * a_re - x * a_im) / den
    b_re, b_im = bre_ref[...], bim_ref[...]
    bbr_ref[...] = f_re * b_re - f_im * b_im
    bbi_ref[...] = f_re * b_im + f_im * b_re
    pr, pi = ab_re, ab_im
    for _ in range(int(math.log2(sub_len))):
        pr, pi = pr * pr - pi * pi, 2.0 * pr * pi
    cr, ci = jnp.ones_like(pr), jnp.zeros_like(pi)
    for j in range(SUBLANES + 1):
        apr_ref[j:j + 1, :] = cr
        api_ref[j:j + 1, :] = ci
        cr, ci = cr * pr - ci * pi, cr * pi + ci * pr


def _ssm_prep(a_re, a_im, log_dt, b_re, b_im, sub_len):
    gp = SSM_COLS
    ldt = jnp.broadcast_to(log_dt[:, None], (SSM_GROUPS, SSM_STATE)).reshape(1, gp)
    b_re_t = b_re.reshape(gp, SSM_GROUP).T
    b_im_t = b_im.reshape(gp, SSM_GROUP).T
    row = jax.ShapeDtypeStruct((1, gp), F32)
    mat = jax.ShapeDtypeStruct((SSM_GROUP, gp), F32)
    pw = jax.ShapeDtypeStruct((SUBLANES + 1, gp), F32)
    return pl.pallas_call(
        functools.partial(_ssm_prep_kernel, sub_len=sub_len),
        out_shape=(row, row, mat, mat, pw, pw),
        compiler_params=_cparams(),
        name="ssm_prep",
    )(a_re.reshape(1, gp), a_im.reshape(1, gp), ldt, b_re_t, b_im_t)


def _expand_b(bbar):
    eye = jnp.eye(GROUPS_PER_K, dtype=F32)
    b = bbar.reshape(SSM_GROUP, N_KT, GROUPS_PER_K, SSM_STATE).transpose(1, 0, 2, 3)
    e = b[:, None, :, :, :] * eye[None, :, None, :, None]
    return e.reshape(N_KT, MXU_K, GROUPS_PER_K * SSM_STATE)


def _expand_c(c):
    eye = jnp.eye(GROUPS_PER_K, dtype=F32)
    cc = c.reshape(N_KT, GROUPS_PER_K, SSM_GROUP, SSM_STATE).transpose(0, 1, 3, 2)
    e = cc[:, :, :, None, :] * eye[None, :, None, :, None]
    return e.reshape(N_KT, GROUPS_PER_K * SSM_STATE, MXU_K)


def _cmul(ar, ai, br, bi):
    return ar * br - ai * bi, ar * bi + ai * br


def _s5_prompt_kernel(x_ref, sh_ref, sc_ref, g1_ref, gmix_ref, abr_ref, abi_ref, apr_ref, api_ref,
                      bre_ref, bim_ref, cre_ref, cim_ref, dsk_ref, wa_ref, wb_ref,
                      o_ref, sre_ref, sim_ref,
                      u_ref, hr_ref, hi_ref, hmr_ref, hmi_ref, car_ref, cai_ref, *, L):
    ls = L // SUBLANES
    kcols = GROUPS_PER_K * SSM_STATE

    @pl.when(pl.program_id(1) == 0)
    def _():
        car_ref[...] = jnp.zeros_like(car_ref)
        cai_ref[...] = jnp.zeros_like(cai_ref)

    x = x_ref[...]
    u = _rms(x, gmix_ref[...]) * (1.0 + sc_ref[...]) + sh_ref[...]
    u_ref[...] = u
    r = lax.broadcasted_iota(I32, (L, L), 0)
    c = lax.broadcasted_iota(I32, (L, L), 1)
    perm = jnp.where(c == (r % SUBLANES) * ls + r // SUBLANES, 1.0, 0.0).astype(BF16)
    unperm = jnp.where(r == (c % SUBLANES) * ls + c // SUBLANES, 1.0, 0.0).astype(BF16)
    ub = jnp.dot(perm, u.astype(BF16), preferred_element_type=F32).astype(BF16)
    for kt in range(N_KT):
        uk = ub[:, kt * MXU_K:(kt + 1) * MXU_K]
        hr_ref[:, kt * kcols:(kt + 1) * kcols] = jnp.dot(uk, bre_ref[kt], preferred_element_type=F32)
        hi_ref[:, kt * kcols:(kt + 1) * kcols] = jnp.dot(uk, bim_ref[kt], preferred_element_type=F32)

    unroll = 4

    for cb in range(SSM_COLS // S5_COLS):
        cols = slice(cb * S5_COLS, (cb + 1) * S5_COLS)
        ar = jnp.broadcast_to(abr_ref[:, cols], (SUBLANES, S5_COLS))
        ai = jnp.broadcast_to(abi_ref[:, cols], (SUBLANES, S5_COLS))

        def scan_body(it, carry, cols=cols, ar=ar, ai=ai):
            sr, si = carry
            for k in range(unroll):
                r0 = pl.multiple_of((it * unroll + k) * SUBLANES, SUBLANES)
                tr, ti = _cmul(ar, ai, sr, si)
                sr = tr + hr_ref[pl.ds(r0, SUBLANES), cols]
                si = ti + hi_ref[pl.ds(r0, SUBLANES), cols]
                hr_ref[pl.ds(r0, SUBLANES), cols] = sr
                hi_ref[pl.ds(r0, SUBLANES), cols] = si
            return sr, si

        zero = jnp.zeros((SUBLANES, S5_COLS), F32)
        lax.fori_loop(0, ls // unroll, scan_body, (zero, zero))

    fr = hr_ref[L - SUBLANES:L, :]
    fi = hi_ref[L - SUBLANES:L, :]
    row = lax.broadcasted_iota(I32, (SUBLANES, SSM_COLS), 0)
    gr, gi = fr, fi
    pr, pi = apr_ref[1:2, :], api_ref[1:2, :]
    for s in (1, 2, 4):
        sr = jnp.where(row >= s, pltpu.roll(gr, s, axis=0), 0.0)
        si = jnp.where(row >= s, pltpu.roll(gi, s, axis=0), 0.0)
        tr, ti = _cmul(pr, pi, sr, si)
        gr, gi = gr + tr, gi + ti
        pr, pi = _cmul(pr, pi, pr, pi)
    c0r, c0i = car_ref[...], cai_ref[...]
    tr, ti = _cmul(apr_ref[0:SUBLANES, :], api_ref[0:SUBLANES, :], c0r, c0i)
    hmr = tr + jnp.where(row >= 1, pltpu.roll(gr, 1, axis=0), 0.0)
    hmi = ti + jnp.where(row >= 1, pltpu.roll(gi, 1, axis=0), 0.0)
    hmr_ref[...] = hmr
    hmi_ref[...] = hmi
    tr, ti = _cmul(apr_ref[SUBLANES:SUBLANES + 1, :], api_ref[SUBLANES:SUBLANES + 1, :], c0r, c0i)
    ncr = tr + gr[SUBLANES - 1:SUBLANES, :]
    nci = ti + gi[SUBLANES - 1:SUBLANES, :]
    car_ref[...] = ncr
    cai_ref[...] = nci
    sre_ref[...] = ncr
    sim_ref[...] = nci

    for cb in range(SSM_COLS // S5_COLS):
        cols = slice(cb * S5_COLS, (cb + 1) * S5_COLS)
        ar = jnp.broadcast_to(abr_ref[:, cols], (SUBLANES, S5_COLS))
        ai = jnp.broadcast_to(abi_ref[:, cols], (SUBLANES, S5_COLS))

        def fix_body(it, carry, cols=cols, ar=ar, ai=ai):
            dr, di = carry
            for k in range(unroll):
                r0 = pl.multiple_of((it * unroll + k) * SUBLANES, SUBLANES)
                dr, di = _cmul(ar, ai, dr, di)
                hr_ref[pl.ds(r0, SUBLANES), cols] = hr_ref[pl.ds(r0, SUBLANES), cols] + dr
                hi_ref[pl.ds(r0, SUBLANES), cols] = hi_ref[pl.ds(r0, SUBLANES), cols] + di
            return dr, di

        lax.fori_loop(0, ls // unroll, fix_body, (hmr_ref[:, cols], hmi_ref[:, cols]))

    ys = []
    for nt in range(N_KT):
        hr = hr_ref[:, nt * kcols:(nt + 1) * kcols].astype(BF16)
        hi = hi_ref[:, nt * kcols:(nt + 1) * kcols].astype(BF16)
        ys.append(jnp.dot(hr, cre_ref[nt], preferred_element_type=F32)
                  + jnp.dot(hi, cim_ref[nt], preferred_element_type=F32))
    yp = jnp.concatenate(ys, axis=-1)
    yp_hi = yp.astype(BF16)
    yp_lo = (yp - yp_hi.astype(F32)).astype(BF16)
    y = (jnp.dot(unperm, yp_hi, preferred_element_type=F32)
         + jnp.dot(unperm, yp_lo, preferred_element_type=F32)) + dsk_ref[...] * u_ref[...]
    z = _gelu_tanh(y).astype(BF16)
    mix = jnp.dot(z, wa_ref[...], preferred_element_type=F32) * _sigmoid(
        jnp.dot(z, wb_ref[...], preferred_element_type=F32))
    o_ref[...] = x_ref[...] + g1_ref[...] * mix


def _s5_prompt(x, sh, sc, g1, gmix, prep, bexp, cexp, dsk, wa, wb):
    b, t, _ = x.shape
    L = S5_CHUNK
    abr, abi, apr, api = prep
    bre, bim = bexp
    cre, cim = cexp
    kcols = GROUPS_PER_K * SSM_STATE
    full = lambda shape: pl.BlockSpec(shape, lambda bi, ci: (0,) * len(shape))
    mod = pl.BlockSpec((None, 1, D_MODEL), lambda bi, ci: (bi, 0, 0))
    return pl.pallas_call(
        functools.partial(_s5_prompt_kernel, L=L),
        grid=(b, t // L),
        in_specs=[pl.BlockSpec((None, L, D_MODEL), lambda bi, ci: (bi, ci, 0)), mod, mod, mod,
                  full((1, D_MODEL)), full((1, SSM_COLS)), full((1, SSM_COLS)),
                  full((SUBLANES + 1, SSM_COLS)), full((SUBLANES + 1, SSM_COLS)),
                  full((N_KT, MXU_K, kcols)), full((N_KT, MXU_K, kcols)),
                  full((N_KT, kcols, MXU_K)), full((N_KT, kcols, MXU_K)),
                  full((1, D_MODEL)), full((D_MODEL, D_MODEL)), full((D_MODEL, D_MODEL))],
        out_specs=[pl.BlockSpec((None, L, D_MODEL), lambda bi, ci: (bi, ci, 0)),
                   pl.BlockSpec((None, 1, SSM_COLS), lambda bi, ci: (bi, 0, 0)),
                   pl.BlockSpec((None, 1, SSM_COLS), lambda bi, ci: (bi, 0, 0))],
        out_shape=(jax.ShapeDtypeStruct((b, t, D_MODEL), F32),
                   jax.ShapeDtypeStruct((b, 1, SSM_COLS), F32),
                   jax.ShapeDtypeStruct((b, 1, SSM_COLS), F32)),
        scratch_shapes=[pltpu.VMEM((L, D_MODEL), F32),
                        pltpu.VMEM((L, SSM_COLS), F32), pltpu.VMEM((L, SSM_COLS), F32),
                        pltpu.VMEM((SUBLANES, SSM_COLS), F32), pltpu.VMEM((SUBLANES, SSM_COLS), F32),
                        pltpu.VMEM((1, SSM_COLS), F32), pltpu.VMEM((1, SSM_COLS), F32)],
        compiler_params=_cparams(("arbitrary", "arbitrary")),
        name="s5_prompt",
    )(x, sh, sc, g1, gmix, abr, abi, apr, api, bre, bim, cre, cim, dsk, wa, wb)


def _s5_sample_kernel(x_ref, sh_ref, sc_ref, g1_ref, gmix_ref, h0r_ref, h0i_ref, abr_ref, abi_ref,
                      brh_ref, brl_ref, bih_ref, bil_ref, cre_ref, cim_ref, dsk_ref, wa_ref, wb_ref,
                      o_ref, sre_ref, sim_ref, *, steps):
    kcols = GROUPS_PER_K * SSM_STATE
    ar, ai = abr_ref[...], abi_ref[...]
    sr, si = h0r_ref[...], h0i_ref[...]
    for t in range(steps):
        x = x_ref[t]
        u = _rms(x, gmix_ref[...]) * (1.0 + sc_ref[...]) + sh_ref[...]
        bur, bui = [], []
        for kt in range(N_KT):
            uk = u[:, kt * MXU_K:(kt + 1) * MXU_K]
            bur.append(_dot3(uk, brh_ref[kt], brl_ref[kt]))
            bui.append(_dot3(uk, bih_ref[kt], bil_ref[kt]))
        tr, ti = _cmul(ar, ai, sr, si)
        sr = tr + jnp.concatenate(bur, axis=-1)
        si = ti + jnp.concatenate(bui, axis=-1)
        ys = []
        for nt in range(N_KT):
            ys.append(_bdot(sr[:, nt * kcols:(nt + 1) * kcols], cre_ref[nt])
                      + _bdot(si[:, nt * kcols:(nt + 1) * kcols], cim_ref[nt]))
        y = jnp.concatenate(ys, axis=-1) + dsk_ref[...] * u
        z = _gelu_tanh(y).astype(BF16)
        mix = jnp.dot(z, wa_ref[...], preferred_element_type=F32) * _sigmoid(
            jnp.dot(z, wb_ref[...], preferred_element_type=F32))
        o_ref[t] = x + g1_ref[...] * mix
    sre_ref[...] = sr
    sim_ref[...] = si


def _s5_sample(x_t, sh, sc, g1, gmix, h0r, h0i, abr, abi, bsplit, cexp, dsk, wa, wb):
    steps, n, _ = x_t.shape
    (brh, brl), (bih, bil) = bsplit
    cre, cim = cexp
    return pl.pallas_call(
        functools.partial(_s5_sample_kernel, steps=steps),
        out_shape=(jax.ShapeDtypeStruct((steps, n, D_MODEL), F32),
                   jax.ShapeDtypeStruct((n, SSM_COLS), F32),
                   jax.ShapeDtypeStruct((n, SSM_COLS), F32)),
        compiler_params=_cparams(),
        name="s5_sample",
    )(x_t, sh, sc, g1, gmix, h0r, h0i, abr, abi, brh, brl, bih, bil, cre, cim, dsk, wa, wb)


class _Tiles:
    def __init__(self, n_prompt, n_sample, seq_len):
        self.ntp = n_prompt // TM
        self.nts = n_sample // TM
        self.nt = self.ntp + self.nts
        self.tiles_per_seq = seq_len // TM
        self.n_batch = n_prompt // seq_len
        self.n_tok = n_prompt + n_sample

    def p_rows(self, width=D_MODEL):
        return pl.BlockSpec((TM, width), lambda t, *_: (jnp.minimum(t, self.ntp - 1), 0))

    def s_rows(self, width=D_MODEL):
        return pl.BlockSpec((TM, width), lambda t, *_: (jnp.maximum(t - self.ntp, 0), 0))

    def p_mod(self):
        return pl.BlockSpec((None, 1, D_MODEL),
                            lambda t, *_: (jnp.minimum(t // self.tiles_per_seq, self.n_batch - 1), 0, 0))

    def all_rows(self, width):
        return pl.BlockSpec((TM, width), lambda t, *_: (t, 0))


def _const_spec(shape):
    return pl.BlockSpec(shape, lambda t, *_: (0,) * len(shape))


def _by_kind(tiles, body, prompt_refs, sample_refs):
    t = pl.program_id(0)

    @pl.when(t < tiles.ntp)
    def _():
        body(*prompt_refs)

    @pl.when(t >= tiles.ntp)
    def _():
        body(*sample_refs)


def _moe_input(x_ref, sh_ref, sc_ref, g_ref):
    return _rms(x_ref[...], g_ref[...]) * (1.0 + sc_ref[...]) + sh_ref[...]


def _router_kernel(xp_ref, xs_ref, shp_ref, shs_ref, scp_ref, scs_ref, g_ref, wh_ref, wl_ref, b_ref, tri_ref,
                   mi_ref, mw_ref, cnt_ref, run_ref, *, tiles):
    @pl.when(pl.program_id(0) == 0)
    def _():
        run_ref[...] = jnp.zeros_like(run_ref)

    def route(x_ref, sh_ref, sc_ref):
        h = _moe_input(x_ref, sh_ref, sc_ref, g_ref)
        logits = _dot3(h, wh_ref[...], wl_ref[...]) + b_ref[...]
        lane = lax.broadcasted_iota(I32, (TM, LANES), 1)
        vals, firsts, hots = [], [], []
        work = logits
        for _ in range(TOP_K):
            m = jnp.max(work, axis=-1, keepdims=True)
            first = jnp.min(jnp.where(work == m, lane, LANES), axis=-1, keepdims=True)
            hot = lane == first
            vals.append(m)
            firsts.append(first)
            hots.append(hot)
            work = jnp.where(hot, -jnp.inf, work)
        exps = [jnp.exp(v - vals[0]) for v in vals]
        den = exps[0] + exps[1] + exps[2] + exps[3]

        chosen = jnp.zeros((TM, LANES), F32)
        for hot in hots:
            chosen = jnp.where(hot, 1.0, chosen)
        rank_all = jnp.dot(tri_ref[...], chosen.astype(BF16), preferred_element_type=F32) + run_ref[...]
        run_ref[...] = run_ref[...] + jnp.sum(chosen, axis=0, keepdims=True)
        cnt_ref[...] = run_ref[...]

        mi = jnp.zeros((TM, LANES), I32)
        mw = jnp.zeros((TM, LANES), F32)
        for k in range(TOP_K):
            r_k = jnp.sum(jnp.where(hots[k], rank_all, 0.0), axis=-1, keepdims=True).astype(I32)
            mi = jnp.where(lane == k, firsts[k], mi)
            mi = jnp.where(lane == TOP_K + k, r_k, mi)
            mw = jnp.where(lane == k, exps[k] / den, mw)
        mi_ref[...] = mi
        mw_ref[...] = mw

    _by_kind(tiles, route, (xp_ref, shp_ref, scp_ref), (xs_ref, shs_ref, scs_ref))


def _router(tiles, xp, xs, shp, shs, scp, scs, g, w_hi, w_lo, b_pad):
    tri = jnp.asarray(np.tril(np.ones((TM, TM), np.float32), -1), BF16)
    return pl.pallas_call(
        functools.partial(_router_kernel, tiles=tiles),
        grid=(tiles.nt,),
        in_specs=[tiles.p_rows(), tiles.s_rows(), tiles.p_mod(), tiles.s_rows(), tiles.p_mod(),
                  tiles.s_rows(), _const_spec((1, D_MODEL)), _const_spec((D_MODEL, LANES)),
                  _const_spec((D_MODEL, LANES)), _const_spec((1, LANES)), _const_spec((TM, TM))],
        out_specs=[tiles.all_rows(LANES), tiles.all_rows(LANES), _const_spec((1, LANES))],
        out_shape=(jax.ShapeDtypeStruct((tiles.n_tok, LANES), I32),
                   jax.ShapeDtypeStruct((tiles.n_tok, LANES), F32),
                   jax.ShapeDtypeStruct((1, LANES), F32)),
        scratch_shapes=[pltpu.VMEM((1, LANES), F32)],
        compiler_params=_cparams(("arbitrary",)),
        name="router",
    )(xp, xs, shp, shs, scp, scs, g, w_hi, w_lo, b_pad, tri)


PAD_ARMS = tuple(1 << i for i in reversed(range(int(math.log2(TM)))))


def _dispatch_kernel(slot_ref, pstart_ref, pcnt_ref,
                     xp_ref, xs_ref, shp_ref, shs_ref, scp_ref, scs_ref, g_ref,
                     o_hbm, rows_ref, zero_ref, sems, zsem, *, tiles):
    t = pl.program_id(0)
    buf = t % 2
    tile_rows = TM * ROW_TILES
    base = pl.multiple_of(buf * tile_rows, tile_rows)

    def tile_wait(b):
        b0 = pl.multiple_of(b * tile_rows, tile_rows)
        for _ in range(TOP_K):
            pltpu.make_async_copy(rows_ref.at[pl.ds(b0, tile_rows), :],
                                  o_hbm.at[pl.ds(0, tile_rows), :], sems.at[b]).wait()

    def pad_copies(act):
        def body(e, carry):
            start, cnt = pstart_ref[e], pcnt_ref[e]
            for p in PAD_ARMS:
                @pl.when((cnt & p) != 0)
                def _(start=start, p=p):
                    cp = pltpu.make_async_copy(
                        zero_ref.at[pl.ds(0, p * ROW_TILES), :],
                        o_hbm.at[pl.ds(pl.multiple_of(start * ROW_TILES, ROW_TILES), p * ROW_TILES), :],
                        zsem)
                    cp.start() if act == "start" else cp.wait()
                start = start + jnp.where((cnt & p) != 0, p, 0)
            return carry
        lax.fori_loop(0, N_EXPERTS, body, 0)

    @pl.when(t == 0)
    def _():
        zero_ref[...] = jnp.zeros_like(zero_ref)
        pad_copies("start")

    @pl.when(t >= 2)
    def _():
        tile_wait(buf)

    def stage(x_ref, sh_ref, sc_ref):
        h = _moe_input(x_ref, sh_ref, sc_ref, g_ref)
        for c in range(ROW_TILES):
            rows_ref[pl.ds(base + c, TM, stride=ROW_TILES), :] = h[:, c * LANES:(c + 1) * LANES]

    _by_kind(tiles, stage, (xp_ref, shp_ref, scp_ref), (xs_ref, shs_ref, scs_ref))

    def issue(n, carry):
        src = rows_ref.at[pl.ds(pl.multiple_of(base + n * ROW_TILES, ROW_TILES), ROW_TILES), :]
        for k in range(TOP_K):
            slot = slot_ref[(t * TM + n) * TOP_K + k]
            dst = o_hbm.at[pl.ds(pl.multiple_of(slot * ROW_TILES, ROW_TILES), ROW_TILES), :]
            pltpu.make_async_copy(src, dst, sems.at[buf]).start()
        return carry
    lax.fori_loop(0, TM, issue, 0)

    @pl.when(t == tiles.nt - 1)
    def _():
        if tiles.nt >= 2:
            tile_wait(1 - buf)
        tile_wait(buf)
        pad_copies("wait")


def _dispatch(tiles, n_slots, slots, pad_start, pad_cnt, xp, xs, shp, shs, scp, scs, g):
    grid_spec = pltpu.PrefetchScalarGridSpec(
        num_scalar_prefetch=3,
        grid=(tiles.nt,),
        in_specs=[tiles.p_rows(), tiles.s_rows(), tiles.p_mod(), tiles.s_rows(), tiles.p_mod(),
                  tiles.s_rows(), _const_spec((1, D_MODEL))],
        out_specs=pl.BlockSpec(memory_space=pl.ANY),
        scratch_shapes=[pltpu.VMEM((2 * TM * ROW_TILES, LANES), F32),
                        pltpu.VMEM((PAD_ARMS[0] * ROW_TILES, LANES), F32),
                        pltpu.SemaphoreType.DMA((2,)), pltpu.SemaphoreType.DMA(())],
    )
    return pl.pallas_call(
        functools.partial(_dispatch_kernel, tiles=tiles),
        grid_spec=grid_spec,
        out_shape=jax.ShapeDtypeStruct((n_slots * ROW_TILES, LANES), F32),
        compiler_params=_cparams(("arbitrary",)),
        name="dispatch",
    )(slots, pad_start, pad_cnt, xp, xs, shp, shs, scp, scs, g)


def _expert_kernel(te_ref, nt_ref, hs_ref, wgu_ref, bgu_ref, wd_ref, bd_ref, o_ref,
                   wgu_bf, wd_bf, lhs_ref):
    i = pl.program_id(0)

    @pl.when(i < nt_ref[0])
    def _():
        prev = te_ref[jnp.maximum(i - 1, 0)]

        @pl.when((i == 0) | (te_ref[i] != prev))
        def _():
            wgu_bf[...] = wgu_ref[...].astype(BF16)
            wd_bf[...] = wd_ref[...].astype(BF16)

        for c in range(ROW_TILES):
            lhs_ref[:, c * LANES:(c + 1) * LANES] = hs_ref[pl.ds(c, TM, stride=ROW_TILES), :].astype(BF16)
        gu = jnp.dot(lhs_ref[...], wgu_bf[...], preferred_element_type=F32) + bgu_ref[...]
        gate = jnp.minimum(gu[:, :D_FF], SWIGLU_LIMIT)
        up = jnp.clip(gu[:, D_FF:], -SWIGLU_LIMIT, SWIGLU_LIMIT)
        act = ((up + 1.0) * gate * _sigmoid(SWIGLU_ALPHA * gate)).astype(BF16)
        y = jnp.dot(act, wd_bf[...], preferred_element_type=F32) + bd_ref[...]
        for c in range(ROW_TILES):
            o_ref[pl.ds(c, TM, stride=ROW_TILES), :] = y[:, c * LANES:(c + 1) * LANES]


def _experts(max_tiles, tile_expert, n_tiles, hs, w_gu, b_gu, w_down, b_down, layer):
    tile_rows = TM * ROW_TILES
    row_map = lambda i, te, nt: (jnp.minimum(i, nt[0] - 1), 0)
    w_map = lambda i, te, nt: (layer, te[i], 0, 0)
    grid_spec = pltpu.PrefetchScalarGridSpec(
        num_scalar_prefetch=2,
        grid=(max_tiles,),
        in_specs=[pl.BlockSpec((tile_rows, LANES), row_map),
                  pl.BlockSpec((None, None, D_MODEL, 2 * D_FF), w_map),
                  pl.BlockSpec((None, None, 1, 2 * D_FF), w_map),
                  pl.BlockSpec((None, None, D_FF, D_MODEL), w_map),
                  pl.BlockSpec((None, None, 1, D_MODEL), w_map)],
        out_specs=pl.BlockSpec((tile_rows, LANES), row_map),
        scratch_shapes=[pltpu.VMEM((D_MODEL, 2 * D_FF), BF16), pltpu.VMEM((D_FF, D_MODEL), BF16),
                        pltpu.VMEM((TM, D_MODEL), BF16)],
    )
    return pl.pallas_call(
        _expert_kernel,
        grid_spec=grid_spec,
        out_shape=jax.ShapeDtypeStruct(hs.shape, F32),
        compiler_params=_cparams(("arbitrary",)),
        name="experts",
    )(tile_expert, n_tiles, hs, w_gu, b_gu[:, :, None, :], w_down, b_down[:, :, None, :])


def _combine_kernel(slot_ref, y_hbm, mw_ref, xp_ref, xs_ref, g2p_ref, g2s_ref, gfin_ref,
                    op_ref, os_ref, gbuf, acc_ref, sems, *, tiles, final_norm):
    t = pl.program_id(0)
    tile_rows = TM * ROW_TILES
    buf_rows = TOP_K * tile_rows

    def issue_tile(tt, b):
        b0 = pl.multiple_of(b * buf_rows, buf_rows)

        def issue(n, carry):
            for k in range(TOP_K):
                slot = slot_ref[(tt * TM + n) * TOP_K + k]
                src = y_hbm.at[pl.ds(pl.multiple_of(slot * ROW_TILES, ROW_TILES), ROW_TILES), :]
                dst = gbuf.at[pl.ds(pl.multiple_of(b0 + k * tile_rows + n * ROW_TILES, ROW_TILES),
                                    ROW_TILES), :]
                pltpu.make_async_copy(src, dst, sems.at[b]).start()
            return carry
        lax.fori_loop(0, TM, issue, 0)

    @pl.when(t == 0)
    def _():
        issue_tile(0, 0)

    @pl.when(t + 1 < tiles.nt)
    def _():
        issue_tile(t + 1, (t + 1) % 2)

    buf = t % 2
    b0 = pl.multiple_of(buf * buf_rows, buf_rows)
    for _ in range(TOP_K):
        pltpu.make_async_copy(y_hbm.at[pl.ds(0, tile_rows), :],
                              gbuf.at[pl.ds(b0, tile_rows), :], sems.at[buf]).wait()

    w = mw_ref[...]
    for c in range(ROW_TILES):
        acc = None
        for k in range(TOP_K):
            rows = gbuf[pl.ds(b0 + k * tile_rows + c, TM, stride=ROW_TILES), :]
            term = w[:, k:k + 1] * rows
            acc = term if acc is None else acc + term
        acc_ref[:, c * LANES:(c + 1) * LANES] = acc

    def finish(x_ref, g2_ref, o_ref):
        res = x_ref[...] + g2_ref[...] * acc_ref[...]
        if final_norm:
            res = _rms(res, gfin_ref[...])
        o_ref[...] = res

    _by_kind(tiles, finish, (xp_ref, g2p_ref, op_ref), (xs_ref, g2s_ref, os_ref))


def _combine(tiles, slots, y_sorted, mw, xp, xs, g2p, g2s, gfin, final_norm):
    grid_spec = pltpu.PrefetchScalarGridSpec(
        num_scalar_prefetch=1,
        grid=(tiles.nt,),
        in_specs=[pl.BlockSpec(memory_space=pl.ANY), tiles.all_rows(LANES), tiles.p_rows(), tiles.s_rows(),
                  tiles.p_mod(), tiles.s_rows(), _const_spec((1, D_MODEL))],
        out_specs=[tiles.p_rows(), tiles.s_rows()],
        scratch_shapes=[pltpu.VMEM((2 * TOP_K * TM * ROW_TILES, LANES), F32),
                        pltpu.VMEM((TM, D_MODEL), F32), pltpu.SemaphoreType.DMA((2,))],
    )
    return pl.pallas_call(
        functools.partial(_combine_kernel, tiles=tiles, final_norm=final_norm),
        grid_spec=grid_spec,
        out_shape=(jax.ShapeDtypeStruct(xp.shape, F32), jax.ShapeDtypeStruct(xs.shape, F32)),
        compiler_params=_cparams(("arbitrary",)),
        name="combine",
    )(slots, y_sorted, mw, xp, xs, g2p, g2s, gfin)


def _moe_layer(tiles, xp, xs, shp, shs, scp, scs, g2p, g2s, g_ffn, w_router, b_router, w_gu, b_gu,
               w_down, b_down, gfin, layer, final_norm):
    n_tok = tiles.n_tok
    w_pad = jnp.pad(w_router, ((0, 0), (0, LANES - N_EXPERTS)))
    w_hi, w_lo = _split_bf16(w_pad)
    b_pad = jnp.pad(b_router, (0, LANES - N_EXPERTS), constant_values=NEG_INF).reshape(1, LANES)
    mi, mw, counts = _router(tiles, xp, xs, shp, shs, scp, scs, g_ffn, w_hi, w_lo, b_pad)

    cnt = counts[0, :N_EXPERTS].astype(I32)
    tiles_e = (cnt + TM - 1) // TM
    tile_end = jnp.cumsum(tiles_e)
    tile_start = tile_end - tiles_e
    idx, rank = mi[:, :TOP_K], mi[:, TOP_K:2 * TOP_K]
    offs = jnp.sum(jnp.where(idx[..., None] == jnp.arange(N_EXPERTS, dtype=I32),
                             (tile_start * TM)[None, None, :], 0), axis=-1)
    slots = (offs + rank).reshape(n_tok * TOP_K)
    max_tiles = (n_tok * TOP_K) // TM + N_EXPERTS
    tile_expert = jnp.minimum(
        jnp.sum(jnp.arange(max_tiles, dtype=I32)[:, None] >= tile_end[None, :], axis=-1), N_EXPERTS - 1
    ).astype(I32)
    n_tiles = tile_end[-1:].astype(I32)
    pad_start = (tile_start * TM + cnt).astype(I32)
    pad_cnt = (tiles_e * TM - cnt).astype(I32)

    hs = _dispatch(tiles, max_tiles * TM, slots, pad_start, pad_cnt, xp, xs, shp, shs, scp, scs, g_ffn)
    ys = _experts(max_tiles, tile_expert, n_tiles, hs, w_gu, b_gu, w_down, b_down, layer)
    return _combine(tiles, slots, ys, mw, xp, xs, g2p, g2s, gfin, final_norm)


def _slopes():
    return [2.0 ** (-8.0 * (h + 1) / N_HEADS) for h in range(N_HEADS)]


def _attn_prompt_kernel(sink_ref, x_ref, shkv_ref, sckv_ref, sh_ref, sc_ref, g1_ref, gkv_ref, gmix_ref,
                        wkv_ref, bkv_ref, wq_ref, bq_ref, wo_ref,
                        o_ref, kl_ref, vl_ref, kk_ref, vv_ref, oh_ref):
    j = pl.program_id(1)
    blk = WINDOW

    @pl.when(j == 0)
    def _():
        kk_ref[...] = jnp.zeros_like(kk_ref)
        vv_ref[...] = jnp.zeros_like(vv_ref)

    @pl.when(j > 0)
    def _():
        kk_ref[0:blk, :] = kk_ref[blk:2 * blk, :]
        vv_ref[0:blk, :] = vv_ref[blk:2 * blk, :]

    x = x_ref[...]
    hkv = _rms(x, gkv_ref[...]) * (1.0 + sckv_ref[...]) + shkv_ref[...]
    kv = _bdot(hkv, wkv_ref[...]) + bkv_ref[...]
    kl_ref[...] = kv[:, :KV_DIM]
    vl_ref[...] = kv[:, KV_DIM:]
    kk_ref[blk:2 * blk, :] = kv[:, :KV_DIM].astype(BF16)
    vv_ref[blk:2 * blk, :] = kv[:, KV_DIM:].astype(BF16)

    h = _rms(x, gmix_ref[...]) * (1.0 + sc_ref[...]) + sh_ref[...]
    q = _bdot(h, wq_ref[...]) + bq_ref[...]

    r = lax.broadcasted_iota(I32, (blk, 2 * blk), 0)
    c = lax.broadcasted_iota(I32, (blk, 2 * blk), 1)
    dist_i = r + blk - c
    valid = (dist_i >= 0) & (dist_i < WINDOW) & jnp.logical_not((j == 0) & (c < blk))
    dist = dist_i.astype(F32)
    slopes = _slopes()
    for g in range(N_KV_HEADS):
        kg = kk_ref[:, g * HEAD_DIM:(g + 1) * HEAD_DIM]
        vg = vv_ref[:, g * HEAD_DIM:(g + 1) * HEAD_DIM]
        for qh in range(Q_PER_KV):
            hd = g * Q_PER_KV + qh
            qd = q[:, hd * HEAD_DIM:(hd + 1) * HEAD_DIM].astype(BF16)
            s = lax.dot_general(qd, kg, (((1,), (1,)), ((), ())), preferred_element_type=F32)
            s = s * (HEAD_DIM ** -0.5) - slopes[hd] * dist
            s = jnp.where(valid, s, NEG_INF)
            sink = sink_ref[hd]
            m = jnp.maximum(jnp.max(s, axis=-1, keepdims=True), sink)
            e = jnp.exp(s - m)
            den = jnp.sum(e, axis=-1, keepdims=True) + jnp.exp(sink - m)
            p = (e / den).astype(BF16)
            oh_ref[:, hd * HEAD_DIM:(hd + 1) * HEAD_DIM] = jnp.dot(p, vg, preferred_element_type=F32)
    mix = _bdot(oh_ref[...], wo_ref[...])
    o_ref[...] = x + g1_ref[...] * mix


def _attn_prompt(x, shkv, sckv, sh, sc, g1, gkv, gmix, wkv, bkv, wq, bq, wo, sinks):
    b, t, _ = x.shape
    blk = WINDOW
    qd = N_HEADS * HEAD_DIM
    full = lambda shape: pl.BlockSpec(shape, lambda bi, ji, *_: (0,) * len(shape))
    mod = pl.BlockSpec((None, 1, D_MODEL), lambda bi, ji, *_: (bi, 0, 0))
    rows = pl.BlockSpec((None, blk, D_MODEL), lambda bi, ji, *_: (bi, ji, 0))
    last = pl.BlockSpec((None, blk, KV_DIM), lambda bi, ji, *_: (bi, 0, 0))
    grid_spec = pltpu.PrefetchScalarGridSpec(
        num_scalar_prefetch=1,
        grid=(b, t // blk),
        in_specs=[rows, mod, mod, mod, mod, mod, full((1, D_MODEL)), full((1, D_MODEL)),
                  full((D_MODEL, 2 * KV_DIM)), full((1, 2 * KV_DIM)), full((D_MODEL, qd)), full((1, qd)),
                  full((qd, D_MODEL))],
        out_specs=[rows, last, last],
        scratch_shapes=[pltpu.VMEM((2 * blk, KV_DIM), BF16), pltpu.VMEM((2 * blk, KV_DIM), BF16),
                        pltpu.VMEM((blk, qd), F32)],
    )
    return pl.pallas_call(
        _attn_prompt_kernel,
        grid_spec=grid_spec,
        out_shape=(jax.ShapeDtypeStruct((b, t, D_MODEL), F32),
                   jax.ShapeDtypeStruct((b, blk, KV_DIM), F32),
                   jax.ShapeDtypeStruct((b, blk, KV_DIM), F32)),
        compiler_params=_cparams(("arbitrary", "arbitrary")),
        name="attn_prompt",
    )(sinks, x, shkv, sckv, sh, sc, g1, gkv, gmix, wkv, bkv, wq, bq, wo)


def _qkv_sample_kernel(x_ref, shkv_ref, sckv_ref, sh_ref, sc_ref, gkv_ref, gmix_ref,
                       wkv_ref, bkv_ref, wq_ref, bq_ref, q_ref, k_ref, v_ref):
    x = x_ref[...]
    hkv = _rms(x, gkv_ref[...]) * (1.0 + sckv_ref[...]) + shkv_ref[...]
    kv = _bdot(hkv, wkv_ref[...]) + bkv_ref[...]
    k_ref[...] = kv[:, :KV_DIM]
    v_ref[...] = kv[:, KV_DIM:]
    h = _rms(x, gmix_ref[...]) * (1.0 + sc_ref[...]) + sh_ref[...]
    q_ref[...] = _bdot(h, wq_ref[...]) + bq_ref[...]


def _qkv_sample(x, shkv, sckv, sh, sc, gkv, gmix, wkv, bkv, wq, bq):
    n = x.shape[0]
    return pl.pallas_call(
        _qkv_sample_kernel,
        out_shape=(jax.ShapeDtypeStruct((n, N_HEADS * HEAD_DIM), F32),
                   jax.ShapeDtypeStruct((n, KV_DIM), F32), jax.ShapeDtypeStruct((n, KV_DIM), F32)),
        compiler_params=_cparams(),
        name="qkv_sample",
    )(x, shkv, sckv, sh, sc, gkv, gmix, wkv, bkv, wq, bq)


def _attn_sample_kernel(q_ref, ck_ref, cv_ref, nk_ref, nv_ref, slope_ref, sink_ref, o_ref, *, steps):
    rows = steps * Q_PER_KV
    sb = q_ref.shape[0]
    r_c = lax.broadcasted_iota(I32, (sb, rows, WINDOW), 1) // Q_PER_KV
    j_c = lax.broadcasted_iota(I32, (sb, rows, WINDOW), 2)
    dist_c = r_c + WINDOW - j_c
    valid_c = (dist_c >= 0) & (dist_c < WINDOW)
    r_n = lax.broadcasted_iota(I32, (sb, rows, steps), 1) // Q_PER_KV
    j_n = lax.broadcasted_iota(I32, (sb, rows, steps), 2)
    dist_n = r_n - j_n
    valid_n = (dist_n >= 0) & (dist_n < WINDOW)
    scale = HEAD_DIM ** -0.5
    for g in range(N_KV_HEADS):
        lanes = slice(g * HEAD_DIM, (g + 1) * HEAD_DIM)
        qg = q_ref[:, g].astype(BF16)
        slope = slope_ref[g][None]
        sink = sink_ref[g][None]
        kc, vc = ck_ref[:, :, lanes].astype(BF16), cv_ref[:, :, lanes].astype(BF16)
        kn, vn = nk_ref[:, :, lanes].astype(BF16), nv_ref[:, :, lanes].astype(BF16)
        s_c = jnp.einsum('nrd,njd->nrj', qg, kc, preferred_element_type=F32) * scale
        s_n = jnp.einsum('nrd,njd->nrj', qg, kn, preferred_element_type=F32) * scale
        s_c = jnp.where(valid_c, s_c - slope * dist_c.astype(F32), NEG_INF)
        s_n = jnp.where(valid_n, s_n - slope * dist_n.astype(F32), NEG_INF)
        m = jnp.maximum(jnp.maximum(jnp.max(s_c, axis=-1, keepdims=True),
                                    jnp.max(s_n, axis=-1, keepdims=True)), sink)
        e_c = jnp.exp(s_c - m)
        e_n = jnp.exp(s_n - m)
        den = (jnp.sum(e_c, axis=-1, keepdims=True) + jnp.sum(e_n, axis=-1, keepdims=True)
               + jnp.exp(sink - m))
        o = (jnp.einsum('nrj,njd->nrd', (e_c / den).astype(BF16), vc, preferred_element_type=F32)
             + jnp.einsum('nrj,njd->nrd', (e_n / den).astype(BF16), vn, preferred_element_type=F32))
        o_ref[:, g] = o


def _attn_sample(qg, cache_k, cache_v, k_new, v_new, slope_rows, sink_rows):
    n, _, rows, _ = qg.shape
    steps = rows // Q_PER_KV
    sb = 8
    blk = lambda shape: pl.BlockSpec((sb,) + shape, lambda i: (i,) + (0,) * len(shape))
    full = lambda shape: pl.BlockSpec(shape, lambda i: (0,) * len(shape))
    return pl.pallas_call(
        functools.partial(_attn_sample_kernel, steps=steps),
        grid=(n // sb,),
        in_specs=[blk((N_KV_HEADS, rows, HEAD_DIM)), blk((WINDOW, KV_DIM)), blk((WINDOW, KV_DIM)),
                  blk((steps, KV_DIM)), blk((steps, KV_DIM)),
                  full((N_KV_HEADS, rows, 1)), full((N_KV_HEADS, rows, 1))],
        out_specs=blk((N_KV_HEADS, rows, HEAD_DIM)),
        out_shape=jax.ShapeDtypeStruct(qg.shape, F32),
        compiler_params=_cparams(("arbitrary",)),
        name="attn_sample",
    )(qg, cache_k, cache_v, k_new, v_new, slope_rows, sink_rows)


def _oproj_kernel(o_ref, x_ref, g1_ref, wo_ref, y_ref):
    y_ref[...] = x_ref[...] + g1_ref[...] * _bdot(o_ref[...], wo_ref[...])


def _oproj(o, x, g1, wo):
    return pl.pallas_call(
        _oproj_kernel,
        out_shape=jax.ShapeDtypeStruct(x.shape, F32),
        compiler_params=_cparams(),
        name="oproj_sample",
    )(o, x, g1, wo)


def kernel(x_prompt, x_sample, state_ssm_re, state_ssm_im, cache_k, cache_v, c_prompt, c_sample, g_mix, g_ffn, w_ada, b_ada, ssm_a_re, ssm_a_im, ssm_log_dt, ssm_b_re, ssm_b_im, ssm_c_re, ssm_c_im, ssm_d, w_glu_a, w_glu_b, g_kv, w_ada_kv, b_ada_kv, w_kv, b_kv, w_q, b_q, w_o, attn_sinks, w_router, b_router, w_gu, b_gu, w_down, b_down, g_final):
    bsz, seq, d = x_prompt.shape
    n_seq, steps, _ = x_sample.shape
    assert d == D_MODEL and seq % S5_CHUNK == 0 and seq % TM == 0 and (n_seq * steps) % TM == 0
    assert g_mix.shape[0] == 2 and ssm_a_re.shape[0] == 1 and w_q.shape[0] == 1
    n_p, n_s = bsz * seq, n_seq * steps
    tiles = _Tiles(n_p, n_s, seq)
    row = lambda v: v.reshape(1, -1)

    c_all = jnp.concatenate([c_prompt, c_sample], axis=0)
    c_rows = -(-c_all.shape[0] // SUBLANES) * SUBLANES
    c_all = jnp.pad(c_all, ((0, c_rows - c_all.shape[0]), (0, 0)))
    mods = [_ada(c_all, w_ada, b_ada, l) for l in range(2)]
    mod_kv = _ada(c_all, w_ada_kv[None], b_ada_kv[None], 0)

    def split(m, n):
        out = []
        for i in range(n):
            col = m[:, i * D_MODEL:(i + 1) * D_MODEL]
            out.append((col[:bsz, None, :], jnp.repeat(col[bsz:bsz + n_seq], steps, axis=0)))
        return out

    (sh1p, sh1s), (sc1p, sc1s), (g1p, g1s), (sh2p, sh2s), (sc2p, sc2s), (g2p, g2s) = split(mods[0], 6)
    abr, abi, bbr, bbi, apr, api = _ssm_prep(ssm_a_re[0], ssm_a_im[0], ssm_log_dt[0], ssm_b_re[0],
                                             ssm_b_im[0], S5_CHUNK // SUBLANES)
    b_re_x, b_im_x = _expand_b(bbr), _expand_b(bbi)
    cexp = (_expand_c(ssm_c_re[0]).astype(BF16), _expand_c(-ssm_c_im[0]).astype(BF16))
    wa, wb = w_glu_a[0].astype(BF16), w_glu_b[0].astype(BF16)
    xp, sre_p, sim_p = _s5_prompt(x_prompt, sh1p, sc1p, g1p, row(g_mix[0]), (abr, abi, apr, api),
                                  (b_re_x.astype(BF16), b_im_x.astype(BF16)), cexp, row(ssm_d[0]), wa, wb)
    per_seq = lambda v: v.reshape(n_seq, steps, D_MODEL)[:, 0, :]
    xs_t, sre_s, sim_s = _s5_sample(
        x_sample.transpose(1, 0, 2), per_seq(sh1s), per_seq(sc1s), per_seq(g1s), row(g_mix[0]),
        state_ssm_re[0].reshape(n_seq, SSM_COLS), state_ssm_im[0].reshape(n_seq, SSM_COLS), abr, abi,
        (_split_bf16(b_re_x), _split_bf16(b_im_x)), cexp, row(ssm_d[0]), wa, wb)
    xp = xp.reshape(n_p, D_MODEL)
    xs = xs_t.transpose(1, 0, 2).reshape(n_s, D_MODEL)
    xp, xs = _moe_layer(tiles, xp, xs, sh2p, sh2s, sc2p, sc2s, g2p, g2s, row(g_ffn[0]), w_router[0],
                        b_router[0], w_gu, b_gu, w_down, b_down, row(g_final), 0, False)

    (sh1p, sh1s), (sc1p, sc1s), (g1p, g1s), (sh2p, sh2s), (sc2p, sc2s), (g2p, g2s) = split(mods[1], 6)
    (shkp, shks), (sckp, scks) = split(mod_kv, 2)
    wkv, wq, wo = w_kv.astype(BF16), w_q[0].astype(BF16), w_o[0].astype(BF16)
    xp3, k_p, v_p = _attn_prompt(xp.reshape(bsz, seq, D_MODEL), shkp, sckp, sh1p, sc1p, g1p, row(g_kv),
                                 row(g_mix[1]), wkv, row(b_kv), wq, row(b_q[0]), wo, attn_sinks[0])
    xp = xp3.reshape(n_p, D_MODEL)

    q_s, k_s, v_s = _qkv_sample(xs, shks, scks, sh1s, sc1s, row(g_kv), row(g_mix[1]), wkv, row(b_kv), wq,
                                row(b_q[0]))
    rows = steps * Q_PER_KV
    qg = q_s.reshape(n_seq, steps, N_KV_HEADS, Q_PER_KV, HEAD_DIM).transpose(0, 2, 1, 3, 4).reshape(
        n_seq, N_KV_HEADS, rows, HEAD_DIM)
    head_of_row = (np.arange(N_KV_HEADS)[:, None] * Q_PER_KV + np.arange(rows)[None, :] % Q_PER_KV)
    slope_rows = jnp.asarray(np.asarray(_slopes(), np.float32)[head_of_row][..., None])
    sink_rows = attn_sinks[0][head_of_row][..., None]
    k_new = k_s.reshape(n_seq, steps, KV_DIM)
    v_new = v_s.reshape(n_seq, steps, KV_DIM)
    ck = cache_k.reshape(n_seq, WINDOW, KV_DIM)
    cv = cache_v.reshape(n_seq, WINDOW, KV_DIM)
    og = _attn_sample(qg, ck, cv, k_new, v_new, slope_rows, sink_rows)
    o_s = og.reshape(n_seq, N_KV_HEADS, steps, Q_PER_KV, HEAD_DIM).transpose(0, 2, 1, 3, 4).reshape(
        n_s, N_HEADS * HEAD_DIM)
    xs = _oproj(o_s, xs, g1s, wo)

    yp, ys = _moe_layer(tiles, xp, xs, sh2p, sh2s, sc2p, sc2s, g2p, g2s, row(g_ffn[1]), w_router[1],
                        b_router[1], w_gu, b_gu, w_down, b_down, row(g_final), 1, True)

    shape4 = lambda a: a.reshape(a.shape[0], a.shape[1], N_KV_HEADS, HEAD_DIM)
    state = lambda a, n: a.reshape(1, n, SSM_GROUPS, SSM_STATE)
    return (yp.reshape(bsz, seq, D_MODEL), ys.reshape(n_seq, steps, D_MODEL),
            state(sre_p, bsz), state(sim_p, bsz), shape4(k_p), shape4(v_p),
            state(sre_s, n_seq), state(sim_s, n_seq),
            shape4(jnp.concatenate([ck[:, steps:], k_new], axis=1)),
            shape4(jnp.concatenate([cv[:, steps:], v_new], axis=1)))
```

```python
import functools
import math

import numpy as np
import jax
import jax.numpy as jnp
from jax import lax
from jax.experimental import pallas as pl
from jax.experimental.pallas import tpu as pltpu

F32 = jnp.float32
BF16 = jnp.bfloat16
I32 = jnp.int32

D_MODEL = 1024
SSM_GROUP = 16
SSM_GROUPS = D_MODEL // SSM_GROUP
SSM_STATE = 64
SSM_COLS = SSM_GROUPS * SSM_STATE
N_HEADS = 16
HEAD_DIM = 64
N_KV_HEADS = 4
Q_PER_KV = N_HEADS // N_KV_HEADS
KV_DIM = N_KV_HEADS * HEAD_DIM
WINDOW = 128
N_EXPERTS = 32
TOP_K = 4
D_FF = D_MODEL
SWIGLU_LIMIT = 7.0
SWIGLU_ALPHA = 1.702
RMS_EPS = 1e-5
NEG_INF = -1e30

LANES = 128
SUBLANES = 8
ROW_TILES = D_MODEL // LANES
TM = 256
S5_CHUNK = 256
S5_COLS = 512
MXU_K = 256
GROUPS_PER_K = MXU_K // SSM_GROUP
N_KT = D_MODEL // MXU_K
VMEM_LIMIT = 56 * 1024 * 1024


def _cparams(sem=None):
    return pltpu.CompilerParams(dimension_semantics=sem, vmem_limit_bytes=VMEM_LIMIT)


def _sigmoid(x):
    return 1.0 / (1.0 + jnp.exp(-x))


def _rms(x, g):
    return x * lax.rsqrt(jnp.mean(x * x, axis=-1, keepdims=True) + RMS_EPS) * g


def _bdot(a, b):
    return jnp.dot(a.astype(BF16), b.astype(BF16), preferred_element_type=F32)


def _gelu_tanh(x):
    return 0.5 * x * (1.0 + jnp.tanh(math.sqrt(2.0 / math.pi) * (x + 0.044715 * (x * x * x))))


def _split_bf16(w):
    hi = w.astype(BF16)
    lo = (w - hi.astype(F32)).astype(BF16)
    return hi, lo


def _dot3(a, b_hi, b_lo):
    a_hi = a.astype(BF16)
    a_lo = (a - a_hi.astype(F32)).astype(BF16)
    return (jnp.dot(a_hi, b_hi, preferred_element_type=F32)
            + jnp.dot(a_lo, b_hi, preferred_element_type=F32)
            + jnp.dot(a_hi, b_lo, preferred_element_type=F32))


def _ada_kernel(c_ref, w_ref, b_ref, o_ref):
    c = c_ref[...]
    o_ref[...] = _bdot(c * _sigmoid(c), w_ref[...]) + b_ref[...]


def _ada(c, w, b, layer):
    rows, cols = c.shape[0], w.shape[2]
    tn = 1024
    return pl.pallas_call(
        _ada_kernel,
        grid=(cols // tn,),
        in_specs=[pl.BlockSpec((rows, D_MODEL), lambda j: (0, 0)),
                  pl.BlockSpec((None, D_MODEL, tn), lambda j: (layer, 0, j)),
                  pl.BlockSpec((None, 1, tn), lambda j: (layer, 0, j))],
        out_specs=pl.BlockSpec((rows, tn), lambda j: (0, j)),
        out_shape=jax.ShapeDtypeStruct((rows, cols), F32),
        compiler_params=_cparams(("arbitrary",)),
        name="ada",
    )(c, w, b.reshape(b.shape[0], 1, cols))


def _ssm_prep_kernel(are_ref, aim_ref, ldt_ref, bre_ref, bim_ref,
                     abr_ref, abi_ref, bbr_ref, bbi_ref, apr_ref, api_ref, *, sub_len):
    a_re, a_im = are_ref[...], aim_ref[...]
    dt = jnp.exp(ldt_ref[...])
    mag = jnp.exp(a_re * dt)
    ab_re = mag * jnp.cos(a_im * dt)
    ab_im = mag * jnp.sin(a_im * dt)
    abr_ref[...] = ab_re
    abi_ref[...] = ab_im
    den = a_re * a_re + a_im * a_im
    x, y = ab_re - 1.0, ab_im
    f_re = (x * a_re + y * a_im) / den
    f_im = (y * a_re - x * a_im) / den
    b_re, b_im = bre_ref[...], bim_ref[...]
    bbr_ref[...] = f_re * b_re - f_im * b_im
    bbi_ref[...] = f_re * b_im + f_im * b_re
    pr, pi = ab_re, ab_im
    for _ in range(int(math.log2(sub_len))):
        pr, pi = pr * pr - pi * pi, 2.0 * pr * pi
    cr, ci = jnp.ones_like(pr), jnp.zeros_like(pi)
    for j in range(SUBLANES + 1):
        apr_ref[j:j + 1, :] = cr
        api_ref[j:j + 1, :] = ci
        cr, ci = cr * pr - ci * pi, cr * pi + ci * pr


def _ssm_prep(a_re, a_im, log_dt, b_re, b_im, sub_len):
    gp = SSM_COLS
    ldt = jnp.broadcast_to(log_dt[:, None], (SSM_GROUPS, SSM_STATE)).reshape(1, gp)
    b_re_t = b_re.reshape(gp, SSM_GROUP).T
    b_im_t = b_im.reshape(gp, SSM_GROUP).T
    row = jax.ShapeDtypeStruct((1, gp), F32)
    mat = jax.ShapeDtypeStruct((SSM_GROUP, gp), F32)
    pw = jax.ShapeDtypeStruct((SUBLANES + 1, gp), F32)
    return pl.pallas_call(
        functools.partial(_ssm_prep_kernel, sub_len=sub_len),
        out_shape=(row, row, mat, mat, pw, pw),
        compiler_params=_cparams(),
        name="ssm_prep",
    )(a_re.reshape(1, gp), a_im.reshape(1, gp), ldt, b_re_t, b_im_t)


def _expand_b(bbar):
    eye = jnp.eye(GROUPS_PER_K, dtype=F32)
    b = bbar.reshape(SSM_GROUP, N_KT, GROUPS_PER_K, SSM_STATE).transpose(1, 0, 2, 3)
    e = b[:, None, :, :, :] * eye[None, :, None, :, None]
    return e.reshape(N_KT, MXU_K, GROUPS_PER_K * SSM_STATE)


def _expand_c(c):
    eye = jnp.eye(GROUPS_PER_K, dtype=F32)
    cc = c.reshape(N_KT, GROUPS_PER_K, SSM_GROUP, SSM_STATE).transpose(0, 1, 3, 2)
    e = cc[:, :, :, None, :] * eye[None, :, None, :, None]
    return e.reshape(N_KT, GROUPS_PER_K * SSM_STATE, MXU_K)


def _cmul(ar, ai, br, bi):
    return ar * br - ai * bi, ar * bi + ai * br


def _s5_prompt_kernel(x_ref, sh_ref, sc_ref, g1_ref, gmix_ref, abr_ref, abi_ref, apr_ref, api_ref,
                      bre_ref, bim_ref, cre_ref, cim_ref, dsk_ref, wa_ref, wb_ref,
                      o_ref, sre_ref, sim_ref,
                      u_ref, hr_ref, hi_ref, hmr_ref, hmi_ref, car_ref, cai_ref, *, L):
    ls = L // SUBLANES
    kcols = GROUPS_PER_K * SSM_STATE

    @pl.when(pl.program_id(1) == 0)
    def _():
        car_ref[...] = jnp.zeros_like(car_ref)
        cai_ref[...] = jnp.zeros_like(cai_ref)

    x = x_ref[...]
    u = _rms(x, gmix_ref[...]) * (1.0 + sc_ref[...]) + sh_ref[...]
    u_ref[...] = u
    r = lax.broadcasted_iota(I32, (L, L), 0)
    c = lax.broadcasted_iota(I32, (L, L), 1)
    perm = jnp.where(c == (r % SUBLANES) * ls + r // SUBLANES, 1.0, 0.0).astype(BF16)
    unperm = jnp.where(r == (c % SUBLANES) * ls + c // SUBLANES, 1.0, 0.0).astype(BF16)
    ub = jnp.dot(perm, u.astype(BF16), preferred_element_type=F32).astype(BF16)
    for kt in range(N_KT):
        uk = ub[:, kt * MXU_K:(kt + 1) * MXU_K]
        hr_ref[:, kt * kcols:(kt + 1) * kcols] = jnp.dot(uk, bre_ref[kt], preferred_element_type=F32)
        hi_ref[:, kt * kcols:(kt + 1) * kcols] = jnp.dot(uk, bim_ref[kt], preferred_element_type=F32)

    unroll = 4

    for cb in range(SSM_COLS // S5_COLS):
        cols = slice(cb * S5_COLS, (cb + 1) * S5_COLS)
        ar = jnp.broadcast_to(abr_ref[:, cols], (SUBLANES, S5_COLS))
        ai = jnp.broadcast_to(abi_ref[:, cols], (SUBLANES, S5_COLS))

        def scan_body(it, carry, cols=cols, ar=ar, ai=ai):
            sr, si = carry
            for k in range(unroll):
                r0 = pl.multiple_of((it * unroll + k) * SUBLANES, SUBLANES)
                tr, ti = _cmul(ar, ai, sr, si)
                sr = tr + hr_ref[pl.ds(r0, SUBLANES), cols]
                si = ti + hi_ref[pl.ds(r0, SUBLANES), cols]
                hr_ref[pl.ds(r0, SUBLANES), cols] = sr
                hi_ref[pl.ds(r0, SUBLANES), cols] = si
            return sr, si

        zero = jnp.zeros((SUBLANES, S5_COLS), F32)
        lax.fori_loop(0, ls // unroll, scan_body, (zero, zero))

    fr = hr_ref[L - SUBLANES:L, :]
    fi = hi_ref[L - SUBLANES:L, :]
    row = lax.broadcasted_iota(I32, (SUBLANES, SSM_COLS), 0)
    gr, gi = fr, fi
    pr, pi = apr_ref[1:2, :], api_ref[1:2, :]
    for s in (1, 2, 4):
        sr = jnp.where(row >= s, pltpu.roll(gr, s, axis=0), 0.0)
        si = jnp.where(row >= s, pltpu.roll(gi, s, axis=0), 0.0)
        tr, ti = _cmul(pr, pi, sr, si)
        gr, gi = gr + tr, gi + ti
        pr, pi = _cmul(pr, pi, pr, pi)
    c0r, c0i = car_ref[...], cai_ref[...]
    tr, ti = _cmul(apr_ref[0:SUBLANES, :], api_ref[0:SUBLANES, :], c0r, c0i)
    hmr = tr + jnp.where(row >= 1, pltpu.roll(gr, 1, axis=0), 0.0)
    hmi = ti + jnp.where(row >= 1, pltpu.roll(gi, 1, axis=0), 0.0)
    hmr_ref[...] = hmr
    hmi_ref[...] = hmi
    tr, ti = _cmul(apr_ref[SUBLANES:SUBLANES + 1, :], api_ref[SUBLANES:SUBLANES + 1, :], c0r, c0i)
    ncr = tr + gr[SUBLANES - 1:SUBLANES, :]
    nci = ti + gi[SUBLANES - 1:SUBLANES, :]
    car_ref[...] = ncr
    cai_ref[...] = nci
    sre_ref[...] = ncr
    sim_ref[...] = nci

    for cb in range(SSM_COLS // S5_COLS):
        cols = slice(cb * S5_COLS, (cb + 1) * S5_COLS)
        ar = jnp.broadcast_to(abr_ref[:, cols], (SUBLANES, S5_COLS))
        ai = jnp.broadcast_to(abi_ref[:, cols], (SUBLANES, S5_COLS))

        def fix_body(it, carry, cols=cols, ar=ar, ai=ai):
            dr, di = carry
            for k in range(unroll):
                r0 = pl.multiple_of((it * unroll + k) * SUBLANES, SUBLANES)
                dr, di = _cmul(ar, ai, dr, di)
                hr_ref[pl.ds(r0, SUBLANES), cols] = hr_ref[pl.ds(r0, SUBLANES), cols] + dr
                hi_ref[pl.ds(r0, SUBLANES), cols] = hi_ref[pl.ds(r0, SUBLANES), cols] + di
            return dr, di

        lax.fori_loop(0, ls // unroll, fix_body, (hmr_ref[:, cols], hmi_ref[:, cols]))

    ys = []
    for nt in range(N_KT):
        hr = hr_ref[:, nt * kcols:(nt + 1) * kcols].astype(BF16)
        hi = hi_ref[:, nt * kcols:(nt + 1) * kcols].astype(BF16)
        ys.append(jnp.dot(hr, cre_ref[nt], preferred_element_type=F32)
                  + jnp.dot(hi, cim_ref[nt], preferred_element_type=F32))
    yp = jnp.concatenate(ys, axis=-1)
    yp_hi = yp.astype(BF16)
    yp_lo = (yp - yp_hi.astype(F32)).astype(BF16)
    y = (jnp.dot(unperm, yp_hi, preferred_element_type=F32)
         + jnp.dot(unperm, yp_lo, preferred_element_type=F32)) + dsk_ref[...] * u_ref[...]
    z = _gelu_tanh(y).astype(BF16)
    mix = jnp.dot(z, wa_ref[...], preferred_element_type=F32) * _sigmoid(
        jnp.dot(z, wb_ref[...], preferred_element_type=F32))
    o_ref[...] = x_ref[...] + g1_ref[...] * mix


def _s5_prompt(x, sh, sc, g1, gmix, prep, bexp, cexp, dsk, wa, wb):
    b, t, _ = x.shape
    L = S5_CHUNK
    abr, abi, apr, api = prep
    bre, bim = bexp
    cre, cim = cexp
    kcols = GROUPS_PER_K * SSM_STATE
    full = lambda shape: pl.BlockSpec(shape, lambda bi, ci: (0,) * len(shape))
    mod = pl.BlockSpec((None, 1, D_MODEL), lambda bi, ci: (bi, 0, 0))
    return pl.pallas_call(
        functools.partial(_s5_prompt_kernel, L=L),
        grid=(b, t // L),
        in_specs=[pl.BlockSpec((None, L, D_MODEL), lambda bi, ci: (bi, ci, 0)), mod, mod, mod,
                  full((1, D_MODEL)), full((1, SSM_COLS)), full((1, SSM_COLS)),
                  full((SUBLANES + 1, SSM_COLS)), full((SUBLANES + 1, SSM_COLS)),
                  full((N_KT, MXU_K, kcols)), full((N_KT, MXU_K, kcols)),
                  full((N_KT, kcols, MXU_K)), full((N_KT, kcols, MXU_K)),
                  full((1, D_MODEL)), full((D_MODEL, D_MODEL)), full((D_MODEL, D_MODEL))],
        out_specs=[pl.BlockSpec((None, L, D_MODEL), lambda bi, ci: (bi, ci, 0)),
                   pl.BlockSpec((None, 1, SSM_COLS), lambda bi, ci: (bi, 0, 0)),
                   pl.BlockSpec((None, 1, SSM_COLS), lambda bi, ci: (bi, 0, 0))],
        out_shape=(jax.ShapeDtypeStruct((b, t, D_MODEL), F32),
                   jax.ShapeDtypeStruct((b, 1, SSM_COLS), F32),
                   jax.ShapeDtypeStruct((b, 1, SSM_COLS), F32)),
        scratch_shapes=[pltpu.VMEM((L, D_MODEL), F32),
                        pltpu.VMEM((L, SSM_COLS), F32), pltpu.VMEM((L, SSM_COLS), F32),
                        pltpu.VMEM((SUBLANES, SSM_COLS), F32), pltpu.VMEM((SUBLANES, SSM_COLS), F32),
                        pltpu.VMEM((1, SSM_COLS), F32), pltpu.VMEM((1, SSM_COLS), F32)],
        compiler_params=_cparams(("arbitrary", "arbitrary")),
        name="s5_prompt",
    )(x, sh, sc, g1, gmix, abr, abi, apr, api, bre, bim, cre, cim, dsk, wa, wb)


def _s5_sample_kernel(x_ref, sh_ref, sc_ref, g1_ref, gmix_ref, h0r_ref, h0i_ref, abr_ref, abi_ref,
                      brh_ref, brl_ref, bih_ref, bil_ref, cre_ref, cim_ref, dsk_ref, wa_ref, wb_ref,
                      o_ref, sre_ref, sim_ref, *, steps):
    kcols = GROUPS_PER_K * SSM_STATE
    ar, ai = abr_ref[...], abi_ref[...]
    sr, si = h0r_ref[...], h0i_ref[...]
    for t in range(steps):
        x = x_ref[t]
        u = _rms(x, gmix_ref[...]) * (1.0 + sc_ref[...]) + sh_ref[...]
        bur, bui = [], []
        for kt in range(N_KT):
            uk = u[:, kt * MXU_K:(kt + 1) * MXU_K]
            bur.append(_dot3(uk, brh_ref[kt], brl_ref[kt]))
            bui.append(_dot3(uk, bih_ref[kt], bil_ref[kt]))
        tr, ti = _cmul(ar, ai, sr, si)
        sr = tr + jnp.concatenate(bur, axis=-1)
        si = ti + jnp.concatenate(bui, axis=-1)
        ys = []
        for nt in range(N_KT):
            ys.append(_bdot(sr[:, nt * kcols:(nt + 1) * kcols], cre_ref[nt])
                      + _bdot(si[:, nt * kcols:(nt + 1) * kcols], cim_ref[nt]))
        y = jnp.concatenate(ys, axis=-1) + dsk_ref[...] * u
        z = _gelu_tanh(y).astype(BF16)
        mix = jnp.dot(z, wa_ref[...], preferred_element_type=F32) * _sigmoid(
            jnp.dot(z, wb_ref[...], preferred_element_type=F32))
        o_ref[t] = x + g1_ref[...] * mix
    sre_ref[...] = sr
    sim_ref[...] = si


def _s5_sample(x_t, sh, sc, g1, gmix, h0r, h0i, abr, abi, bsplit, cexp, dsk, wa, wb):
    steps, n, _ = x_t.shape
    (brh, brl), (bih, bil) = bsplit
    cre, cim = cexp
    return pl.pallas_call(
        functools.partial(_s5_sample_kernel, steps=steps),
        out_shape=(jax.ShapeDtypeStruct((steps, n, D_MODEL), F32),
                   jax.ShapeDtypeStruct((n, SSM_COLS), F32),
                   jax.ShapeDtypeStruct((n, SSM_COLS), F32)),
        compiler_params=_cparams(),
        name="s5_sample",
    )(x_t, sh, sc, g1, gmix, h0r, h0i, abr, abi, brh, brl, bih, bil, cre, cim, dsk, wa, wb)


class _Tiles:
    def __init__(self, n_prompt, n_sample, seq_len):
        self.ntp = n_prompt // TM
        self.nts = n_sample // TM
        self.nt = self.ntp + self.nts
        self.tiles_per_seq = seq_len // TM
        self.n_batch = n_prompt // seq_len
        self.n_tok = n_prompt + n_sample

    def p_rows(self, width=D_MODEL):
        return pl.BlockSpec((TM, width), lambda t, *_: (jnp.minimum(t, self.ntp - 1), 0))

    def s_rows(self, width=D_MODEL):
        return pl.BlockSpec((TM, width), lambda t, *_: (jnp.maximum(t - self.ntp, 0), 0))

    def p_mod(self):
        return pl.BlockSpec((None, 1, D_MODEL),
                            lambda t, *_: (jnp.minimum(t // self.tiles_per_seq, self.n_batch - 1), 0, 0))

    def all_rows(self, width):
        return self.all_rows_of(TM, width)

    def all_rows_of(self, rows, width):
        return pl.BlockSpec((rows, width), lambda t, *_: (t, 0))


def _const_spec(shape):
    return pl.BlockSpec(shape, lambda t, *_: (0,) * len(shape))


def _by_kind(tiles, body, prompt_refs, sample_refs):
    t = pl.program_id(0)

    @pl.when(t < tiles.ntp)
    def _():
        body(*prompt_refs)

    @pl.when(t >= tiles.ntp)
    def _():
        body(*sample_refs)


def _moe_input(x_ref, sh_ref, sc_ref, g_ref):
    return _rms(x_ref[...], g_ref[...]) * (1.0 + sc_ref[...]) + sh_ref[...]


def _router_kernel(xp_ref, xs_ref, shp_ref, shs_ref, scp_ref, scs_ref, g_ref, wh_ref, wl_ref, b_ref, tri_ref,
                   mi_ref, mw_ref, cnt_ref, h_ref, run_ref, *, tiles):
    @pl.when(pl.program_id(0) == 0)
    def _():
        run_ref[...] = jnp.zeros_like(run_ref)

    def route(x_ref, sh_ref, sc_ref):
        h = _moe_input(x_ref, sh_ref, sc_ref, g_ref)
        for c in range(ROW_TILES):
            h_ref[pl.ds(c, TM, stride=ROW_TILES), :] = h[:, c * LANES:(c + 1) * LANES]
        logits = _dot3(h, wh_ref[...], wl_ref[...]) + b_ref[...]
        lane = lax.broadcasted_iota(I32, (TM, LANES), 1)
        vals, firsts, hots = [], [], []
        work = logits
        for _ in range(TOP_K):
            m = jnp.max(work, axis=-1, keepdims=True)
            first = jnp.min(jnp.where(work == m, lane, LANES), axis=-1, keepdims=True)
            hot = lane == first
            vals.append(m)
            firsts.append(first)
            hots.append(hot)
            work = jnp.where(hot, -jnp.inf, work)
        exps = [jnp.exp(v - vals[0]) for v in vals]
        den = exps[0] + exps[1] + exps[2] + exps[3]

        chosen = jnp.zeros((TM, LANES), F32)
        for hot in hots:
            chosen = jnp.where(hot, 1.0, chosen)
        rank_all = jnp.dot(tri_ref[...], chosen.astype(BF16), preferred_element_type=F32) + run_ref[...]
        run_ref[...] = run_ref[...] + jnp.sum(chosen, axis=0, keepdims=True)
        cnt_ref[...] = run_ref[...]

        mi = jnp.zeros((TM, LANES), I32)
        mw = jnp.zeros((TM, LANES), F32)
        for k in range(TOP_K):
            r_k = jnp.sum(jnp.where(hots[k], rank_all, 0.0), axis=-1, keepdims=True).astype(I32)
            mi = jnp.where(lane == k, firsts[k], mi)
            mi = jnp.where(lane == TOP_K + k, r_k, mi)
            mw = jnp.where(lane == k, exps[k] / den, mw)
        mi_ref[...] = mi
        mw_ref[...] = mw

    _by_kind(tiles, route, (xp_ref, shp_ref, scp_ref), (xs_ref, shs_ref, scs_ref))


def _router(tiles, xp, xs, shp, shs, scp, scs, g, w_hi, w_lo, b_pad):
    tri = jnp.asarray(np.tril(np.ones((TM, TM), np.float32), -1), BF16)
    return pl.pallas_call(
        functools.partial(_router_kernel, tiles=tiles),
        grid=(tiles.nt,),
        in_specs=[tiles.p_rows(), tiles.s_rows(), tiles.p_mod(), tiles.s_rows(), tiles.p_mod(),
                  tiles.s_rows(), _const_spec((1, D_MODEL)), _const_spec((D_MODEL, LANES)),
                  _const_spec((D_MODEL, LANES)), _const_spec((1, LANES)), _const_spec((TM, TM))],
        out_specs=[tiles.all_rows(LANES), tiles.all_rows(LANES), _const_spec((1, LANES)),
                   pl.BlockSpec((TM * ROW_TILES, LANES), lambda t: (t, 0))],
        out_shape=(jax.ShapeDtypeStruct((tiles.n_tok, LANES), I32),
                   jax.ShapeDtypeStruct((tiles.n_tok, LANES), F32),
                   jax.ShapeDtypeStruct((1, LANES), F32),
                   jax.ShapeDtypeStruct((tiles.n_tok * ROW_TILES, LANES), F32)),
        scratch_shapes=[pltpu.VMEM((1, LANES), F32)],
        compiler_params=_cparams(("arbitrary",)),
        name="router",
    )(xp, xs, shp, shs, scp, scs, g, w_hi, w_lo, b_pad, tri)


PLAN_UNROLL = 8


def _plan_kernel(pstart_ref, pcnt_ref, nt_ref, slots_hbm, inv_ref, chunk_ref, sem, *, n_pairs, n_slots):
    chunk = chunk_ref.shape[0]
    def fill(s, carry):
        inv_ref[s] = n_pairs + s
        return carry

    def pad_body(e, carry):
        lax.fori_loop(pstart_ref[e], pstart_ref[e] + pcnt_ref[e], fill, 0)
        return carry
    lax.fori_loop(0, N_EXPERTS, pad_body, 0)
    lax.fori_loop(nt_ref[0] * TM, n_slots, fill, 0)

    for ch in range(n_pairs // chunk):
        cp = pltpu.make_async_copy(slots_hbm.at[pl.ds(ch * chunk, chunk)], chunk_ref, sem)
        cp.start()
        cp.wait()

        def body(i, carry, ch=ch):
            for u in range(PLAN_UNROLL):
                j = i * PLAN_UNROLL + u
                inv_ref[chunk_ref[j]] = ch * chunk + j
            return carry
        lax.fori_loop(0, chunk // PLAN_UNROLL, body, 0)


def _plan(n_slots, slots, pad_start, pad_cnt, n_tiles):
    n_pairs = slots.shape[0]
    granule = 1024
    per_chunk = max(d for d in range(1, 9) if (n_pairs // granule) % d == 0)
    assert n_pairs % granule == 0
    grid_spec = pltpu.PrefetchScalarGridSpec(
        num_scalar_prefetch=3,
        grid=(1,),
        in_specs=[pl.BlockSpec(memory_space=pl.ANY)],
        out_specs=pl.BlockSpec(memory_space=pltpu.SMEM),
        scratch_shapes=[pltpu.SMEM((granule * per_chunk,), I32), pltpu.SemaphoreType.DMA(())],
    )
    return pl.pallas_call(
        functools.partial(_plan_kernel, n_pairs=n_pairs, n_slots=n_slots),
        grid_spec=grid_spec,
        out_shape=jax.ShapeDtypeStruct((n_slots,), I32),
        compiler_params=_cparams(("arbitrary",)),
        name="plan",
    )(pad_start, pad_cnt, n_tiles, slots)


ISSUE_UNROLL = 4


def _pair_token(p):
    tm_bits = int(math.log2(TM))
    tile_bits = int(math.log2(TM * TOP_K))
    return ((p >> tile_bits) << tm_bits) + (p & (TM - 1))


def _expert_kernel(te_ref, nt_ref, inv_ref, h_hbm, wgu_ref, bgu_ref, wd_ref, bd_ref, y_hbm,
                   wgu_bf, wd_bf, lhs_ref, gin0, gin1, gout0, gout1, sem_in, sem_out, *, n_tok, n_ids, max_tiles):
    i = pl.program_id(0)
    nt = nt_ref[0]
    gin, gout = (gin0, gin1), (gout0, gout1)

    def row(ref, r):
        if isinstance(r, int):
            return ref.at[r * ROW_TILES:(r + 1) * ROW_TILES, :]
        return ref.at[pl.ds(pl.multiple_of(r * ROW_TILES, ROW_TILES), ROW_TILES), :]

    def gather_one(tile, n, b):
        tok = jnp.minimum(_pair_token(inv_ref[tile * TM + n]), n_tok - 1)
        pltpu.make_async_copy(row(h_hbm, tok), row(gin[b], n), sem_in.at[b]).start()

    def scatter_one(tile, n, b, junk):
        pid = jnp.where(junk, n_ids + n, inv_ref[tile * TM + n])
        pltpu.make_async_copy(row(gout[b], n), row(y_hbm, pid), sem_out.at[b]).start()

    def rolled(fn):
        def body(it, carry):
            for u in range(ISSUE_UNROLL):
                fn(it * ISSUE_UNROLL + u)
            return carry
        lax.fori_loop(0, TM // ISSUE_UNROLL, body, 0)

    def wait_in(b):
        pltpu.make_async_copy(h_hbm.at[pl.ds(0, TM * ROW_TILES), :], gin[b], sem_in.at[b]).wait()

    def wait_out(b):
        pltpu.make_async_copy(gout[b], y_hbm.at[pl.ds(0, TM * ROW_TILES), :], sem_out.at[b]).wait()

    def step(cur):
        nxt = 1 - cur

        @pl.when(i == 0)
        def _():
            gout[nxt][...] = jnp.zeros_like(gout[nxt])
            rolled(lambda n: gather_one(0, n, cur))

        prev = te_ref[jnp.maximum(i - 1, 0)]

        @pl.when((i == 0) | (te_ref[i] != prev))
        def _():
            wgu_bf[...] = wgu_ref[...].astype(BF16)
            wd_bf[...] = wd_ref[...].astype(BF16)

        wait_in(cur)

        @pl.when(i >= 1)
        def _():
            wait_out(cur)

        for c in range(ROW_TILES):
            lhs_ref[:, c * LANES:(c + 1) * LANES] = gin[cur][pl.ds(c, TM, stride=ROW_TILES), :].astype(BF16)
        next_tile = jnp.minimum(i + 1, max_tiles - 1)
        prev_tile = jnp.maximum(i - 1, 0)
        first = i == 0
        for n in range(TM):
            gather_one(next_tile, n, nxt)
            scatter_one(prev_tile, n, nxt, first)
        gu = jnp.dot(lhs_ref[...], wgu_bf[...], preferred_element_type=F32) + bgu_ref[...]
        gate = jnp.minimum(gu[:, :D_FF], SWIGLU_LIMIT)
        up = jnp.clip(gu[:, D_FF:], -SWIGLU_LIMIT, SWIGLU_LIMIT)
        act = ((up + 1.0) * gate * _sigmoid(SWIGLU_ALPHA * gate)).astype(BF16)
        y = jnp.dot(act, wd_bf[...], preferred_element_type=F32) + bd_ref[...]
        for c in range(ROW_TILES):
            gout[cur][pl.ds(c, TM, stride=ROW_TILES), :] = y[:, c * LANES:(c + 1) * LANES]

        @pl.when(i == nt - 1)
        def _():
            rolled(lambda n: scatter_one(i, n, cur, False))
            wait_in(nxt)
            wait_out(nxt)
            wait_out(cur)

    @pl.when((i < nt) & (i % 2 == 0))
    def _():
        step(0)

    @pl.when((i < nt) & (i % 2 == 1))
    def _():
        step(1)


def _experts(max_tiles, n_tok, tile_expert, n_tiles, inv, h_rows, w_gu, b_gu, w_down, b_down, layer):
    tile_rows = TM * ROW_TILES
    w_map = lambda i, te, nt, inv: (layer, te[i], 0, 0)
    grid_spec = pltpu.PrefetchScalarGridSpec(
        num_scalar_prefetch=3,
        grid=(max_tiles,),
        in_specs=[pl.BlockSpec(memory_space=pl.ANY),
                  pl.BlockSpec((None, None, D_MODEL, 2 * D_FF), w_map),
                  pl.BlockSpec((None, None, 1, 2 * D_FF), w_map),
                  pl.BlockSpec((None, None, D_FF, D_MODEL), w_map),
                  pl.BlockSpec((None, None, 1, D_MODEL), w_map)],
        out_specs=pl.BlockSpec(memory_space=pl.ANY),
        scratch_shapes=[pltpu.VMEM((D_MODEL, 2 * D_FF), BF16), pltpu.VMEM((D_FF, D_MODEL), BF16),
                        pltpu.VMEM((TM, D_MODEL), BF16)]
        + [pltpu.VMEM((tile_rows, LANES), F32)] * 4
        + [pltpu.SemaphoreType.DMA((2,)), pltpu.SemaphoreType.DMA((2,))],
    )
    n_ids = n_tok * TOP_K + max_tiles * TM
    return pl.pallas_call(
        functools.partial(_expert_kernel, n_tok=n_tok, n_ids=n_ids, max_tiles=max_tiles),
        grid_spec=grid_spec,
        out_shape=jax.ShapeDtypeStruct(((n_ids + TM) * ROW_TILES, LANES), F32),
        compiler_params=_cparams(("arbitrary",)),
        name="experts",
    )(tile_expert, n_tiles, inv, h_rows, w_gu, b_gu[:, :, None, :], w_down, b_down[:, :, None, :])


def _combine_kernel(y_ref, mw_ref, xp_ref, xs_ref, g2p_ref, g2s_ref, gfin_ref,
                    op_ref, os_ref, acc_ref, *, tiles, final_norm):
    tile_rows = TM * ROW_TILES
    w = mw_ref[...]
    for c in range(ROW_TILES):
        acc = None
        for k in range(TOP_K):
            rows = y_ref[pl.ds(k * tile_rows + c, TM, stride=ROW_TILES), :]
            term = w[:, k:k + 1] * rows
            acc = term if acc is None else acc + term
        acc_ref[:, c * LANES:(c + 1) * LANES] = acc

    def finish(x_ref, g2_ref, o_ref):
        res = x_ref[...] + g2_ref[...] * acc_ref[...]
        if final_norm:
            res = _rms(res, gfin_ref[...])
        o_ref[...] = res

    _by_kind(tiles, finish, (xp_ref, g2p_ref, op_ref), (xs_ref, g2s_ref, os_ref))


def _combine(tiles, y_pairs, mw, xp, xs, g2p, g2s, gfin, final_norm):
    return pl.pallas_call(
        functools.partial(_combine_kernel, tiles=tiles, final_norm=final_norm),
        grid=(tiles.nt,),
        in_specs=[tiles.all_rows_of(TOP_K * TM * ROW_TILES, LANES), tiles.all_rows(LANES), tiles.p_rows(),
                  tiles.s_rows(), tiles.p_mod(), tiles.s_rows(), _const_spec((1, D_MODEL))],
        out_specs=[tiles.p_rows(), tiles.s_rows()],
        out_shape=(jax.ShapeDtypeStruct(xp.shape, F32), jax.ShapeDtypeStruct(xs.shape, F32)),
        scratch_shapes=[pltpu.VMEM((TM, D_MODEL), F32)],
        compiler_params=_cparams(("arbitrary",)),
        name="combine",
    )(y_pairs, mw, xp, xs, g2p, g2s, gfin)


def _moe_layer(tiles, xp, xs, shp, shs, scp, scs, g2p, g2s, g_ffn, w_router, b_router, w_gu, b_gu,
               w_down, b_down, gfin, layer, final_norm):
    n_tok = tiles.n_tok
    w_pad = jnp.pad(w_router, ((0, 0), (0, LANES - N_EXPERTS)))
    w_hi, w_lo = _split_bf16(w_pad)
    b_pad = jnp.pad(b_router, (0, LANES - N_EXPERTS), constant_values=NEG_INF).reshape(1, LANES)
    mi, mw, counts, h_rows = _router(tiles, xp, xs, shp, shs, scp, scs, g_ffn, w_hi, w_lo, b_pad)

    cnt = counts[0, :N_EXPERTS].astype(I32)
    tiles_e = (cnt + TM - 1) // TM
    tile_end = jnp.cumsum(tiles_e)
    tile_start = tile_end - tiles_e
    idx, rank = mi[:, :TOP_K], mi[:, TOP_K:2 * TOP_K]
    offs = jnp.sum(jnp.where(idx[..., None] == jnp.arange(N_EXPERTS, dtype=I32),
                             (tile_start * TM)[None, None, :], 0), axis=-1)
    slots = (offs + rank).reshape(tiles.nt, TM, TOP_K).transpose(0, 2, 1).reshape(n_tok * TOP_K)
    max_tiles = (n_tok * TOP_K) // TM + N_EXPERTS
    tile_expert = jnp.minimum(
        jnp.sum(jnp.arange(max_tiles, dtype=I32)[:, None] >= tile_end[None, :], axis=-1), N_EXPERTS - 1
    ).astype(I32)
    n_tiles = tile_end[-1:].astype(I32)
    pad_start = (tile_start * TM + cnt).astype(I32)
    pad_cnt = (tiles_e * TM - cnt).astype(I32)

    inv = _plan(max_tiles * TM, slots, pad_start, pad_cnt, n_tiles)
    ys = _experts(max_tiles, n_tok, tile_expert, n_tiles, inv, h_rows, w_gu, b_gu, w_down, b_down, layer)
    return _combine(tiles, ys, mw, xp, xs, g2p, g2s, gfin, final_norm)


def _slopes():
    return [2.0 ** (-8.0 * (h + 1) / N_HEADS) for h in range(N_HEADS)]


def _attn_prompt_kernel(sink_ref, x_ref, shkv_ref, sckv_ref, sh_ref, sc_ref, g1_ref, gkv_ref, gmix_ref,
                        wkv_ref, bkv_ref, wq_ref, bq_ref, wo_ref,
                        o_ref, kl_ref, vl_ref, kk_ref, vv_ref, oh_ref):
    j = pl.program_id(1)
    blk = WINDOW

    @pl.when(j == 0)
    def _():
        kk_ref[...] = jnp.zeros_like(kk_ref)
        vv_ref[...] = jnp.zeros_like(vv_ref)

    @pl.when(j > 0)
    def _():
        kk_ref[0:blk, :] = kk_ref[blk:2 * blk, :]
        vv_ref[0:blk, :] = vv_ref[blk:2 * blk, :]

    x = x_ref[...]
    hkv = _rms(x, gkv_ref[...]) * (1.0 + sckv_ref[...]) + shkv_ref[...]
    kv = _bdot(hkv, wkv_ref[...]) + bkv_ref[...]
    kl_ref[...] = kv[:, :KV_DIM]
    vl_ref[...] = kv[:, KV_DIM:]
    kk_ref[blk:2 * blk, :] = kv[:, :KV_DIM].astype(BF16)
    vv_ref[blk:2 * blk, :] = kv[:, KV_DIM:].astype(BF16)

    h = _rms(x, gmix_ref[...]) * (1.0 + sc_ref[...]) + sh_ref[...]
    q = _bdot(h, wq_ref[...]) + bq_ref[...]

    r = lax.broadcasted_iota(I32, (blk, 2 * blk), 0)
    c = lax.broadcasted_iota(I32, (blk, 2 * blk), 1)
    dist_i = r + blk - c
    valid = (dist_i >= 0) & (dist_i < WINDOW) & jnp.logical_not((j == 0) & (c < blk))
    dist = dist_i.astype(F32)
    slopes = _slopes()
    for g in range(N_KV_HEADS):
        kg = kk_ref[:, g * HEAD_DIM:(g + 1) * HEAD_DIM]
        vg = vv_ref[:, g * HEAD_DIM:(g + 1) * HEAD_DIM]
        for qh in range(Q_PER_KV):
            hd = g * Q_PER_KV + qh
            qd = q[:, hd * HEAD_DIM:(hd + 1) * HEAD_DIM].astype(BF16)
            s = lax.dot_general(qd, kg, (((1,), (1,)), ((), ())), preferred_element_type=F32)
            s = s * (HEAD_DIM ** -0.5) - slopes[hd] * dist
            s = jnp.where(valid, s, NEG_INF)
            sink = sink_ref[hd]
            m = jnp.maximum(jnp.max(s, axis=-1, keepdims=True), sink)
            e = jnp.exp(s - m)
            den = jnp.sum(e, axis=-1, keepdims=True) + jnp.exp(sink - m)
            p = (e / den).astype(BF16)
            oh_ref[:, hd * HEAD_DIM:(hd + 1) * HEAD_DIM] = jnp.dot(p, vg, preferred_element_type=F32)
    mix = _bdot(oh_ref[...], wo_ref[...])
    o_ref[...] = x + g1_ref[...] * mix


def _attn_prompt(x, shkv, sckv, sh, sc, g1, gkv, gmix, wkv, bkv, wq, bq, wo, sinks):
    b, t, _ = x.shape
    blk = WINDOW
    qd = N_HEADS * HEAD_DIM
    full = lambda shape: pl.BlockSpec(shape, lambda bi, ji, *_: (0,) * len(shape))
    mod = pl.BlockSpec((None, 1, D_MODEL), lambda bi, ji, *_: (bi, 0, 0))
    rows = pl.BlockSpec((None, blk, D_MODEL), lambda bi, ji, *_: (bi, ji, 0))
    last = pl.BlockSpec((None, blk, KV_DIM), lambda bi, ji, *_: (bi, 0, 0))
    grid_spec = pltpu.PrefetchScalarGridSpec(
        num_scalar_prefetch=1,
        grid=(b, t // blk),
        in_specs=[rows, mod, mod, mod, mod, mod, full((1, D_MODEL)), full((1, D_MODEL)),
                  full((D_MODEL, 2 * KV_DIM)), full((1, 2 * KV_DIM)), full((D_MODEL, qd)), full((1, qd)),
                  full((qd, D_MODEL))],
        out_specs=[rows, last, last],
        scratch_shapes=[pltpu.VMEM((2 * blk, KV_DIM), BF16), pltpu.VMEM((2 * blk, KV_DIM), BF16),
                        pltpu.VMEM((blk, qd), F32)],
    )
    return pl.pallas_call(
        _attn_prompt_kernel,
        grid_spec=grid_spec,
        out_shape=(jax.ShapeDtypeStruct((b, t, D_MODEL), F32),
                   jax.ShapeDtypeStruct((b, blk, KV_DIM), F32),
                   jax.ShapeDtypeStruct((b, blk, KV_DIM), F32)),
        compiler_params=_cparams(("arbitrary", "arbitrary")),
        name="attn_prompt",
    )(sinks, x, shkv, sckv, sh, sc, g1, gkv, gmix, wkv, bkv, wq, bq, wo)


def _qkv_sample_kernel(x_ref, shkv_ref, sckv_ref, sh_ref, sc_ref, gkv_ref, gmix_ref,
                       wkv_ref, bkv_ref, wq_ref, bq_ref, q_ref, k_ref, v_ref):
    x = x_ref[...]
    hkv = _rms(x, gkv_ref[...]) * (1.0 + sckv_ref[...]) + shkv_ref[...]
    kv = _bdot(hkv, wkv_ref[...]) + bkv_ref[...]
    k_ref[...] = kv[:, :KV_DIM]
    v_ref[...] = kv[:, KV_DIM:]
    h = _rms(x, gmix_ref[...]) * (1.0 + sc_ref[...]) + sh_ref[...]
    q_ref[...] = _bdot(h, wq_ref[...]) + bq_ref[...]


def _qkv_sample(x, shkv, sckv, sh, sc, gkv, gmix, wkv, bkv, wq, bq):
    n = x.shape[0]
    return pl.pallas_call(
        _qkv_sample_kernel,
        out_shape=(jax.ShapeDtypeStruct((n, N_HEADS * HEAD_DIM), F32),
                   jax.ShapeDtypeStruct((n, KV_DIM), F32), jax.ShapeDtypeStruct((n, KV_DIM), F32)),
        compiler_params=_cparams(),
        name="qkv_sample",
    )(x, shkv, sckv, sh, sc, gkv, gmix, wkv, bkv, wq, bq)


def _attn_sample_kernel(q_ref, ck_ref, cv_ref, nk_ref, nv_ref, slope_ref, sink_ref, o_ref, *, steps):
    rows = steps * Q_PER_KV
    sb = q_ref.shape[0]
    r_c = lax.broadcasted_iota(I32, (sb, rows, WINDOW), 1) // Q_PER_KV
    j_c = lax.broadcasted_iota(I32, (sb, rows, WINDOW), 2)
    dist_c = r_c + WINDOW - j_c
    valid_c = (dist_c >= 0) & (dist_c < WINDOW)
    r_n = lax.broadcasted_iota(I32, (sb, rows, steps), 1) // Q_PER_KV
    j_n = lax.broadcasted_iota(I32, (sb, rows, steps), 2)
    dist_n = r_n - j_n
    valid_n = (dist_n >= 0) & (dist_n < WINDOW)
    scale = HEAD_DIM ** -0.5
    for g in range(N_KV_HEADS):
        lanes = slice(g * HEAD_DIM, (g + 1) * HEAD_DIM)
        qg = q_ref[:, g].astype(BF16)
        slope = slope_ref[g][None]
        sink = sink_ref[g][None]
        kc, vc = ck_ref[:, :, lanes].astype(BF16), cv_ref[:, :, lanes].astype(BF16)
        kn, vn = nk_ref[:, :, lanes].astype(BF16), nv_ref[:, :, lanes].astype(BF16)
        s_c = jnp.einsum('nrd,njd->nrj', qg, kc, preferred_element_type=F32) * scale
        s_n = jnp.einsum('nrd,njd->nrj', qg, kn, preferred_element_type=F32) * scale
        s_c = jnp.where(valid_c, s_c - slope * dist_c.astype(F32), NEG_INF)
        s_n = jnp.where(valid_n, s_n - slope * dist_n.astype(F32), NEG_INF)
        m = jnp.maximum(jnp.maximum(jnp.max(s_c, axis=-1, keepdims=True),
                                    jnp.max(s_n, axis=-1, keepdims=True)), sink)
        e_c = jnp.exp(s_c - m)
        e_n = jnp.exp(s_n - m)
        den = (jnp.sum(e_c, axis=-1, keepdims=True) + jnp.sum(e_n, axis=-1, keepdims=True)
               + jnp.exp(sink - m))
        o = (jnp.einsum('nrj,njd->nrd', (e_c / den).astype(BF16), vc, preferred_element_type=F32)
             + jnp.einsum('nrj,njd->nrd', (e_n / den).astype(BF16), vn, preferred_element_type=F32))
        o_ref[:, g] = o


def _attn_sample(qg, cache_k, cache_v, k_new, v_new, slope_rows, sink_rows):
    n, _, rows, _ = qg.shape
    steps = rows // Q_PER_KV
    sb = 8
    blk = lambda shape: pl.BlockSpec((sb,) + shape, lambda i: (i,) + (0,) * len(shape))
    full = lambda shape: pl.BlockSpec(shape, lambda i: (0,) * len(shape))
    return pl.pallas_call(
        functools.partial(_attn_sample_kernel, steps=steps),
        grid=(n // sb,),
        in_specs=[blk((N_KV_HEADS, rows, HEAD_DIM)), blk((WINDOW, KV_DIM)), blk((WINDOW, KV_DIM)),
                  blk((steps, KV_DIM)), blk((steps, KV_DIM)),
                  full((N_KV_HEADS, rows, 1)), full((N_KV_HEADS, rows, 1))],
        out_specs=blk((N_KV_HEADS, rows, HEAD_DIM)),
        out_shape=jax.ShapeDtypeStruct(qg.shape, F32),
        compiler_params=_cparams(("arbitrary",)),
        name="attn_sample",
    )(qg, cache_k, cache_v, k_new, v_new, slope_rows, sink_rows)


def _oproj_kernel(o_ref, x_ref, g1_ref, wo_ref, y_ref):
    y_ref[...] = x_ref[...] + g1_ref[...] * _bdot(o_ref[...], wo_ref[...])


def _oproj(o, x, g1, wo):
    return pl.pallas_call(
        _oproj_kernel,
        out_shape=jax.ShapeDtypeStruct(x.shape, F32),
        compiler_params=_cparams(),
        name="oproj_sample",
    )(o, x, g1, wo)


def kernel(x_prompt, x_sample, state_ssm_re, state_ssm_im, cache_k, cache_v, c_prompt, c_sample, g_mix, g_ffn, w_ada, b_ada, ssm_a_re, ssm_a_im, ssm_log_dt, ssm_b_re, ssm_b_im, ssm_c_re, ssm_c_im, ssm_d, w_glu_a, w_glu_b, g_kv, w_ada_kv, b_ada_kv, w_kv, b_kv, w_q, b_q, w_o, attn_sinks, w_router, b_router, w_gu, b_gu, w_down, b_down, g_final):
    bsz, seq, d = x_prompt.shape
    n_seq, steps, _ = x_sample.shape
    assert d == D_MODEL and seq % S5_CHUNK == 0 and seq % TM == 0 and (n_seq * steps) % TM == 0
    assert g_mix.shape[0] == 2 and ssm_a_re.shape[0] == 1 and w_q.shape[0] == 1
    n_p, n_s = bsz * seq, n_seq * steps
    tiles = _Tiles(n_p, n_s, seq)
    row = lambda v: v.reshape(1, -1)

    c_all = jnp.concatenate([c_prompt, c_sample], axis=0)
    c_rows = -(-c_all.shape[0] // SUBLANES) * SUBLANES
    c_all = jnp.pad(c_all, ((0, c_rows - c_all.shape[0]), (0, 0)))
    mods = [_ada(c_all, w_ada, b_ada, l) for l in range(2)]
    mod_kv = _ada(c_all, w_ada_kv[None], b_ada_kv[None], 0)

    def split(m, n):
        out = []
        for i in range(n):
            col = m[:, i * D_MODEL:(i + 1) * D_MODEL]
            out.append((col[:bsz, None, :], jnp.repeat(col[bsz:bsz + n_seq], steps, axis=0)))
        return out

    (sh1p, sh1s), (sc1p, sc1s), (g1p, g1s), (sh2p, sh2s), (sc2p, sc2s), (g2p, g2s) = split(mods[0], 6)
    abr, abi, bbr, bbi, apr, api = _ssm_prep(ssm_a_re[0], ssm_a_im[0], ssm_log_dt[0], ssm_b_re[0],
                                             ssm_b_im[0], S5_CHUNK // SUBLANES)
    b_re_x, b_im_x = _expand_b(bbr), _expand_b(bbi)
    cexp = (_expand_c(ssm_c_re[0]).astype(BF16), _expand_c(-ssm_c_im[0]).astype(BF16))
    wa, wb = w_glu_a[0].astype(BF16), w_glu_b[0].astype(BF16)
    xp, sre_p, sim_p = _s5_prompt(x_prompt, sh1p, sc1p, g1p, row(g_mix[0]), (abr, abi, apr, api),
                                  (b_re_x.astype(BF16), b_im_x.astype(BF16)), cexp, row(ssm_d[0]), wa, wb)
    per_seq = lambda v: v.reshape(n_seq, steps, D_MODEL)[:, 0, :]
    xs_t, sre_s, sim_s = _s5_sample(
        x_sample.transpose(1, 0, 2), per_seq(sh1s), per_seq(sc1s), per_seq(g1s), row(g_mix[0]),
        state_ssm_re[0].reshape(n_seq, SSM_COLS), state_ssm_im[0].reshape(n_seq, SSM_COLS), abr, abi,
        (_split_bf16(b_re_x), _split_bf16(b_im_x)), cexp, row(ssm_d[0]), wa, wb)
    xp = xp.reshape(n_p, D_MODEL)
    xs = xs_t.transpose(1, 0, 2).reshape(n_s, D_MODEL)
    xp, xs = _moe_layer(tiles, xp, xs, sh2p, sh2s, sc2p, sc2s, g2p, g2s, row(g_ffn[0]), w_router[0],
                        b_router[0], w_gu, b_gu, w_down, b_down, row(g_final), 0, False)

    (sh1p, sh1s), (sc1p, sc1s), (g1p, g1s), (sh2p, sh2s), (sc2p, sc2s), (g2p, g2s) = split(mods[1], 6)
    (shkp, shks), (sckp, scks) = split(mod_kv, 2)
    wkv, wq, wo = w_kv.astype(BF16), w_q[0].astype(BF16), w_o[0].astype(BF16)
    xp3, k_p, v_p = _attn_prompt(xp.reshape(bsz, seq, D_MODEL), shkp, sckp, sh1p, sc1p, g1p, row(g_kv),
                                 row(g_mix[1]), wkv, row(b_kv), wq, row(b_q[0]), wo, attn_sinks[0])
    xp = xp3.reshape(n_p, D_MODEL)

    q_s, k_s, v_s = _qkv_sample(xs, shks, scks, sh1s, sc1s, row(g_kv), row(g_mix[1]), wkv, row(b_kv), wq,
                                row(b_q[0]))
    rows = steps * Q_PER_KV
    qg = q_s.reshape(n_seq, steps, N_KV_HEADS, Q_PER_KV, HEAD_DIM).transpose(0, 2, 1, 3, 4).reshape(
        n_seq, N_KV_HEADS, rows, HEAD_DIM)
    head_of_row = (np.arange(N_KV_HEADS)[:, None] * Q_PER_KV + np.arange(rows)[None, :] % Q_PER_KV)
    slope_rows = jnp.asarray(np.asarray(_slopes(), np.float32)[head_of_row][..., None])
    sink_rows = attn_sinks[0][head_of_row][..., None]
    k_new = k_s.reshape(n_seq, steps, KV_DIM)
    v_new = v_s.reshape(n_seq, steps, KV_DIM)
    ck = cache_k.reshape(n_seq, WINDOW, KV_DIM)
    cv = cache_v.reshape(n_seq, WINDOW, KV_DIM)
    og = _attn_sample(qg, ck, cv, k_new, v_new, slope_rows, sink_rows)
    o_s = og.reshape(n_seq, N_KV_HEADS, steps, Q_PER_KV, HEAD_DIM).transpose(0, 2, 1, 3, 4).reshape(
        n_s, N_HEADS * HEAD_DIM)
    xs = _oproj(o_s, xs, g1s, wo)

    yp, ys = _moe_layer(tiles, xp, xs, sh2p, sh2s, sc2p, sc2s, g2p, g2s, row(g_ffn[1]), w_router[1],
                        b_router[1], w_gu, b_gu, w_down, b_down, row(g_final), 1, True)

    shape4 = lambda a: a.reshape(a.shape[0], a.shape[1], N_KV_HEADS, HEAD_DIM)
    state = lambda a, n: a.reshape(1, n, SSM_GROUPS, SSM_STATE)
    return (yp.reshape(bsz, seq, D_MODEL), ys.reshape(n_seq, steps, D_MODEL),
            state(sre_p, bsz), state(sim_p, bsz), shape4(k_p), shape4(v_p),
            state(sre_s, n_seq), state(sim_s, n_seq),
            shape4(jnp.concatenate([ck[:, steps:], k_new], axis=1)),
            shape4(jnp.concatenate([cv[:, steps:], v_new], axis=1)))
```

```python
import functools
import math

import numpy as np
import jax
import jax.numpy as jnp
from jax import lax
from jax.experimental import pallas as pl
from jax.experimental.pallas import tpu as pltpu

F32 = jnp.float32
BF16 = jnp.bfloat16
I32 = jnp.int32

D_MODEL = 1024
SSM_GROUP = 16
SSM_GROUPS = D_MODEL // SSM_GROUP
SSM_STATE = 64
SSM_COLS = SSM_GROUPS * SSM_STATE
N_HEADS = 16
HEAD_DIM = 64
N_KV_HEADS = 4
Q_PER_KV = N_HEADS // N_KV_HEADS
KV_DIM = N_KV_HEADS * HEAD_DIM
WINDOW = 128
N_EXPERTS = 32
TOP_K = 4
D_FF = D_MODEL
SWIGLU_LIMIT = 7.0
SWIGLU_ALPHA = 1.702
RMS_EPS = 1e-5
NEG_INF = -1e30

LANES = 128
SUBLANES = 8
ROW_TILES = D_MODEL // LANES
TM = 256
S5_CHUNK = 256
S5_COLS = 512
MXU_K = 256
GROUPS_PER_K = MXU_K // SSM_GROUP
N_KT = D_MODEL // MXU_K
VMEM_LIMIT = 56 * 1024 * 1024


def _cparams(sem=None):
    return pltpu.CompilerParams(dimension_semantics=sem, vmem_limit_bytes=VMEM_LIMIT)


def _sigmoid(x):
    return 1.0 / (1.0 + jnp.exp(-x))


def _rms(x, g):
    return x * lax.rsqrt(jnp.mean(x * x, axis=-1, keepdims=True) + RMS_EPS) * g


def _bdot(a, b):
    return jnp.dot(a.astype(BF16), b.astype(BF16), preferred_element_type=F32)


def _gelu_tanh(x):
    return 0.5 * x * (1.0 + jnp.tanh(math.sqrt(2.0 / math.pi) * (x + 0.044715 * (x * x * x))))


def _split_bf16(w):
    hi = w.astype(BF16)
    lo = (w - hi.astype(F32)).astype(BF16)
    return hi, lo


def _dot3(a, b_hi, b_lo):
    a_hi = a.astype(BF16)
    a_lo = (a - a_hi.astype(F32)).astype(BF16)
    return (jnp.dot(a_hi, b_hi, preferred_element_type=F32)
            + jnp.dot(a_lo, b_hi, preferred_element_type=F32)
            + jnp.dot(a_hi, b_lo, preferred_element_type=F32))


def _ada_kernel(c_ref, w_ref, b_ref, o_ref):
    c = c_ref[...]
    o_ref[...] = _bdot(c * _sigmoid(c), w_ref[...]) + b_ref[...]


def _ada(c, w, b, layer):
    rows, cols = c.shape[0], w.shape[2]
    tn = 1024
    return pl.pallas_call(
        _ada_kernel,
        grid=(cols // tn,),
        in_specs=[pl.BlockSpec((rows, D_MODEL), lambda j: (0, 0)),
                  pl.BlockSpec((None, D_MODEL, tn), lambda j: (layer, 0, j)),
                  pl.BlockSpec((None, 1, tn), lambda j: (layer, 0, j))],
        out_specs=pl.BlockSpec((rows, tn), lambda j: (0, j)),
        out_shape=jax.ShapeDtypeStruct((rows, cols), F32),
        compiler_params=_cparams(("arbitrary",)),
        name="ada",
    )(c, w, b.reshape(b.shape[0], 1, cols))


def _ssm_prep_kernel(are_ref, aim_ref, ldt_ref, bre_ref, bim_ref,
                     abr_ref, abi_ref, bbr_ref, bbi_ref, apr_ref, api_ref, *, sub_len):
    a_re, a_im = are_ref[...], aim_ref[...]
    dt = jnp.exp(ldt_ref[...])
    mag = jnp.exp(a_re * dt)
    ab_re = mag * jnp.cos(a_im * dt)
    ab_im = mag * jnp.sin(a_im * dt)
    abr_ref[...] = ab_re
    abi_ref[...] = ab_im
    den = a_re * a_re + a_im * a_im
    x, y = ab_re - 1.0, ab_im
    f_re = (x * a_re + y * a_im) / den
    f_im = (y * a_re - x * a_im) / den
    b_re, b_im = bre_ref[...], bim_ref[...]
    bbr_ref[...] = f_re * b_re - f_im * b_im
    bbi_ref[...] = f_re * b_im + f_im * b_re
    pr, pi = ab_re, ab_im
    for _ in range(int(math.log2(sub_len))):
        pr, pi = pr * pr - pi * pi, 2.0 * pr * pi
    cr, ci = jnp.ones_like(pr), jnp.zeros_like(pi)
    for j in range(SUBLANES + 1):
        apr_ref[j:j + 1, :] = cr
        api_ref[j:j + 1, :] = ci
        cr, ci = cr * pr - ci * pi, cr * pi + ci * pr


def _ssm_prep(a_re, a_im, log_dt, b_re, b_im, sub_len):
    gp = SSM_COLS
    ldt = jnp.broadcast_to(log_dt[:, None], (SSM_GROUPS, SSM_STATE)).reshape(1, gp)
    b_re_t = b_re.reshape(gp, SSM_GROUP).T
    b_im_t = b_im.reshape(gp, SSM_GROUP).T
    row = jax.ShapeDtypeStruct((1, gp), F32)
    mat = jax.ShapeDtypeStruct((SSM_GROUP, gp), F32)
    pw = jax.ShapeDtypeStruct((SUBLANES + 1, gp), F32)
    return pl.pallas_call(
        functools.partial(_ssm_prep_kernel, sub_len=sub_len),
        out_shape=(row, row, mat, mat, pw, pw),
        compiler_params=_cparams(),
        name="ssm_prep",
    )(a_re.reshape(1, gp), a_im.reshape(1, gp), ldt, b_re_t, b_im_t)


def _expand_b(bbar):
    eye = jnp.eye(GROUPS_PER_K, dtype=F32)
    b = bbar.reshape(SSM_GROUP, N_KT, GROUPS_PER_K, SSM_STATE).transpose(1, 0, 2, 3)
    e = b[:, None, :, :, :] * eye[None, :, None, :, None]
    return e.reshape(N_KT, MXU_K, GROUPS_PER_K * SSM_STATE)


def _expand_c(c):
    eye = jnp.eye(GROUPS_PER_K, dtype=F32)
    cc = c.reshape(N_KT, GROUPS_PER_K, SSM_GROUP, SSM_STATE).transpose(0, 1, 3, 2)
    e = cc[:, :, :, None, :] * eye[None, :, None, :, None]
    return e.reshape(N_KT, GROUPS_PER_K * SSM_STATE, MXU_K)


def _cmul(ar, ai, br, bi):
    return ar * br - ai * bi, ar * bi + ai * br


def _s5_prompt_kernel(x_ref, sh_ref, sc_ref, g1_ref, gmix_ref, abr_ref, abi_ref, apr_ref, api_ref,
                      bre_ref, bim_ref, cre_ref, cim_ref, dsk_ref, wa_ref, wb_ref,
                      o_ref, sre_ref, sim_ref,
                      u_ref, hr_ref, hi_ref, hmr_ref, hmi_ref, car_ref, cai_ref, *, L):
    ls = L // SUBLANES
    kcols = GROUPS_PER_K * SSM_STATE

    @pl.when(pl.program_id(1) == 0)
    def _():
        car_ref[...] = jnp.zeros_like(car_ref)
        cai_ref[...] = jnp.zeros_like(cai_ref)

    x = x_ref[...]
    u = _rms(x, gmix_ref[...]) * (1.0 + sc_ref[...]) + sh_ref[...]
    u_ref[...] = u
    r = lax.broadcasted_iota(I32, (L, L), 0)
    c = lax.broadcasted_iota(I32, (L, L), 1)
    perm = jnp.where(c == (r % SUBLANES) * ls + r // SUBLANES, 1.0, 0.0).astype(BF16)
    unperm = jnp.where(r == (c % SUBLANES) * ls + c // SUBLANES, 1.0, 0.0).astype(BF16)
    ub = jnp.dot(perm, u.astype(BF16), preferred_element_type=F32).astype(BF16)
    for kt in range(N_KT):
        uk = ub[:, kt * MXU_K:(kt + 1) * MXU_K]
        hr_ref[:, kt * kcols:(kt + 1) * kcols] = jnp.dot(uk, bre_ref[kt], preferred_element_type=F32)
        hi_ref[:, kt * kcols:(kt + 1) * kcols] = jnp.dot(uk, bim_ref[kt], preferred_element_type=F32)

    unroll = 4

    for cb in range(SSM_COLS // S5_COLS):
        cols = slice(cb * S5_COLS, (cb + 1) * S5_COLS)
        ar = jnp.broadcast_to(abr_ref[:, cols], (SUBLANES, S5_COLS))
        ai = jnp.broadcast_to(abi_ref[:, cols], (SUBLANES, S5_COLS))

        def scan_body(it, carry, cols=cols, ar=ar, ai=ai):
            sr, si = carry
            for k in range(unroll):
                r0 = pl.multiple_of((it * unroll + k) * SUBLANES, SUBLANES)
                tr, ti = _cmul(ar, ai, sr, si)
                sr = tr + hr_ref[pl.ds(r0, SUBLANES), cols]
                si = ti + hi_ref[pl.ds(r0, SUBLANES), cols]
                hr_ref[pl.ds(r0, SUBLANES), cols] = sr
                hi_ref[pl.ds(r0, SUBLANES), cols] = si
            return sr, si

        zero = jnp.zeros((SUBLANES, S5_COLS), F32)
        lax.fori_loop(0, ls // unroll, scan_body, (zero, zero))

    fr = hr_ref[L - SUBLANES:L, :]
    fi = hi_ref[L - SUBLANES:L, :]
    row = lax.broadcasted_iota(I32, (SUBLANES, SSM_COLS), 0)
    gr, gi = fr, fi
    pr, pi = apr_ref[1:2, :], api_ref[1:2, :]
    for s in (1, 2, 4):
        sr = jnp.where(row >= s, pltpu.roll(gr, s, axis=0), 0.0)
        si = jnp.where(row >= s, pltpu.roll(gi, s, axis=0), 0.0)
        tr, ti = _cmul(pr, pi, sr, si)
        gr, gi = gr + tr, gi + ti
        pr, pi = _cmul(pr, pi, pr, pi)
    c0r, c0i = car_ref[...], cai_ref[...]
    tr, ti = _cmul(apr_ref[0:SUBLANES, :], api_ref[0:SUBLANES, :], c0r, c0i)
    hmr = tr + jnp.where(row >= 1, pltpu.roll(gr, 1, axis=0), 0.0)
    hmi = ti + jnp.where(row >= 1, pltpu.roll(gi, 1, axis=0), 0.0)
    hmr_ref[...] = hmr
    hmi_ref[...] = hmi
    tr, ti = _cmul(apr_ref[SUBLANES:SUBLANES + 1, :], api_ref[SUBLANES:SUBLANES + 1, :], c0r, c0i)
    ncr = tr + gr[SUBLANES - 1:SUBLANES, :]
    nci = ti + gi[SUBLANES - 1:SUBLANES, :]
    car_ref[...] = ncr
    cai_ref[...] = nci
    sre_ref[...] = ncr
    sim_ref[...] = nci

    for cb in range(SSM_COLS // S5_COLS):
        cols = slice(cb * S5_COLS, (cb + 1) * S5_COLS)
        ar = jnp.broadcast_to(abr_ref[:, cols], (SUBLANES, S5_COLS))
        ai = jnp.broadcast_to(abi_ref[:, cols], (SUBLANES, S5_COLS))

        def fix_body(it, carry, cols=cols, ar=ar, ai=ai):
            dr, di = carry
            for k in range(unroll):
                r0 = pl.multiple_of((it * unroll + k) * SUBLANES, SUBLANES)
                dr, di = _cmul(ar, ai, dr, di)
                hr_ref[pl.ds(r0, SUBLANES), cols] = hr_ref[pl.ds(r0, SUBLANES), cols] + dr
                hi_ref[pl.ds(r0, SUBLANES), cols] = hi_ref[pl.ds(r0, SUBLANES), cols] + di
            return dr, di

        lax.fori_loop(0, ls // unroll, fix_body, (hmr_ref[:, cols], hmi_ref[:, cols]))

    ys = []
    for nt in range(N_KT):
        hr = hr_ref[:, nt * kcols:(nt + 1) * kcols].astype(BF16)
        hi = hi_ref[:, nt * kcols:(nt + 1) * kcols].astype(BF16)
        ys.append(jnp.dot(hr, cre_ref[nt], preferred_element_type=F32)
                  + jnp.dot(hi, cim_ref[nt], preferred_element_type=F32))
    yp = jnp.concatenate(ys, axis=-1)
    yp_hi = yp.astype(BF16)
    yp_lo = (yp - yp_hi.astype(F32)).astype(BF16)
    y = (jnp.dot(unperm, yp_hi, preferred_element_type=F32)
         + jnp.dot(unperm, yp_lo, preferred_element_type=F32)) + dsk_ref[...] * u_ref[...]
    z = _gelu_tanh(y).astype(BF16)
    mix = jnp.dot(z, wa_ref[...], preferred_element_type=F32) * _sigmoid(
        jnp.dot(z, wb_ref[...], preferred_element_type=F32))
    o_ref[...] = x_ref[...] + g1_ref[...] * mix


SH1, SC1, G1 = 0, 1, 2


def _s5_prompt(x, mod, gmix, prep, bexp, cexp, dsk, wa, wb):
    b, t, _ = x.shape
    L = S5_CHUNK
    abr, abi, apr, api = prep
    bre, bim = bexp
    cre, cim = cexp
    kcols = GROUPS_PER_K * SSM_STATE
    full = lambda shape: pl.BlockSpec(shape, lambda bi, ci: (0,) * len(shape))
    mods = [pl.BlockSpec((None, 1, D_MODEL), lambda bi, ci, k=k: (bi, 0, k)) for k in (SH1, SC1, G1)]
    return pl.pallas_call(
        functools.partial(_s5_prompt_kernel, L=L),
        grid=(b, t // L),
        in_specs=[pl.BlockSpec((None, L, D_MODEL), lambda bi, ci: (bi, ci, 0))] + mods + [
                  full((1, D_MODEL)), full((1, SSM_COLS)), full((1, SSM_COLS)),
                  full((SUBLANES + 1, SSM_COLS)), full((SUBLANES + 1, SSM_COLS)),
                  full((N_KT, MXU_K, kcols)), full((N_KT, MXU_K, kcols)),
                  full((N_KT, kcols, MXU_K)), full((N_KT, kcols, MXU_K)),
                  full((1, D_MODEL)), full((D_MODEL, D_MODEL)), full((D_MODEL, D_MODEL))],
        out_specs=[pl.BlockSpec((None, L, D_MODEL), lambda bi, ci: (bi, ci, 0)),
                   pl.BlockSpec((None, 1, SSM_COLS), lambda bi, ci: (bi, 0, 0)),
                   pl.BlockSpec((None, 1, SSM_COLS), lambda bi, ci: (bi, 0, 0))],
        out_shape=(jax.ShapeDtypeStruct((b, t, D_MODEL), F32),
                   jax.ShapeDtypeStruct((b, 1, SSM_COLS), F32),
                   jax.ShapeDtypeStruct((b, 1, SSM_COLS), F32)),
        scratch_shapes=[pltpu.VMEM((L, D_MODEL), F32),
                        pltpu.VMEM((L, SSM_COLS), F32), pltpu.VMEM((L, SSM_COLS), F32),
                        pltpu.VMEM((SUBLANES, SSM_COLS), F32), pltpu.VMEM((SUBLANES, SSM_COLS), F32),
                        pltpu.VMEM((1, SSM_COLS), F32), pltpu.VMEM((1, SSM_COLS), F32)],
        compiler_params=_cparams(("arbitrary", "arbitrary")),
        name="s5_prompt",
    )(x, mod, mod, mod, gmix, abr, abi, apr, api, bre, bim, cre, cim, dsk, wa, wb)


def _mod_cols(m_ref, k):
    return m_ref[:, k * D_MODEL:(k + 1) * D_MODEL]


def _s5_sample_kernel(x_ref, m_ref, gmix_ref, h0r_ref, h0i_ref, abr_ref, abi_ref,
                      brh_ref, brl_ref, bih_ref, bil_ref, cre_ref, cim_ref, dsk_ref, wa_ref, wb_ref,
                      o_ref, sre_ref, sim_ref, *, steps):
    kcols = GROUPS_PER_K * SSM_STATE
    ar, ai = abr_ref[...], abi_ref[...]
    sr, si = h0r_ref[...], h0i_ref[...]
    sh, sc, g1 = _mod_cols(m_ref, SH1), _mod_cols(m_ref, SC1), _mod_cols(m_ref, G1)
    for t in range(steps):
        x = x_ref[t]
        u = _rms(x, gmix_ref[...]) * (1.0 + sc) + sh
        bur, bui = [], []
        for kt in range(N_KT):
            uk = u[:, kt * MXU_K:(kt + 1) * MXU_K]
            bur.append(_dot3(uk, brh_ref[kt], brl_ref[kt]))
            bui.append(_dot3(uk, bih_ref[kt], bil_ref[kt]))
        tr, ti = _cmul(ar, ai, sr, si)
        sr = tr + jnp.concatenate(bur, axis=-1)
        si = ti + jnp.concatenate(bui, axis=-1)
        ys = []
        for nt in range(N_KT):
            ys.append(_bdot(sr[:, nt * kcols:(nt + 1) * kcols], cre_ref[nt])
                      + _bdot(si[:, nt * kcols:(nt + 1) * kcols], cim_ref[nt]))
        y = jnp.concatenate(ys, axis=-1) + dsk_ref[...] * u
        z = _gelu_tanh(y).astype(BF16)
        mix = jnp.dot(z, wa_ref[...], preferred_element_type=F32) * _sigmoid(
            jnp.dot(z, wb_ref[...], preferred_element_type=F32))
        o_ref[t] = x + g1 * mix
    sre_ref[...] = sr
    sim_ref[...] = si


def _s5_sample(x_t, mod, gmix, h0r, h0i, abr, abi, bsplit, cexp, dsk, wa, wb):
    steps, n, _ = x_t.shape
    (brh, brl), (bih, bil) = bsplit
    cre, cim = cexp
    return pl.pallas_call(
        functools.partial(_s5_sample_kernel, steps=steps),
        out_shape=(jax.ShapeDtypeStruct((steps, n, D_MODEL), F32),
                   jax.ShapeDtypeStruct((n, SSM_COLS), F32),
                   jax.ShapeDtypeStruct((n, SSM_COLS), F32)),
        compiler_params=_cparams(),
        name="s5_sample",
    )(x_t, mod, gmix, h0r, h0i, abr, abi, brh, brl, bih, bil, cre, cim, dsk, wa, wb)


class _Tiles:
    def __init__(self, n_prompt, n_sample, seq_len):
        self.ntp = n_prompt // TM
        self.nts = n_sample // TM
        self.nt = self.ntp + self.nts
        self.tiles_per_seq = seq_len // TM
        self.n_batch = n_prompt // seq_len
        self.n_tok = n_prompt + n_sample

    def p_rows(self):
        return pl.BlockSpec((TM, D_MODEL), lambda t, *_: (jnp.minimum(t, self.ntp - 1), 0))

    def s_rows(self, chunk=0):
        return pl.BlockSpec((TM, D_MODEL), lambda t, *_: (jnp.maximum(t - self.ntp, 0), chunk))

    def p_mod(self, chunk):
        return pl.BlockSpec((None, 1, D_MODEL),
                            lambda t, *_: (jnp.minimum(t // self.tiles_per_seq, self.n_batch - 1), 0, chunk))

    def all_rows(self, width):
        return pl.BlockSpec((TM, width), lambda t, *_: (t, 0))


def _const_spec(shape):
    return pl.BlockSpec(shape, lambda t, *_: (0,) * len(shape))


def _by_kind(tiles, body, prompt_refs, sample_refs):
    t = pl.program_id(0)

    @pl.when(t < tiles.ntp)
    def _():
        body(*prompt_refs)

    @pl.when(t >= tiles.ntp)
    def _():
        body(*sample_refs)


def _moe_input(x_ref, sh_ref, sc_ref, g_ref):
    return _rms(x_ref[...], g_ref[...]) * (1.0 + sc_ref[...]) + sh_ref[...]


def _router_kernel(xp_ref, xs_ref, shp_ref, shs_ref, scp_ref, scs_ref, g_ref, wh_ref, wl_ref, b_ref, tri_ref,
                   mi_ref, mw_ref, cnt_ref, run_ref, *, tiles):
    @pl.when(pl.program_id(0) == 0)
    def _():
        run_ref[...] = jnp.zeros_like(run_ref)

    def route(x_ref, sh_ref, sc_ref):
        h = _moe_input(x_ref, sh_ref, sc_ref, g_ref)
        logits = _dot3(h, wh_ref[...], wl_ref[...]) + b_ref[...]
        lane = lax.broadcasted_iota(I32, (TM, LANES), 1)
        lane_f = lane.astype(F32)
        vals, firsts, hots = [], [], []
        work = logits
        for _ in range(TOP_K):
            m = jnp.max(work, axis=-1, keepdims=True)
            first = jnp.min(jnp.where(work == m, lane_f, float(LANES)), axis=-1, keepdims=True)
            hot = lane_f == first
            vals.append(m)
            firsts.append(first.astype(I32))
            hots.append(hot)
            work = jnp.where(hot, -jnp.inf, work)
        exps = [jnp.exp(v - vals[0]) for v in vals]
        den = exps[0] + exps[1] + exps[2] + exps[3]

        chosen = jnp.zeros((TM, LANES), F32)
        for hot in hots:
            chosen = jnp.where(hot, 1.0, chosen)
        rank_all = jnp.dot(tri_ref[...], chosen.astype(BF16), preferred_element_type=F32) + run_ref[...]
        run_ref[...] = run_ref[...] + jnp.sum(chosen, axis=0, keepdims=True)
        cnt_ref[...] = run_ref[...]

        mi = jnp.zeros((TM, LANES), I32)
        mw = jnp.zeros((TM, LANES), F32)
        for k in range(TOP_K):
            r_k = jnp.sum(jnp.where(hots[k], rank_all, 0.0), axis=-1, keepdims=True).astype(I32)
            mi = jnp.where(lane == k, r_k * N_EXPERTS + firsts[k], mi)
            mw = jnp.where(lane == k, exps[k] / den, mw)
        mi_ref[...] = mi
        mw_ref[...] = mw

    _by_kind(tiles, route, (xp_ref, shp_ref, scp_ref), (xs_ref, shs_ref, scs_ref))


SH2, SC2, G2 = 3, 4, 5


def _router(tiles, xp, xs, mod_p, mod_s, g, w_hi, w_lo, b_pad):
    tri = jnp.asarray(np.tril(np.ones((TM, TM), np.float32), -1), BF16)
    return pl.pallas_call(
        functools.partial(_router_kernel, tiles=tiles),
        grid=(tiles.nt,),
        in_specs=[tiles.p_rows(), tiles.s_rows(), tiles.p_mod(SH2), tiles.s_rows(SH2), tiles.p_mod(SC2),
                  tiles.s_rows(SC2), _const_spec((1, D_MODEL)), _const_spec((D_MODEL, LANES)),
                  _const_spec((D_MODEL, LANES)), _const_spec((1, LANES)), _const_spec((TM, TM))],
        out_specs=[tiles.all_rows(LANES), tiles.all_rows(LANES), _const_spec((1, LANES))],
        out_shape=(jax.ShapeDtypeStruct((tiles.n_tok, LANES), I32),
                   jax.ShapeDtypeStruct((tiles.n_tok, LANES), F32),
                   jax.ShapeDtypeStruct((1, LANES), F32)),
        scratch_shapes=[pltpu.VMEM((1, LANES), F32)],
        compiler_params=_cparams(("arbitrary",)),
        name="router",
    )(xp, xs, mod_p, mod_s, mod_p, mod_s, g, w_hi, w_lo, b_pad, tri)


PAD_ARMS = tuple(1 << i for i in reversed(range(int(math.log2(TM)))))


def _slot(code_ref, off_ref, pair):
    code = code_ref[pair]
    return off_ref[code & (N_EXPERTS - 1)] + (code >> int(math.log2(N_EXPERTS)))


def _dispatch_kernel(code_ref, off_ref, pstart_ref, pcnt_ref,
                     xp_ref, xs_ref, shp_ref, shs_ref, scp_ref, scs_ref, g_ref,
                     o_hbm, rows_ref, zero_ref, sems, zsem, *, tiles):
    t = pl.program_id(0)
    buf = t % 2
    tile_rows = TM * ROW_TILES
    base = pl.multiple_of(buf * tile_rows, tile_rows)

    def tile_wait(b):
        b0 = pl.multiple_of(b * tile_rows, tile_rows)
        for _ in range(TOP_K):
            pltpu.make_async_copy(rows_ref.at[pl.ds(b0, tile_rows), :],
                                  o_hbm.at[pl.ds(0, tile_rows), :], sems.at[b]).wait()

    def pad_copies(act):
        def body(e, carry):
            start, cnt = pstart_ref[e], pcnt_ref[e]
            for p in PAD_ARMS:
                @pl.when((cnt & p) != 0)
                def _(start=start, p=p):
                    cp = pltpu.make_async_copy(
                        zero_ref.at[pl.ds(0, p * ROW_TILES), :],
                        o_hbm.at[pl.ds(pl.multiple_of(start * ROW_TILES, ROW_TILES), p * ROW_TILES), :],
                        zsem)
                    cp.start() if act == "start" else cp.wait()
                start = start + jnp.where((cnt & p) != 0, p, 0)
            return carry
        lax.fori_loop(0, N_EXPERTS, body, 0)

    @pl.when(t == 0)
    def _():
        zero_ref[...] = jnp.zeros_like(zero_ref)
        pad_copies("start")

    @pl.when(t >= 2)
    def _():
        tile_wait(buf)

    def stage(x_ref, sh_ref, sc_ref):
        h = _moe_input(x_ref, sh_ref, sc_ref, g_ref)
        for c in range(ROW_TILES):
            rows_ref[pl.ds(base + c, TM, stride=ROW_TILES), :] = h[:, c * LANES:(c + 1) * LANES]

    _by_kind(tiles, stage, (xp_ref, shp_ref, scp_ref), (xs_ref, shs_ref, scs_ref))

    def issue(n, carry):
        src = rows_ref.at[pl.ds(pl.multiple_of(base + n * ROW_TILES, ROW_TILES), ROW_TILES), :]
        for k in range(TOP_K):
            slot = _slot(code_ref, off_ref, (t * TM + n) * TOP_K + k)
            dst = o_hbm.at[pl.ds(pl.multiple_of(slot * ROW_TILES, ROW_TILES), ROW_TILES), :]
            pltpu.make_async_copy(src, dst, sems.at[buf]).start()
        return carry
    lax.fori_loop(0, TM, issue, 0)

    @pl.when(t == tiles.nt - 1)
    def _():
        if tiles.nt >= 2:
            tile_wait(1 - buf)
        tile_wait(buf)
        pad_copies("wait")


def _dispatch(tiles, n_slots, codes, offs, pad_start, pad_cnt, xp, xs, mod_p, mod_s, g):
    grid_spec = pltpu.PrefetchScalarGridSpec(
        num_scalar_prefetch=4,
        grid=(tiles.nt,),
        in_specs=[tiles.p_rows(), tiles.s_rows(), tiles.p_mod(SH2), tiles.s_rows(SH2), tiles.p_mod(SC2),
                  tiles.s_rows(SC2), _const_spec((1, D_MODEL))],
        out_specs=pl.BlockSpec(memory_space=pl.ANY),
        scratch_shapes=[pltpu.VMEM((2 * TM * ROW_TILES, LANES), F32),
                        pltpu.VMEM((PAD_ARMS[0] * ROW_TILES, LANES), F32),
                        pltpu.SemaphoreType.DMA((2,)), pltpu.SemaphoreType.DMA(())],
    )
    return pl.pallas_call(
        functools.partial(_dispatch_kernel, tiles=tiles),
        grid_spec=grid_spec,
        out_shape=jax.ShapeDtypeStruct((n_slots * ROW_TILES, LANES), F32),
        compiler_params=_cparams(("arbitrary",)),
        name="dispatch",
    )(codes, offs, pad_start, pad_cnt, xp, xs, mod_p, mod_s, mod_p, mod_s, g)


def _expert_kernel(te_ref, nx_ref, nt_ref, hs_ref, wgu_hbm, bgu_ref, wd_hbm, bd_ref, o_ref,
                   wgu_f32, wd_f32, wgu_bf, wd_bf, lhs_ref, wsem, *, layer):
    i = pl.program_id(0)

    def fetch(expert):
        return (pltpu.make_async_copy(wgu_hbm.at[layer, expert], wgu_f32, wsem.at[0]),
                pltpu.make_async_copy(wd_hbm.at[layer, expert], wd_f32, wsem.at[1]))

    @pl.when(i < nt_ref[0])
    def _():
        expert = te_ref[i]

        @pl.when(i == 0)
        def _():
            for cp in fetch(expert):
                cp.start()

        @pl.when((i == 0) | (expert != te_ref[jnp.maximum(i - 1, 0)]))
        def _():
            for cp in fetch(expert):
                cp.wait()
            wgu_bf[...] = wgu_f32[...].astype(BF16)
            wd_bf[...] = wd_f32[...].astype(BF16)
            upcoming = nx_ref[i]

            @pl.when(upcoming >= 0)
            def _():
                for cp in fetch(upcoming):
                    cp.start()

        for c in range(ROW_TILES):
            lhs_ref[:, c * LANES:(c + 1) * LANES] = hs_ref[pl.ds(c, TM, stride=ROW_TILES), :].astype(BF16)
        gu = jnp.dot(lhs_ref[...], wgu_bf[...], preferred_element_type=F32) + bgu_ref[...]
        gate = jnp.minimum(gu[:, :D_FF], SWIGLU_LIMIT)
        up = jnp.clip(gu[:, D_FF:], -SWIGLU_LIMIT, SWIGLU_LIMIT)
        act = ((up + 1.0) * gate * _sigmoid(SWIGLU_ALPHA * gate)).astype(BF16)
        y = jnp.dot(act, wd_bf[...], preferred_element_type=F32) + bd_ref[...]
        for c in range(ROW_TILES):
            o_ref[pl.ds(c, TM, stride=ROW_TILES), :] = y[:, c * LANES:(c + 1) * LANES]


def _experts(max_tiles, tile_expert, next_expert, n_tiles, hs, w_gu, b_gu, w_down, b_down, layer):
    tile_rows = TM * ROW_TILES
    row_map = lambda i, te, nx, nt: (jnp.minimum(i, nt[0] - 1), 0)
    b_map = lambda i, te, nx, nt: (layer, te[i], 0, 0)
    grid_spec = pltpu.PrefetchScalarGridSpec(
        num_scalar_prefetch=3,
        grid=(max_tiles,),
        in_specs=[pl.BlockSpec((tile_rows, LANES), row_map),
                  pl.BlockSpec(memory_space=pl.ANY),
                  pl.BlockSpec((None, None, 1, 2 * D_FF), b_map),
                  pl.BlockSpec(memory_space=pl.ANY),
                  pl.BlockSpec((None, None, 1, D_MODEL), b_map)],
        out_specs=pl.BlockSpec((tile_rows, LANES), row_map),
        scratch_shapes=[pltpu.VMEM((D_MODEL, 2 * D_FF), F32), pltpu.VMEM((D_FF, D_MODEL), F32),
                        pltpu.VMEM((D_MODEL, 2 * D_FF), BF16), pltpu.VMEM((D_FF, D_MODEL), BF16),
                        pltpu.VMEM((TM, D_MODEL), BF16), pltpu.SemaphoreType.DMA((2,))],
    )
    return pl.pallas_call(
        functools.partial(_expert_kernel, layer=layer),
        grid_spec=grid_spec,
        out_shape=jax.ShapeDtypeStruct(hs.shape, F32),
        compiler_params=_cparams(("arbitrary",)),
        name="experts",
    )(tile_expert, next_expert, n_tiles, hs, w_gu, b_gu[:, :, None, :], w_down, b_down[:, :, None, :])


def _combine_kernel(code_ref, off_ref, y_hbm, mw_ref, xp_ref, xs_ref, g2p_ref, g2s_ref, gfin_ref,
                    op_ref, os_ref, gbuf, acc_ref, sems, *, tiles, final_norm):
    t = pl.program_id(0)
    tile_rows = TM * ROW_TILES
    buf_rows = TOP_K * tile_rows

    def issue_tile(tt, b):
        b0 = pl.multiple_of(b * buf_rows, buf_rows)

        def issue(n, carry):
            for k in range(TOP_K):
                slot = _slot(code_ref, off_ref, (tt * TM + n) * TOP_K + k)
                src = y_hbm.at[pl.ds(pl.multiple_of(slot * ROW_TILES, ROW_TILES), ROW_TILES), :]
                dst = gbuf.at[pl.ds(pl.multiple_of(b0 + k * tile_rows + n * ROW_TILES, ROW_TILES),
                                    ROW_TILES), :]
                pltpu.make_async_copy(src, dst, sems.at[b]).start()
            return carry
        lax.fori_loop(0, TM, issue, 0)

    @pl.when(t == 0)
    def _():
        issue_tile(0, 0)

    @pl.when(t + 1 < tiles.nt)
    def _():
        issue_tile(t + 1, (t + 1) % 2)

    buf = t % 2
    b0 = pl.multiple_of(buf * buf_rows, buf_rows)
    for _ in range(TOP_K):
        pltpu.make_async_copy(y_hbm.at[pl.ds(0, tile_rows), :],
                              gbuf.at[pl.ds(b0, tile_rows), :], sems.at[buf]).wait()

    w = mw_ref[...]
    for c in range(ROW_TILES):
        acc = None
        for k in range(TOP_K):
            rows = gbuf[pl.ds(b0 + k * tile_rows + c, TM, stride=ROW_TILES), :]
            term = w[:, k:k + 1] * rows
            acc = term if acc is None else acc + term
        acc_ref[:, c * LANES:(c + 1) * LANES] = acc

    def finish(x_ref, g2_ref, o_ref):
        res = x_ref[...] + g2_ref[...] * acc_ref[...]
        if final_norm:
            res = _rms(res, gfin_ref[...])
        o_ref[...] = res

    _by_kind(tiles, finish, (xp_ref, g2p_ref, op_ref), (xs_ref, g2s_ref, os_ref))


def _combine(tiles, codes, offs, y_sorted, mw, xp, xs, mod_p, mod_s, gfin, final_norm):
    grid_spec = pltpu.PrefetchScalarGridSpec(
        num_scalar_prefetch=2,
        grid=(tiles.nt,),
        in_specs=[pl.BlockSpec(memory_space=pl.ANY), tiles.all_rows(LANES), tiles.p_rows(), tiles.s_rows(),
                  tiles.p_mod(G2), tiles.s_rows(G2), _const_spec((1, D_MODEL))],
        out_specs=[tiles.p_rows(), tiles.s_rows()],
        scratch_shapes=[pltpu.VMEM((2 * TOP_K * TM * ROW_TILES, LANES), F32),
                        pltpu.VMEM((TM, D_MODEL), F32), pltpu.SemaphoreType.DMA((2,))],
    )
    return pl.pallas_call(
        functools.partial(_combine_kernel, tiles=tiles, final_norm=final_norm),
        grid_spec=grid_spec,
        out_shape=(jax.ShapeDtypeStruct(xp.shape, F32), jax.ShapeDtypeStruct(xs.shape, F32)),
        compiler_params=_cparams(("arbitrary",)),
        name="combine",
    )(codes, offs, y_sorted, mw, xp, xs, mod_p, mod_s, gfin)


def _moe_layer(tiles, xp, xs, mod_p, mod_s, g_ffn, w_router, b_router, w_gu, b_gu, w_down, b_down, gfin,
               layer, final_norm):
    n_tok = tiles.n_tok
    w_pad = jnp.pad(w_router, ((0, 0), (0, LANES - N_EXPERTS)))
    w_hi, w_lo = _split_bf16(w_pad)
    b_pad = jnp.pad(b_router, (0, LANES - N_EXPERTS), constant_values=NEG_INF).reshape(1, LANES)
    mi, mw, counts = _router(tiles, xp, xs, mod_p, mod_s, g_ffn, w_hi, w_lo, b_pad)

    cnt = counts[0, :N_EXPERTS].astype(I32)
    tiles_e = (cnt + TM - 1) // TM
    tile_end = jnp.cumsum(tiles_e)
    tile_start = tile_end - tiles_e
    offs = (tile_start * TM).astype(I32)
    codes = mi[:, :TOP_K].reshape(n_tok * TOP_K)
    max_tiles = (n_tok * TOP_K) // TM + N_EXPERTS
    n_tiles = tile_end[-1:].astype(I32)
    tile_ids = jnp.arange(max_tiles, dtype=I32)
    tile_expert = jnp.minimum(jnp.sum(tile_ids[:, None] >= tile_end[None, :], axis=-1), N_EXPERTS - 1).astype(I32)
    group_end = jnp.sum(jnp.where(tile_expert[:, None] == jnp.arange(N_EXPERTS, dtype=I32), tile_end[None, :], 0),
                        axis=-1)
    after = jnp.minimum(jnp.sum(group_end[:, None] >= tile_end[None, :], axis=-1), N_EXPERTS - 1).astype(I32)
    next_expert = jnp.where(group_end < n_tiles[0], after, -1).astype(I32)
    pad_start = (tile_start * TM + cnt).astype(I32)
    pad_cnt = (tiles_e * TM - cnt).astype(I32)

    hs = _dispatch(tiles, max_tiles * TM, codes, offs, pad_start, pad_cnt, xp, xs, mod_p, mod_s, g_ffn)
    ys = _experts(max_tiles, tile_expert, next_expert, n_tiles, hs, w_gu, b_gu, w_down, b_down, layer)
    return _combine(tiles, codes, offs, ys, mw, xp, xs, mod_p, mod_s, gfin, final_norm)


def _slopes():
    return [2.0 ** (-8.0 * (h + 1) / N_HEADS) for h in range(N_HEADS)]


def _attn_prompt_kernel(sink_ref, x_ref, shkv_ref, sckv_ref, sh_ref, sc_ref, g1_ref, gkv_ref, gmix_ref,
                        wkv_ref, bkv_ref, wq_ref, bq_ref, wo_ref,
                        o_ref, kl_ref, vl_ref, kk_ref, vv_ref, oh_ref):
    j = pl.program_id(1)
    blk = WINDOW

    @pl.when(j == 0)
    def _():
        kk_ref[...] = jnp.zeros_like(kk_ref)
        vv_ref[...] = jnp.zeros_like(vv_ref)

    @pl.when(j > 0)
    def _():
        kk_ref[0:blk, :] = kk_ref[blk:2 * blk, :]
        vv_ref[0:blk, :] = vv_ref[blk:2 * blk, :]

    x = x_ref[...]
    hkv = _rms(x, gkv_ref[...]) * (1.0 + sckv_ref[...]) + shkv_ref[...]
    kv = _bdot(hkv, wkv_ref[...]) + bkv_ref[...]
    kl_ref[...] = kv[:, :KV_DIM]
    vl_ref[...] = kv[:, KV_DIM:]
    kk_ref[blk:2 * blk, :] = kv[:, :KV_DIM].astype(BF16)
    vv_ref[blk:2 * blk, :] = kv[:, KV_DIM:].astype(BF16)

    h = _rms(x, gmix_ref[...]) * (1.0 + sc_ref[...]) + sh_ref[...]
    q = _bdot(h, wq_ref[...]) + bq_ref[...]

    r = lax.broadcasted_iota(I32, (blk, 2 * blk), 0)
    c = lax.broadcasted_iota(I32, (blk, 2 * blk), 1)
    dist_i = r + blk - c
    valid = (dist_i >= 0) & (dist_i < WINDOW) & jnp.logical_not((j == 0) & (c < blk))
    dist = dist_i.astype(F32)
    slopes = _slopes()
    for g in range(N_KV_HEADS):
        kg = kk_ref[:, g * HEAD_DIM:(g + 1) * HEAD_DIM]
        vg = vv_ref[:, g * HEAD_DIM:(g + 1) * HEAD_DIM]
        for qh in range(Q_PER_KV):
            hd = g * Q_PER_KV + qh
            qd = q[:, hd * HEAD_DIM:(hd + 1) * HEAD_DIM].astype(BF16)
            s = lax.dot_general(qd, kg, (((1,), (1,)), ((), ())), preferred_element_type=F32)
            s = s * (HEAD_DIM ** -0.5) - slopes[hd] * dist
            s = jnp.where(valid, s, NEG_INF)
            sink = sink_ref[hd]
            m = jnp.maximum(jnp.max(s, axis=-1, keepdims=True), sink)
            e = jnp.exp(s - m)
            den = jnp.sum(e, axis=-1, keepdims=True) + jnp.exp(sink - m)
            p = (e / den).astype(BF16)
            oh_ref[:, hd * HEAD_DIM:(hd + 1) * HEAD_DIM] = jnp.dot(p, vg, preferred_element_type=F32)
    mix = _bdot(oh_ref[...], wo_ref[...])
    o_ref[...] = x + g1_ref[...] * mix


SHKV, SCKV = 0, 1


def _attn_prompt(x, mod_kv, mod, gkv, gmix, wkv, bkv, wq, bq, wo, sinks):
    b, t, _ = x.shape
    blk = WINDOW
    qd = N_HEADS * HEAD_DIM
    full = lambda shape: pl.BlockSpec(shape, lambda bi, ji, *_: (0,) * len(shape))
    mods = [pl.BlockSpec((None, 1, D_MODEL), lambda bi, ji, *_, k=k: (bi, 0, k))
            for k in (SHKV, SCKV, SH1, SC1, G1)]
    rows = pl.BlockSpec((None, blk, D_MODEL), lambda bi, ji, *_: (bi, ji, 0))
    last = pl.BlockSpec((None, blk, KV_DIM), lambda bi, ji, *_: (bi, 0, 0))
    grid_spec = pltpu.PrefetchScalarGridSpec(
        num_scalar_prefetch=1,
        grid=(b, t // blk),
        in_specs=[rows] + mods + [full((1, D_MODEL)), full((1, D_MODEL)),
                  full((D_MODEL, 2 * KV_DIM)), full((1, 2 * KV_DIM)), full((D_MODEL, qd)), full((1, qd)),
                  full((qd, D_MODEL))],
        out_specs=[rows, last, last],
        scratch_shapes=[pltpu.VMEM((2 * blk, KV_DIM), BF16), pltpu.VMEM((2 * blk, KV_DIM), BF16),
                        pltpu.VMEM((blk, qd), F32)],
    )
    return pl.pallas_call(
        _attn_prompt_kernel,
        grid_spec=grid_spec,
        out_shape=(jax.ShapeDtypeStruct((b, t, D_MODEL), F32),
                   jax.ShapeDtypeStruct((b, blk, KV_DIM), F32),
                   jax.ShapeDtypeStruct((b, blk, KV_DIM), F32)),
        compiler_params=_cparams(("arbitrary", "arbitrary")),
        name="attn_prompt",
    )(sinks, x, mod_kv, mod_kv, mod, mod, mod, gkv, gmix, wkv, bkv, wq, bq, wo)


def _qkv_sample_kernel(x_ref, mkv_ref, m_ref, gkv_ref, gmix_ref,
                       wkv_ref, bkv_ref, wq_ref, bq_ref, q_ref, k_ref, v_ref):
    x = x_ref[...]
    hkv = _rms(x, gkv_ref[...]) * (1.0 + _mod_cols(mkv_ref, SCKV)) + _mod_cols(mkv_ref, SHKV)
    kv = _bdot(hkv, wkv_ref[...]) + bkv_ref[...]
    k_ref[...] = kv[:, :KV_DIM]
    v_ref[...] = kv[:, KV_DIM:]
    h = _rms(x, gmix_ref[...]) * (1.0 + _mod_cols(m_ref, SC1)) + _mod_cols(m_ref, SH1)
    q_ref[...] = _bdot(h, wq_ref[...]) + bq_ref[...]


def _qkv_sample(x, mod_kv, mod, gkv, gmix, wkv, bkv, wq, bq):
    n = x.shape[0]
    return pl.pallas_call(
        _qkv_sample_kernel,
        out_shape=(jax.ShapeDtypeStruct((n, N_HEADS * HEAD_DIM), F32),
                   jax.ShapeDtypeStruct((n, KV_DIM), F32), jax.ShapeDtypeStruct((n, KV_DIM), F32)),
        compiler_params=_cparams(),
        name="qkv_sample",
    )(x, mod_kv, mod, gkv, gmix, wkv, bkv, wq, bq)


def _attn_sample_kernel(q_ref, ck_ref, cv_ref, nk_ref, nv_ref, slope_ref, sink_ref, o_ref, *, steps):
    rows = steps * Q_PER_KV
    sb = q_ref.shape[0]
    r_c = lax.broadcasted_iota(I32, (sb, rows, WINDOW), 1) // Q_PER_KV
    j_c = lax.broadcasted_iota(I32, (sb, rows, WINDOW), 2)
    dist_c = r_c + WINDOW - j_c
    valid_c = (dist_c >= 0) & (dist_c < WINDOW)
    r_n = lax.broadcasted_iota(I32, (sb, rows, steps), 1) // Q_PER_KV
    j_n = lax.broadcasted_iota(I32, (sb, rows, steps), 2)
    dist_n = r_n - j_n
    valid_n = (dist_n >= 0) & (dist_n < WINDOW)
    scale = HEAD_DIM ** -0.5
    for g in range(N_KV_HEADS):
        lanes = slice(g * HEAD_DIM, (g + 1) * HEAD_DIM)
        qg = q_ref[:, g].astype(BF16)
        slope = slope_ref[g][None]
        sink = sink_ref[g][None]
        kc, vc = ck_ref[:, :, lanes].astype(BF16), cv_ref[:, :, lanes].astype(BF16)
        kn, vn = nk_ref[:, :, lanes].astype(BF16), nv_ref[:, :, lanes].astype(BF16)
        s_c = jnp.einsum('nrd,njd->nrj', qg, kc, preferred_element_type=F32) * scale
        s_n = jnp.einsum('nrd,njd->nrj', qg, kn, preferred_element_type=F32) * scale
        s_c = jnp.where(valid_c, s_c - slope * dist_c.astype(F32), NEG_INF)
        s_n = jnp.where(valid_n, s_n - slope * dist_n.astype(F32), NEG_INF)
        m = jnp.maximum(jnp.maximum(jnp.max(s_c, axis=-1, keepdims=True),
                                    jnp.max(s_n, axis=-1, keepdims=True)), sink)
        e_c = jnp.exp(s_c - m)
        e_n = jnp.exp(s_n - m)
        den = (jnp.sum(e_c, axis=-1, keepdims=True) + jnp.sum(e_n, axis=-1, keepdims=True)
               + jnp.exp(sink - m))
        o = (jnp.einsum('nrj,njd->nrd', (e_c / den).astype(BF16), vc, preferred_element_type=F32)
             + jnp.einsum('nrj,njd->nrd', (e_n / den).astype(BF16), vn, preferred_element_type=F32))
        o_ref[:, g] = o


def _attn_sample(qg, cache_k, cache_v, k_new, v_new, slope_rows, sink_rows):
    n, _, rows, _ = qg.shape
    steps = rows // Q_PER_KV
    sb = 8
    blk = lambda shape: pl.BlockSpec((sb,) + shape, lambda i: (i,) + (0,) * len(shape))
    full = lambda shape: pl.BlockSpec(shape, lambda i: (0,) * len(shape))
    return pl.pallas_call(
        functools.partial(_attn_sample_kernel, steps=steps),
        grid=(n // sb,),
        in_specs=[blk((N_KV_HEADS, rows, HEAD_DIM)), blk((WINDOW, KV_DIM)), blk((WINDOW, KV_DIM)),
                  blk((steps, KV_DIM)), blk((steps, KV_DIM)),
                  full((N_KV_HEADS, rows, 1)), full((N_KV_HEADS, rows, 1))],
        out_specs=blk((N_KV_HEADS, rows, HEAD_DIM)),
        out_shape=jax.ShapeDtypeStruct(qg.shape, F32),
        compiler_params=_cparams(("arbitrary",)),
        name="attn_sample",
    )(qg, cache_k, cache_v, k_new, v_new, slope_rows, sink_rows)


def _oproj_kernel(o_ref, x_ref, m_ref, wo_ref, y_ref):
    y_ref[...] = x_ref[...] + _mod_cols(m_ref, G1) * _bdot(o_ref[...], wo_ref[...])


def _oproj(o, x, mod, wo):
    return pl.pallas_call(
        _oproj_kernel,
        out_shape=jax.ShapeDtypeStruct(x.shape, F32),
        compiler_params=_cparams(),
        name="oproj_sample",
    )(o, x, mod, wo)


def kernel(x_prompt, x_sample, state_ssm_re, state_ssm_im, cache_k, cache_v, c_prompt, c_sample, g_mix, g_ffn, w_ada, b_ada, ssm_a_re, ssm_a_im, ssm_log_dt, ssm_b_re, ssm_b_im, ssm_c_re, ssm_c_im, ssm_d, w_glu_a, w_glu_b, g_kv, w_ada_kv, b_ada_kv, w_kv, b_kv, w_q, b_q, w_o, attn_sinks, w_router, b_router, w_gu, b_gu, w_down, b_down, g_final):
    bsz, seq, d = x_prompt.shape
    n_seq, steps, _ = x_sample.shape
    assert d == D_MODEL and seq % S5_CHUNK == 0 and seq % TM == 0 and (n_seq * steps) % TM == 0
    assert g_mix.shape[0] == 2 and ssm_a_re.shape[0] == 1 and w_q.shape[0] == 1
    n_p, n_s = bsz * seq, n_seq * steps
    tiles = _Tiles(n_p, n_s, seq)
    row = lambda v: v.reshape(1, -1)

    c_all = jnp.concatenate([c_prompt, c_sample], axis=0)
    c_rows = -(-c_all.shape[0] // SUBLANES) * SUBLANES
    c_all = jnp.pad(c_all, ((0, c_rows - c_all.shape[0]), (0, 0)))
    mods = [_ada(c_all, w_ada, b_ada, l) for l in range(2)]
    mod_kv = _ada(c_all, w_ada_kv[None], b_ada_kv[None], 0)

    def split(m):
        per_seq = m[bsz:bsz + n_seq]
        return m[:bsz, None, :], per_seq, jnp.repeat(per_seq, steps, axis=0)

    mod_p, mod_q, mod_s = split(mods[0])
    abr, abi, bbr, bbi, apr, api = _ssm_prep(ssm_a_re[0], ssm_a_im[0], ssm_log_dt[0], ssm_b_re[0],
                                             ssm_b_im[0], S5_CHUNK // SUBLANES)
    b_re_x, b_im_x = _expand_b(bbr), _expand_b(bbi)
    cexp = (_expand_c(ssm_c_re[0]).astype(BF16), _expand_c(-ssm_c_im[0]).astype(BF16))
    wa, wb = w_glu_a[0].astype(BF16), w_glu_b[0].astype(BF16)
    xp, sre_p, sim_p = _s5_prompt(x_prompt, mod_p, row(g_mix[0]), (abr, abi, apr, api),
                                  (b_re_x.astype(BF16), b_im_x.astype(BF16)), cexp, row(ssm_d[0]), wa, wb)
    xs_t, sre_s, sim_s = _s5_sample(
        x_sample.transpose(1, 0, 2), mod_q, row(g_mix[0]),
        state_ssm_re[0].reshape(n_seq, SSM_COLS), state_ssm_im[0].reshape(n_seq, SSM_COLS), abr, abi,
        (_split_bf16(b_re_x), _split_bf16(b_im_x)), cexp, row(ssm_d[0]), wa, wb)
    xp = xp.reshape(n_p, D_MODEL)
    xs = xs_t.transpose(1, 0, 2).reshape(n_s, D_MODEL)
    xp, xs = _moe_layer(tiles, xp, xs, mod_p, mod_s, row(g_ffn[0]), w_router[0], b_router[0], w_gu, b_gu,
                        w_down, b_down, row(g_final), 0, False)

    mod_p, _, mod_s = split(mods[1])
    modkv_p, _, modkv_s = split(mod_kv)
    wkv, wq, wo = w_kv.astype(BF16), w_q[0].astype(BF16), w_o[0].astype(BF16)
    xp3, k_p, v_p = _attn_prompt(xp.reshape(bsz, seq, D_MODEL), modkv_p, mod_p, row(g_kv), row(g_mix[1]), wkv,
                                 row(b_kv), wq, row(b_q[0]), wo, attn_sinks[0])
    xp = xp3.reshape(n_p, D_MODEL)

    q_s, k_s, v_s = _qkv_sample(xs, modkv_s, mod_s, row(g_kv), row(g_mix[1]), wkv, row(b_kv), wq, row(b_q[0]))
    rows = steps * Q_PER_KV
    qg = q_s.reshape(n_seq, steps, N_KV_HEADS, Q_PER_KV, HEAD_DIM).transpose(0, 2, 1, 3, 4).reshape(
        n_seq, N_KV_HEADS, rows, HEAD_DIM)
    head_of_row = (np.arange(N_KV_HEADS)[:, None] * Q_PER_KV + np.arange(rows)[None, :] % Q_PER_KV)
    slope_rows = jnp.asarray(np.asarray(_slopes(), np.float32)[head_of_row][..., None])
    sink_rows = attn_sinks[0][head_of_row][..., None]
    k_new = k_s.reshape(n_seq, steps, KV_DIM)
    v_new = v_s.reshape(n_seq, steps, KV_DIM)
    ck = cache_k.reshape(n_seq, WINDOW, KV_DIM)
    cv = cache_v.reshape(n_seq, WINDOW, KV_DIM)
    og = _attn_sample(qg, ck, cv, k_new, v_new, slope_rows, sink_rows)
    o_s = og.reshape(n_seq, N_KV_HEADS, steps, Q_PER_KV, HEAD_DIM).transpose(0, 2, 1, 3, 4).reshape(
        n_s, N_HEADS * HEAD_DIM)
    xs = _oproj(o_s, xs, mod_s, wo)

    yp, ys = _moe_layer(tiles, xp, xs, mod_p, mod_s, row(g_ffn[1]), w_router[1], b_router[1], w_gu, b_gu,
                        w_down, b_down, row(g_final), 1, True)

    shape4 = lambda a: a.reshape(a.shape[0], a.shape[1], N_KV_HEADS, HEAD_DIM)
    state = lambda a, n: a.reshape(1, n, SSM_GROUPS, SSM_STATE)
    return (yp.reshape(bsz, seq, D_MODEL), ys.reshape(n_seq, steps, D_MODEL),
            state(sre_p, bsz), state(sim_p, bsz), shape4(k_p), shape4(v_p),
            state(sre_s, n_seq), state(sim_s, n_seq),
            shape4(jnp.concatenate([ck[:, steps:], k_new], axis=1)),
            shape4(jnp.concatenate([cv[:, steps:], v_new], axis=1)))
```

```python
import functools
import math

import numpy as np
import jax
import jax.numpy as jnp
from jax import lax
from jax.experimental import pallas as pl
from jax.experimental.pallas import tpu as pltpu

F32 = jnp.float32
BF16 = jnp.bfloat16
I32 = jnp.int32

D_MODEL = 1024
SSM_GROUP = 16
SSM_GROUPS = D_MODEL // SSM_GROUP
SSM_STATE = 64
SSM_COLS = SSM_GROUPS * SSM_STATE
N_HEADS = 16
HEAD_DIM = 64
N_KV_HEADS = 4
Q_PER_KV = N_HEADS // N_KV_HEADS
KV_DIM = N_KV_HEADS * HEAD_DIM
WINDOW = 128
N_EXPERTS = 32
TOP_K = 4
D_FF = D_MODEL
SWIGLU_LIMIT = 7.0
SWIGLU_ALPHA = 1.702
RMS_EPS = 1e-5
NEG_INF = -1e30

LANES = 128
SUBLANES = 8
ROW_TILES = D_MODEL // LANES
TM = 256
S5_CHUNK = 256
S5_COLS = 512
MXU_K = 256
GROUPS_PER_K = MXU_K // SSM_GROUP
N_KT = D_MODEL // MXU_K
VMEM_LIMIT = 56 * 1024 * 1024


def _cparams(sem=None):
    return pltpu.CompilerParams(dimension_semantics=sem, vmem_limit_bytes=VMEM_LIMIT)


def _sigmoid(x):
    return 1.0 / (1.0 + jnp.exp(-x))


def _rms(x, g):
    return x * lax.rsqrt(jnp.mean(x * x, axis=-1, keepdims=True) + RMS_EPS) * g


def _bdot(a, b):
    return jnp.dot(a.astype(BF16), b.astype(BF16), preferred_element_type=F32)


def _gelu_tanh(x):
    return 0.5 * x * (1.0 + jnp.tanh(math.sqrt(2.0 / math.pi) * (x + 0.044715 * (x * x * x))))


def _split_bf16(w):
    hi = w.astype(BF16)
    lo = (w - hi.astype(F32)).astype(BF16)
    return hi, lo


def _dot3(a, b_hi, b_lo):
    a_hi = a.astype(BF16)
    a_lo = (a - a_hi.astype(F32)).astype(BF16)
    return (jnp.dot(a_hi, b_hi, preferred_element_type=F32)
            + jnp.dot(a_lo, b_hi, preferred_element_type=F32)
            + jnp.dot(a_hi, b_lo, preferred_element_type=F32))


def _ada_kernel(c_ref, w_ref, b_ref, o_ref):
    c = c_ref[...]
    o_ref[...] = _bdot(c * _sigmoid(c), w_ref[...]) + b_ref[...]


def _ada(c, w, b, layer):
    rows, cols = c.shape[0], w.shape[2]
    tn = 1024
    return pl.pallas_call(
        _ada_kernel,
        grid=(cols // tn,),
        in_specs=[pl.BlockSpec((rows, D_MODEL), lambda j: (0, 0)),
                  pl.BlockSpec((None, D_MODEL, tn), lambda j: (layer, 0, j)),
                  pl.BlockSpec((None, 1, tn), lambda j: (layer, 0, j))],
        out_specs=pl.BlockSpec((rows, tn), lambda j: (0, j)),
        out_shape=jax.ShapeDtypeStruct((rows, cols), F32),
        compiler_params=_cparams(("arbitrary",)),
        name="ada",
    )(c, w, b.reshape(b.shape[0], 1, cols))


def _ssm_prep_kernel(are_ref, aim_ref, ldt_ref, bre_ref, bim_ref,
                     abr_ref, abi_ref, bbr_ref, bbi_ref, apr_ref, api_ref, *, sub_len):
    a_re, a_im = are_ref[...], aim_ref[...]
    dt = jnp.exp(ldt_ref[...])
    mag = jnp.exp(a_re * dt)
    ab_re = mag * jnp.cos(a_im * dt)
    ab_im = mag * jnp.sin(a_im * dt)
    abr_ref[...] = ab_re
    abi_ref[...] = ab_im
    den = a_re * a_re + a_im * a_im
    x, y = ab_re - 1.0, ab_im
    f_re = (x * a_re + y * a_im) / den
    f_im = (y * a_re - x * a_im) / den
    b_re, b_im = bre_ref[...], bim_ref[...]
    bbr_ref[...] = f_re * b_re - f_im * b_im
    bbi_ref[...] = f_re * b_im + f_im * b_re
    pr, pi = ab_re, ab_im
    for _ in range(int(math.log2(sub_len))):
        pr, pi = pr * pr - pi * pi, 2.0 * pr * pi
    cr, ci = jnp.ones_like(pr), jnp.zeros_like(pi)
    for j in range(SUBLANES + 1):
        apr_ref[j:j + 1, :] = cr
        api_ref[j:j + 1, :] = ci
        cr, ci = cr * pr - ci * pi, cr * pi + ci * pr


def _ssm_prep(a_re, a_im, log_dt, b_re, b_im, sub_len):
    gp = SSM_COLS
    ldt = jnp.broadcast_to(log_dt[:, None], (SSM_GROUPS, SSM_STATE)).reshape(1, gp)
    b_re_t = b_re.reshape(gp, SSM_GROUP).T
    b_im_t = b_im.reshape(gp, SSM_GROUP).T
    row = jax.ShapeDtypeStruct((1, gp), F32)
    mat = jax.ShapeDtypeStruct((SSM_GROUP, gp), F32)
    pw = jax.ShapeDtypeStruct((SUBLANES + 1, gp), F32)
    return pl.pallas_call(
        functools.partial(_ssm_prep_kernel, sub_len=sub_len),
        out_shape=(row, row, mat, mat, pw, pw),
        compiler_params=_cparams(),
        name="ssm_prep",
    )(a_re.reshape(1, gp), a_im.reshape(1, gp), ldt, b_re_t, b_im_t)


def _expand_b(bbar):
    eye = jnp.eye(GROUPS_PER_K, dtype=F32)
    b = bbar.reshape(SSM_GROUP, N_KT, GROUPS_PER_K, SSM_STATE).transpose(1, 0, 2, 3)
    e = b[:, None, :, :, :] * eye[None, :, None, :, None]
    return e.reshape(N_KT, MXU_K, GROUPS_PER_K * SSM_STATE)


def _expand_c(c):
    eye = jnp.eye(GROUPS_PER_K, dtype=F32)
    cc = c.reshape(N_KT, GROUPS_PER_K, SSM_GROUP, SSM_STATE).transpose(0, 1, 3, 2)
    e = cc[:, :, :, None, :] * eye[None, :, None, :, None]
    return e.reshape(N_KT, GROUPS_PER_K * SSM_STATE, MXU_K)


def _cmul(ar, ai, br, bi):
    return ar * br - ai * bi, ar * bi + ai * br


def _s5_prompt_kernel(x_ref, sh_ref, sc_ref, g1_ref, gmix_ref, abr_ref, abi_ref, apr_ref, api_ref,
                      bre_ref, bim_ref, cre_ref, cim_ref, dsk_ref, wa_ref, wb_ref,
                      o_ref, sre_ref, sim_ref,
                      u_ref, hr_ref, hi_ref, hmr_ref, hmi_ref, car_ref, cai_ref, *, L):
    ls = L // SUBLANES
    kcols = GROUPS_PER_K * SSM_STATE

    @pl.when(pl.program_id(1) == 0)
    def _():
        car_ref[...] = jnp.zeros_like(car_ref)
        cai_ref[...] = jnp.zeros_like(cai_ref)

    x = x_ref[...]
    u = _rms(x, gmix_ref[...]) * (1.0 + sc_ref[...]) + sh_ref[...]
    u_ref[...] = u
    r = lax.broadcasted_iota(I32, (L, L), 0)
    c = lax.broadcasted_iota(I32, (L, L), 1)
    perm = jnp.where(c == (r % SUBLANES) * ls + r // SUBLANES, 1.0, 0.0).astype(BF16)
    unperm = jnp.where(r == (c % SUBLANES) * ls + c // SUBLANES, 1.0, 0.0).astype(BF16)
    ub = jnp.dot(perm, u.astype(BF16), preferred_element_type=F32).astype(BF16)
    for kt in range(N_KT):
        uk = ub[:, kt * MXU_K:(kt + 1) * MXU_K]
        hr_ref[:, kt * kcols:(kt + 1) * kcols] = jnp.dot(uk, bre_ref[kt], preferred_element_type=F32)
        hi_ref[:, kt * kcols:(kt + 1) * kcols] = jnp.dot(uk, bim_ref[kt], preferred_element_type=F32)

    unroll = 4

    for cb in range(SSM_COLS // S5_COLS):
        cols = slice(cb * S5_COLS, (cb + 1) * S5_COLS)
        ar = jnp.broadcast_to(abr_ref[:, cols], (SUBLANES, S5_COLS))
        ai = jnp.broadcast_to(abi_ref[:, cols], (SUBLANES, S5_COLS))

        def scan_body(it, carry, cols=cols, ar=ar, ai=ai):
            sr, si = carry
            for k in range(unroll):
                r0 = pl.multiple_of((it * unroll + k) * SUBLANES, SUBLANES)
                tr, ti = _cmul(ar, ai, sr, si)
                sr = tr + hr_ref[pl.ds(r0, SUBLANES), cols]
                si = ti + hi_ref[pl.ds(r0, SUBLANES), cols]
                hr_ref[pl.ds(r0, SUBLANES), cols] = sr
                hi_ref[pl.ds(r0, SUBLANES), cols] = si
            return sr, si

        zero = jnp.zeros((SUBLANES, S5_COLS), F32)
        lax.fori_loop(0, ls // unroll, scan_body, (zero, zero))

    fr = hr_ref[L - SUBLANES:L, :]
    fi = hi_ref[L - SUBLANES:L, :]
    row = lax.broadcasted_iota(I32, (SUBLANES, SSM_COLS), 0)
    gr, gi = fr, fi
    pr, pi = apr_ref[1:2, :], api_ref[1:2, :]
    for s in (1, 2, 4):
        sr = jnp.where(row >= s, pltpu.roll(gr, s, axis=0), 0.0)
        si = jnp.where(row >= s, pltpu.roll(gi, s, axis=0), 0.0)
        tr, ti = _cmul(pr, pi, sr, si)
        gr, gi = gr + tr, gi + ti
        pr, pi = _cmul(pr, pi, pr, pi)
    c0r, c0i = car_ref[...], cai_ref[...]
    tr, ti = _cmul(apr_ref[0:SUBLANES, :], api_ref[0:SUBLANES, :], c0r, c0i)
    hmr = tr + jnp.where(row >= 1, pltpu.roll(gr, 1, axis=0), 0.0)
    hmi = ti + jnp.where(row >= 1, pltpu.roll(gi, 1, axis=0), 0.0)
    hmr_ref[...] = hmr
    hmi_ref[...] = hmi
    tr, ti = _cmul(apr_ref[SUBLANES:SUBLANES + 1, :], api_ref[SUBLANES:SUBLANES + 1, :], c0r, c0i)
    ncr = tr + gr[SUBLANES - 1:SUBLANES, :]
    nci = ti + gi[SUBLANES - 1:SUBLANES, :]
    car_ref[...] = ncr
    cai_ref[...] = nci
    sre_ref[...] = ncr
    sim_ref[...] = nci

    for cb in range(SSM_COLS // S5_COLS):
        cols = slice(cb * S5_COLS, (cb + 1) * S5_COLS)
        ar = jnp.broadcast_to(abr_ref[:, cols], (SUBLANES, S5_COLS))
        ai = jnp.broadcast_to(abi_ref[:, cols], (SUBLANES, S5_COLS))

        def fix_body(it, carry, cols=cols, ar=ar, ai=ai):
            dr, di = carry
            for k in range(unroll):
                r0 = pl.multiple_of((it * unroll + k) * SUBLANES, SUBLANES)
                dr, di = _cmul(ar, ai, dr, di)
                hr_ref[pl.ds(r0, SUBLANES), cols] = hr_ref[pl.ds(r0, SUBLANES), cols] + dr
                hi_ref[pl.ds(r0, SUBLANES), cols] = hi_ref[pl.ds(r0, SUBLANES), cols] + di
            return dr, di

        lax.fori_loop(0, ls // unroll, fix_body, (hmr_ref[:, cols], hmi_ref[:, cols]))

    ys = []
    for nt in range(N_KT):
        hr = hr_ref[:, nt * kcols:(nt + 1) * kcols].astype(BF16)
        hi = hi_ref[:, nt * kcols:(nt + 1) * kcols].astype(BF16)
        ys.append(jnp.dot(hr, cre_ref[nt], preferred_element_type=F32)
                  + jnp.dot(hi, cim_ref[nt], preferred_element_type=F32))
    yp = jnp.concatenate(ys, axis=-1)
    yp_hi = yp.astype(BF16)
    yp_lo = (yp - yp_hi.astype(F32)).astype(BF16)
    y = (jnp.dot(unperm, yp_hi, preferred_element_type=F32)
         + jnp.dot(unperm, yp_lo, preferred_element_type=F32)) + dsk_ref[...] * u_ref[...]
    z = _gelu_tanh(y).astype(BF16)
    mix = jnp.dot(z, wa_ref[...], preferred_element_type=F32) * _sigmoid(
        jnp.dot(z, wb_ref[...], preferred_element_type=F32))
    o_ref[...] = x_ref[...] + g1_ref[...] * mix


SH1, SC1, G1 = 0, 1, 2


def _s5_prompt(x, mod, gmix, prep, bexp, cexp, dsk, wa, wb):
    b, t, _ = x.shape
    L = S5_CHUNK
    abr, abi, apr, api = prep
    bre, bim = bexp
    cre, cim = cexp
    kcols = GROUPS_PER_K * SSM_STATE
    full = lambda shape: pl.BlockSpec(shape, lambda bi, ci: (0,) * len(shape))
    mods = [pl.BlockSpec((None, 1, D_MODEL), lambda bi, ci, k=k: (bi, 0, k)) for k in (SH1, SC1, G1)]
    return pl.pallas_call(
        functools.partial(_s5_prompt_kernel, L=L),
        grid=(b, t // L),
        in_specs=[pl.BlockSpec((None, L, D_MODEL), lambda bi, ci: (bi, ci, 0))] + mods + [
                  full((1, D_MODEL)), full((1, SSM_COLS)), full((1, SSM_COLS)),
                  full((SUBLANES + 1, SSM_COLS)), full((SUBLANES + 1, SSM_COLS)),
                  full((N_KT, MXU_K, kcols)), full((N_KT, MXU_K, kcols)),
                  full((N_KT, kcols, MXU_K)), full((N_KT, kcols, MXU_K)),
                  full((1, D_MODEL)), full((D_MODEL, D_MODEL)), full((D_MODEL, D_MODEL))],
        out_specs=[pl.BlockSpec((None, L, D_MODEL), lambda bi, ci: (bi, ci, 0)),
                   pl.BlockSpec((None, 1, SSM_COLS), lambda bi, ci: (bi, 0, 0)),
                   pl.BlockSpec((None, 1, SSM_COLS), lambda bi, ci: (bi, 0, 0))],
        out_shape=(jax.ShapeDtypeStruct((b, t, D_MODEL), F32),
                   jax.ShapeDtypeStruct((b, 1, SSM_COLS), F32),
                   jax.ShapeDtypeStruct((b, 1, SSM_COLS), F32)),
        scratch_shapes=[pltpu.VMEM((L, D_MODEL), F32),
                        pltpu.VMEM((L, SSM_COLS), F32), pltpu.VMEM((L, SSM_COLS), F32),
                        pltpu.VMEM((SUBLANES, SSM_COLS), F32), pltpu.VMEM((SUBLANES, SSM_COLS), F32),
                        pltpu.VMEM((1, SSM_COLS), F32), pltpu.VMEM((1, SSM_COLS), F32)],
        compiler_params=_cparams(("arbitrary", "arbitrary")),
        name="s5_prompt",
    )(x, mod, mod, mod, gmix, abr, abi, apr, api, bre, bim, cre, cim, dsk, wa, wb)


def _mod_cols(m_ref, k):
    return m_ref[:, k * D_MODEL:(k + 1) * D_MODEL]


def _s5_sample_kernel(x_ref, m_ref, gmix_ref, h0r_ref, h0i_ref, abr_ref, abi_ref,
                      brh_ref, brl_ref, bih_ref, bil_ref, cre_ref, cim_ref, dsk_ref, wa_ref, wb_ref,
                      o_ref, sre_ref, sim_ref, *, steps):
    kcols = GROUPS_PER_K * SSM_STATE
    ar, ai = abr_ref[...], abi_ref[...]
    sr, si = h0r_ref[...], h0i_ref[...]
    sh, sc, g1 = _mod_cols(m_ref, SH1), _mod_cols(m_ref, SC1), _mod_cols(m_ref, G1)
    for t in range(steps):
        x = x_ref[t]
        u = _rms(x, gmix_ref[...]) * (1.0 + sc) + sh
        bur, bui = [], []
        for kt in range(N_KT):
            uk = u[:, kt * MXU_K:(kt + 1) * MXU_K]
            bur.append(_dot3(uk, brh_ref[kt], brl_ref[kt]))
            bui.append(_dot3(uk, bih_ref[kt], bil_ref[kt]))
        tr, ti = _cmul(ar, ai, sr, si)
        sr = tr + jnp.concatenate(bur, axis=-1)
        si = ti + jnp.concatenate(bui, axis=-1)
        ys = []
        for nt in range(N_KT):
            ys.append(_bdot(sr[:, nt * kcols:(nt + 1) * kcols], cre_ref[nt])
                      + _bdot(si[:, nt * kcols:(nt + 1) * kcols], cim_ref[nt]))
        y = jnp.concatenate(ys, axis=-1) + dsk_ref[...] * u
        z = _gelu_tanh(y).astype(BF16)
        mix = jnp.dot(z, wa_ref[...], preferred_element_type=F32) * _sigmoid(
            jnp.dot(z, wb_ref[...], preferred_element_type=F32))
        o_ref[t] = x + g1 * mix
    sre_ref[...] = sr
    sim_ref[...] = si


def _s5_sample(x_t, mod, gmix, h0r, h0i, abr, abi, bsplit, cexp, dsk, wa, wb):
    steps, n, _ = x_t.shape
    (brh, brl), (bih, bil) = bsplit
    cre, cim = cexp
    return pl.pallas_call(
        functools.partial(_s5_sample_kernel, steps=steps),
        out_shape=(jax.ShapeDtypeStruct((steps, n, D_MODEL), F32),
                   jax.ShapeDtypeStruct((n, SSM_COLS), F32),
                   jax.ShapeDtypeStruct((n, SSM_COLS), F32)),
        compiler_params=_cparams(),
        name="s5_sample",
    )(x_t, mod, gmix, h0r, h0i, abr, abi, brh, brl, bih, bil, cre, cim, dsk, wa, wb)


class _Tiles:
    def __init__(self, n_prompt, n_sample, seq_len):
        self.ntp = n_prompt // TM
        self.nts = n_sample // TM
        self.nt = self.ntp + self.nts
        self.tiles_per_seq = seq_len // TM
        self.n_batch = n_prompt // seq_len
        self.n_tok = n_prompt + n_sample

    def p_rows(self):
        return pl.BlockSpec((TM, D_MODEL), lambda t, *_: (jnp.minimum(t, self.ntp - 1), 0))

    def s_rows(self, chunk=0):
        return pl.BlockSpec((TM, D_MODEL), lambda t, *_: (jnp.maximum(t - self.ntp, 0), chunk))

    def p_mod(self, chunk):
        return pl.BlockSpec((None, 1, D_MODEL),
                            lambda t, *_: (jnp.minimum(t // self.tiles_per_seq, self.n_batch - 1), 0, chunk))

    def all_rows(self, width):
        return pl.BlockSpec((TM, width), lambda t, *_: (t, 0))


def _const_spec(shape):
    return pl.BlockSpec(shape, lambda t, *_: (0,) * len(shape))


def _by_kind(tiles, body, prompt_refs, sample_refs):
    t = pl.program_id(0)

    @pl.when(t < tiles.ntp)
    def _():
        body(*prompt_refs)

    @pl.when(t >= tiles.ntp)
    def _():
        body(*sample_refs)


def _moe_input(x_ref, sh_ref, sc_ref, g_ref):
    return _rms(x_ref[...], g_ref[...]) * (1.0 + sc_ref[...]) + sh_ref[...]


def _router_kernel(xp_ref, xs_ref, shp_ref, shs_ref, scp_ref, scs_ref, g_ref, wh_ref, wl_ref, b_ref, tri_ref,
                   mi_ref, mw_ref, cnt_ref, run_ref, *, tiles):
    @pl.when(pl.program_id(0) == 0)
    def _():
        run_ref[...] = jnp.zeros_like(run_ref)

    def route(x_ref, sh_ref, sc_ref):
        h = _moe_input(x_ref, sh_ref, sc_ref, g_ref)
        logits = _dot3(h, wh_ref[...], wl_ref[...]) + b_ref[...]
        lane = lax.broadcasted_iota(I32, (TM, LANES), 1)
        lane_f = lane.astype(F32)
        vals, firsts, hots = [], [], []
        work = logits
        for _ in range(TOP_K):
            m = jnp.max(work, axis=-1, keepdims=True)
            first = jnp.min(jnp.where(work == m, lane_f, float(LANES)), axis=-1, keepdims=True)
            hot = lane_f == first
            vals.append(m)
            firsts.append(first.astype(I32))
            hots.append(hot)
            work = jnp.where(hot, -jnp.inf, work)
        exps = [jnp.exp(v - vals[0]) for v in vals]
        den = exps[0] + exps[1] + exps[2] + exps[3]

        chosen = jnp.zeros((TM, LANES), F32)
        for hot in hots:
            chosen = jnp.where(hot, 1.0, chosen)
        rank_all = jnp.dot(tri_ref[...], chosen.astype(BF16), preferred_element_type=F32) + run_ref[...]
        run_ref[...] = run_ref[...] + jnp.sum(chosen, axis=0, keepdims=True)
        cnt_ref[...] = run_ref[...]

        mi = jnp.zeros((TM, LANES), I32)
        mw = jnp.zeros((TM, LANES), F32)
        for k in range(TOP_K):
            r_k = jnp.sum(jnp.where(hots[k], rank_all, 0.0), axis=-1, keepdims=True).astype(I32)
            mi = jnp.where(lane == k, r_k * N_EXPERTS + firsts[k], mi)
            mw = jnp.where(lane == k, exps[k] / den, mw)
        mi_ref[...] = mi
        mw_ref[...] = mw

    _by_kind(tiles, route, (xp_ref, shp_ref, scp_ref), (xs_ref, shs_ref, scs_ref))


SH2, SC2, G2 = 3, 4, 5


def _router(tiles, xp, xs, mod_p, mod_s, g, w_hi, w_lo, b_pad):
    tri = jnp.asarray(np.tril(np.ones((TM, TM), np.float32), -1), BF16)
    return pl.pallas_call(
        functools.partial(_router_kernel, tiles=tiles),
        grid=(tiles.nt,),
        in_specs=[tiles.p_rows(), tiles.s_rows(), tiles.p_mod(SH2), tiles.s_rows(SH2), tiles.p_mod(SC2),
                  tiles.s_rows(SC2), _const_spec((1, D_MODEL)), _const_spec((D_MODEL, LANES)),
                  _const_spec((D_MODEL, LANES)), _const_spec((1, LANES)), _const_spec((TM, TM))],
        out_specs=[tiles.all_rows(LANES), tiles.all_rows(LANES), _const_spec((1, LANES))],
        out_shape=(jax.ShapeDtypeStruct((tiles.n_tok, LANES), I32),
                   jax.ShapeDtypeStruct((tiles.n_tok, LANES), F32),
                   jax.ShapeDtypeStruct((1, LANES), F32)),
        scratch_shapes=[pltpu.VMEM((1, LANES), F32)],
        compiler_params=_cparams(("arbitrary",)),
        name="router",
    )(xp, xs, mod_p, mod_s, mod_p, mod_s, g, w_hi, w_lo, b_pad, tri)


PAD_ARMS = tuple(1 << i for i in reversed(range(int(math.log2(TM)))))


def _dispatch_kernel(slot_ref, pstart_ref, pcnt_ref,
                     xp_ref, xs_ref, shp_ref, shs_ref, scp_ref, scs_ref, g_ref,
                     o_hbm, rows_ref, zero_ref, sems, zsem, *, tiles):
    t = pl.program_id(0)
    buf = t % 2
    tile_rows = TM * ROW_TILES
    base = pl.multiple_of(buf * tile_rows, tile_rows)

    def tile_wait(b):
        b0 = pl.multiple_of(b * tile_rows, tile_rows)
        for _ in range(TOP_K):
            pltpu.make_async_copy(rows_ref.at[pl.ds(b0, tile_rows), :],
                                  o_hbm.at[pl.ds(0, tile_rows), :], sems.at[b]).wait()

    def pad_copies(act):
        def body(e, carry):
            start, cnt = pstart_ref[e], pcnt_ref[e]
            for p in PAD_ARMS:
                @pl.when((cnt & p) != 0)
                def _(start=start, p=p):
                    cp = pltpu.make_async_copy(
                        zero_ref.at[pl.ds(0, p * ROW_TILES), :],
                        o_hbm.at[pl.ds(pl.multiple_of(start * ROW_TILES, ROW_TILES), p * ROW_TILES), :],
                        zsem)
                    cp.start() if act == "start" else cp.wait()
                start = start + jnp.where((cnt & p) != 0, p, 0)
            return carry
        lax.fori_loop(0, N_EXPERTS, body, 0)

    @pl.when(t == 0)
    def _():
        zero_ref[...] = jnp.zeros_like(zero_ref)
        pad_copies("start")

    @pl.when(t >= 2)
    def _():
        tile_wait(buf)

    def stage(x_ref, sh_ref, sc_ref):
        h = _moe_input(x_ref, sh_ref, sc_ref, g_ref)
        for c in range(ROW_TILES):
            rows_ref[pl.ds(base + c, TM, stride=ROW_TILES), :] = h[:, c * LANES:(c + 1) * LANES]

    _by_kind(tiles, stage, (xp_ref, shp_ref, scp_ref), (xs_ref, shs_ref, scs_ref))

    def issue(n, carry):
        src = rows_ref.at[pl.ds(pl.multiple_of(base + n * ROW_TILES, ROW_TILES), ROW_TILES), :]
        for k in range(TOP_K):
            slot = slot_ref[(t * TM + n) * TOP_K + k]
            dst = o_hbm.at[pl.ds(pl.multiple_of(slot * ROW_TILES, ROW_TILES), ROW_TILES), :]
            pltpu.make_async_copy(src, dst, sems.at[buf]).start(priority=k % 2)
        return carry
    lax.fori_loop(0, TM, issue, 0)

    @pl.when(t == tiles.nt - 1)
    def _():
        if tiles.nt >= 2:
            tile_wait(1 - buf)
        tile_wait(buf)
        pad_copies("wait")


def _dispatch(tiles, n_slots, slots, pad_start, pad_cnt, xp, xs, mod_p, mod_s, g):
    grid_spec = pltpu.PrefetchScalarGridSpec(
        num_scalar_prefetch=3,
        grid=(tiles.nt,),
        in_specs=[tiles.p_rows(), tiles.s_rows(), tiles.p_mod(SH2), tiles.s_rows(SH2), tiles.p_mod(SC2),
                  tiles.s_rows(SC2), _const_spec((1, D_MODEL))],
        out_specs=pl.BlockSpec(memory_space=pl.ANY),
        scratch_shapes=[pltpu.VMEM((2 * TM * ROW_TILES, LANES), F32),
                        pltpu.VMEM((PAD_ARMS[0] * ROW_TILES, LANES), F32),
                        pltpu.SemaphoreType.DMA((2,)), pltpu.SemaphoreType.DMA(())],
    )
    return pl.pallas_call(
        functools.partial(_dispatch_kernel, tiles=tiles),
        grid_spec=grid_spec,
        out_shape=jax.ShapeDtypeStruct((n_slots * ROW_TILES, LANES), F32),
        compiler_params=_cparams(("arbitrary",)),
        name="dispatch",
    )(slots, pad_start, pad_cnt, xp, xs, mod_p, mod_s, mod_p, mod_s, g)


def _expert_kernel(te_ref, nx_ref, nt_ref, hs_ref, wgu_hbm, bgu_ref, wd_hbm, bd_ref, o_ref,
                   wgu_f32, wd_f32, wgu_bf, wd_bf, lhs_ref, wsem, *, layer):
    i = pl.program_id(0)

    def fetch(expert):
        return (pltpu.make_async_copy(wgu_hbm.at[layer, expert], wgu_f32, wsem.at[0]),
                pltpu.make_async_copy(wd_hbm.at[layer, expert], wd_f32, wsem.at[1]))

    @pl.when(i < nt_ref[0])
    def _():
        expert = te_ref[i]

        @pl.when(i == 0)
        def _():
            for cp in fetch(expert):
                cp.start()

        @pl.when((i == 0) | (expert != te_ref[jnp.maximum(i - 1, 0)]))
        def _():
            for cp in fetch(expert):
                cp.wait()
            wgu_bf[...] = wgu_f32[...].astype(BF16)
            wd_bf[...] = wd_f32[...].astype(BF16)
            upcoming = nx_ref[i]

            @pl.when(upcoming >= 0)
            def _():
                for cp in fetch(upcoming):
                    cp.start()

        for c in range(ROW_TILES):
            lhs_ref[:, c * LANES:(c + 1) * LANES] = hs_ref[pl.ds(c, TM, stride=ROW_TILES), :].astype(BF16)
        gu = jnp.dot(lhs_ref[...], wgu_bf[...], preferred_element_type=F32) + bgu_ref[...]
        gate = jnp.minimum(gu[:, :D_FF], SWIGLU_LIMIT)
        up = jnp.clip(gu[:, D_FF:], -SWIGLU_LIMIT, SWIGLU_LIMIT)
        act = ((up + 1.0) * gate * _sigmoid(SWIGLU_ALPHA * gate)).astype(BF16)
        y = jnp.dot(act, wd_bf[...], preferred_element_type=F32) + bd_ref[...]
        for c in range(ROW_TILES):
            o_ref[pl.ds(c, TM, stride=ROW_TILES), :] = y[:, c * LANES:(c + 1) * LANES]


def _experts(max_tiles, tile_expert, next_expert, n_tiles, hs, w_gu, b_gu, w_down, b_down, layer):
    tile_rows = TM * ROW_TILES
    row_map = lambda i, te, nx, nt: (jnp.minimum(i, nt[0] - 1), 0)
    b_map = lambda i, te, nx, nt: (layer, te[i], 0, 0)
    grid_spec = pltpu.PrefetchScalarGridSpec(
        num_scalar_prefetch=3,
        grid=(max_tiles,),
        in_specs=[pl.BlockSpec((tile_rows, LANES), row_map),
                  pl.BlockSpec(memory_space=pl.ANY),
                  pl.BlockSpec((None, None, 1, 2 * D_FF), b_map),
                  pl.BlockSpec(memory_space=pl.ANY),
                  pl.BlockSpec((None, None, 1, D_MODEL), b_map)],
        out_specs=pl.BlockSpec((tile_rows, LANES), row_map),
        scratch_shapes=[pltpu.VMEM((D_MODEL, 2 * D_FF), F32), pltpu.VMEM((D_FF, D_MODEL), F32),
                        pltpu.VMEM((D_MODEL, 2 * D_FF), BF16), pltpu.VMEM((D_FF, D_MODEL), BF16),
                        pltpu.VMEM((TM, D_MODEL), BF16), pltpu.SemaphoreType.DMA((2,))],
    )
    return pl.pallas_call(
        functools.partial(_expert_kernel, layer=layer),
        grid_spec=grid_spec,
        out_shape=jax.ShapeDtypeStruct(hs.shape, F32),
        compiler_params=_cparams(("arbitrary",)),
        name="experts",
    )(tile_expert, next_expert, n_tiles, hs, w_gu, b_gu[:, :, None, :], w_down, b_down[:, :, None, :])


def _combine_kernel(slot_ref, y_hbm, mw_ref, xp_ref, xs_ref, g2p_ref, g2s_ref, gfin_ref,
                    op_ref, os_ref, gbuf, acc_ref, sems, *, tiles, final_norm):
    t = pl.program_id(0)
    tile_rows = TM * ROW_TILES
    buf_rows = TOP_K * tile_rows

    def issue_tile(tt, b):
        b0 = pl.multiple_of(b * buf_rows, buf_rows)

        def issue(n, carry):
            for k in range(TOP_K):
                slot = slot_ref[(tt * TM + n) * TOP_K + k]
                src = y_hbm.at[pl.ds(pl.multiple_of(slot * ROW_TILES, ROW_TILES), ROW_TILES), :]
                dst = gbuf.at[pl.ds(pl.multiple_of(b0 + k * tile_rows + n * ROW_TILES, ROW_TILES),
                                    ROW_TILES), :]
                pltpu.make_async_copy(src, dst, sems.at[b]).start()
            return carry
        lax.fori_loop(0, TM, issue, 0)

    @pl.when(t == 0)
    def _():
        issue_tile(0, 0)

    @pl.when(t + 1 < tiles.nt)
    def _():
        issue_tile(t + 1, (t + 1) % 2)

    buf = t % 2
    b0 = pl.multiple_of(buf * buf_rows, buf_rows)
    for _ in range(TOP_K):
        pltpu.make_async_copy(y_hbm.at[pl.ds(0, tile_rows), :],
                              gbuf.at[pl.ds(b0, tile_rows), :], sems.at[buf]).wait()

    w = mw_ref[...]
    for c in range(ROW_TILES):
        acc = None
        for k in range(TOP_K):
            rows = gbuf[pl.ds(b0 + k * tile_rows + c, TM, stride=ROW_TILES), :]
            term = w[:, k:k + 1] * rows
            acc = term if acc is None else acc + term
        acc_ref[:, c * LANES:(c + 1) * LANES] = acc

    def finish(x_ref, g2_ref, o_ref):
        res = x_ref[...] + g2_ref[...] * acc_ref[...]
        if final_norm:
            res = _rms(res, gfin_ref[...])
        o_ref[...] = res

    _by_kind(tiles, finish, (xp_ref, g2p_ref, op_ref), (xs_ref, g2s_ref, os_ref))


def _combine(tiles, slots, y_sorted, mw, xp, xs, mod_p, mod_s, gfin, final_norm):
    grid_spec = pltpu.PrefetchScalarGridSpec(
        num_scalar_prefetch=1,
        grid=(tiles.nt,),
        in_specs=[pl.BlockSpec(memory_space=pl.ANY), tiles.all_rows(LANES), tiles.p_rows(), tiles.s_rows(),
                  tiles.p_mod(G2), tiles.s_rows(G2), _const_spec((1, D_MODEL))],
        out_specs=[tiles.p_rows(), tiles.s_rows()],
        scratch_shapes=[pltpu.VMEM((2 * TOP_K * TM * ROW_TILES, LANES), F32),
                        pltpu.VMEM((TM, D_MODEL), F32), pltpu.SemaphoreType.DMA((2,))],
    )
    return pl.pallas_call(
        functools.partial(_combine_kernel, tiles=tiles, final_norm=final_norm),
        grid_spec=grid_spec,
        out_shape=(jax.ShapeDtypeStruct(xp.shape, F32), jax.ShapeDtypeStruct(xs.shape, F32)),
        compiler_params=_cparams(("arbitrary",)),
        name="combine",
    )(slots, y_sorted, mw, xp, xs, mod_p, mod_s, gfin)


def _moe_layer(tiles, xp, xs, mod_p, mod_s, g_ffn, w_router, b_router, w_gu, b_gu, w_down, b_down, gfin,
               layer, final_norm):
    n_tok = tiles.n_tok
    w_pad = jnp.pad(w_router, ((0, 0), (0, LANES - N_EXPERTS)))
    w_hi, w_lo = _split_bf16(w_pad)
    b_pad = jnp.pad(b_router, (0, LANES - N_EXPERTS), constant_values=NEG_INF).reshape(1, LANES)
    mi, mw, counts = _router(tiles, xp, xs, mod_p, mod_s, g_ffn, w_hi, w_lo, b_pad)

    cnt = counts[0, :N_EXPERTS].astype(I32)
    tiles_e = (cnt + TM - 1) // TM
    tile_end = jnp.cumsum(tiles_e)
    tile_start = tile_end - tiles_e
    codes = mi[:, :TOP_K]
    first_slot = jnp.sum(jnp.where((codes & (N_EXPERTS - 1))[..., None] == jnp.arange(N_EXPERTS, dtype=I32),
                                   (tile_start * TM)[None, None, :], 0), axis=-1)
    slots = (first_slot + (codes >> int(math.log2(N_EXPERTS)))).reshape(n_tok * TOP_K)
    max_tiles = (n_tok * TOP_K) // TM + N_EXPERTS
    n_tiles = tile_end[-1:].astype(I32)
    tile_ids = jnp.arange(max_tiles, dtype=I32)
    tile_expert = jnp.minimum(jnp.sum(tile_ids[:, None] >= tile_end[None, :], axis=-1), N_EXPERTS - 1).astype(I32)
    group_end = jnp.sum(jnp.where(tile_expert[:, None] == jnp.arange(N_EXPERTS, dtype=I32), tile_end[None, :], 0),
                        axis=-1)
    after = jnp.minimum(jnp.sum(group_end[:, None] >= tile_end[None, :], axis=-1), N_EXPERTS - 1).astype(I32)
    next_expert = jnp.where(group_end < n_tiles[0], after, -1).astype(I32)
    pad_start = (tile_start * TM + cnt).astype(I32)
    pad_cnt = (tiles_e * TM - cnt).astype(I32)

    hs = _dispatch(tiles, max_tiles * TM, slots, pad_start, pad_cnt, xp, xs, mod_p, mod_s, g_ffn)
    ys = _experts(max_tiles, tile_expert, next_expert, n_tiles, hs, w_gu, b_gu, w_down, b_down, layer)
    return _combine(tiles, slots, ys, mw, xp, xs, mod_p, mod_s, gfin, final_norm)


def _slopes():
    return [2.0 ** (-8.0 * (h + 1) / N_HEADS) for h in range(N_HEADS)]


def _attn_prompt_kernel(sink_ref, x_ref, shkv_ref, sckv_ref, sh_ref, sc_ref, g1_ref, gkv_ref, gmix_ref,
                        wkv_ref, bkv_ref, wq_ref, bq_ref, wo_ref,
                        o_ref, kl_ref, vl_ref, kk_ref, vv_ref, oh_ref):
    j = pl.program_id(1)
    blk = WINDOW

    @pl.when(j == 0)
    def _():
        kk_ref[...] = jnp.zeros_like(kk_ref)
        vv_ref[...] = jnp.zeros_like(vv_ref)

    @pl.when(j > 0)
    def _():
        kk_ref[0:blk, :] = kk_ref[blk:2 * blk, :]
        vv_ref[0:blk, :] = vv_ref[blk:2 * blk, :]

    x = x_ref[...]
    hkv = _rms(x, gkv_ref[...]) * (1.0 + sckv_ref[...]) + shkv_ref[...]
    kv = _bdot(hkv, wkv_ref[...]) + bkv_ref[...]
    kl_ref[...] = kv[:, :KV_DIM]
    vl_ref[...] = kv[:, KV_DIM:]
    kk_ref[blk:2 * blk, :] = kv[:, :KV_DIM].astype(BF16)
    vv_ref[blk:2 * blk, :] = kv[:, KV_DIM:].astype(BF16)

    h = _rms(x, gmix_ref[...]) * (1.0 + sc_ref[...]) + sh_ref[...]
    q = _bdot(h, wq_ref[...]) + bq_ref[...]

    r = lax.broadcasted_iota(I32, (blk, 2 * blk), 0)
    c = lax.broadcasted_iota(I32, (blk, 2 * blk), 1)
    dist_i = r + blk - c
    valid = (dist_i >= 0) & (dist_i < WINDOW) & jnp.logical_not((j == 0) & (c < blk))
    dist = dist_i.astype(F32)
    slopes = _slopes()
    for g in range(N_KV_HEADS):
        kg = kk_ref[:, g * HEAD_DIM:(g + 1) * HEAD_DIM]
        vg = vv_ref[:, g * HEAD_DIM:(g + 1) * HEAD_DIM]
        for qh in range(Q_PER_KV):
            hd = g * Q_PER_KV + qh
            qd = q[:, hd * HEAD_DIM:(hd + 1) * HEAD_DIM].astype(BF16)
            s = lax.dot_general(qd, kg, (((1,), (1,)), ((), ())), preferred_element_type=F32)
            s = s * (HEAD_DIM ** -0.5) - slopes[hd] * dist
            s = jnp.where(valid, s, NEG_INF)
            sink = sink_ref[hd]
            m = jnp.maximum(jnp.max(s, axis=-1, keepdims=True), sink)
            e = jnp.exp(s - m)
            den = jnp.sum(e, axis=-1, keepdims=True) + jnp.exp(sink - m)
            p = (e / den).astype(BF16)
            oh_ref[:, hd * HEAD_DIM:(hd + 1) * HEAD_DIM] = jnp.dot(p, vg, preferred_element_type=F32)
    mix = _bdot(oh_ref[...], wo_ref[...])
    o_ref[...] = x + g1_ref[...] * mix


SHKV, SCKV = 0, 1


def _attn_prompt(x, mod_kv, mod, gkv, gmix, wkv, bkv, wq, bq, wo, sinks):
    b, t, _ = x.shape
    blk = WINDOW
    qd = N_HEADS * HEAD_DIM
    full = lambda shape: pl.BlockSpec(shape, lambda bi, ji, *_: (0,) * len(shape))
    mods = [pl.BlockSpec((None, 1, D_MODEL), lambda bi, ji, *_, k=k: (bi, 0, k))
            for k in (SHKV, SCKV, SH1, SC1, G1)]
    rows = pl.BlockSpec((None, blk, D_MODEL), lambda bi, ji, *_: (bi, ji, 0))
    last = pl.BlockSpec((None, blk, KV_DIM), lambda bi, ji, *_: (bi, 0, 0))
    grid_spec = pltpu.PrefetchScalarGridSpec(
        num_scalar_prefetch=1,
        grid=(b, t // blk),
        in_specs=[rows] + mods + [full((1, D_MODEL)), full((1, D_MODEL)),
                  full((D_MODEL, 2 * KV_DIM)), full((1, 2 * KV_DIM)), full((D_MODEL, qd)), full((1, qd)),
                  full((qd, D_MODEL))],
        out_specs=[rows, last, last],
        scratch_shapes=[pltpu.VMEM((2 * blk, KV_DIM), BF16), pltpu.VMEM((2 * blk, KV_DIM), BF16),
                        pltpu.VMEM((blk, qd), F32)],
    )
    return pl.pallas_call(
        _attn_prompt_kernel,
        grid_spec=grid_spec,
        out_shape=(jax.ShapeDtypeStruct((b, t, D_MODEL), F32),
                   jax.ShapeDtypeStruct((b, blk, KV_DIM), F32),
                   jax.ShapeDtypeStruct((b, blk, KV_DIM), F32)),
        compiler_params=_cparams(("arbitrary", "arbitrary")),
        name="attn_prompt",
    )(sinks, x, mod_kv, mod_kv, mod, mod, mod, gkv, gmix, wkv, bkv, wq, bq, wo)


def _qkv_sample_kernel(x_ref, mkv_ref, m_ref, gkv_ref, gmix_ref,
                       wkv_ref, bkv_ref, wq_ref, bq_ref, q_ref, k_ref, v_ref):
    x = x_ref[...]
    hkv = _rms(x, gkv_ref[...]) * (1.0 + _mod_cols(mkv_ref, SCKV)) + _mod_cols(mkv_ref, SHKV)
    kv = _bdot(hkv, wkv_ref[...]) + bkv_ref[...]
    k_ref[...] = kv[:, :KV_DIM]
    v_ref[...] = kv[:, KV_DIM:]
    h = _rms(x, gmix_ref[...]) * (1.0 + _mod_cols(m_ref, SC1)) + _mod_cols(m_ref, SH1)
    q_ref[...] = _bdot(h, wq_ref[...]) + bq_ref[...]


def _qkv_sample(x, mod_kv, mod, gkv, gmix, wkv, bkv, wq, bq):
    n = x.shape[0]
    return pl.pallas_call(
        _qkv_sample_kernel,
        out_shape=(jax.ShapeDtypeStruct((n, N_HEADS * HEAD_DIM), F32),
                   jax.ShapeDtypeStruct((n, KV_DIM), F32), jax.ShapeDtypeStruct((n, KV_DIM), F32)),
        compiler_params=_cparams(),
        name="qkv_sample",
    )(x, mod_kv, mod, gkv, gmix, wkv, bkv, wq, bq)


def _attn_sample_kernel(q_ref, ck_ref, cv_ref, nk_ref, nv_ref, slope_ref, sink_ref, o_ref, *, steps):
    rows = steps * Q_PER_KV
    sb = q_ref.shape[0]
    r_c = lax.broadcasted_iota(I32, (sb, rows, WINDOW), 1) // Q_PER_KV
    j_c = lax.broadcasted_iota(I32, (sb, rows, WINDOW), 2)
    dist_c = r_c + WINDOW - j_c
    valid_c = (dist_c >= 0) & (dist_c < WINDOW)
    r_n = lax.broadcasted_iota(I32, (sb, rows, steps), 1) // Q_PER_KV
    j_n = lax.broadcasted_iota(I32, (sb, rows, steps), 2)
    dist_n = r_n - j_n
    valid_n = (dist_n >= 0) & (dist_n < WINDOW)
    scale = HEAD_DIM ** -0.5
    for g in range(N_KV_HEADS):
        lanes = slice(g * HEAD_DIM, (g + 1) * HEAD_DIM)
        qg = q_ref[:, g].astype(BF16)
        slope = slope_ref[g][None]
        sink = sink_ref[g][None]
        kc, vc = ck_ref[:, :, lanes].astype(BF16), cv_ref[:, :, lanes].astype(BF16)
        kn, vn = nk_ref[:, :, lanes].astype(BF16), nv_ref[:, :, lanes].astype(BF16)
        s_c = jnp.einsum('nrd,njd->nrj', qg, kc, preferred_element_type=F32) * scale
        s_n = jnp.einsum('nrd,njd->nrj', qg, kn, preferred_element_type=F32) * scale
        s_c = jnp.where(valid_c, s_c - slope * dist_c.astype(F32), NEG_INF)
        s_n = jnp.where(valid_n, s_n - slope * dist_n.astype(F32), NEG_INF)
        m = jnp.maximum(jnp.maximum(jnp.max(s_c, axis=-1, keepdims=True),
                                    jnp.max(s_n, axis=-1, keepdims=True)), sink)
        e_c = jnp.exp(s_c - m)
        e_n = jnp.exp(s_n - m)
        den = (jnp.sum(e_c, axis=-1, keepdims=True) + jnp.sum(e_n, axis=-1, keepdims=True)
               + jnp.exp(sink - m))
        o = (jnp.einsum('nrj,njd->nrd', (e_c / den).astype(BF16), vc, preferred_element_type=F32)
             + jnp.einsum('nrj,njd->nrd', (e_n / den).astype(BF16), vn, preferred_element_type=F32))
        o_ref[:, g] = o


def _attn_sample(qg, cache_k, cache_v, k_new, v_new, slope_rows, sink_rows):
    n, _, rows, _ = qg.shape
    steps = rows // Q_PER_KV
    sb = 8
    blk = lambda shape: pl.BlockSpec((sb,) + shape, lambda i: (i,) + (0,) * len(shape))
    full = lambda shape: pl.BlockSpec(shape, lambda i: (0,) * len(shape))
    return pl.pallas_call(
        functools.partial(_attn_sample_kernel, steps=steps),
        grid=(n // sb,),
        in_specs=[blk((N_KV_HEADS, rows, HEAD_DIM)), blk((WINDOW, KV_DIM)), blk((WINDOW, KV_DIM)),
                  blk((steps, KV_DIM)), blk((steps, KV_DIM)),
                  full((N_KV_HEADS, rows, 1)), full((N_KV_HEADS, rows, 1))],
        out_specs=blk((N_KV_HEADS, rows, HEAD_DIM)),
        out_shape=jax.ShapeDtypeStruct(qg.shape, F32),
        compiler_params=_cparams(("arbitrary",)),
        name="attn_sample",
    )(qg, cache_k, cache_v, k_new, v_new, slope_rows, sink_rows)


def _oproj_kernel(o_ref, x_ref, m_ref, wo_ref, y_ref):
    y_ref[...] = x_ref[...] + _mod_cols(m_ref, G1) * _bdot(o_ref[...], wo_ref[...])


def _oproj(o, x, mod, wo):
    return pl.pallas_call(
        _oproj_kernel,
        out_shape=jax.ShapeDtypeStruct(x.shape, F32),
        compiler_params=_cparams(),
        name="oproj_sample",
    )(o, x, mod, wo)


def kernel(x_prompt, x_sample, state_ssm_re, state_ssm_im, cache_k, cache_v, c_prompt, c_sample, g_mix, g_ffn, w_ada, b_ada, ssm_a_re, ssm_a_im, ssm_log_dt, ssm_b_re, ssm_b_im, ssm_c_re, ssm_c_im, ssm_d, w_glu_a, w_glu_b, g_kv, w_ada_kv, b_ada_kv, w_kv, b_kv, w_q, b_q, w_o, attn_sinks, w_router, b_router, w_gu, b_gu, w_down, b_down, g_final):
    bsz, seq, d = x_prompt.shape
    n_seq, steps, _ = x_sample.shape
    assert d == D_MODEL and seq % S5_CHUNK == 0 and seq % TM == 0 and (n_seq * steps) % TM == 0
    assert g_mix.shape[0] == 2 and ssm_a_re.shape[0] == 1 and w_q.shape[0] == 1
    n_p, n_s = bsz * seq, n_seq * steps
    tiles = _Tiles(n_p, n_s, seq)
    row = lambda v: v.reshape(1, -1)

    c_all = jnp.concatenate([c_prompt, c_sample], axis=0)
    c_rows = -(-c_all.shape[0] // SUBLANES) * SUBLANES
    c_all = jnp.pad(c_all, ((0, c_rows - c_all.shape[0]), (0, 0)))
    mods = [_ada(c_all, w_ada, b_ada, l) for l in range(2)]
    mod_kv = _ada(c_all, w_ada_kv[None], b_ada_kv[None], 0)

    def split(m):
        per_seq = m[bsz:bsz + n_seq]
        return m[:bsz, None, :], per_seq, jnp.repeat(per_seq, steps, axis=0)

    mod_p, mod_q, mod_s = split(mods[0])
    abr, abi, bbr, bbi, apr, api = _ssm_prep(ssm_a_re[0], ssm_a_im[0], ssm_log_dt[0], ssm_b_re[0],
                                             ssm_b_im[0], S5_CHUNK // SUBLANES)
    b_re_x, b_im_x = _expand_b(bbr), _expand_b(bbi)
    cexp = (_expand_c(ssm_c_re[0]).astype(BF16), _expand_c(-ssm_c_im[0]).astype(BF16))
    wa, wb = w_glu_a[0].astype(BF16), w_glu_b[0].astype(BF16)
    xp, sre_p, sim_p = _s5_prompt(x_prompt, mod_p, row(g_mix[0]), (abr, abi, apr, api),
                                  (b_re_x.astype(BF16), b_im_x.astype(BF16)), cexp, row(ssm_d[0]), wa, wb)
    xs_t, sre_s, sim_s = _s5_sample(
        x_sample.transpose(1, 0, 2), mod_q, row(g_mix[0]),
        state_ssm_re[0].reshape(n_seq, SSM_COLS), state_ssm_im[0].reshape(n_seq, SSM_COLS), abr, abi,
        (_split_bf16(b_re_x), _split_bf16(b_im_x)), cexp, row(ssm_d[0]), wa, wb)
    xp = xp.reshape(n_p, D_MODEL)
    xs = xs_t.transpose(1, 0, 2).reshape(n_s, D_MODEL)
    xp, xs = _moe_layer(tiles, xp, xs, mod_p, mod_s, row(g_ffn[0]), w_router[0], b_router[0], w_gu, b_gu,
                        w_down, b_down, row(g_final), 0, False)

    mod_p, _, mod_s = split(mods[1])
    modkv_p, _, modkv_s = split(mod_kv)
    wkv, wq, wo = w_kv.astype(BF16), w_q[0].astype(BF16), w_o[0].astype(BF16)
    xp3, k_p, v_p = _attn_prompt(xp.reshape(bsz, seq, D_MODEL), modkv_p, mod_p, row(g_kv), row(g_mix[1]), wkv,
                                 row(b_kv), wq, row(b_q[0]), wo, attn_sinks[0])
    xp = xp3.reshape(n_p, D_MODEL)

    q_s, k_s, v_s = _qkv_sample(xs, modkv_s, mod_s, row(g_kv), row(g_mix[1]), wkv, row(b_kv), wq, row(b_q[0]))
    rows = steps * Q_PER_KV
    qg = q_s.reshape(n_seq, steps, N_KV_HEADS, Q_PER_KV, HEAD_DIM).transpose(0, 2, 1, 3, 4).reshape(
        n_seq, N_KV_HEADS, rows, HEAD_DIM)
    head_of_row = (np.arange(N_KV_HEADS)[:, None] * Q_PER_KV + np.arange(rows)[None, :] % Q_PER_KV)
    slope_rows = jnp.asarray(np.asarray(_slopes(), np.float32)[head_of_row][..., None])
    sink_rows = attn_sinks[0][head_of_row][..., None]
    k_new = k_s.reshape(n_seq, steps, KV_DIM)
    v_new = v_s.reshape(n_seq, steps, KV_DIM)
    ck = cache_k.reshape(n_seq, WINDOW, KV_DIM)
    cv = cache_v.reshape(n_seq, WINDOW, KV_DIM)
    og = _attn_sample(qg, ck, cv, k_new, v_new, slope_rows, sink_rows)
    o_s = og.reshape(n_seq, N_KV_HEADS, steps, Q_PER_KV, HEAD_DIM).transpose(0, 2, 1, 3, 4).reshape(
        n_s, N_HEADS * HEAD_DIM)
    xs = _oproj(o_s, xs, mod_s, wo)

    yp, ys = _moe_layer(tiles, xp, xs, mod_p, mod_s, row(g_ffn[1]), w_router[1], b_router[1], w_gu, b_gu,
                        w_down, b_down, row(g_final), 1, True)

    shape4 = lambda a: a.reshape(a.shape[0], a.shape[1], N_KV_HEADS, HEAD_DIM)
    state = lambda a, n: a.reshape(1, n, SSM_GROUPS, SSM_STATE)
    return (yp.reshape(bsz, seq, D_MODEL), ys.reshape(n_seq, steps, D_MODEL),
            state(sre_p, bsz), state(sim_p, bsz), shape4(k_p), shape4(v_p),
            state(sre_s, n_seq), state(sim_s, n_seq),
            shape4(jnp.concatenate([ck[:, steps:], k_new], axis=1)),
            shape4(jnp.concatenate([cv[:, steps:], v_new], axis=1)))
```

```python
import functools
import math

import numpy as np
import jax
import jax.numpy as jnp
from jax import lax
from jax.experimental import pallas as pl
from jax.experimental.pallas import tpu as pltpu

F32 = jnp.float32
BF16 = jnp.bfloat16
I32 = jnp.int32

D_MODEL = 1024
SSM_GROUP = 16
SSM_GROUPS = D_MODEL // SSM_GROUP
SSM_STATE = 64
SSM_COLS = SSM_GROUPS * SSM_STATE
N_HEADS = 16
HEAD_DIM = 64
N_KV_HEADS = 4
Q_PER_KV = N_HEADS // N_KV_HEADS
KV_DIM = N_KV_HEADS * HEAD_DIM
WINDOW = 128
N_EXPERTS = 32
TOP_K = 4
D_FF = D_MODEL
SWIGLU_LIMIT = 7.0
SWIGLU_ALPHA = 1.702
RMS_EPS = 1e-5
NEG_INF = -1e30

LANES = 128
SUBLANES = 8
ROW_TILES = D_MODEL // LANES
TM = 256
S5_CHUNK = 256
S5_COLS = 512
MXU_K = 256
GROUPS_PER_K = MXU_K // SSM_GROUP
N_KT = D_MODEL // MXU_K
VMEM_LIMIT = 56 * 1024 * 1024


def _cparams(sem=None):
    return pltpu.CompilerParams(dimension_semantics=sem, vmem_limit_bytes=VMEM_LIMIT)


def _sigmoid(x):
    return 1.0 / (1.0 + jnp.exp(-x))


def _rms(x, g):
    return x * lax.rsqrt(jnp.mean(x * x, axis=-1, keepdims=True) + RMS_EPS) * g


def _bdot(a, b):
    return jnp.dot(a.astype(BF16), b.astype(BF16), preferred_element_type=F32)


def _gelu_tanh(x):
    return 0.5 * x * (1.0 + jnp.tanh(math.sqrt(2.0 / math.pi) * (x + 0.044715 * (x * x * x))))


def _split_bf16(w):
    hi = w.astype(BF16)
    lo = (w - hi.astype(F32)).astype(BF16)
    return hi, lo


def _dot3(a, b_hi, b_lo):
    a_hi = a.astype(BF16)
    a_lo = (a - a_hi.astype(F32)).astype(BF16)
    return (jnp.dot(a_hi, b_hi, preferred_element_type=F32)
            + jnp.dot(a_lo, b_hi, preferred_element_type=F32)
            + jnp.dot(a_hi, b_lo, preferred_element_type=F32))


def _ada_kernel(c_ref, w_ref, b_ref, o_ref):
    c = c_ref[...]
    o_ref[...] = _bdot(c * _sigmoid(c), w_ref[...]) + b_ref[...]


def _ada(c, w, b, layer):
    rows, cols = c.shape[0], w.shape[2]
    tn = 1024
    return pl.pallas_call(
        _ada_kernel,
        grid=(cols // tn,),
        in_specs=[pl.BlockSpec((rows, D_MODEL), lambda j: (0, 0)),
                  pl.BlockSpec((None, D_MODEL, tn), lambda j: (layer, 0, j)),
                  pl.BlockSpec((None, 1, tn), lambda j: (layer, 0, j))],
        out_specs=pl.BlockSpec((rows, tn), lambda j: (0, j)),
        out_shape=jax.ShapeDtypeStruct((rows, cols), F32),
        compiler_params=_cparams(("arbitrary",)),
        name="ada",
    )(c, w, b.reshape(b.shape[0], 1, cols))


def _ssm_prep_kernel(are_ref, aim_ref, ldt_ref, bre_ref, bim_ref,
                     abr_ref, abi_ref, bbr_ref, bbi_ref, apr_ref, api_ref, *, sub_len):
    a_re, a_im = are_ref[...], aim_ref[...]
    dt = jnp.exp(ldt_ref[...])
    mag = jnp.exp(a_re * dt)
    ab_re = mag * jnp.cos(a_im * dt)
    ab_im = mag * jnp.sin(a_im * dt)
    abr_ref[...] = ab_re
    abi_ref[...] = ab_im
    den = a_re * a_re + a_im * a_im
    x, y = ab_re - 1.0, ab_im
    f_re = (x * a_re + y * a_im) / den
    f_im = (y * a_re - x * a_im) / den
    b_re, b_im = bre_ref[...], bim_ref[...]
    bbr_ref[...] = f_re * b_re - f_im * b_im
    bbi_ref[...] = f_re * b_im + f_im * b_re
    pr, pi = ab_re, ab_im
    for _ in range(int(math.log2(sub_len))):
        pr, pi = pr * pr - pi * pi, 2.0 * pr * pi
    cr, ci = jnp.ones_like(pr), jnp.zeros_like(pi)
    for j in range(SUBLANES + 1):
        apr_ref[j:j + 1, :] = cr
        api_ref[j:j + 1, :] = ci
        cr, ci = cr * pr - ci * pi, cr * pi + ci * pr


def _ssm_prep(a_re, a_im, log_dt, b_re, b_im, sub_len):
    gp = SSM_COLS
    ldt = jnp.broadcast_to(log_dt[:, None], (SSM_GROUPS, SSM_STATE)).reshape(1, gp)
    b_re_t = b_re.reshape(gp, SSM_GROUP).T
    b_im_t = b_im.reshape(gp, SSM_GROUP).T
    row = jax.ShapeDtypeStruct((1, gp), F32)
    mat = jax.ShapeDtypeStruct((SSM_GROUP, gp), F32)
    pw = jax.ShapeDtypeStruct((SUBLANES + 1, gp), F32)
    return pl.pallas_call(
        functools.partial(_ssm_prep_kernel, sub_len=sub_len),
        out_shape=(row, row, mat, mat, pw, pw),
        compiler_params=_cparams(),
        name="ssm_prep",
    )(a_re.reshape(1, gp), a_im.reshape(1, gp), ldt, b_re_t, b_im_t)


def _expand_b(bbar):
    eye = jnp.eye(GROUPS_PER_K, dtype=F32)
    b = bbar.reshape(SSM_GROUP, N_KT, GROUPS_PER_K, SSM_STATE).transpose(1, 0, 2, 3)
    e = b[:, None, :, :, :] * eye[None, :, None, :, None]
    return e.reshape(N_KT, MXU_K, GROUPS_PER_K * SSM_STATE)


def _expand_c(c):
    eye = jnp.eye(GROUPS_PER_K, dtype=F32)
    cc = c.reshape(N_KT, GROUPS_PER_K, SSM_GROUP, SSM_STATE).transpose(0, 1, 3, 2)
    e = cc[:, :, :, None, :] * eye[None, :, None, :, None]
    return e.reshape(N_KT, GROUPS_PER_K * SSM_STATE, MXU_K)


def _cmul(ar, ai, br, bi):
    return ar * br - ai * bi, ar * bi + ai * br


def _s5_prompt_kernel(x_ref, sh_ref, sc_ref, g1_ref, gmix_ref, abr_ref, abi_ref, apr_ref, api_ref,
                      bre_ref, bim_ref, cre_ref, cim_ref, dsk_ref, wa_ref, wb_ref,
                      o_ref, sre_ref, sim_ref,
                      u_ref, hr_ref, hi_ref, hmr_ref, hmi_ref, car_ref, cai_ref, *, L):
    ls = L // SUBLANES
    kcols = GROUPS_PER_K * SSM_STATE

    @pl.when(pl.program_id(1) == 0)
    def _():
        car_ref[...] = jnp.zeros_like(car_ref)
        cai_ref[...] = jnp.zeros_like(cai_ref)

    x = x_ref[...]
    u = _rms(x, gmix_ref[...]) * (1.0 + sc_ref[...]) + sh_ref[...]
    u_ref[...] = u
    r = lax.broadcasted_iota(I32, (L, L), 0)
    c = lax.broadcasted_iota(I32, (L, L), 1)
    perm = jnp.where(c == (r % SUBLANES) * ls + r // SUBLANES, 1.0, 0.0).astype(BF16)
    unperm = jnp.where(r == (c % SUBLANES) * ls + c // SUBLANES, 1.0, 0.0).astype(BF16)
    ub = jnp.dot(perm, u.astype(BF16), preferred_element_type=F32).astype(BF16)
    for kt in range(N_KT):
        uk = ub[:, kt * MXU_K:(kt + 1) * MXU_K]
        hr_ref[:, kt * kcols:(kt + 1) * kcols] = jnp.dot(uk, bre_ref[kt], preferred_element_type=F32)
        hi_ref[:, kt * kcols:(kt + 1) * kcols] = jnp.dot(uk, bim_ref[kt], preferred_element_type=F32)

    unroll = 4

    for cb in range(SSM_COLS // S5_COLS):
        cols = slice(cb * S5_COLS, (cb + 1) * S5_COLS)
        ar = jnp.broadcast_to(abr_ref[:, cols], (SUBLANES, S5_COLS))
        ai = jnp.broadcast_to(abi_ref[:, cols], (SUBLANES, S5_COLS))

        def scan_body(it, carry, cols=cols, ar=ar, ai=ai):
            sr, si = carry
            for k in range(unroll):
                r0 = pl.multiple_of((it * unroll + k) * SUBLANES, SUBLANES)
                tr, ti = _cmul(ar, ai, sr, si)
                sr = tr + hr_ref[pl.ds(r0, SUBLANES), cols]
                si = ti + hi_ref[pl.ds(r0, SUBLANES), cols]
                hr_ref[pl.ds(r0, SUBLANES), cols] = sr
                hi_ref[pl.ds(r0, SUBLANES), cols] = si
            return sr, si

        zero = jnp.zeros((SUBLANES, S5_COLS), F32)
        lax.fori_loop(0, ls // unroll, scan_body, (zero, zero))

    fr = hr_ref[L - SUBLANES:L, :]
    fi = hi_ref[L - SUBLANES:L, :]
    row = lax.broadcasted_iota(I32, (SUBLANES, SSM_COLS), 0)
    gr, gi = fr, fi
    pr, pi = apr_ref[1:2, :], api_ref[1:2, :]
    for s in (1, 2, 4):
        sr = jnp.where(row >= s, pltpu.roll(gr, s, axis=0), 0.0)
        si = jnp.where(row >= s, pltpu.roll(gi, s, axis=0), 0.0)
        tr, ti = _cmul(pr, pi, sr, si)
        gr, gi = gr + tr, gi + ti
        pr, pi = _cmul(pr, pi, pr, pi)
    c0r, c0i = car_ref[...], cai_ref[...]
    tr, ti = _cmul(apr_ref[0:SUBLANES, :], api_ref[0:SUBLANES, :], c0r, c0i)
    hmr = tr + jnp.where(row >= 1, pltpu.roll(gr, 1, axis=0), 0.0)
    hmi = ti + jnp.where(row >= 1, pltpu.roll(gi, 1, axis=0), 0.0)
    hmr_ref[...] = hmr
    hmi_ref[...] = hmi
    tr, ti = _cmul(apr_ref[SUBLANES:SUBLANES + 1, :], api_ref[SUBLANES:SUBLANES + 1, :], c0r, c0i)
    ncr = tr + gr[SUBLANES - 1:SUBLANES, :]
    nci = ti + gi[SUBLANES - 1:SUBLANES, :]
    car_ref[...] = ncr
    cai_ref[...] = nci
    sre_ref[...] = ncr
    sim_ref[...] = nci

    for cb in range(SSM_COLS // S5_COLS):
        cols = slice(cb * S5_COLS, (cb + 1) * S5_COLS)
        ar = jnp.broadcast_to(abr_ref[:, cols], (SUBLANES, S5_COLS))
        ai = jnp.broadcast_to(abi_ref[:, cols], (SUBLANES, S5_COLS))

        def fix_body(it, carry, cols=cols, ar=ar, ai=ai):
            dr, di = carry
            for k in range(unroll):
                r0 = pl.multiple_of((it * unroll + k) * SUBLANES, SUBLANES)
                dr, di = _cmul(ar, ai, dr, di)
                hr_ref[pl.ds(r0, SUBLANES), cols] = hr_ref[pl.ds(r0, SUBLANES), cols] + dr
                hi_ref[pl.ds(r0, SUBLANES), cols] = hi_ref[pl.ds(r0, SUBLANES), cols] + di
            return dr, di

        lax.fori_loop(0, ls // unroll, fix_body, (hmr_ref[:, cols], hmi_ref[:, cols]))

    ys = []
    for nt in range(N_KT):
        hr = hr_ref[:, nt * kcols:(nt + 1) * kcols].astype(BF16)
        hi = hi_ref[:, nt * kcols:(nt + 1) * kcols].astype(BF16)
        ys.append(jnp.dot(hr, cre_ref[nt], preferred_element_type=F32)
                  + jnp.dot(hi, cim_ref[nt], preferred_element_type=F32))
    yp = jnp.concatenate(ys, axis=-1)
    yp_hi = yp.astype(BF16)
    yp_lo = (yp - yp_hi.astype(F32)).astype(BF16)
    y = (jnp.dot(unperm, yp_hi, preferred_element_type=F32)
         + jnp.dot(unperm, yp_lo, preferred_element_type=F32)) + dsk_ref[...] * u_ref[...]
    z = _gelu_tanh(y).astype(BF16)
    mix = jnp.dot(z, wa_ref[...], preferred_element_type=F32) * _sigmoid(
        jnp.dot(z, wb_ref[...], preferred_element_type=F32))
    o_ref[...] = x_ref[...] + g1_ref[...] * mix


SH1, SC1, G1 = 0, 1, 2


def _s5_prompt(x, mod, gmix, prep, bexp, cexp, dsk, wa, wb):
    b, t, _ = x.shape
    L = S5_CHUNK
    abr, abi, apr, api = prep
    bre, bim = bexp
    cre, cim = cexp
    kcols = GROUPS_PER_K * SSM_STATE
    full = lambda shape: pl.BlockSpec(shape, lambda bi, ci: (0,) * len(shape))
    mods = [pl.BlockSpec((None, 1, D_MODEL), lambda bi, ci, k=k: (bi, 0, k)) for k in (SH1, SC1, G1)]
    return pl.pallas_call(
        functools.partial(_s5_prompt_kernel, L=L),
        grid=(b, t // L),
        in_specs=[pl.BlockSpec((None, L, D_MODEL), lambda bi, ci: (bi, ci, 0))] + mods + [
                  full((1, D_MODEL)), full((1, SSM_COLS)), full((1, SSM_COLS)),
                  full((SUBLANES + 1, SSM_COLS)), full((SUBLANES + 1, SSM_COLS)),
                  full((N_KT, MXU_K, kcols)), full((N_KT, MXU_K, kcols)),
                  full((N_KT, kcols, MXU_K)), full((N_KT, kcols, MXU_K)),
                  full((1, D_MODEL)), full((D_MODEL, D_MODEL)), full((D_MODEL, D_MODEL))],
        out_specs=[pl.BlockSpec((None, L, D_MODEL), lambda bi, ci: (bi, ci, 0)),
                   pl.BlockSpec((None, 1, SSM_COLS), lambda bi, ci: (bi, 0, 0)),
                   pl.BlockSpec((None, 1, SSM_COLS), lambda bi, ci: (bi, 0, 0))],
        out_shape=(jax.ShapeDtypeStruct((b, t, D_MODEL), F32),
                   jax.ShapeDtypeStruct((b, 1, SSM_COLS), F32),
                   jax.ShapeDtypeStruct((b, 1, SSM_COLS), F32)),
        scratch_shapes=[pltpu.VMEM((L, D_MODEL), F32),
                        pltpu.VMEM((L, SSM_COLS), F32), pltpu.VMEM((L, SSM_COLS), F32),
                        pltpu.VMEM((SUBLANES, SSM_COLS), F32), pltpu.VMEM((SUBLANES, SSM_COLS), F32),
                        pltpu.VMEM((1, SSM_COLS), F32), pltpu.VMEM((1, SSM_COLS), F32)],
        compiler_params=_cparams(("arbitrary", "arbitrary")),
        name="s5_prompt",
    )(x, mod, mod, mod, gmix, abr, abi, apr, api, bre, bim, cre, cim, dsk, wa, wb)


def _tile_rows(v, rows):
    reps = rows // v.shape[0]
    return v if reps == 1 else jnp.concatenate([v] * reps, axis=0)


def _mod_cols(m_ref, k, rows=None):
    v = m_ref[:, k * D_MODEL:(k + 1) * D_MODEL]
    return v if rows is None else _tile_rows(v, rows)


def _s5_sample_kernel(x_ref, m_ref, gmix_ref, h0r_ref, h0i_ref, abr_ref, abi_ref,
                      brh_ref, brl_ref, bih_ref, bil_ref, cre_ref, cim_ref, dsk_ref, wa_ref, wb_ref,
                      o_ref, sre_ref, sim_ref, *, steps):
    kcols = GROUPS_PER_K * SSM_STATE
    ar, ai = abr_ref[...], abi_ref[...]
    sr, si = h0r_ref[...], h0i_ref[...]
    sh, sc, g1 = _mod_cols(m_ref, SH1), _mod_cols(m_ref, SC1), _mod_cols(m_ref, G1)
    for t in range(steps):
        x = x_ref[t]
        u = _rms(x, gmix_ref[...]) * (1.0 + sc) + sh
        bur, bui = [], []
        for kt in range(N_KT):
            uk = u[:, kt * MXU_K:(kt + 1) * MXU_K]
            bur.append(_dot3(uk, brh_ref[kt], brl_ref[kt]))
            bui.append(_dot3(uk, bih_ref[kt], bil_ref[kt]))
        tr, ti = _cmul(ar, ai, sr, si)
        sr = tr + jnp.concatenate(bur, axis=-1)
        si = ti + jnp.concatenate(bui, axis=-1)
        ys = []
        for nt in range(N_KT):
            ys.append(_bdot(sr[:, nt * kcols:(nt + 1) * kcols], cre_ref[nt])
                      + _bdot(si[:, nt * kcols:(nt + 1) * kcols], cim_ref[nt]))
        y = jnp.concatenate(ys, axis=-1) + dsk_ref[...] * u
        z = _gelu_tanh(y).astype(BF16)
        mix = jnp.dot(z, wa_ref[...], preferred_element_type=F32) * _sigmoid(
            jnp.dot(z, wb_ref[...], preferred_element_type=F32))
        o_ref[t] = x + g1 * mix
    sre_ref[...] = sr
    sim_ref[...] = si


def _s5_sample(x_t, mod, gmix, h0r, h0i, abr, abi, bsplit, cexp, dsk, wa, wb):
    steps, n, _ = x_t.shape
    (brh, brl), (bih, bil) = bsplit
    cre, cim = cexp
    return pl.pallas_call(
        functools.partial(_s5_sample_kernel, steps=steps),
        out_shape=(jax.ShapeDtypeStruct((steps, n, D_MODEL), F32),
                   jax.ShapeDtypeStruct((n, SSM_COLS), F32),
                   jax.ShapeDtypeStruct((n, SSM_COLS), F32)),
        compiler_params=_cparams(),
        name="s5_sample",
    )(x_t, mod, gmix, h0r, h0i, abr, abi, brh, brl, bih, bil, cre, cim, dsk, wa, wb)


class _Tiles:
    def __init__(self, n_prompt, n_sample, seq_len, n_seq):
        assert TM % n_seq == 0
        self.n_seq = n_seq
        self.ntp = n_prompt // TM
        self.nts = n_sample // TM
        self.nt = self.ntp + self.nts
        self.tiles_per_seq = seq_len // TM
        self.n_batch = n_prompt // seq_len
        self.n_tok = n_prompt + n_sample

    def p_rows(self):
        return pl.BlockSpec((TM, D_MODEL), lambda t, *_: (jnp.minimum(t, self.ntp - 1), 0))

    def s_rows(self):
        return pl.BlockSpec((TM, D_MODEL), lambda t, *_: (jnp.maximum(t - self.ntp, 0), 0))

    def s_mod(self, chunk):
        return pl.BlockSpec((self.n_seq, D_MODEL), lambda t, *_: (0, chunk))

    def p_mod(self, chunk):
        return pl.BlockSpec((None, 1, D_MODEL),
                            lambda t, *_: (jnp.minimum(t // self.tiles_per_seq, self.n_batch - 1), 0, chunk))

    def all_rows(self, width):
        return pl.BlockSpec((TM, width), lambda t, *_: (t, 0))


def _const_spec(shape):
    return pl.BlockSpec(shape, lambda t, *_: (0,) * len(shape))


def _by_kind(tiles, body, prompt_refs, sample_refs):
    t = pl.program_id(0)

    @pl.when(t < tiles.ntp)
    def _():
        body(*prompt_refs)

    @pl.when(t >= tiles.ntp)
    def _():
        body(*sample_refs)


def _mod_rows(m_ref):
    v = m_ref[...]
    return v if v.shape[0] == 1 else _tile_rows(v, TM)


def _moe_input(x_ref, sh_ref, sc_ref, g_ref):
    return _rms(x_ref[...], g_ref[...]) * (1.0 + _mod_rows(sc_ref)) + _mod_rows(sh_ref)


def _router_kernel(xp_ref, xs_ref, shp_ref, shs_ref, scp_ref, scs_ref, g_ref, wh_ref, wl_ref, b_ref, tri_ref,
                   mi_ref, mw_ref, cnt_ref, run_ref, *, tiles):
    @pl.when(pl.program_id(0) == 0)
    def _():
        run_ref[...] = jnp.zeros_like(run_ref)

    def route(x_ref, sh_ref, sc_ref):
        h = _moe_input(x_ref, sh_ref, sc_ref, g_ref)
        logits = _dot3(h, wh_ref[...], wl_ref[...]) + b_ref[...]
        lane = lax.broadcasted_iota(I32, (TM, LANES), 1)
        lane_f = lane.astype(F32)
        vals, firsts, hots = [], [], []
        work = logits
        for _ in range(TOP_K):
            m = jnp.max(work, axis=-1, keepdims=True)
            first = jnp.min(jnp.where(work == m, lane_f, float(LANES)), axis=-1, keepdims=True)
            hot = lane_f == first
            vals.append(m)
            firsts.append(first)
            hots.append(hot)
            work = jnp.where(hot, -jnp.inf, work)
        exps = [jnp.exp(v - vals[0]) for v in vals]
        den = exps[0] + exps[1] + exps[2] + exps[3]

        chosen = jnp.zeros((TM, LANES), F32)
        for hot in hots:
            chosen = jnp.where(hot, 1.0, chosen)
        rank_all = jnp.dot(tri_ref[...], chosen.astype(BF16), preferred_element_type=F32) + run_ref[...]
        run_ref[...] = run_ref[...] + jnp.sum(chosen, axis=0, keepdims=True)
        cnt_ref[...] = run_ref[...]

        code = jnp.zeros((TM, LANES), F32)
        mw = jnp.zeros((TM, LANES), F32)
        for k in range(TOP_K):
            r_k = jnp.sum(jnp.where(hots[k], rank_all, 0.0), axis=-1, keepdims=True)
            code = jnp.where(lane == k, r_k * float(N_EXPERTS) + firsts[k], code)
            mw = jnp.where(lane == k, exps[k] / den, mw)
        mw_ref[...] = mw
        code_t = code.T
        mi_ref[...] = jnp.concatenate(
            [code_t[k:k + 1, c * LANES:(c + 1) * LANES] for k in range(TOP_K) for c in range(TM // LANES)],
            axis=0).astype(I32)

    _by_kind(tiles, route, (xp_ref, shp_ref, scp_ref), (xs_ref, shs_ref, scs_ref))


SH2, SC2, G2 = 3, 4, 5


def _router(tiles, xp, xs, mod_p, mod_s, g, w_hi, w_lo, b_pad):
    tri = jnp.asarray(np.tril(np.ones((TM, TM), np.float32), -1), BF16)
    return pl.pallas_call(
        functools.partial(_router_kernel, tiles=tiles),
        grid=(tiles.nt,),
        in_specs=[tiles.p_rows(), tiles.s_rows(), tiles.p_mod(SH2), tiles.s_mod(SH2), tiles.p_mod(SC2),
                  tiles.s_mod(SC2), _const_spec((1, D_MODEL)), _const_spec((D_MODEL, LANES)),
                  _const_spec((D_MODEL, LANES)), _const_spec((1, LANES)), _const_spec((TM, TM))],
        out_specs=[pl.BlockSpec((TOP_K * TM // LANES, LANES), lambda t: (t, 0)), tiles.all_rows(LANES),
                   _const_spec((1, LANES))],
        out_shape=(jax.ShapeDtypeStruct((tiles.n_tok * TOP_K // LANES, LANES), I32),
                   jax.ShapeDtypeStruct((tiles.n_tok, LANES), F32),
                   jax.ShapeDtypeStruct((1, LANES), F32)),
        scratch_shapes=[pltpu.VMEM((1, LANES), F32)],
        compiler_params=_cparams(("arbitrary",)),
        name="router",
    )(xp, xs, mod_p, mod_s, mod_p, mod_s, g, w_hi, w_lo, b_pad, tri)


PAD_ARMS = tuple(1 << i for i in reversed(range(int(math.log2(TM)))))


def _dispatch_kernel(slot_ref, pstart_ref, pcnt_ref,
                     xp_ref, xs_ref, shp_ref, shs_ref, scp_ref, scs_ref, g_ref,
                     o_hbm, rows_ref, zero_ref, sems, zsem, *, tiles):
    t = pl.program_id(0)
    buf = t % 2
    tile_rows = TM * ROW_TILES
    base = pl.multiple_of(buf * tile_rows, tile_rows)

    def tile_wait(b):
        b0 = pl.multiple_of(b * tile_rows, tile_rows)
        for _ in range(TOP_K):
            pltpu.make_async_copy(rows_ref.at[pl.ds(b0, tile_rows), :],
                                  o_hbm.at[pl.ds(0, tile_rows), :], sems.at[b]).wait()

    def pad_copies(act):
        def body(e, carry):
            start, cnt = pstart_ref[e], pcnt_ref[e]
            for p in PAD_ARMS:
                @pl.when((cnt & p) != 0)
                def _(start=start, p=p):
                    cp = pltpu.make_async_copy(
                        zero_ref.at[pl.ds(0, p * ROW_TILES), :],
                        o_hbm.at[pl.ds(pl.multiple_of(start * ROW_TILES, ROW_TILES), p * ROW_TILES), :],
                        zsem)
                    cp.start() if act == "start" else cp.wait()
                start = start + jnp.where((cnt & p) != 0, p, 0)
            return carry
        lax.fori_loop(0, N_EXPERTS, body, 0)

    @pl.when(t == 0)
    def _():
        zero_ref[...] = jnp.zeros_like(zero_ref)
        pad_copies("start")

    @pl.when(t >= 2)
    def _():
        tile_wait(buf)

    def stage(x_ref, sh_ref, sc_ref):
        h = _moe_input(x_ref, sh_ref, sc_ref, g_ref)
        for c in range(ROW_TILES):
            rows_ref[pl.ds(base + c, TM, stride=ROW_TILES), :] = h[:, c * LANES:(c + 1) * LANES]

    _by_kind(tiles, stage, (xp_ref, shp_ref, scp_ref), (xs_ref, shs_ref, scs_ref))

    def issue(n, carry):
        src = rows_ref.at[pl.ds(pl.multiple_of(base + n * ROW_TILES, ROW_TILES), ROW_TILES), :]
        for k in range(TOP_K):
            slot = slot_ref[(t * TOP_K + k) * TM + n]
            dst = o_hbm.at[pl.ds(pl.multiple_of(slot * ROW_TILES, ROW_TILES), ROW_TILES), :]
            pltpu.make_async_copy(src, dst, sems.at[buf]).start()
        return carry
    lax.fori_loop(0, TM, issue, 0)

    @pl.when(t == tiles.nt - 1)
    def _():
        if tiles.nt >= 2:
            tile_wait(1 - buf)
        tile_wait(buf)
        pad_copies("wait")


def _dispatch(tiles, n_slots, slots, pad_start, pad_cnt, xp, xs, mod_p, mod_s, g):
    grid_spec = pltpu.PrefetchScalarGridSpec(
        num_scalar_prefetch=3,
        grid=(tiles.nt,),
        in_specs=[tiles.p_rows(), tiles.s_rows(), tiles.p_mod(SH2), tiles.s_mod(SH2), tiles.p_mod(SC2),
                  tiles.s_mod(SC2), _const_spec((1, D_MODEL))],
        out_specs=pl.BlockSpec(memory_space=pl.ANY),
        scratch_shapes=[pltpu.VMEM((2 * TM * ROW_TILES, LANES), F32),
                        pltpu.VMEM((PAD_ARMS[0] * ROW_TILES, LANES), F32),
                        pltpu.SemaphoreType.DMA((2,)), pltpu.SemaphoreType.DMA(())],
    )
    return pl.pallas_call(
        functools.partial(_dispatch_kernel, tiles=tiles),
        grid_spec=grid_spec,
        out_shape=jax.ShapeDtypeStruct((n_slots * ROW_TILES, LANES), F32),
        compiler_params=_cparams(("arbitrary",)),
        name="dispatch",
    )(slots, pad_start, pad_cnt, xp, xs, mod_p, mod_s, mod_p, mod_s, g)


def _expert_kernel(te_ref, nx_ref, nt_ref, hs_ref, wgu_hbm, bgu_ref, wd_hbm, bd_ref, o_ref,
                   wgu_f32, wd_f32, wgu_bf, wd_bf, lhs_ref, wsem, *, layer):
    i = pl.program_id(0)

    def fetch(expert):
        return (pltpu.make_async_copy(wgu_hbm.at[layer, expert], wgu_f32, wsem.at[0]),
                pltpu.make_async_copy(wd_hbm.at[layer, expert], wd_f32, wsem.at[1]))

    @pl.when(i < nt_ref[0])
    def _():
        expert = te_ref[i]

        @pl.when(i == 0)
        def _():
            for cp in fetch(expert):
                cp.start()

        @pl.when((i == 0) | (expert != te_ref[jnp.maximum(i - 1, 0)]))
        def _():
            for cp in fetch(expert):
                cp.wait()
            wgu_bf[...] = wgu_f32[...].astype(BF16)
            wd_bf[...] = wd_f32[...].astype(BF16)
            upcoming = nx_ref[i]

            @pl.when(upcoming >= 0)
            def _():
                for cp in fetch(upcoming):
                    cp.start()

        for c in range(ROW_TILES):
            lhs_ref[:, c * LANES:(c + 1) * LANES] = hs_ref[pl.ds(c, TM, stride=ROW_TILES), :].astype(BF16)
        gu = jnp.dot(lhs_ref[...], wgu_bf[...], preferred_element_type=F32) + bgu_ref[...]
        gate = jnp.minimum(gu[:, :D_FF], SWIGLU_LIMIT)
        up = jnp.clip(gu[:, D_FF:], -SWIGLU_LIMIT, SWIGLU_LIMIT)
        act = ((up + 1.0) * gate * _sigmoid(SWIGLU_ALPHA * gate)).astype(BF16)
        y = jnp.dot(act, wd_bf[...], preferred_element_type=F32) + bd_ref[...]
        for c in range(ROW_TILES):
            o_ref[pl.ds(c, TM, stride=ROW_TILES), :] = y[:, c * LANES:(c + 1) * LANES]


def _experts(max_tiles, tile_expert, next_expert, n_tiles, hs, w_gu, b_gu, w_down, b_down, layer):
    tile_rows = TM * ROW_TILES
    row_map = lambda i, te, nx, nt: (jnp.minimum(i, nt[0] - 1), 0)
    b_map = lambda i, te, nx, nt: (layer, te[i], 0, 0)
    grid_spec = pltpu.PrefetchScalarGridSpec(
        num_scalar_prefetch=3,
        grid=(max_tiles,),
        in_specs=[pl.BlockSpec((tile_rows, LANES), row_map),
                  pl.BlockSpec(memory_space=pl.ANY),
                  pl.BlockSpec((None, None, 1, 2 * D_FF), b_map),
                  pl.BlockSpec(memory_space=pl.ANY),
                  pl.BlockSpec((None, None, 1, D_MODEL), b_map)],
        out_specs=pl.BlockSpec((tile_rows, LANES), row_map),
        scratch_shapes=[pltpu.VMEM((D_MODEL, 2 * D_FF), F32), pltpu.VMEM((D_FF, D_MODEL), F32),
                        pltpu.VMEM((D_MODEL, 2 * D_FF), BF16), pltpu.VMEM((D_FF, D_MODEL), BF16),
                        pltpu.VMEM((TM, D_MODEL), BF16), pltpu.SemaphoreType.DMA((2,))],
    )
    return pl.pallas_call(
        functools.partial(_expert_kernel, layer=layer),
        grid_spec=grid_spec,
        out_shape=jax.ShapeDtypeStruct(hs.shape, F32),
        compiler_params=_cparams(("arbitrary",)),
        name="experts",
    )(tile_expert, next_expert, n_tiles, hs, w_gu, b_gu[:, :, None, :], w_down, b_down[:, :, None, :])


def _combine_kernel(slot_ref, y_hbm, mw_ref, xp_ref, xs_ref, g2p_ref, g2s_ref, gfin_ref,
                    op_ref, os_ref, gbuf, acc_ref, sems, *, tiles, final_norm):
    t = pl.program_id(0)
    tile_rows = TM * ROW_TILES
    buf_rows = TOP_K * tile_rows

    def issue_tile(tt, b):
        b0 = pl.multiple_of(b * buf_rows, buf_rows)

        def issue(n, carry):
            for k in range(TOP_K):
                slot = slot_ref[(tt * TOP_K + k) * TM + n]
                src = y_hbm.at[pl.ds(pl.multiple_of(slot * ROW_TILES, ROW_TILES), ROW_TILES), :]
                dst = gbuf.at[pl.ds(pl.multiple_of(b0 + k * tile_rows + n * ROW_TILES, ROW_TILES),
                                    ROW_TILES), :]
                pltpu.make_async_copy(src, dst, sems.at[b]).start()
            return carry
        lax.fori_loop(0, TM, issue, 0)

    @pl.when(t == 0)
    def _():
        issue_tile(0, 0)

    @pl.when(t + 1 < tiles.nt)
    def _():
        issue_tile(t + 1, (t + 1) % 2)

    buf = t % 2
    b0 = pl.multiple_of(buf * buf_rows, buf_rows)
    for _ in range(TOP_K):
        pltpu.make_async_copy(y_hbm.at[pl.ds(0, tile_rows), :],
                              gbuf.at[pl.ds(b0, tile_rows), :], sems.at[buf]).wait()

    w = mw_ref[...]
    for c in range(ROW_TILES):
        acc = None
        for k in range(TOP_K):
            rows = gbuf[pl.ds(b0 + k * tile_rows + c, TM, stride=ROW_TILES), :]
            term = w[:, k:k + 1] * rows
            acc = term if acc is None else acc + term
        acc_ref[:, c * LANES:(c + 1) * LANES] = acc

    def finish(x_ref, g2_ref, o_ref):
        res = x_ref[...] + _mod_rows(g2_ref) * acc_ref[...]
        if final_norm:
            res = _rms(res, gfin_ref[...])
        o_ref[...] = res

    _by_kind(tiles, finish, (xp_ref, g2p_ref, op_ref), (xs_ref, g2s_ref, os_ref))


def _combine(tiles, slots, y_sorted, mw, xp, xs, mod_p, mod_s, gfin, final_norm):
    grid_spec = pltpu.PrefetchScalarGridSpec(
        num_scalar_prefetch=1,
        grid=(tiles.nt,),
        in_specs=[pl.BlockSpec(memory_space=pl.ANY), tiles.all_rows(LANES), tiles.p_rows(), tiles.s_rows(),
                  tiles.p_mod(G2), tiles.s_mod(G2), _const_spec((1, D_MODEL))],
        out_specs=[tiles.p_rows(), tiles.s_rows()],
        scratch_shapes=[pltpu.VMEM((2 * TOP_K * TM * ROW_TILES, LANES), F32),
                        pltpu.VMEM((TM, D_MODEL), F32), pltpu.SemaphoreType.DMA((2,))],
    )
    return pl.pallas_call(
        functools.partial(_combine_kernel, tiles=tiles, final_norm=final_norm),
        grid_spec=grid_spec,
        out_shape=(jax.ShapeDtypeStruct(xp.shape, F32), jax.ShapeDtypeStruct(xs.shape, F32)),
        compiler_params=_cparams(("arbitrary",)),
        name="combine",
    )(slots, y_sorted, mw, xp, xs, mod_p, mod_s, gfin)


def _moe_layer(tiles, xp, xs, mod_p, mod_s, g_ffn, w_router, b_router, w_gu, b_gu, w_down, b_down, gfin,
               layer, final_norm):
    n_tok = tiles.n_tok
    w_pad = jnp.pad(w_router, ((0, 0), (0, LANES - N_EXPERTS)))
    w_hi, w_lo = _split_bf16(w_pad)
    b_pad = jnp.pad(b_router, (0, LANES - N_EXPERTS), constant_values=NEG_INF).reshape(1, LANES)
    mi, mw, counts = _router(tiles, xp, xs, mod_p, mod_s, g_ffn, w_hi, w_lo, b_pad)

    cnt = counts[0, :N_EXPERTS].astype(I32)
    tiles_e = (cnt + TM - 1) // TM
    tile_end = jnp.cumsum(tiles_e)
    tile_start = tile_end - tiles_e
    first_slot = jnp.sum(jnp.where((mi & (N_EXPERTS - 1))[..., None] == jnp.arange(N_EXPERTS, dtype=I32),
                                   (tile_start * TM)[None, None, :], 0), axis=-1)
    slots = (first_slot + (mi >> int(math.log2(N_EXPERTS)))).reshape(n_tok * TOP_K)
    max_tiles = (n_tok * TOP_K) // TM + N_EXPERTS
    n_tiles = tile_end[-1:].astype(I32)
    tile_ids = jnp.arange(max_tiles, dtype=I32)
    tile_expert = jnp.minimum(jnp.sum(tile_ids[:, None] >= tile_end[None, :], axis=-1), N_EXPERTS - 1).astype(I32)
    group_end = jnp.sum(jnp.where(tile_expert[:, None] == jnp.arange(N_EXPERTS, dtype=I32), tile_end[None, :], 0),
                        axis=-1)
    after = jnp.minimum(jnp.sum(group_end[:, None] >= tile_end[None, :], axis=-1), N_EXPERTS - 1).astype(I32)
    next_expert = jnp.where(group_end < n_tiles[0], after, -1).astype(I32)
    pad_start = (tile_start * TM + cnt).astype(I32)
    pad_cnt = (tiles_e * TM - cnt).astype(I32)

    hs = _dispatch(tiles, max_tiles * TM, slots, pad_start, pad_cnt, xp, xs, mod_p, mod_s, g_ffn)
    ys = _experts(max_tiles, tile_expert, next_expert, n_tiles, hs, w_gu, b_gu, w_down, b_down, layer)
    return _combine(tiles, slots, ys, mw, xp, xs, mod_p, mod_s, gfin, final_norm)


def _slopes():
    return [2.0 ** (-8.0 * (h + 1) / N_HEADS) for h in range(N_HEADS)]


def _attn_prompt_kernel(sink_ref, x_ref, shkv_ref, sckv_ref, sh_ref, sc_ref, g1_ref, gkv_ref, gmix_ref,
                        wkv_ref, bkv_ref, wq_ref, bq_ref, wo_ref,
                        o_ref, kl_ref, vl_ref, kk_ref, vv_ref, oh_ref):
    j = pl.program_id(1)
    blk = WINDOW

    @pl.when(j == 0)
    def _():
        kk_ref[...] = jnp.zeros_like(kk_ref)
        vv_ref[...] = jnp.zeros_like(vv_ref)

    @pl.when(j > 0)
    def _():
        kk_ref[0:blk, :] = kk_ref[blk:2 * blk, :]
        vv_ref[0:blk, :] = vv_ref[blk:2 * blk, :]

    x = x_ref[...]
    hkv = _rms(x, gkv_ref[...]) * (1.0 + sckv_ref[...]) + shkv_ref[...]
    kv = _bdot(hkv, wkv_ref[...]) + bkv_ref[...]
    kl_ref[...] = kv[:, :KV_DIM]
    vl_ref[...] = kv[:, KV_DIM:]
    kk_ref[blk:2 * blk, :] = kv[:, :KV_DIM].astype(BF16)
    vv_ref[blk:2 * blk, :] = kv[:, KV_DIM:].astype(BF16)

    h = _rms(x, gmix_ref[...]) * (1.0 + sc_ref[...]) + sh_ref[...]
    q = _bdot(h, wq_ref[...]) + bq_ref[...]

    r = lax.broadcasted_iota(I32, (blk, 2 * blk), 0)
    c = lax.broadcasted_iota(I32, (blk, 2 * blk), 1)
    dist_i = r + blk - c
    valid = (dist_i >= 0) & (dist_i < WINDOW) & jnp.logical_not((j == 0) & (c < blk))
    dist = dist_i.astype(F32)
    slopes = _slopes()
    for g in range(N_KV_HEADS):
        kg = kk_ref[:, g * HEAD_DIM:(g + 1) * HEAD_DIM]
        vg = vv_ref[:, g * HEAD_DIM:(g + 1) * HEAD_DIM]
        for qh in range(Q_PER_KV):
            hd = g * Q_PER_KV + qh
            qd = q[:, hd * HEAD_DIM:(hd + 1) * HEAD_DIM].astype(BF16)
            s = lax.dot_general(qd, kg, (((1,), (1,)), ((), ())), preferred_element_type=F32)
            s = s * (HEAD_DIM ** -0.5) - slopes[hd] * dist
            s = jnp.where(valid, s, NEG_INF)
            sink = sink_ref[hd]
            m = jnp.maximum(jnp.max(s, axis=-1, keepdims=True), sink)
            e = jnp.exp(s - m)
            den = jnp.sum(e, axis=-1, keepdims=True) + jnp.exp(sink - m)
            p = (e / den).astype(BF16)
            oh_ref[:, hd * HEAD_DIM:(hd + 1) * HEAD_DIM] = jnp.dot(p, vg, preferred_element_type=F32)
    mix = _bdot(oh_ref[...], wo_ref[...])
    o_ref[...] = x + g1_ref[...] * mix


SHKV, SCKV = 0, 1


def _attn_prompt(x, mod_kv, mod, gkv, gmix, wkv, bkv, wq, bq, wo, sinks):
    b, t, _ = x.shape
    blk = WINDOW
    qd = N_HEADS * HEAD_DIM
    full = lambda shape: pl.BlockSpec(shape, lambda bi, ji, *_: (0,) * len(shape))
    mods = [pl.BlockSpec((None, 1, D_MODEL), lambda bi, ji, *_, k=k: (bi, 0, k))
            for k in (SHKV, SCKV, SH1, SC1, G1)]
    rows = pl.BlockSpec((None, blk, D_MODEL), lambda bi, ji, *_: (bi, ji, 0))
    last = pl.BlockSpec((None, blk, KV_DIM), lambda bi, ji, *_: (bi, 0, 0))
    grid_spec = pltpu.PrefetchScalarGridSpec(
        num_scalar_prefetch=1,
        grid=(b, t // blk),
        in_specs=[rows] + mods + [full((1, D_MODEL)), full((1, D_MODEL)),
                  full((D_MODEL, 2 * KV_DIM)), full((1, 2 * KV_DIM)), full((D_MODEL, qd)), full((1, qd)),
                  full((qd, D_MODEL))],
        out_specs=[rows, last, last],
        scratch_shapes=[pltpu.VMEM((2 * blk, KV_DIM), BF16), pltpu.VMEM((2 * blk, KV_DIM), BF16),
                        pltpu.VMEM((blk, qd), F32)],
    )
    return pl.pallas_call(
        _attn_prompt_kernel,
        grid_spec=grid_spec,
        out_shape=(jax.ShapeDtypeStruct((b, t, D_MODEL), F32),
                   jax.ShapeDtypeStruct((b, blk, KV_DIM), F32),
                   jax.ShapeDtypeStruct((b, blk, KV_DIM), F32)),
        compiler_params=_cparams(("arbitrary", "arbitrary")),
        name="attn_prompt",
    )(sinks, x, mod_kv, mod_kv, mod, mod, mod, gkv, gmix, wkv, bkv, wq, bq, wo)


def _qkv_sample_kernel(x_ref, mkv_ref, m_ref, gkv_ref, gmix_ref,
                       wkv_ref, bkv_ref, wq_ref, bq_ref, q_ref, k_ref, v_ref):
    x = x_ref[...]
    n = x.shape[0]
    hkv = _rms(x, gkv_ref[...]) * (1.0 + _mod_cols(mkv_ref, SCKV, n)) + _mod_cols(mkv_ref, SHKV, n)
    kv = _bdot(hkv, wkv_ref[...]) + bkv_ref[...]
    k_ref[...] = kv[:, :KV_DIM]
    v_ref[...] = kv[:, KV_DIM:]
    h = _rms(x, gmix_ref[...]) * (1.0 + _mod_cols(m_ref, SC1, n)) + _mod_cols(m_ref, SH1, n)
    q_ref[...] = _bdot(h, wq_ref[...]) + bq_ref[...]


def _qkv_sample(x, mod_kv, mod, gkv, gmix, wkv, bkv, wq, bq):
    n = x.shape[0]
    return pl.pallas_call(
        _qkv_sample_kernel,
        out_shape=(jax.ShapeDtypeStruct((n, N_HEADS * HEAD_DIM), F32),
                   jax.ShapeDtypeStruct((n, KV_DIM), F32), jax.ShapeDtypeStruct((n, KV_DIM), F32)),
        compiler_params=_cparams(),
        name="qkv_sample",
    )(x, mod_kv, mod, gkv, gmix, wkv, bkv, wq, bq)


def _attn_sample_kernel(q_ref, ck_ref, cv_ref, nk_ref, nv_ref, slope_ref, sink_ref, o_ref, *, steps):
    rows = steps * Q_PER_KV
    sb = q_ref.shape[0]
    r_c = lax.broadcasted_iota(I32, (sb, rows, WINDOW), 1) // Q_PER_KV
    j_c = lax.broadcasted_iota(I32, (sb, rows, WINDOW), 2)
    dist_c = r_c + WINDOW - j_c
    valid_c = (dist_c >= 0) & (dist_c < WINDOW)
    r_n = lax.broadcasted_iota(I32, (sb, rows, steps), 1) // Q_PER_KV
    j_n = lax.broadcasted_iota(I32, (sb, rows, steps), 2)
    dist_n = r_n - j_n
    valid_n = (dist_n >= 0) & (dist_n < WINDOW)
    scale = HEAD_DIM ** -0.5
    for g in range(N_KV_HEADS):
        lanes = slice(g * HEAD_DIM, (g + 1) * HEAD_DIM)
        qg = q_ref[:, g].astype(BF16)
        slope = slope_ref[g][None]
        sink = sink_ref[g][None]
        kc, vc = ck_ref[:, :, lanes].astype(BF16), cv_ref[:, :, lanes].astype(BF16)
        kn, vn = nk_ref[:, :, lanes].astype(BF16), nv_ref[:, :, lanes].astype(BF16)
        s_c = jnp.einsum('nrd,njd->nrj', qg, kc, preferred_element_type=F32) * scale
        s_n = jnp.einsum('nrd,njd->nrj', qg, kn, preferred_element_type=F32) * scale
        s_c = jnp.where(valid_c, s_c - slope * dist_c.astype(F32), NEG_INF)
        s_n = jnp.where(valid_n, s_n - slope * dist_n.astype(F32), NEG_INF)
        m = jnp.maximum(jnp.maximum(jnp.max(s_c, axis=-1, keepdims=True),
                                    jnp.max(s_n, axis=-1, keepdims=True)), sink)
        e_c = jnp.exp(s_c - m)
        e_n = jnp.exp(s_n - m)
        den = (jnp.sum(e_c, axis=-1, keepdims=True) + jnp.sum(e_n, axis=-1, keepdims=True)
               + jnp.exp(sink - m))
        o = (jnp.einsum('nrj,njd->nrd', (e_c / den).astype(BF16), vc, preferred_element_type=F32)
             + jnp.einsum('nrj,njd->nrd', (e_n / den).astype(BF16), vn, preferred_element_type=F32))
        o_ref[:, g] = o


def _attn_sample(qg, cache_k, cache_v, k_new, v_new, slope_rows, sink_rows):
    n, _, rows, _ = qg.shape
    steps = rows // Q_PER_KV
    sb = 8
    blk = lambda shape: pl.BlockSpec((sb,) + shape, lambda i: (i,) + (0,) * len(shape))
    full = lambda shape: pl.BlockSpec(shape, lambda i: (0,) * len(shape))
    return pl.pallas_call(
        functools.partial(_attn_sample_kernel, steps=steps),
        grid=(n // sb,),
        in_specs=[blk((N_KV_HEADS, rows, HEAD_DIM)), blk((WINDOW, KV_DIM)), blk((WINDOW, KV_DIM)),
                  blk((steps, KV_DIM)), blk((steps, KV_DIM)),
                  full((N_KV_HEADS, rows, 1)), full((N_KV_HEADS, rows, 1))],
        out_specs=blk((N_KV_HEADS, rows, HEAD_DIM)),
        out_shape=jax.ShapeDtypeStruct(qg.shape, F32),
        compiler_params=_cparams(("arbitrary",)),
        name="attn_sample",
    )(qg, cache_k, cache_v, k_new, v_new, slope_rows, sink_rows)


def _oproj_kernel(o_ref, x_ref, m_ref, wo_ref, y_ref):
    y_ref[...] = x_ref[...] + _mod_cols(m_ref, G1, x_ref.shape[0]) * _bdot(o_ref[...], wo_ref[...])


def _oproj(o, x, mod, wo):
    return pl.pallas_call(
        _oproj_kernel,
        out_shape=jax.ShapeDtypeStruct(x.shape, F32),
        compiler_params=_cparams(),
        name="oproj_sample",
    )(o, x, mod, wo)


def kernel(x_prompt, x_sample, state_ssm_re, state_ssm_im, cache_k, cache_v, c_prompt, c_sample, g_mix, g_ffn, w_ada, b_ada, ssm_a_re, ssm_a_im, ssm_log_dt, ssm_b_re, ssm_b_im, ssm_c_re, ssm_c_im, ssm_d, w_glu_a, w_glu_b, g_kv, w_ada_kv, b_ada_kv, w_kv, b_kv, w_q, b_q, w_o, attn_sinks, w_router, b_router, w_gu, b_gu, w_down, b_down, g_final):
    bsz, seq, d = x_prompt.shape
    n_seq, steps, _ = x_sample.shape
    assert d == D_MODEL and seq % S5_CHUNK == 0 and seq % TM == 0 and (n_seq * steps) % TM == 0
    assert g_mix.shape[0] == 2 and ssm_a_re.shape[0] == 1 and w_q.shape[0] == 1
    n_p, n_s = bsz * seq, n_seq * steps
    tiles = _Tiles(n_p, n_s, seq, n_seq)
    row = lambda v: v.reshape(1, -1)

    c_all = jnp.concatenate([c_prompt, c_sample], axis=0)
    c_rows = -(-c_all.shape[0] // SUBLANES) * SUBLANES
    c_all = jnp.pad(c_all, ((0, c_rows - c_all.shape[0]), (0, 0)))
    mods = [_ada(c_all, w_ada, b_ada, l) for l in range(2)]
    mod_kv = _ada(c_all, w_ada_kv[None], b_ada_kv[None], 0)

    def split(m):
        return m[:bsz, None, :], m[bsz:bsz + n_seq]

    mod_p, mod_s = split(mods[0])
    abr, abi, bbr, bbi, apr, api = _ssm_prep(ssm_a_re[0], ssm_a_im[0], ssm_log_dt[0], ssm_b_re[0],
                                             ssm_b_im[0], S5_CHUNK // SUBLANES)
    b_re_x, b_im_x = _expand_b(bbr), _expand_b(bbi)
    cexp = (_expand_c(ssm_c_re[0]).astype(BF16), _expand_c(-ssm_c_im[0]).astype(BF16))
    wa, wb = w_glu_a[0].astype(BF16), w_glu_b[0].astype(BF16)
    xp, sre_p, sim_p = _s5_prompt(x_prompt, mod_p, row(g_mix[0]), (abr, abi, apr, api),
                                  (b_re_x.astype(BF16), b_im_x.astype(BF16)), cexp, row(ssm_d[0]), wa, wb)
    xs_t, sre_s, sim_s = _s5_sample(
        x_sample.transpose(1, 0, 2), mod_s, row(g_mix[0]),
        state_ssm_re[0].reshape(n_seq, SSM_COLS), state_ssm_im[0].reshape(n_seq, SSM_COLS), abr, abi,
        (_split_bf16(b_re_x), _split_bf16(b_im_x)), cexp, row(ssm_d[0]), wa, wb)
    xp = xp.reshape(n_p, D_MODEL)
    xs = xs_t.reshape(n_s, D_MODEL)
    xp, xs = _moe_layer(tiles, xp, xs, mod_p, mod_s, row(g_ffn[0]), w_router[0], b_router[0], w_gu, b_gu,
                        w_down, b_down, row(g_final), 0, False)

    mod_p, mod_s = split(mods[1])
    modkv_p, modkv_s = split(mod_kv)
    wkv, wq, wo = w_kv.astype(BF16), w_q[0].astype(BF16), w_o[0].astype(BF16)
    xp3, k_p, v_p = _attn_prompt(xp.reshape(bsz, seq, D_MODEL), modkv_p, mod_p, row(g_kv), row(g_mix[1]), wkv,
                                 row(b_kv), wq, row(b_q[0]), wo, attn_sinks[0])
    xp = xp3.reshape(n_p, D_MODEL)

    q_s, k_s, v_s = _qkv_sample(xs, modkv_s, mod_s, row(g_kv), row(g_mix[1]), wkv, row(b_kv), wq, row(b_q[0]))
    rows = steps * Q_PER_KV
    qg = q_s.reshape(steps, n_seq, N_KV_HEADS, Q_PER_KV, HEAD_DIM).transpose(1, 2, 0, 3, 4).reshape(
        n_seq, N_KV_HEADS, rows, HEAD_DIM)
    head_of_row = (np.arange(N_KV_HEADS)[:, None] * Q_PER_KV + np.arange(rows)[None, :] % Q_PER_KV)
    slope_rows = jnp.asarray(np.asarray(_slopes(), np.float32)[head_of_row][..., None])
    sink_rows = attn_sinks[0][head_of_row][..., None]
    k_new = k_s.reshape(steps, n_seq, KV_DIM).transpose(1, 0, 2)
    v_new = v_s.reshape(steps, n_seq, KV_DIM).transpose(1, 0, 2)
    ck = cache_k.reshape(n_seq, WINDOW, KV_DIM)
    cv = cache_v.reshape(n_seq, WINDOW, KV_DIM)
    og = _attn_sample(qg, ck, cv, k_new, v_new, slope_rows, sink_rows)
    o_s = og.reshape(n_seq, N_KV_HEADS, steps, Q_PER_KV, HEAD_DIM).transpose(2, 0, 1, 3, 4).reshape(
        n_s, N_HEADS * HEAD_DIM)
    xs = _oproj(o_s, xs, mod_s, wo)

    yp, ys = _moe_layer(tiles, xp, xs, mod_p, mod_s, row(g_ffn[1]), w_router[1], b_router[1], w_gu, b_gu,
                        w_down, b_down, row(g_final), 1, True)

    shape4 = lambda a: a.reshape(a.shape[0], a.shape[1], N_KV_HEADS, HEAD_DIM)
    state = lambda a, n: a.reshape(1, n, SSM_GROUPS, SSM_STATE)
    return (yp.reshape(bsz, seq, D_MODEL), ys.reshape(steps, n_seq, D_MODEL).transpose(1, 0, 2),
            state(sre_p, bsz), state(sim_p, bsz), shape4(k_p), shape4(v_p),
            state(sre_s, n_seq), state(sim_s, n_seq),
            jnp.concatenate([cache_k[:, steps:], shape4(k_new)], axis=1),
            jnp.concatenate([cache_v[:, steps:], shape4(v_new)], axis=1))
```

```python
import functools
import math

import numpy as np
import jax
import jax.numpy as jnp
from jax import lax
from jax.experimental import pallas as pl
from jax.experimental.pallas import tpu as pltpu

F32 = jnp.float32
BF16 = jnp.bfloat16
I32 = jnp.int32

D_MODEL = 1024
SSM_GROUP = 16
SSM_GROUPS = D_MODEL // SSM_GROUP
SSM_STATE = 64
SSM_COLS = SSM_GROUPS * SSM_STATE
N_HEADS = 16
HEAD_DIM = 64
N_KV_HEADS = 4
Q_PER_KV = N_HEADS // N_KV_HEADS
KV_DIM = N_KV_HEADS * HEAD_DIM
WINDOW = 128
N_EXPERTS = 32
TOP_K = 4
D_FF = D_MODEL
SWIGLU_LIMIT = 7.0
SWIGLU_ALPHA = 1.702
RMS_EPS = 1e-5
NEG_INF = -1e30

LANES = 128
SUBLANES = 8
ROW_TILES = D_MODEL // LANES
TM = 256
S5_CHUNK = 256
S5_COLS = 512
MXU_K = 256
GROUPS_PER_K = MXU_K // SSM_GROUP
N_KT = D_MODEL // MXU_K
VMEM_LIMIT = 56 * 1024 * 1024


def _cparams(sem=None):
    return pltpu.CompilerParams(dimension_semantics=sem, vmem_limit_bytes=VMEM_LIMIT)


def _sigmoid(x):
    return 1.0 / (1.0 + jnp.exp(-x))


def _rms(x, g):
    return x * lax.rsqrt(jnp.mean(x * x, axis=-1, keepdims=True) + RMS_EPS) * g


def _bdot(a, b):
    return jnp.dot(a.astype(BF16), b.astype(BF16), preferred_element_type=F32)


def _gelu_tanh(x):
    return 0.5 * x * (1.0 + jnp.tanh(math.sqrt(2.0 / math.pi) * (x + 0.044715 * (x * x * x))))


def _split_bf16(w):
    hi = w.astype(BF16)
    lo = (w - hi.astype(F32)).astype(BF16)
    return hi, lo


def _dot3(a, b_hi, b_lo):
    a_hi = a.astype(BF16)
    a_lo = (a - a_hi.astype(F32)).astype(BF16)
    return (jnp.dot(a_hi, b_hi, preferred_element_type=F32)
            + jnp.dot(a_lo, b_hi, preferred_element_type=F32)
            + jnp.dot(a_hi, b_lo, preferred_element_type=F32))


def _ada_kernel(c_ref, w_ref, b_ref, o_ref):
    c = c_ref[...]
    o_ref[...] = _bdot(c * _sigmoid(c), w_ref[...]) + b_ref[...]


def _ada(c, w, b, layer):
    rows, cols = c.shape[0], w.shape[2]
    tn = 1024
    return pl.pallas_call(
        _ada_kernel,
        grid=(cols // tn,),
        in_specs=[pl.BlockSpec((rows, D_MODEL), lambda j: (0, 0)),
                  pl.BlockSpec((None, D_MODEL, tn), lambda j: (layer, 0, j)),
                  pl.BlockSpec((None, 1, tn), lambda j: (layer, 0, j))],
        out_specs=pl.BlockSpec((rows, tn), lambda j: (0, j)),
        out_shape=jax.ShapeDtypeStruct((rows, cols), F32),
        compiler_params=_cparams(("arbitrary",)),
        name="ada",
    )(c, w, b.reshape(b.shape[0], 1, cols))


def _ssm_prep_kernel(are_ref, aim_ref, ldt_ref, bre_ref, bim_ref,
                     abr_ref, abi_ref, bbr_ref, bbi_ref, apr_ref, api_ref, *, sub_len):
    a_re, a_im = are_ref[...], aim_ref[...]
    dt = jnp.exp(ldt_ref[...])
    mag = jnp.exp(a_re * dt)
    ab_re = mag * jnp.cos(a_im * dt)
    ab_im = mag * jnp.sin(a_im * dt)
    abr_ref[...] = ab_re
    abi_ref[...] = ab_im
    den = a_re * a_re + a_im * a_im
    x, y = ab_re - 1.0, ab_im
    f_re = (x * a_re + y * a_im) / den
    f_im = (y * a_re - x * a_im) / den
    b_re, b_im = bre_ref[...], bim_ref[...]
    bbr_ref[...] = f_re * b_re - f_im * b_im
    bbi_ref[...] = f_re * b_im + f_im * b_re
    pr, pi = ab_re, ab_im
    for _ in range(int(math.log2(sub_len))):
        pr, pi = pr * pr - pi * pi, 2.0 * pr * pi
    cr, ci = jnp.ones_like(pr), jnp.zeros_like(pi)
    for j in range(SUBLANES + 1):
        apr_ref[j:j + 1, :] = cr
        api_ref[j:j + 1, :] = ci
        cr, ci = cr * pr - ci * pi, cr * pi + ci * pr


def _ssm_prep(a_re, a_im, log_dt, b_re, b_im, sub_len):
    gp = SSM_COLS
    ldt = jnp.broadcast_to(log_dt[:, None], (SSM_GROUPS, SSM_STATE)).reshape(1, gp)
    b_re_t = b_re.reshape(gp, SSM_GROUP).T
    b_im_t = b_im.reshape(gp, SSM_GROUP).T
    row = jax.ShapeDtypeStruct((1, gp), F32)
    mat = jax.ShapeDtypeStruct((SSM_GROUP, gp), F32)
    pw = jax.ShapeDtypeStruct((SUBLANES + 1, gp), F32)
    return pl.pallas_call(
        functools.partial(_ssm_prep_kernel, sub_len=sub_len),
        out_shape=(row, row, mat, mat, pw, pw),
        compiler_params=_cparams(),
        name="ssm_prep",
    )(a_re.reshape(1, gp), a_im.reshape(1, gp), ldt, b_re_t, b_im_t)


def _expand_b(bbar):
    eye = jnp.eye(GROUPS_PER_K, dtype=F32)
    b = bbar.reshape(SSM_GROUP, N_KT, GROUPS_PER_K, SSM_STATE).transpose(1, 0, 2, 3)
    e = b[:, None, :, :, :] * eye[None, :, None, :, None]
    return e.reshape(N_KT, MXU_K, GROUPS_PER_K * SSM_STATE)


def _expand_c(c):
    eye = jnp.eye(GROUPS_PER_K, dtype=F32)
    cc = c.reshape(N_KT, GROUPS_PER_K, SSM_GROUP, SSM_STATE).transpose(0, 1, 3, 2)
    e = cc[:, :, :, None, :] * eye[None, :, None, :, None]
    return e.reshape(N_KT, GROUPS_PER_K * SSM_STATE, MXU_K)


def _cmul(ar, ai, br, bi):
    return ar * br - ai * bi, ar * bi + ai * br


def _s5_prompt_kernel(x_ref, sh_ref, sc_ref, g1_ref, gmix_ref, abr_ref, abi_ref, apr_ref, api_ref,
                      bre_ref, bim_ref, cre_ref, cim_ref, dsk_ref, wa_ref, wb_ref, perm_ref, unperm_ref,
                      o_ref, sre_ref, sim_ref,
                      u_ref, hr_ref, hi_ref, hmr_ref, hmi_ref, car_ref, cai_ref, *, L):
    ls = L // SUBLANES
    kcols = GROUPS_PER_K * SSM_STATE

    @pl.when(pl.program_id(1) == 0)
    def _():
        car_ref[...] = jnp.zeros_like(car_ref)
        cai_ref[...] = jnp.zeros_like(cai_ref)

    x = x_ref[...]
    u = _rms(x, gmix_ref[...]) * (1.0 + sc_ref[...]) + sh_ref[...]
    u_ref[...] = u
    ub = jnp.dot(perm_ref[...], u.astype(BF16), preferred_element_type=F32).astype(BF16)
    for kt in range(N_KT):
        uk = ub[:, kt * MXU_K:(kt + 1) * MXU_K]
        hr_ref[:, kt * kcols:(kt + 1) * kcols] = jnp.dot(uk, bre_ref[kt], preferred_element_type=F32)
        hi_ref[:, kt * kcols:(kt + 1) * kcols] = jnp.dot(uk, bim_ref[kt], preferred_element_type=F32)

    for cb in range(SSM_COLS // S5_COLS):
        cols = slice(cb * S5_COLS, (cb + 1) * S5_COLS)
        ar = jnp.broadcast_to(abr_ref[:, cols], (SUBLANES, S5_COLS))
        ai = jnp.broadcast_to(abi_ref[:, cols], (SUBLANES, S5_COLS))

        sr = si = jnp.zeros((SUBLANES, S5_COLS), F32)
        for i in range(ls):
            rows = slice(i * SUBLANES, (i + 1) * SUBLANES)
            tr, ti = _cmul(ar, ai, sr, si)
            sr = tr + hr_ref[rows, cols]
            si = ti + hi_ref[rows, cols]
            hr_ref[rows, cols] = sr
            hi_ref[rows, cols] = si

    fr = hr_ref[L - SUBLANES:L, :]
    fi = hi_ref[L - SUBLANES:L, :]
    row = lax.broadcasted_iota(I32, (SUBLANES, SSM_COLS), 0)
    gr, gi = fr, fi
    pr, pi = apr_ref[1:2, :], api_ref[1:2, :]
    for s in (1, 2, 4):
        sr = jnp.where(row >= s, pltpu.roll(gr, s, axis=0), 0.0)
        si = jnp.where(row >= s, pltpu.roll(gi, s, axis=0), 0.0)
        tr, ti = _cmul(pr, pi, sr, si)
        gr, gi = gr + tr, gi + ti
        pr, pi = _cmul(pr, pi, pr, pi)
    c0r, c0i = car_ref[...], cai_ref[...]
    tr, ti = _cmul(apr_ref[0:SUBLANES, :], api_ref[0:SUBLANES, :], c0r, c0i)
    hmr = tr + jnp.where(row >= 1, pltpu.roll(gr, 1, axis=0), 0.0)
    hmi = ti + jnp.where(row >= 1, pltpu.roll(gi, 1, axis=0), 0.0)
    hmr_ref[...] = hmr
    hmi_ref[...] = hmi
    tr, ti = _cmul(apr_ref[SUBLANES:SUBLANES + 1, :], api_ref[SUBLANES:SUBLANES + 1, :], c0r, c0i)
    ncr = tr + gr[SUBLANES - 1:SUBLANES, :]
    nci = ti + gi[SUBLANES - 1:SUBLANES, :]
    car_ref[...] = ncr
    cai_ref[...] = nci
    sre_ref[...] = ncr
    sim_ref[...] = nci

    for cb in range(SSM_COLS // S5_COLS):
        cols = slice(cb * S5_COLS, (cb + 1) * S5_COLS)
        ar = jnp.broadcast_to(abr_ref[:, cols], (SUBLANES, S5_COLS))
        ai = jnp.broadcast_to(abi_ref[:, cols], (SUBLANES, S5_COLS))

        dr, di = hmr_ref[:, cols], hmi_ref[:, cols]
        for i in range(ls):
            rows = slice(i * SUBLANES, (i + 1) * SUBLANES)
            dr, di = _cmul(ar, ai, dr, di)
            hr_ref[rows, cols] = hr_ref[rows, cols] + dr
            hi_ref[rows, cols] = hi_ref[rows, cols] + di

    ys = []
    for nt in range(N_KT):
        hr = hr_ref[:, nt * kcols:(nt + 1) * kcols].astype(BF16)
        hi = hi_ref[:, nt * kcols:(nt + 1) * kcols].astype(BF16)
        ys.append(jnp.dot(hr, cre_ref[nt], preferred_element_type=F32)
                  + jnp.dot(hi, cim_ref[nt], preferred_element_type=F32))
    yp = jnp.concatenate(ys, axis=-1)
    yp_hi = yp.astype(BF16)
    yp_lo = (yp - yp_hi.astype(F32)).astype(BF16)
    unperm = unperm_ref[...]
    y = (jnp.dot(unperm, yp_hi, preferred_element_type=F32)
         + jnp.dot(unperm, yp_lo, preferred_element_type=F32)) + dsk_ref[...] * u_ref[...]
    z = _gelu_tanh(y).astype(BF16)
    mix = jnp.dot(z, wa_ref[...], preferred_element_type=F32) * _sigmoid(
        jnp.dot(z, wb_ref[...], preferred_element_type=F32))
    o_ref[...] = x_ref[...] + g1_ref[...] * mix


SH1, SC1, G1 = 0, 1, 2


def _s5_prompt(x, mod, gmix, prep, bexp, cexp, dsk, wa, wb):
    b, t, _ = x.shape
    L = S5_CHUNK
    abr, abi, apr, api = prep
    bre, bim = bexp
    cre, cim = cexp
    kcols = GROUPS_PER_K * SSM_STATE
    full = lambda shape: pl.BlockSpec(shape, lambda bi, ci: (0,) * len(shape))
    mods = [pl.BlockSpec((None, 1, D_MODEL), lambda bi, ci, k=k: (bi, 0, k)) for k in (SH1, SC1, G1)]
    r = np.arange(L)
    perm_np = np.zeros((L, L), np.float32)
    perm_np[r, (r % SUBLANES) * (L // SUBLANES) + r // SUBLANES] = 1.0
    perm, unperm = jnp.asarray(perm_np, BF16), jnp.asarray(perm_np.T, BF16)
    return pl.pallas_call(
        functools.partial(_s5_prompt_kernel, L=L),
        grid=(b, t // L),
        in_specs=[pl.BlockSpec((None, L, D_MODEL), lambda bi, ci: (bi, ci, 0))] + mods + [
                  full((1, D_MODEL)), full((1, SSM_COLS)), full((1, SSM_COLS)),
                  full((SUBLANES + 1, SSM_COLS)), full((SUBLANES + 1, SSM_COLS)),
                  full((N_KT, MXU_K, kcols)), full((N_KT, MXU_K, kcols)),
                  full((N_KT, kcols, MXU_K)), full((N_KT, kcols, MXU_K)),
                  full((1, D_MODEL)), full((D_MODEL, D_MODEL)), full((D_MODEL, D_MODEL)),
                  full((L, L)), full((L, L))],
        out_specs=[pl.BlockSpec((None, L, D_MODEL), lambda bi, ci: (bi, ci, 0)),
                   pl.BlockSpec((None, 1, SSM_COLS), lambda bi, ci: (bi, 0, 0)),
                   pl.BlockSpec((None, 1, SSM_COLS), lambda bi, ci: (bi, 0, 0))],
        out_shape=(jax.ShapeDtypeStruct((b, t, D_MODEL), F32),
                   jax.ShapeDtypeStruct((b, 1, SSM_COLS), F32),
                   jax.ShapeDtypeStruct((b, 1, SSM_COLS), F32)),
        scratch_shapes=[pltpu.VMEM((L, D_MODEL), F32),
                        pltpu.VMEM((L, SSM_COLS), F32), pltpu.VMEM((L, SSM_COLS), F32),
                        pltpu.VMEM((SUBLANES, SSM_COLS), F32), pltpu.VMEM((SUBLANES, SSM_COLS), F32),
                        pltpu.VMEM((1, SSM_COLS), F32), pltpu.VMEM((1, SSM_COLS), F32)],
        compiler_params=_cparams(("arbitrary", "arbitrary")),
        name="s5_prompt",
    )(x, mod, mod, mod, gmix, abr, abi, apr, api, bre, bim, cre, cim, dsk, wa, wb, perm, unperm)


def _tile_rows(v, rows):
    reps = rows // v.shape[0]
    return v if reps == 1 else jnp.concatenate([v] * reps, axis=0)


def _mod_cols(m_ref, k, rows=None):
    v = m_ref[:, k * D_MODEL:(k + 1) * D_MODEL]
    return v if rows is None else _tile_rows(v, rows)


def _s5_sample_kernel(x_ref, m_ref, gmix_ref, h0r_ref, h0i_ref, abr_ref, abi_ref,
                      brh_ref, brl_ref, bih_ref, bil_ref, cre_ref, cim_ref, dsk_ref, wa_ref, wb_ref,
                      o_ref, sre_ref, sim_ref, *, steps):
    kcols = GROUPS_PER_K * SSM_STATE
    ar, ai = abr_ref[...], abi_ref[...]
    sr, si = h0r_ref[...], h0i_ref[...]
    sh, sc, g1 = _mod_cols(m_ref, SH1), _mod_cols(m_ref, SC1), _mod_cols(m_ref, G1)
    for t in range(steps):
        x = x_ref[t]
        u = _rms(x, gmix_ref[...]) * (1.0 + sc) + sh
        bur, bui = [], []
        for kt in range(N_KT):
            uk = u[:, kt * MXU_K:(kt + 1) * MXU_K]
            bur.append(_dot3(uk, brh_ref[kt], brl_ref[kt]))
            bui.append(_dot3(uk, bih_ref[kt], bil_ref[kt]))
        tr, ti = _cmul(ar, ai, sr, si)
        sr = tr + jnp.concatenate(bur, axis=-1)
        si = ti + jnp.concatenate(bui, axis=-1)
        ys = []
        for nt in range(N_KT):
            ys.append(_bdot(sr[:, nt * kcols:(nt + 1) * kcols], cre_ref[nt])
                      + _bdot(si[:, nt * kcols:(nt + 1) * kcols], cim_ref[nt]))
        y = jnp.concatenate(ys, axis=-1) + dsk_ref[...] * u
        z = _gelu_tanh(y).astype(BF16)
        mix = jnp.dot(z, wa_ref[...], preferred_element_type=F32) * _sigmoid(
            jnp.dot(z, wb_ref[...], preferred_element_type=F32))
        o_ref[t] = x + g1 * mix
    sre_ref[...] = sr
    sim_ref[...] = si


def _s5_sample(x_t, mod, gmix, h0r, h0i, abr, abi, bsplit, cexp, dsk, wa, wb):
    steps, n, _ = x_t.shape
    (brh, brl), (bih, bil) = bsplit
    cre, cim = cexp
    return pl.pallas_call(
        functools.partial(_s5_sample_kernel, steps=steps),
        out_shape=(jax.ShapeDtypeStruct((steps, n, D_MODEL), F32),
                   jax.ShapeDtypeStruct((n, SSM_COLS), F32),
                   jax.ShapeDtypeStruct((n, SSM_COLS), F32)),
        compiler_params=_cparams(),
        name="s5_sample",
    )(x_t, mod, gmix, h0r, h0i, abr, abi, brh, brl, bih, bil, cre, cim, dsk, wa, wb)


class _Tiles:
    def __init__(self, n_prompt, n_sample, seq_len, n_seq):
        assert TM % n_seq == 0
        self.n_seq = n_seq
        self.ntp = n_prompt // TM
        self.nts = n_sample // TM
        self.nt = self.ntp + self.nts
        self.tiles_per_seq = seq_len // TM
        self.n_batch = n_prompt // seq_len
        self.n_tok = n_prompt + n_sample

    def p_rows(self):
        return pl.BlockSpec((TM, D_MODEL), lambda t, *_: (jnp.minimum(t, self.ntp - 1), 0))

    def s_rows(self):
        return pl.BlockSpec((TM, D_MODEL), lambda t, *_: (jnp.maximum(t - self.ntp, 0), 0))

    def s_mod(self, chunk):
        return pl.BlockSpec((self.n_seq, D_MODEL), lambda t, *_: (0, chunk))

    def p_mod(self, chunk):
        return pl.BlockSpec((None, 1, D_MODEL),
                            lambda t, *_: (jnp.minimum(t // self.tiles_per_seq, self.n_batch - 1), 0, chunk))

    def all_rows(self, width):
        return pl.BlockSpec((TM, width), lambda t, *_: (t, 0))


def _const_spec(shape):
    return pl.BlockSpec(shape, lambda t, *_: (0,) * len(shape))


def _by_kind(tiles, body, prompt_refs, sample_refs):
    t = pl.program_id(0)

    @pl.when(t < tiles.ntp)
    def _():
        body(*prompt_refs)

    @pl.when(t >= tiles.ntp)
    def _():
        body(*sample_refs)


def _mod_rows(m_ref):
    v = m_ref[...]
    return v if v.shape[0] == 1 else _tile_rows(v, TM)


def _moe_input(x_ref, sh_ref, sc_ref, g_ref):
    return _rms(x_ref[...], g_ref[...]) * (1.0 + _mod_rows(sc_ref)) + _mod_rows(sh_ref)


def _router_kernel(xp_ref, xs_ref, shp_ref, shs_ref, scp_ref, scs_ref, g_ref, wh_ref, wl_ref, b_ref, tri_ref,
                   mi_ref, mw_ref, cnt_ref, run_ref, *, tiles):
    @pl.when(pl.program_id(0) == 0)
    def _():
        run_ref[...] = jnp.zeros_like(run_ref)

    def route(x_ref, sh_ref, sc_ref):
        h = _moe_input(x_ref, sh_ref, sc_ref, g_ref)
        logits = _dot3(h, wh_ref[...], wl_ref[...]) + b_ref[...]
        lane = lax.broadcasted_iota(I32, (TM, LANES), 1)
        lane_f = lane.astype(F32)
        vals, firsts, hots = [], [], []
        work = logits
        for _ in range(TOP_K):
            m = jnp.max(work, axis=-1, keepdims=True)
            first = jnp.min(jnp.where(work == m, lane_f, float(LANES)), axis=-1, keepdims=True)
            hot = lane_f == first
            vals.append(m)
            firsts.append(first)
            hots.append(hot)
            work = jnp.where(hot, -jnp.inf, work)
        exps = [jnp.exp(v - vals[0]) for v in vals]
        den = exps[0] + exps[1] + exps[2] + exps[3]

        chosen = jnp.zeros((TM, LANES), F32)
        for hot in hots:
            chosen = jnp.where(hot, 1.0, chosen)
        rank_all = jnp.dot(tri_ref[...], chosen.astype(BF16), preferred_element_type=F32) + run_ref[...]
        run_ref[...] = run_ref[...] + jnp.sum(chosen, axis=0, keepdims=True)
        cnt_ref[...] = run_ref[...]

        code = jnp.zeros((TM, LANES), F32)
        mw = jnp.zeros((TM, LANES), F32)
        for k in range(TOP_K):
            r_k = jnp.sum(jnp.where(hots[k], rank_all, 0.0), axis=-1, keepdims=True)
            code = jnp.where(lane == k, r_k * float(N_EXPERTS) + firsts[k], code)
            mw = jnp.where(lane == k, exps[k] / den, mw)
        mw_ref[...] = mw
        code_t = code.T
        mi_ref[...] = jnp.concatenate(
            [code_t[k:k + 1, c * LANES:(c + 1) * LANES] for k in range(TOP_K) for c in range(TM // LANES)],
            axis=0).astype(I32)

    _by_kind(tiles, route, (xp_ref, shp_ref, scp_ref), (xs_ref, shs_ref, scs_ref))


SH2, SC2, G2 = 3, 4, 5


def _router(tiles, xp, xs, mod_p, mod_s, g, w_hi, w_lo, b_pad):
    tri = jnp.asarray(np.tril(np.ones((TM, TM), np.float32), -1), BF16)
    return pl.pallas_call(
        functools.partial(_router_kernel, tiles=tiles),
        grid=(tiles.nt,),
        in_specs=[tiles.p_rows(), tiles.s_rows(), tiles.p_mod(SH2), tiles.s_mod(SH2), tiles.p_mod(SC2),
                  tiles.s_mod(SC2), _const_spec((1, D_MODEL)), _const_spec((D_MODEL, LANES)),
                  _const_spec((D_MODEL, LANES)), _const_spec((1, LANES)), _const_spec((TM, TM))],
        out_specs=[pl.BlockSpec((TOP_K * TM // LANES, LANES), lambda t: (t, 0)), tiles.all_rows(LANES),
                   _const_spec((1, LANES))],
        out_shape=(jax.ShapeDtypeStruct((tiles.n_tok * TOP_K // LANES, LANES), I32),
                   jax.ShapeDtypeStruct((tiles.n_tok, LANES), F32),
                   jax.ShapeDtypeStruct((1, LANES), F32)),
        scratch_shapes=[pltpu.VMEM((1, LANES), F32)],
        compiler_params=_cparams(("arbitrary",)),
        name="router",
    )(xp, xs, mod_p, mod_s, mod_p, mod_s, g, w_hi, w_lo, b_pad, tri)


PAD_ARMS = tuple(1 << i for i in reversed(range(int(math.log2(TM)))))


def _dispatch_kernel(slot_ref, pstart_ref, pcnt_ref,
                     xp_ref, xs_ref, shp_ref, shs_ref, scp_ref, scs_ref, g_ref,
                     o_hbm, rows_ref, zero_ref, sems, zsem, *, tiles):
    t = pl.program_id(0)
    buf = t % 2
    tile_rows = TM * ROW_TILES
    base = pl.multiple_of(buf * tile_rows, tile_rows)

    def tile_wait(b):
        b0 = pl.multiple_of(b * tile_rows, tile_rows)
        for _ in range(TOP_K):
            pltpu.make_async_copy(rows_ref.at[pl.ds(b0, tile_rows), :],
                                  o_hbm.at[pl.ds(0, tile_rows), :], sems.at[b]).wait()

    def pad_copies(act):
        def body(e, carry):
            start, cnt = pstart_ref[e], pcnt_ref[e]
            for p in PAD_ARMS:
                @pl.when((cnt & p) != 0)
                def _(start=start, p=p):
                    cp = pltpu.make_async_copy(
                        zero_ref.at[pl.ds(0, p * ROW_TILES), :],
                        o_hbm.at[pl.ds(pl.multiple_of(start * ROW_TILES, ROW_TILES), p * ROW_TILES), :],
                        zsem)
                    cp.start() if act == "start" else cp.wait()
                start = start + jnp.where((cnt & p) != 0, p, 0)
            return carry
        lax.fori_loop(0, N_EXPERTS, body, 0)

    @pl.when(t == 0)
    def _():
        zero_ref[...] = jnp.zeros_like(zero_ref)
        pad_copies("start")

    @pl.when(t >= 2)
    def _():
        tile_wait(buf)

    def stage(x_ref, sh_ref, sc_ref):
        h = _moe_input(x_ref, sh_ref, sc_ref, g_ref)
        for c in range(ROW_TILES):
            rows_ref[pl.ds(base + c, TM, stride=ROW_TILES), :] = h[:, c * LANES:(c + 1) * LANES]

    _by_kind(tiles, stage, (xp_ref, shp_ref, scp_ref), (xs_ref, shs_ref, scs_ref))

    def issue(n, carry):
        src = rows_ref.at[pl.ds(pl.multiple_of(base + n * ROW_TILES, ROW_TILES), ROW_TILES), :]
        for k in range(TOP_K):
            slot = slot_ref[(t * TOP_K + k) * TM + n]
            dst = o_hbm.at[pl.ds(pl.multiple_of(slot * ROW_TILES, ROW_TILES), ROW_TILES), :]
            pltpu.make_async_copy(src, dst, sems.at[buf]).start()
        return carry
    lax.fori_loop(0, TM, issue, 0)

    @pl.when(t == tiles.nt - 1)
    def _():
        if tiles.nt >= 2:
            tile_wait(1 - buf)
        tile_wait(buf)
        pad_copies("wait")


def _dispatch(tiles, n_slots, slots, pad_start, pad_cnt, xp, xs, mod_p, mod_s, g):
    grid_spec = pltpu.PrefetchScalarGridSpec(
        num_scalar_prefetch=3,
        grid=(tiles.nt,),
        in_specs=[tiles.p_rows(), tiles.s_rows(), tiles.p_mod(SH2), tiles.s_mod(SH2), tiles.p_mod(SC2),
                  tiles.s_mod(SC2), _const_spec((1, D_MODEL))],
        out_specs=pl.BlockSpec(memory_space=pl.ANY),
        scratch_shapes=[pltpu.VMEM((2 * TM * ROW_TILES, LANES), F32),
                        pltpu.VMEM((PAD_ARMS[0] * ROW_TILES, LANES), F32),
                        pltpu.SemaphoreType.DMA((2,)), pltpu.SemaphoreType.DMA(())],
    )
    return pl.pallas_call(
        functools.partial(_dispatch_kernel, tiles=tiles),
        grid_spec=grid_spec,
        out_shape=jax.ShapeDtypeStruct((n_slots * ROW_TILES, LANES), F32),
        compiler_params=_cparams(("arbitrary",)),
        name="dispatch",
    )(slots, pad_start, pad_cnt, xp, xs, mod_p, mod_s, mod_p, mod_s, g)


def _expert_kernel(te_ref, nx_ref, nt_ref, hs_ref, wgu_hbm, bgu_ref, wd_hbm, bd_ref, o_ref,
                   wgu_f32, wd_f32, wgu_bf, wd_bf, lhs_ref, wsem, *, layer):
    i = pl.program_id(0)

    def fetch(expert):
        return (pltpu.make_async_copy(wgu_hbm.at[layer, expert], wgu_f32, wsem.at[0]),
                pltpu.make_async_copy(wd_hbm.at[layer, expert], wd_f32, wsem.at[1]))

    @pl.when(i < nt_ref[0])
    def _():
        expert = te_ref[i]

        @pl.when(i == 0)
        def _():
            for cp in fetch(expert):
                cp.start()

        @pl.when((i == 0) | (expert != te_ref[jnp.maximum(i - 1, 0)]))
        def _():
            for cp in fetch(expert):
                cp.wait()
            wgu_bf[...] = wgu_f32[...].astype(BF16)
            wd_bf[...] = wd_f32[...].astype(BF16)
            upcoming = nx_ref[i]

            @pl.when(upcoming >= 0)
            def _():
                for cp in fetch(upcoming):
                    cp.start()

        for c in range(ROW_TILES):
            lhs_ref[:, c * LANES:(c + 1) * LANES] = hs_ref[pl.ds(c, TM, stride=ROW_TILES), :].astype(BF16)
        gu = jnp.dot(lhs_ref[...], wgu_bf[...], preferred_element_type=F32) + bgu_ref[...]
        gate = jnp.minimum(gu[:, :D_FF], SWIGLU_LIMIT)
        up = jnp.clip(gu[:, D_FF:], -SWIGLU_LIMIT, SWIGLU_LIMIT)
        act = ((up + 1.0) * gate * _sigmoid(SWIGLU_ALPHA * gate)).astype(BF16)
        y = jnp.dot(act, wd_bf[...], preferred_element_type=F32) + bd_ref[...]
        for c in range(ROW_TILES):
            o_ref[pl.ds(c, TM, stride=ROW_TILES), :] = y[:, c * LANES:(c + 1) * LANES]


def _experts(max_tiles, tile_expert, next_expert, n_tiles, hs, w_gu, b_gu, w_down, b_down, layer):
    tile_rows = TM * ROW_TILES
    row_map = lambda i, te, nx, nt: (jnp.minimum(i, nt[0] - 1), 0)
    b_map = lambda i, te, nx, nt: (layer, te[i], 0, 0)
    grid_spec = pltpu.PrefetchScalarGridSpec(
        num_scalar_prefetch=3,
        grid=(max_tiles,),
        in_specs=[pl.BlockSpec((tile_rows, LANES), row_map),
                  pl.BlockSpec(memory_space=pl.ANY),
                  pl.BlockSpec((None, None, 1, 2 * D_FF), b_map),
                  pl.BlockSpec(memory_space=pl.ANY),
                  pl.BlockSpec((None, None, 1, D_MODEL), b_map)],
        out_specs=pl.BlockSpec((tile_rows, LANES), row_map),
        scratch_shapes=[pltpu.VMEM((D_MODEL, 2 * D_FF), F32), pltpu.VMEM((D_FF, D_MODEL), F32),
                        pltpu.VMEM((D_MODEL, 2 * D_FF), BF16), pltpu.VMEM((D_FF, D_MODEL), BF16),
                        pltpu.VMEM((TM, D_MODEL), BF16), pltpu.SemaphoreType.DMA((2,))],
    )
    return pl.pallas_call(
        functools.partial(_expert_kernel, layer=layer),
        grid_spec=grid_spec,
        out_shape=jax.ShapeDtypeStruct(hs.shape, F32),
        compiler_params=_cparams(("arbitrary",)),
        name="experts",
    )(tile_expert, next_expert, n_tiles, hs, w_gu, b_gu[:, :, None, :], w_down, b_down[:, :, None, :])


def _combine_kernel(slot_ref, y_hbm, mw_ref, xp_ref, xs_ref, g2p_ref, g2s_ref, gfin_ref,
                    op_ref, os_ref, gbuf, acc_ref, sems, *, tiles, final_norm):
    t = pl.program_id(0)
    tile_rows = TM * ROW_TILES
    buf_rows = TOP_K * tile_rows

    def issue_tile(tt, b):
        b0 = pl.multiple_of(b * buf_rows, buf_rows)

        def issue(n, carry):
            for k in range(TOP_K):
                slot = slot_ref[(tt * TOP_K + k) * TM + n]
                src = y_hbm.at[pl.ds(pl.multiple_of(slot * ROW_TILES, ROW_TILES), ROW_TILES), :]
                dst = gbuf.at[pl.ds(pl.multiple_of(b0 + k * tile_rows + n * ROW_TILES, ROW_TILES),
                                    ROW_TILES), :]
                pltpu.make_async_copy(src, dst, sems.at[b]).start()
            return carry
        lax.fori_loop(0, TM, issue, 0)

    @pl.when(t == 0)
    def _():
        issue_tile(0, 0)

    @pl.when(t + 1 < tiles.nt)
    def _():
        issue_tile(t + 1, (t + 1) % 2)

    buf = t % 2
    b0 = pl.multiple_of(buf * buf_rows, buf_rows)
    for _ in range(TOP_K):
        pltpu.make_async_copy(y_hbm.at[pl.ds(0, tile_rows), :],
                              gbuf.at[pl.ds(b0, tile_rows), :], sems.at[buf]).wait()

    w = mw_ref[...]
    for c in range(ROW_TILES):
        acc = None
        for k in range(TOP_K):
            rows = gbuf[pl.ds(b0 + k * tile_rows + c, TM, stride=ROW_TILES), :]
            term = w[:, k:k + 1] * rows
            acc = term if acc is None else acc + term
        acc_ref[:, c * LANES:(c + 1) * LANES] = acc

    def finish(x_ref, g2_ref, o_ref):
        res = x_ref[...] + _mod_rows(g2_ref) * acc_ref[...]
        if final_norm:
            res = _rms(res, gfin_ref[...])
        o_ref[...] = res

    _by_kind(tiles, finish, (xp_ref, g2p_ref, op_ref), (xs_ref, g2s_ref, os_ref))


def _combine(tiles, slots, y_sorted, mw, xp, xs, mod_p, mod_s, gfin, final_norm):
    grid_spec = pltpu.PrefetchScalarGridSpec(
        num_scalar_prefetch=1,
        grid=(tiles.nt,),
        in_specs=[pl.BlockSpec(memory_space=pl.ANY), tiles.all_rows(LANES), tiles.p_rows(), tiles.s_rows(),
                  tiles.p_mod(G2), tiles.s_mod(G2), _const_spec((1, D_MODEL))],
        out_specs=[tiles.p_rows(), tiles.s_rows()],
        scratch_shapes=[pltpu.VMEM((2 * TOP_K * TM * ROW_TILES, LANES), F32),
                        pltpu.VMEM((TM, D_MODEL), F32), pltpu.SemaphoreType.DMA((2,))],
    )
    return pl.pallas_call(
        functools.partial(_combine_kernel, tiles=tiles, final_norm=final_norm),
        grid_spec=grid_spec,
        out_shape=(jax.ShapeDtypeStruct(xp.shape, F32), jax.ShapeDtypeStruct(xs.shape, F32)),
        compiler_params=_cparams(("arbitrary",)),
        name="combine",
    )(slots, y_sorted, mw, xp, xs, mod_p, mod_s, gfin)


def _moe_layer(tiles, xp, xs, mod_p, mod_s, g_ffn, w_router, b_router, w_gu, b_gu, w_down, b_down, gfin,
               layer, final_norm):
    n_tok = tiles.n_tok
    w_pad = jnp.pad(w_router, ((0, 0), (0, LANES - N_EXPERTS)))
    w_hi, w_lo = _split_bf16(w_pad)
    b_pad = jnp.pad(b_router, (0, LANES - N_EXPERTS), constant_values=NEG_INF).reshape(1, LANES)
    mi, mw, counts = _router(tiles, xp, xs, mod_p, mod_s, g_ffn, w_hi, w_lo, b_pad)

    cnt = counts[0, :N_EXPERTS].astype(I32)
    tiles_e = (cnt + TM - 1) // TM
    tile_end = jnp.cumsum(tiles_e)
    tile_start = tile_end - tiles_e
    first_slot = jnp.sum(jnp.where((mi & (N_EXPERTS - 1))[..., None] == jnp.arange(N_EXPERTS, dtype=I32),
                                   (tile_start * TM)[None, None, :], 0), axis=-1)
    slots = (first_slot + (mi >> int(math.log2(N_EXPERTS)))).reshape(n_tok * TOP_K)
    max_tiles = (n_tok * TOP_K) // TM + N_EXPERTS
    n_tiles = tile_end[-1:].astype(I32)
    tile_ids = jnp.arange(max_tiles, dtype=I32)
    tile_expert = jnp.minimum(jnp.sum(tile_ids[:, None] >= tile_end[None, :], axis=-1), N_EXPERTS - 1).astype(I32)
    group_end = jnp.sum(jnp.where(tile_expert[:, None] == jnp.arange(N_EXPERTS, dtype=I32), tile_end[None, :], 0),
                        axis=-1)
    after = jnp.minimum(jnp.sum(group_end[:, None] >= tile_end[None, :], axis=-1), N_EXPERTS - 1).astype(I32)
    next_expert = jnp.where(group_end < n_tiles[0], after, -1).astype(I32)
    pad_start = (tile_start * TM + cnt).astype(I32)
    pad_cnt = (tiles_e * TM - cnt).astype(I32)

    hs = _dispatch(tiles, max_tiles * TM, slots, pad_start, pad_cnt, xp, xs, mod_p, mod_s, g_ffn)
    ys = _experts(max_tiles, tile_expert, next_expert, n_tiles, hs, w_gu, b_gu, w_down, b_down, layer)
    return _combine(tiles, slots, ys, mw, xp, xs, mod_p, mod_s, gfin, final_norm)


def _slopes():
    return [2.0 ** (-8.0 * (h + 1) / N_HEADS) for h in range(N_HEADS)]


def _attn_prompt_kernel(sink_ref, x_ref, shkv_ref, sckv_ref, sh_ref, sc_ref, g1_ref, gkv_ref, gmix_ref,
                        wkv_ref, bkv_ref, wq_ref, bq_ref, wo_ref,
                        o_ref, kl_ref, vl_ref, kk_ref, vv_ref, oh_ref):
    j = pl.program_id(1)
    blk = WINDOW

    @pl.when(j == 0)
    def _():
        kk_ref[...] = jnp.zeros_like(kk_ref)
        vv_ref[...] = jnp.zeros_like(vv_ref)

    @pl.when(j > 0)
    def _():
        kk_ref[0:blk, :] = kk_ref[blk:2 * blk, :]
        vv_ref[0:blk, :] = vv_ref[blk:2 * blk, :]

    x = x_ref[...]
    hkv = _rms(x, gkv_ref[...]) * (1.0 + sckv_ref[...]) + shkv_ref[...]
    kv = _bdot(hkv, wkv_ref[...]) + bkv_ref[...]
    kl_ref[...] = kv[:, :KV_DIM]
    vl_ref[...] = kv[:, KV_DIM:]
    kk_ref[blk:2 * blk, :] = kv[:, :KV_DIM].astype(BF16)
    vv_ref[blk:2 * blk, :] = kv[:, KV_DIM:].astype(BF16)

    h = _rms(x, gmix_ref[...]) * (1.0 + sc_ref[...]) + sh_ref[...]
    q = _bdot(h, wq_ref[...]) + bq_ref[...]

    r = lax.broadcasted_iota(I32, (blk, 2 * blk), 0)
    c = lax.broadcasted_iota(I32, (blk, 2 * blk), 1)
    dist_i = r + blk - c
    valid = (dist_i >= 0) & (dist_i < WINDOW) & jnp.logical_not((j == 0) & (c < blk))
    dist = dist_i.astype(F32)
    slopes = _slopes()
    for g in range(N_KV_HEADS):
        kg = kk_ref[:, g * HEAD_DIM:(g + 1) * HEAD_DIM]
        vg = vv_ref[:, g * HEAD_DIM:(g + 1) * HEAD_DIM]
        for qh in range(Q_PER_KV):
            hd = g * Q_PER_KV + qh
            qd = q[:, hd * HEAD_DIM:(hd + 1) * HEAD_DIM].astype(BF16)
            s = lax.dot_general(qd, kg, (((1,), (1,)), ((), ())), preferred_element_type=F32)
            s = s * (HEAD_DIM ** -0.5) - slopes[hd] * dist
            s = jnp.where(valid, s, NEG_INF)
            sink = sink_ref[hd]
            m = jnp.maximum(jnp.max(s, axis=-1, keepdims=True), sink)
            e = jnp.exp(s - m)
            den = jnp.sum(e, axis=-1, keepdims=True) + jnp.exp(sink - m)
            p = (e / den).astype(BF16)
            oh_ref[:, hd * HEAD_DIM:(hd + 1) * HEAD_DIM] = jnp.dot(p, vg, preferred_element_type=F32)
    mix = _bdot(oh_ref[...], wo_ref[...])
    o_ref[...] = x + g1_ref[...] * mix


SHKV, SCKV = 0, 1


def _attn_prompt(x, mod_kv, mod, gkv, gmix, wkv, bkv, wq, bq, wo, sinks):
    b, t, _ = x.shape
    blk = WINDOW
    qd = N_HEADS * HEAD_DIM
    full = lambda shape: pl.BlockSpec(shape, lambda bi, ji, *_: (0,) * len(shape))
    mods = [pl.BlockSpec((None, 1, D_MODEL), lambda bi, ji, *_, k=k: (bi, 0, k))
            for k in (SHKV, SCKV, SH1, SC1, G1)]
    rows = pl.BlockSpec((None, blk, D_MODEL), lambda bi, ji, *_: (bi, ji, 0))
    last = pl.BlockSpec((None, blk, KV_DIM), lambda bi, ji, *_: (bi, 0, 0))
    grid_spec = pltpu.PrefetchScalarGridSpec(
        num_scalar_prefetch=1,
        grid=(b, t // blk),
        in_specs=[rows] + mods + [full((1, D_MODEL)), full((1, D_MODEL)),
                  full((D_MODEL, 2 * KV_DIM)), full((1, 2 * KV_DIM)), full((D_MODEL, qd)), full((1, qd)),
                  full((qd, D_MODEL))],
        out_specs=[rows, last, last],
        scratch_shapes=[pltpu.VMEM((2 * blk, KV_DIM), BF16), pltpu.VMEM((2 * blk, KV_DIM), BF16),
                        pltpu.VMEM((blk, qd), F32)],
    )
    return pl.pallas_call(
        _attn_prompt_kernel,
        grid_spec=grid_spec,
        out_shape=(jax.ShapeDtypeStruct((b, t, D_MODEL), F32),
                   jax.ShapeDtypeStruct((b, blk, KV_DIM), F32),
                   jax.ShapeDtypeStruct((b, blk, KV_DIM), F32)),
        compiler_params=_cparams(("arbitrary", "arbitrary")),
        name="attn_prompt",
    )(sinks, x, mod_kv, mod_kv, mod, mod, mod, gkv, gmix, wkv, bkv, wq, bq, wo)


def _qkv_sample_kernel(x_ref, mkv_ref, m_ref, gkv_ref, gmix_ref,
                       wkv_ref, bkv_ref, wq_ref, bq_ref, q_ref, k_ref, v_ref):
    x = x_ref[...]
    n = x.shape[0]
    hkv = _rms(x, gkv_ref[...]) * (1.0 + _mod_cols(mkv_ref, SCKV, n)) + _mod_cols(mkv_ref, SHKV, n)
    kv = _bdot(hkv, wkv_ref[...]) + bkv_ref[...]
    k_ref[...] = kv[:, :KV_DIM]
    v_ref[...] = kv[:, KV_DIM:]
    h = _rms(x, gmix_ref[...]) * (1.0 + _mod_cols(m_ref, SC1, n)) + _mod_cols(m_ref, SH1, n)
    q_ref[...] = _bdot(h, wq_ref[...]) + bq_ref[...]


def _qkv_sample(x, mod_kv, mod, gkv, gmix, wkv, bkv, wq, bq):
    n = x.shape[0]
    return pl.pallas_call(
        _qkv_sample_kernel,
        out_shape=(jax.ShapeDtypeStruct((n, N_HEADS * HEAD_DIM), F32),
                   jax.ShapeDtypeStruct((n, KV_DIM), F32), jax.ShapeDtypeStruct((n, KV_DIM), F32)),
        compiler_params=_cparams(),
        name="qkv_sample",
    )(x, mod_kv, mod, gkv, gmix, wkv, bkv, wq, bq)


def _attn_sample_kernel(q_ref, ck_ref, cv_ref, nk_ref, nv_ref, slope_ref, sink_ref, o_ref, *, steps):
    rows = steps * Q_PER_KV
    sb = q_ref.shape[0]
    r_c = lax.broadcasted_iota(I32, (sb, rows, WINDOW), 1) // Q_PER_KV
    j_c = lax.broadcasted_iota(I32, (sb, rows, WINDOW), 2)
    dist_c = r_c + WINDOW - j_c
    valid_c = (dist_c >= 0) & (dist_c < WINDOW)
    r_n = lax.broadcasted_iota(I32, (sb, rows, steps), 1) // Q_PER_KV
    j_n = lax.broadcasted_iota(I32, (sb, rows, steps), 2)
    dist_n = r_n - j_n
    valid_n = (dist_n >= 0) & (dist_n < WINDOW)
    scale = HEAD_DIM ** -0.5
    for g in range(N_KV_HEADS):
        lanes = slice(g * HEAD_DIM, (g + 1) * HEAD_DIM)
        qg = q_ref[:, g].astype(BF16)
        slope = slope_ref[g][None]
        sink = sink_ref[g][None]
        kc, vc = ck_ref[:, :, lanes].astype(BF16), cv_ref[:, :, lanes].astype(BF16)
        kn, vn = nk_ref[:, :, lanes].astype(BF16), nv_ref[:, :, lanes].astype(BF16)
        s_c = jnp.einsum('nrd,njd->nrj', qg, kc, preferred_element_type=F32) * scale
        s_n = jnp.einsum('nrd,njd->nrj', qg, kn, preferred_element_type=F32) * scale
        s_c = jnp.where(valid_c, s_c - slope * dist_c.astype(F32), NEG_INF)
        s_n = jnp.where(valid_n, s_n - slope * dist_n.astype(F32), NEG_INF)
        m = jnp.maximum(jnp.maximum(jnp.max(s_c, axis=-1, keepdims=True),
                                    jnp.max(s_n, axis=-1, keepdims=True)), sink)
        e_c = jnp.exp(s_c - m)
        e_n = jnp.exp(s_n - m)
        den = (jnp.sum(e_c, axis=-1, keepdims=True) + jnp.sum(e_n, axis=-1, keepdims=True)
               + jnp.exp(sink - m))
        o = (jnp.einsum('nrj,njd->nrd', (e_c / den).astype(BF16), vc, preferred_element_type=F32)
             + jnp.einsum('nrj,njd->nrd', (e_n / den).astype(BF16), vn, preferred_element_type=F32))
        o_ref[:, g] = o


def _attn_sample(qg, cache_k, cache_v, k_new, v_new, slope_rows, sink_rows):
    n, _, rows, _ = qg.shape
    steps = rows // Q_PER_KV
    sb = math.gcd(n, 16)
    blk = lambda shape: pl.BlockSpec((sb,) + shape, lambda i: (i,) + (0,) * len(shape))
    full = lambda shape: pl.BlockSpec(shape, lambda i: (0,) * len(shape))
    return pl.pallas_call(
        functools.partial(_attn_sample_kernel, steps=steps),
        grid=(n // sb,),
        in_specs=[blk((N_KV_HEADS, rows, HEAD_DIM)), blk((WINDOW, KV_DIM)), blk((WINDOW, KV_DIM)),
                  blk((steps, KV_DIM)), blk((steps, KV_DIM)),
                  full((N_KV_HEADS, rows, 1)), full((N_KV_HEADS, rows, 1))],
        out_specs=blk((N_KV_HEADS, rows, HEAD_DIM)),
        out_shape=jax.ShapeDtypeStruct(qg.shape, F32),
        compiler_params=_cparams(("arbitrary",)),
        name="attn_sample",
    )(qg, cache_k, cache_v, k_new, v_new, slope_rows, sink_rows)


def _oproj_kernel(o_ref, x_ref, m_ref, wo_ref, y_ref):
    y_ref[...] = x_ref[...] + _mod_cols(m_ref, G1, x_ref.shape[0]) * _bdot(o_ref[...], wo_ref[...])


def _oproj(o, x, mod, wo):
    return pl.pallas_call(
        _oproj_kernel,
        out_shape=jax.ShapeDtypeStruct(x.shape, F32),
        compiler_params=_cparams(),
        name="oproj_sample",
    )(o, x, mod, wo)


def kernel(x_prompt, x_sample, state_ssm_re, state_ssm_im, cache_k, cache_v, c_prompt, c_sample, g_mix, g_ffn, w_ada, b_ada, ssm_a_re, ssm_a_im, ssm_log_dt, ssm_b_re, ssm_b_im, ssm_c_re, ssm_c_im, ssm_d, w_glu_a, w_glu_b, g_kv, w_ada_kv, b_ada_kv, w_kv, b_kv, w_q, b_q, w_o, attn_sinks, w_router, b_router, w_gu, b_gu, w_down, b_down, g_final):
    bsz, seq, d = x_prompt.shape
    n_seq, steps, _ = x_sample.shape
    assert d == D_MODEL and seq % S5_CHUNK == 0 and seq % TM == 0 and (n_seq * steps) % TM == 0
    assert g_mix.shape[0] == 2 and ssm_a_re.shape[0] == 1 and w_q.shape[0] == 1
    n_p, n_s = bsz * seq, n_seq * steps
    tiles = _Tiles(n_p, n_s, seq, n_seq)
    row = lambda v: v.reshape(1, -1)

    c_all = jnp.concatenate([c_prompt, c_sample], axis=0)
    c_rows = -(-c_all.shape[0] // SUBLANES) * SUBLANES
    c_all = jnp.pad(c_all, ((0, c_rows - c_all.shape[0]), (0, 0)))
    mods = [_ada(c_all, w_ada, b_ada, l) for l in range(2)]
    mod_kv = _ada(c_all, w_ada_kv[None], b_ada_kv[None], 0)

    def split(m):
        return m[:bsz, None, :], m[bsz:bsz + n_seq]

    mod_p, mod_s = split(mods[0])
    abr, abi, bbr, bbi, apr, api = _ssm_prep(ssm_a_re[0], ssm_a_im[0], ssm_log_dt[0], ssm_b_re[0],
                                             ssm_b_im[0], S5_CHUNK // SUBLANES)
    b_re_x, b_im_x = _expand_b(bbr), _expand_b(bbi)
    cexp = (_expand_c(ssm_c_re[0]).astype(BF16), _expand_c(-ssm_c_im[0]).astype(BF16))
    wa, wb = w_glu_a[0].astype(BF16), w_glu_b[0].astype(BF16)
    xp, sre_p, sim_p = _s5_prompt(x_prompt, mod_p, row(g_mix[0]), (abr, abi, apr, api),
                                  (b_re_x.astype(BF16), b_im_x.astype(BF16)), cexp, row(ssm_d[0]), wa, wb)
    xs_t, sre_s, sim_s = _s5_sample(
        x_sample.transpose(1, 0, 2), mod_s, row(g_mix[0]),
        state_ssm_re[0].reshape(n_seq, SSM_COLS), state_ssm_im[0].reshape(n_seq, SSM_COLS), abr, abi,
        (_split_bf16(b_re_x), _split_bf16(b_im_x)), cexp, row(ssm_d[0]), wa, wb)
    xp = xp.reshape(n_p, D_MODEL)
    xs = xs_t.reshape(n_s, D_MODEL)
    xp, xs = _moe_layer(tiles, xp, xs, mod_p, mod_s, row(g_ffn[0]), w_router[0], b_router[0], w_gu, b_gu,
                        w_down, b_down, row(g_final), 0, False)

    mod_p, mod_s = split(mods[1])
    modkv_p, modkv_s = split(mod_kv)
    wkv, wq, wo = w_kv.astype(BF16), w_q[0].astype(BF16), w_o[0].astype(BF16)
    xp3, k_p, v_p = _attn_prompt(xp.reshape(bsz, seq, D_MODEL), modkv_p, mod_p, row(g_kv), row(g_mix[1]), wkv,
                                 row(b_kv), wq, row(b_q[0]), wo, attn_sinks[0])
    xp = xp3.reshape(n_p, D_MODEL)

    q_s, k_s, v_s = _qkv_sample(xs, modkv_s, mod_s, row(g_kv), row(g_mix[1]), wkv, row(b_kv), wq, row(b_q[0]))
    rows = steps * Q_PER_KV
    qg = q_s.reshape(steps, n_seq, N_KV_HEADS, Q_PER_KV, HEAD_DIM).transpose(1, 2, 0, 3, 4).reshape(
        n_seq, N_KV_HEADS, rows, HEAD_DIM)
    head_of_row = (np.arange(N_KV_HEADS)[:, None] * Q_PER_KV + np.arange(rows)[None, :] % Q_PER_KV)
    slope_rows = jnp.asarray(np.asarray(_slopes(), np.float32)[head_of_row][..., None])
    sink_rows = attn_sinks[0][head_of_row][..., None]
    k_new = k_s.reshape(steps, n_seq, KV_DIM).transpose(1, 0, 2)
    v_new = v_s.reshape(steps, n_seq, KV_DIM).transpose(1, 0, 2)
    ck = cache_k.reshape(n_seq, WINDOW, KV_DIM)
    cv = cache_v.reshape(n_seq, WINDOW, KV_DIM)
    og = _attn_sample(qg, ck, cv, k_new, v_new, slope_rows, sink_rows)
    o_s = og.reshape(n_seq, N_KV_HEADS, steps, Q_PER_KV, HEAD_DIM).transpose(2, 0, 1, 3, 4).reshape(
        n_s, N_HEADS * HEAD_DIM)
    xs = _oproj(o_s, xs, mod_s, wo)

    yp, ys = _moe_layer(tiles, xp, xs, mod_p, mod_s, row(g_ffn[1]), w_router[1], b_router[1], w_gu, b_gu,
                        w_down, b_down, row(g_final), 1, True)

    shape4 = lambda a: a.reshape(a.shape[0], a.shape[1], N_KV_HEADS, HEAD_DIM)
    state = lambda a, n: a.reshape(1, n, SSM_GROUPS, SSM_STATE)
    return (yp.reshape(bsz, seq, D_MODEL), ys.reshape(steps, n_seq, D_MODEL).transpose(1, 0, 2),
            state(sre_p, bsz), state(sim_p, bsz), shape4(k_p), shape4(v_p),
            state(sre_s, n_seq), state(sim_s, n_seq),
            jnp.concatenate([cache_k[:, steps:], shape4(k_new)], axis=1),
            jnp.concatenate([cache_v[:, steps:], shape4(v_new)], axis=1))
```

```python
import functools
import math

import numpy as np
import jax
import jax.numpy as jnp
from jax import lax
from jax.experimental import pallas as pl
from jax.experimental.pallas import tpu as pltpu

F32 = jnp.float32
BF16 = jnp.bfloat16
I32 = jnp.int32

D_MODEL = 1024
SSM_GROUP = 16
SSM_GROUPS = D_MODEL // SSM_GROUP
SSM_STATE = 64
SSM_COLS = SSM_GROUPS * SSM_STATE
N_HEADS = 16
HEAD_DIM = 64
N_KV_HEADS = 4
Q_PER_KV = N_HEADS // N_KV_HEADS
KV_DIM = N_KV_HEADS * HEAD_DIM
WINDOW = 128
N_EXPERTS = 32
TOP_K = 4
D_FF = D_MODEL
SWIGLU_LIMIT = 7.0
SWIGLU_ALPHA = 1.702
RMS_EPS = 1e-5
NEG_INF = -1e30

LANES = 128
SUBLANES = 8
ROW_TILES = D_MODEL // LANES
TM = 256
S5_CHUNK = 256
S5_COLS = 512
MXU_K = 256
GROUPS_PER_K = MXU_K // SSM_GROUP
N_KT = D_MODEL // MXU_K
VMEM_LIMIT = 56 * 1024 * 1024


def _cparams(sem=None):
    return pltpu.CompilerParams(dimension_semantics=sem, vmem_limit_bytes=VMEM_LIMIT)


def _sigmoid(x):
    return 1.0 / (1.0 + jnp.exp(-x))


def _rms(x, g):
    return x * lax.rsqrt(jnp.mean(x * x, axis=-1, keepdims=True) + RMS_EPS) * g


def _bdot(a, b):
    return jnp.dot(a.astype(BF16), b.astype(BF16), preferred_element_type=F32)


def _gelu_tanh(x):
    return 0.5 * x * (1.0 + jnp.tanh(math.sqrt(2.0 / math.pi) * (x + 0.044715 * (x * x * x))))


def _split_bf16(w):
    hi = w.astype(BF16)
    lo = (w - hi.astype(F32)).astype(BF16)
    return hi, lo


def _dot3(a, b_hi, b_lo):
    a_hi = a.astype(BF16)
    a_lo = (a - a_hi.astype(F32)).astype(BF16)
    return (jnp.dot(a_hi, b_hi, preferred_element_type=F32)
            + jnp.dot(a_lo, b_hi, preferred_element_type=F32)
            + jnp.dot(a_hi, b_lo, preferred_element_type=F32))


def _ada_kernel(c_ref, w_ref, b_ref, o_ref):
    c = c_ref[...]
    o_ref[...] = _bdot(c * _sigmoid(c), w_ref[...]) + b_ref[...]


def _ada(c, w, b, layer):
    rows, cols = c.shape[0], w.shape[2]
    tn = 1024
    return pl.pallas_call(
        _ada_kernel,
        grid=(cols // tn,),
        in_specs=[pl.BlockSpec((rows, D_MODEL), lambda j: (0, 0)),
                  pl.BlockSpec((None, D_MODEL, tn), lambda j: (layer, 0, j)),
                  pl.BlockSpec((None, 1, tn), lambda j: (layer, 0, j))],
        out_specs=pl.BlockSpec((rows, tn), lambda j: (0, j)),
        out_shape=jax.ShapeDtypeStruct((rows, cols), F32),
        compiler_params=_cparams(("arbitrary",)),
        name="ada",
    )(c, w, b.reshape(b.shape[0], 1, cols))


def _ssm_prep_kernel(are_ref, aim_ref, ldt_ref, bre_ref, bim_ref, cre_ref, cim_ref,
                     abr_ref, abi_ref, apr_ref, api_ref, brh_ref, brl_ref, bih_ref, bil_ref, cr_ref, ci_ref,
                     *, sub_len):
    kcols = GROUPS_PER_K * SSM_STATE
    a_re, a_im = are_ref[...], aim_ref[...]
    dt = jnp.exp(ldt_ref[...])
    mag = jnp.exp(a_re * dt)
    ab_re = mag * jnp.cos(a_im * dt)
    ab_im = mag * jnp.sin(a_im * dt)
    abr_ref[...] = ab_re
    abi_ref[...] = ab_im
    den = a_re * a_re + a_im * a_im
    x, y = ab_re - 1.0, ab_im
    f_re = (x * a_re + y * a_im) / den
    f_im = (y * a_re - x * a_im) / den
    b_re, b_im = bre_ref[...], bim_ref[...]
    bb_re = f_re * b_re - f_im * b_im
    bb_im = f_re * b_im + f_im * b_re
    same_b = (lax.broadcasted_iota(I32, (MXU_K, kcols), 0) // SSM_GROUP
              == lax.broadcasted_iota(I32, (MXU_K, kcols), 1) // SSM_STATE)
    same_c = (lax.broadcasted_iota(I32, (kcols, MXU_K), 0) // SSM_STATE
              == lax.broadcasted_iota(I32, (kcols, MXU_K), 1) // SSM_GROUP)
    for kt in range(N_KT):
        for bb, hi_ref, lo_ref in ((bb_re, brh_ref, brl_ref), (bb_im, bih_ref, bil_ref)):
            full = jnp.where(same_b, jnp.concatenate([bb[:, kt * kcols:(kt + 1) * kcols]] * GROUPS_PER_K, axis=0), 0.0)
            hi = full.astype(BF16)
            hi_ref[kt] = hi
            lo_ref[kt] = (full - hi.astype(F32)).astype(BF16)
        cr_ref[kt] = jnp.where(same_c, cre_ref[kt * kcols:(kt + 1) * kcols, :], 0.0).astype(BF16)
        ci_ref[kt] = jnp.where(same_c, -cim_ref[kt * kcols:(kt + 1) * kcols, :], 0.0).astype(BF16)
    pr, pi = ab_re, ab_im
    for _ in range(int(math.log2(sub_len))):
        pr, pi = pr * pr - pi * pi, 2.0 * pr * pi
    cr, ci = jnp.ones_like(pr), jnp.zeros_like(pi)
    for j in range(SUBLANES + 1):
        apr_ref[j:j + 1, :] = cr
        api_ref[j:j + 1, :] = ci
        cr, ci = cr * pr - ci * pi, cr * pi + ci * pr


def _ssm_prep(a_re, a_im, log_dt, b_re, b_im, c_re, c_im, sub_len):
    gp = SSM_COLS
    kcols = GROUPS_PER_K * SSM_STATE
    ldt = jnp.broadcast_to(log_dt[:, None], (SSM_GROUPS, SSM_STATE)).reshape(1, gp)
    b_re_t = b_re.reshape(gp, SSM_GROUP).T
    b_im_t = b_im.reshape(gp, SSM_GROUP).T
    c_rows = lambda c: jnp.tile(c.transpose(0, 2, 1).reshape(gp, SSM_GROUP), (1, GROUPS_PER_K))
    row = jax.ShapeDtypeStruct((1, gp), F32)
    pw = jax.ShapeDtypeStruct((SUBLANES + 1, gp), F32)
    bx = jax.ShapeDtypeStruct((N_KT, MXU_K, kcols), BF16)
    cx = jax.ShapeDtypeStruct((N_KT, kcols, MXU_K), BF16)
    return pl.pallas_call(
        functools.partial(_ssm_prep_kernel, sub_len=sub_len),
        out_shape=(row, row, pw, pw, bx, bx, bx, bx, cx, cx),
        compiler_params=_cparams(),
        name="ssm_prep",
    )(a_re.reshape(1, gp), a_im.reshape(1, gp), ldt, b_re_t, b_im_t, c_rows(c_re), c_rows(c_im))


def _cmul(ar, ai, br, bi):
    return ar * br - ai * bi, ar * bi + ai * br


def _s5_prompt_kernel(x_ref, sh_ref, sc_ref, g1_ref, gmix_ref, abr_ref, abi_ref, apr_ref, api_ref,
                      bre_ref, bim_ref, cre_ref, cim_ref, dsk_ref, wa_ref, wb_ref, perm_ref, unperm_ref,
                      o_ref, sre_ref, sim_ref,
                      u_ref, hr_ref, hi_ref, hmr_ref, hmi_ref, car_ref, cai_ref, *, L):
    ls = L // SUBLANES
    kcols = GROUPS_PER_K * SSM_STATE

    @pl.when(pl.program_id(1) == 0)
    def _():
        car_ref[...] = jnp.zeros_like(car_ref)
        cai_ref[...] = jnp.zeros_like(cai_ref)

    x = x_ref[...]
    u = _rms(x, gmix_ref[...]) * (1.0 + sc_ref[...]) + sh_ref[...]
    u_ref[...] = u
    ub = jnp.dot(perm_ref[...], u.astype(BF16), preferred_element_type=F32).astype(BF16)
    for kt in range(N_KT):
        uk = ub[:, kt * MXU_K:(kt + 1) * MXU_K]
        hr_ref[:, kt * kcols:(kt + 1) * kcols] = jnp.dot(uk, bre_ref[kt], preferred_element_type=F32)
        hi_ref[:, kt * kcols:(kt + 1) * kcols] = jnp.dot(uk, bim_ref[kt], preferred_element_type=F32)

    for cb in range(SSM_COLS // S5_COLS):
        cols = slice(cb * S5_COLS, (cb + 1) * S5_COLS)
        ar = jnp.broadcast_to(abr_ref[:, cols], (SUBLANES, S5_COLS))
        ai = jnp.broadcast_to(abi_ref[:, cols], (SUBLANES, S5_COLS))

        sr = si = jnp.zeros((SUBLANES, S5_COLS), F32)
        for i in range(ls):
            rows = slice(i * SUBLANES, (i + 1) * SUBLANES)
            tr, ti = _cmul(ar, ai, sr, si)
            sr = tr + hr_ref[rows, cols]
            si = ti + hi_ref[rows, cols]
            hr_ref[rows, cols] = sr
            hi_ref[rows, cols] = si

    fr = hr_ref[L - SUBLANES:L, :]
    fi = hi_ref[L - SUBLANES:L, :]
    row = lax.broadcasted_iota(I32, (SUBLANES, SSM_COLS), 0)
    gr, gi = fr, fi
    pr, pi = apr_ref[1:2, :], api_ref[1:2, :]
    for s in (1, 2, 4):
        sr = jnp.where(row >= s, pltpu.roll(gr, s, axis=0), 0.0)
        si = jnp.where(row >= s, pltpu.roll(gi, s, axis=0), 0.0)
        tr, ti = _cmul(pr, pi, sr, si)
        gr, gi = gr + tr, gi + ti
        pr, pi = _cmul(pr, pi, pr, pi)
    c0r, c0i = car_ref[...], cai_ref[...]
    tr, ti = _cmul(apr_ref[0:SUBLANES, :], api_ref[0:SUBLANES, :], c0r, c0i)
    hmr = tr + jnp.where(row >= 1, pltpu.roll(gr, 1, axis=0), 0.0)
    hmi = ti + jnp.where(row >= 1, pltpu.roll(gi, 1, axis=0), 0.0)
    hmr_ref[...] = hmr
    hmi_ref[...] = hmi
    tr, ti = _cmul(apr_ref[SUBLANES:SUBLANES + 1, :], api_ref[SUBLANES:SUBLANES + 1, :], c0r, c0i)
    ncr = tr + gr[SUBLANES - 1:SUBLANES, :]
    nci = ti + gi[SUBLANES - 1:SUBLANES, :]
    car_ref[...] = ncr
    cai_ref[...] = nci
    sre_ref[...] = ncr
    sim_ref[...] = nci

    for cb in range(SSM_COLS // S5_COLS):
        cols = slice(cb * S5_COLS, (cb + 1) * S5_COLS)
        ar = jnp.broadcast_to(abr_ref[:, cols], (SUBLANES, S5_COLS))
        ai = jnp.broadcast_to(abi_ref[:, cols], (SUBLANES, S5_COLS))

        dr, di = hmr_ref[:, cols], hmi_ref[:, cols]
        for i in range(ls):
            rows = slice(i * SUBLANES, (i + 1) * SUBLANES)
            dr, di = _cmul(ar, ai, dr, di)
            hr_ref[rows, cols] = hr_ref[rows, cols] + dr
            hi_ref[rows, cols] = hi_ref[rows, cols] + di

    ys = []
    for nt in range(N_KT):
        hr = hr_ref[:, nt * kcols:(nt + 1) * kcols].astype(BF16)
        hi = hi_ref[:, nt * kcols:(nt + 1) * kcols].astype(BF16)
        ys.append(jnp.dot(hr, cre_ref[nt], preferred_element_type=F32)
                  + jnp.dot(hi, cim_ref[nt], preferred_element_type=F32))
    yp = jnp.concatenate(ys, axis=-1)
    yp_hi = yp.astype(BF16)
    yp_lo = (yp - yp_hi.astype(F32)).astype(BF16)
    unperm = unperm_ref[...]
    y = (jnp.dot(unperm, yp_hi, preferred_element_type=F32)
         + jnp.dot(unperm, yp_lo, preferred_element_type=F32)) + dsk_ref[...] * u_ref[...]
    z = _gelu_tanh(y).astype(BF16)
    mix = jnp.dot(z, wa_ref[...], preferred_element_type=F32) * _sigmoid(
        jnp.dot(z, wb_ref[...], preferred_element_type=F32))
    o_ref[...] = x_ref[...] + g1_ref[...] * mix


SH1, SC1, G1 = 0, 1, 2


def _s5_prompt(x, mod, gmix, prep, bexp, cexp, dsk, wa, wb):
    b, t, _ = x.shape
    L = S5_CHUNK
    abr, abi, apr, api = prep
    bre, bim = bexp
    cre, cim = cexp
    kcols = GROUPS_PER_K * SSM_STATE
    full = lambda shape: pl.BlockSpec(shape, lambda bi, ci: (0,) * len(shape))
    mods = [pl.BlockSpec((None, 1, D_MODEL), lambda bi, ci, k=k: (bi, 0, k)) for k in (SH1, SC1, G1)]
    r = np.arange(L)
    perm_np = np.zeros((L, L), np.float32)
    perm_np[r, (r % SUBLANES) * (L // SUBLANES) + r // SUBLANES] = 1.0
    perm, unperm = jnp.asarray(perm_np, BF16), jnp.asarray(perm_np.T, BF16)
    return pl.pallas_call(
        functools.partial(_s5_prompt_kernel, L=L),
        grid=(b, t // L),
        in_specs=[pl.BlockSpec((None, L, D_MODEL), lambda bi, ci: (bi, ci, 0))] + mods + [
                  full((1, D_MODEL)), full((1, SSM_COLS)), full((1, SSM_COLS)),
                  full((SUBLANES + 1, SSM_COLS)), full((SUBLANES + 1, SSM_COLS)),
                  full((N_KT, MXU_K, kcols)), full((N_KT, MXU_K, kcols)),
                  full((N_KT, kcols, MXU_K)), full((N_KT, kcols, MXU_K)),
                  full((1, D_MODEL)), full((D_MODEL, D_MODEL)), full((D_MODEL, D_MODEL)),
                  full((L, L)), full((L, L))],
        out_specs=[pl.BlockSpec((None, L, D_MODEL), lambda bi, ci: (bi, ci, 0)),
                   pl.BlockSpec((None, 1, SSM_COLS), lambda bi, ci: (bi, 0, 0)),
                   pl.BlockSpec((None, 1, SSM_COLS), lambda bi, ci: (bi, 0, 0))],
        out_shape=(jax.ShapeDtypeStruct((b, t, D_MODEL), F32),
                   jax.ShapeDtypeStruct((b, 1, SSM_COLS), F32),
                   jax.ShapeDtypeStruct((b, 1, SSM_COLS), F32)),
        scratch_shapes=[pltpu.VMEM((L, D_MODEL), F32),
                        pltpu.VMEM((L, SSM_COLS), F32), pltpu.VMEM((L, SSM_COLS), F32),
                        pltpu.VMEM((SUBLANES, SSM_COLS), F32), pltpu.VMEM((SUBLANES, SSM_COLS), F32),
                        pltpu.VMEM((1, SSM_COLS), F32), pltpu.VMEM((1, SSM_COLS), F32)],
        compiler_params=_cparams(("arbitrary", "arbitrary")),
        name="s5_prompt",
    )(x, mod, mod, mod, gmix, abr, abi, apr, api, bre, bim, cre, cim, dsk, wa, wb, perm, unperm)


def _tile_rows(v, rows):
    reps = rows // v.shape[0]
    return v if reps == 1 else jnp.concatenate([v] * reps, axis=0)


def _mod_cols(m_ref, k, rows=None):
    v = m_ref[:, k * D_MODEL:(k + 1) * D_MODEL]
    return v if rows is None else _tile_rows(v, rows)


def _s5_sample_kernel(x_ref, m_ref, gmix_ref, h0r_ref, h0i_ref, abr_ref, abi_ref,
                      brh_ref, brl_ref, bih_ref, bil_ref, cre_ref, cim_ref, dsk_ref, wa_ref, wb_ref,
                      o_ref, sre_ref, sim_ref, *, steps):
    kcols = GROUPS_PER_K * SSM_STATE
    ar, ai = abr_ref[...], abi_ref[...]
    sr, si = h0r_ref[...], h0i_ref[...]
    sh, sc, g1 = _mod_cols(m_ref, SH1), _mod_cols(m_ref, SC1), _mod_cols(m_ref, G1)
    for t in range(steps):
        x = x_ref[t]
        u = _rms(x, gmix_ref[...]) * (1.0 + sc) + sh
        bur, bui = [], []
        for kt in range(N_KT):
            uk = u[:, kt * MXU_K:(kt + 1) * MXU_K]
            bur.append(_dot3(uk, brh_ref[kt], brl_ref[kt]))
            bui.append(_dot3(uk, bih_ref[kt], bil_ref[kt]))
        tr, ti = _cmul(ar, ai, sr, si)
        sr = tr + jnp.concatenate(bur, axis=-1)
        si = ti + jnp.concatenate(bui, axis=-1)
        ys = []
        for nt in range(N_KT):
            ys.append(_bdot(sr[:, nt * kcols:(nt + 1) * kcols], cre_ref[nt])
                      + _bdot(si[:, nt * kcols:(nt + 1) * kcols], cim_ref[nt]))
        y = jnp.concatenate(ys, axis=-1) + dsk_ref[...] * u
        z = _gelu_tanh(y).astype(BF16)
        mix = jnp.dot(z, wa_ref[...], preferred_element_type=F32) * _sigmoid(
            jnp.dot(z, wb_ref[...], preferred_element_type=F32))
        o_ref[t] = x + g1 * mix
    sre_ref[...] = sr
    sim_ref[...] = si


def _s5_sample(x_t, mod, gmix, h0r, h0i, abr, abi, bsplit, cexp, dsk, wa, wb):
    steps, n, _ = x_t.shape
    (brh, brl), (bih, bil) = bsplit
    cre, cim = cexp
    return pl.pallas_call(
        functools.partial(_s5_sample_kernel, steps=steps),
        out_shape=(jax.ShapeDtypeStruct((steps, n, D_MODEL), F32),
                   jax.ShapeDtypeStruct((n, SSM_COLS), F32),
                   jax.ShapeDtypeStruct((n, SSM_COLS), F32)),
        compiler_params=_cparams(),
        name="s5_sample",
    )(x_t, mod, gmix, h0r, h0i, abr, abi, brh, brl, bih, bil, cre, cim, dsk, wa, wb)


class _Tiles:
    def __init__(self, n_prompt, n_sample, seq_len, n_seq):
        assert TM % n_seq == 0
        self.n_seq = n_seq
        self.ntp = n_prompt // TM
        self.nts = n_sample // TM
        self.nt = self.ntp + self.nts
        self.tiles_per_seq = seq_len // TM
        self.n_batch = n_prompt // seq_len
        self.n_tok = n_prompt + n_sample

    def p_rows(self):
        return pl.BlockSpec((TM, D_MODEL), lambda t, *_: (jnp.minimum(t, self.ntp - 1), 0))

    def s_rows(self):
        return pl.BlockSpec((TM, D_MODEL), lambda t, *_: (jnp.maximum(t - self.ntp, 0), 0))

    def s_mod(self, chunk):
        return pl.BlockSpec((self.n_seq, D_MODEL), lambda t, *_: (0, chunk))

    def p_mod(self, chunk):
        return pl.BlockSpec((None, 1, D_MODEL),
                            lambda t, *_: (jnp.minimum(t // self.tiles_per_seq, self.n_batch - 1), 0, chunk))

    def all_rows(self, width):
        return pl.BlockSpec((TM, width), lambda t, *_: (t, 0))


def _const_spec(shape):
    return pl.BlockSpec(shape, lambda t, *_: (0,) * len(shape))


def _by_kind(tiles, body, prompt_refs, sample_refs):
    t = pl.program_id(0)

    @pl.when(t < tiles.ntp)
    def _():
        body(*prompt_refs)

    @pl.when(t >= tiles.ntp)
    def _():
        body(*sample_refs)


def _mod_rows(m_ref):
    v = m_ref[...]
    return v if v.shape[0] == 1 else _tile_rows(v, TM)


def _moe_input(x_ref, sh_ref, sc_ref, g_ref):
    return _rms(x_ref[...], g_ref[...]) * (1.0 + _mod_rows(sc_ref)) + _mod_rows(sh_ref)


def _router_kernel(xp_ref, xs_ref, shp_ref, shs_ref, scp_ref, scs_ref, g_ref, wh_ref, wl_ref, b_ref, tri_ref,
                   mi_ref, mw_ref, cnt_ref, run_ref, *, tiles):
    @pl.when(pl.program_id(0) == 0)
    def _():
        run_ref[...] = jnp.zeros_like(run_ref)

    def route(x_ref, sh_ref, sc_ref):
        h = _moe_input(x_ref, sh_ref, sc_ref, g_ref)
        logits = _dot3(h, wh_ref[...], wl_ref[...]) + b_ref[...]
        lane = lax.broadcasted_iota(I32, (TM, LANES), 1)
        lane_f = lane.astype(F32)
        vals, firsts, hots = [], [], []
        work = logits
        for _ in range(TOP_K):
            m = jnp.max(work, axis=-1, keepdims=True)
            first = jnp.min(jnp.where(work == m, lane_f, float(LANES)), axis=-1, keepdims=True)
            hot = lane_f == first
            vals.append(m)
            firsts.append(first)
            hots.append(hot)
            work = jnp.where(hot, -jnp.inf, work)
        exps = [jnp.exp(v - vals[0]) for v in vals]
        den = exps[0] + exps[1] + exps[2] + exps[3]

        chosen = jnp.zeros((TM, LANES), F32)
        for hot in hots:
            chosen = jnp.where(hot, 1.0, chosen)
        rank_all = jnp.dot(tri_ref[...], chosen.astype(BF16), preferred_element_type=F32) + run_ref[...]
        run_ref[...] = run_ref[...] + jnp.sum(chosen, axis=0, keepdims=True)
        cnt_ref[...] = run_ref[...]

        code = jnp.zeros((TM, LANES), F32)
        mw = jnp.zeros((TM, LANES), F32)
        for k in range(TOP_K):
            r_k = jnp.sum(jnp.where(hots[k], rank_all, 0.0), axis=-1, keepdims=True)
            code = jnp.where(lane == k, r_k * float(N_EXPERTS) + firsts[k], code)
            mw = jnp.where(lane == k, exps[k] / den, mw)
        mw_ref[...] = mw
        code_t = code.T
        mi_ref[...] = jnp.concatenate(
            [code_t[k:k + 1, c * LANES:(c + 1) * LANES] for k in range(TOP_K) for c in range(TM // LANES)],
            axis=0).astype(I32)

    _by_kind(tiles, route, (xp_ref, shp_ref, scp_ref), (xs_ref, shs_ref, scs_ref))


SH2, SC2, G2 = 3, 4, 5


def _router(tiles, xp, xs, mod_p, mod_s, g, w_hi, w_lo, b_pad):
    tri = jnp.asarray(np.tril(np.ones((TM, TM), np.float32), -1), BF16)
    return pl.pallas_call(
        functools.partial(_router_kernel, tiles=tiles),
        grid=(tiles.nt,),
        in_specs=[tiles.p_rows(), tiles.s_rows(), tiles.p_mod(SH2), tiles.s_mod(SH2), tiles.p_mod(SC2),
                  tiles.s_mod(SC2), _const_spec((1, D_MODEL)), _const_spec((D_MODEL, LANES)),
                  _const_spec((D_MODEL, LANES)), _const_spec((1, LANES)), _const_spec((TM, TM))],
        out_specs=[pl.BlockSpec((TOP_K * TM // LANES, LANES), lambda t: (t, 0)), tiles.all_rows(LANES),
                   _const_spec((1, LANES))],
        out_shape=(jax.ShapeDtypeStruct((tiles.n_tok * TOP_K // LANES, LANES), I32),
                   jax.ShapeDtypeStruct((tiles.n_tok, LANES), F32),
                   jax.ShapeDtypeStruct((1, LANES), F32)),
        scratch_shapes=[pltpu.VMEM((1, LANES), F32)],
        compiler_params=_cparams(("arbitrary",)),
        name="router",
    )(xp, xs, mod_p, mod_s, mod_p, mod_s, g, w_hi, w_lo, b_pad, tri)


PAD_ARMS = tuple(1 << i for i in reversed(range(int(math.log2(TM)))))


def _dispatch_kernel(slot_ref, pstart_ref, pcnt_ref,
                     xp_ref, xs_ref, shp_ref, shs_ref, scp_ref, scs_ref, g_ref,
                     o_hbm, rows_ref, zero_ref, sems, zsem, *, tiles):
    t = pl.program_id(0)
    buf = t % 2
    tile_rows = TM * ROW_TILES
    base = pl.multiple_of(buf * tile_rows, tile_rows)

    def tile_wait(b):
        b0 = pl.multiple_of(b * tile_rows, tile_rows)
        for _ in range(TOP_K):
            pltpu.make_async_copy(rows_ref.at[pl.ds(b0, tile_rows), :],
                                  o_hbm.at[pl.ds(0, tile_rows), :], sems.at[b]).wait()

    def pad_copies(act):
        def body(e, carry):
            start, cnt = pstart_ref[e], pcnt_ref[e]
            for p in PAD_ARMS:
                @pl.when((cnt & p) != 0)
                def _(start=start, p=p):
                    cp = pltpu.make_async_copy(
                        zero_ref.at[pl.ds(0, p * ROW_TILES), :],
                        o_hbm.at[pl.ds(pl.multiple_of(start * ROW_TILES, ROW_TILES), p * ROW_TILES), :],
                        zsem)
                    cp.start() if act == "start" else cp.wait()
                start = start + jnp.where((cnt & p) != 0, p, 0)
            return carry
        lax.fori_loop(0, N_EXPERTS, body, 0)

    @pl.when(t == 0)
    def _():
        zero_ref[...] = jnp.zeros_like(zero_ref)
        pad_copies("start")

    @pl.when(t >= 2)
    def _():
        tile_wait(buf)

    def stage(x_ref, sh_ref, sc_ref):
        h = _moe_input(x_ref, sh_ref, sc_ref, g_ref)
        for c in range(ROW_TILES):
            rows_ref[pl.ds(base + c, TM, stride=ROW_TILES), :] = h[:, c * LANES:(c + 1) * LANES]

    _by_kind(tiles, stage, (xp_ref, shp_ref, scp_ref), (xs_ref, shs_ref, scs_ref))

    def issue(n, carry):
        src = rows_ref.at[pl.ds(pl.multiple_of(base + n * ROW_TILES, ROW_TILES), ROW_TILES), :]
        for k in range(TOP_K):
            slot = slot_ref[(t * TOP_K + k) * TM + n]
            dst = o_hbm.at[pl.ds(pl.multiple_of(slot * ROW_TILES, ROW_TILES), ROW_TILES), :]
            pltpu.make_async_copy(src, dst, sems.at[buf]).start()
        return carry
    lax.fori_loop(0, TM, issue, 0)

    @pl.when(t == tiles.nt - 1)
    def _():
        if tiles.nt >= 2:
            tile_wait(1 - buf)
        tile_wait(buf)
        pad_copies("wait")


def _dispatch(tiles, n_slots, slots, pad_start, pad_cnt, xp, xs, mod_p, mod_s, g):
    grid_spec = pltpu.PrefetchScalarGridSpec(
        num_scalar_prefetch=3,
        grid=(tiles.nt,),
        in_specs=[tiles.p_rows(), tiles.s_rows(), tiles.p_mod(SH2), tiles.s_mod(SH2), tiles.p_mod(SC2),
                  tiles.s_mod(SC2), _const_spec((1, D_MODEL))],
        out_specs=pl.BlockSpec(memory_space=pl.ANY),
        scratch_shapes=[pltpu.VMEM((2 * TM * ROW_TILES, LANES), F32),
                        pltpu.VMEM((PAD_ARMS[0] * ROW_TILES, LANES), F32),
                        pltpu.SemaphoreType.DMA((2,)), pltpu.SemaphoreType.DMA(())],
    )
    return pl.pallas_call(
        functools.partial(_dispatch_kernel, tiles=tiles),
        grid_spec=grid_spec,
        out_shape=jax.ShapeDtypeStruct((n_slots * ROW_TILES, LANES), F32),
        compiler_params=_cparams(("arbitrary",)),
        name="dispatch",
    )(slots, pad_start, pad_cnt, xp, xs, mod_p, mod_s, mod_p, mod_s, g)


def _expert_kernel(te_ref, nx_ref, nt_ref, hs_ref, wgu_hbm, bgu_ref, wd_hbm, bd_ref, o_ref,
                   wgu_f32, wd_f32, wgu_bf, wd_bf, lhs_ref, wsem, *, layer):
    i = pl.program_id(0)

    def fetch(expert):
        return (pltpu.make_async_copy(wgu_hbm.at[layer, expert], wgu_f32, wsem.at[0]),
                pltpu.make_async_copy(wd_hbm.at[layer, expert], wd_f32, wsem.at[1]))

    @pl.when(i < nt_ref[0])
    def _():
        expert = te_ref[i]

        @pl.when(i == 0)
        def _():
            for cp in fetch(expert):
                cp.start()

        @pl.when((i == 0) | (expert != te_ref[jnp.maximum(i - 1, 0)]))
        def _():
            for cp in fetch(expert):
                cp.wait()
            wgu_bf[...] = wgu_f32[...].astype(BF16)
            wd_bf[...] = wd_f32[...].astype(BF16)
            upcoming = nx_ref[i]

            @pl.when(upcoming >= 0)
            def _():
                for cp in fetch(upcoming):
                    cp.start()

        for c in range(ROW_TILES):
            lhs_ref[:, c * LANES:(c + 1) * LANES] = hs_ref[pl.ds(c, TM, stride=ROW_TILES), :].astype(BF16)
        gu = jnp.dot(lhs_ref[...], wgu_bf[...], preferred_element_type=F32) + bgu_ref[...]
        gate = jnp.minimum(gu[:, :D_FF], SWIGLU_LIMIT)
        up = jnp.clip(gu[:, D_FF:], -SWIGLU_LIMIT, SWIGLU_LIMIT)
        act = ((up + 1.0) * gate * _sigmoid(SWIGLU_ALPHA * gate)).astype(BF16)
        y = jnp.dot(act, wd_bf[...], preferred_element_type=F32) + bd_ref[...]
        for c in range(ROW_TILES):
            o_ref[pl.ds(c, TM, stride=ROW_TILES), :] = y[:, c * LANES:(c + 1) * LANES]


def _experts(max_tiles, tile_expert, next_expert, n_tiles, hs, w_gu, b_gu, w_down, b_down, layer):
    tile_rows = TM * ROW_TILES
    row_map = lambda i, te, nx, nt: (jnp.minimum(i, nt[0] - 1), 0)
    b_map = lambda i, te, nx, nt: (layer, te[i], 0, 0)
    grid_spec = pltpu.PrefetchScalarGridSpec(
        num_scalar_prefetch=3,
        grid=(max_tiles,),
        in_specs=[pl.BlockSpec((tile_rows, LANES), row_map),
                  pl.BlockSpec(memory_space=pl.ANY),
                  pl.BlockSpec((None, None, 1, 2 * D_FF), b_map),
                  pl.BlockSpec(memory_space=pl.ANY),
                  pl.BlockSpec((None, None, 1, D_MODEL), b_map)],
        out_specs=pl.BlockSpec((tile_rows, LANES), row_map),
        scratch_shapes=[pltpu.VMEM((D_MODEL, 2 * D_FF), F32), pltpu.VMEM((D_FF, D_MODEL), F32),
                        pltpu.VMEM((D_MODEL, 2 * D_FF), BF16), pltpu.VMEM((D_FF, D_MODEL), BF16),
                        pltpu.VMEM((TM, D_MODEL), BF16), pltpu.SemaphoreType.DMA((2,))],
    )
    return pl.pallas_call(
        functools.partial(_expert_kernel, layer=layer),
        grid_spec=grid_spec,
        out_shape=jax.ShapeDtypeStruct(hs.shape, F32),
        compiler_params=_cparams(("arbitrary",)),
        name="experts",
    )(tile_expert, next_expert, n_tiles, hs, w_gu, b_gu[:, :, None, :], w_down, b_down[:, :, None, :])


def _combine_kernel(slot_ref, y_hbm, mw_ref, xp_ref, xs_ref, g2p_ref, g2s_ref, gfin_ref,
                    op_ref, os_ref, gbuf, acc_ref, sems, *, tiles, final_norm):
    t = pl.program_id(0)
    tile_rows = TM * ROW_TILES
    buf_rows = TOP_K * tile_rows

    def issue_tile(tt, b):
        b0 = pl.multiple_of(b * buf_rows, buf_rows)

        for k in range(TOP_K):
            def issue(it, carry, k=k):
                for u in range(TOP_K):
                    n = it * TOP_K + u
                    slot = slot_ref[(tt * TOP_K + k) * TM + n]
                    src = y_hbm.at[pl.ds(pl.multiple_of(slot * ROW_TILES, ROW_TILES), ROW_TILES), :]
                    dst = gbuf.at[pl.ds(pl.multiple_of(b0 + k * tile_rows + n * ROW_TILES, ROW_TILES),
                                        ROW_TILES), :]
                    pltpu.make_async_copy(src, dst, sems.at[b]).start()
                return carry
            lax.fori_loop(0, TM // TOP_K, issue, 0)

    @pl.when(t == 0)
    def _():
        issue_tile(0, 0)

    @pl.when(t + 1 < tiles.nt)
    def _():
        issue_tile(t + 1, (t + 1) % 2)

    buf = t % 2
    b0 = pl.multiple_of(buf * buf_rows, buf_rows)
    for _ in range(TOP_K):
        pltpu.make_async_copy(y_hbm.at[pl.ds(0, tile_rows), :],
                              gbuf.at[pl.ds(b0, tile_rows), :], sems.at[buf]).wait()

    w = mw_ref[...]
    for c in range(ROW_TILES):
        acc = None
        for k in range(TOP_K):
            rows = gbuf[pl.ds(b0 + k * tile_rows + c, TM, stride=ROW_TILES), :]
            term = w[:, k:k + 1] * rows
            acc = term if acc is None else acc + term
        acc_ref[:, c * LANES:(c + 1) * LANES] = acc

    def finish(x_ref, g2_ref, o_ref):
        res = x_ref[...] + _mod_rows(g2_ref) * acc_ref[...]
        if final_norm:
            res = _rms(res, gfin_ref[...])
        o_ref[...] = res

    _by_kind(tiles, finish, (xp_ref, g2p_ref, op_ref), (xs_ref, g2s_ref, os_ref))


def _combine(tiles, slots, y_sorted, mw, xp, xs, mod_p, mod_s, gfin, final_norm):
    grid_spec = pltpu.PrefetchScalarGridSpec(
        num_scalar_prefetch=1,
        grid=(tiles.nt,),
        in_specs=[pl.BlockSpec(memory_space=pl.ANY), tiles.all_rows(LANES), tiles.p_rows(), tiles.s_rows(),
                  tiles.p_mod(G2), tiles.s_mod(G2), _const_spec((1, D_MODEL))],
        out_specs=[tiles.p_rows(), tiles.s_rows()],
        scratch_shapes=[pltpu.VMEM((2 * TOP_K * TM * ROW_TILES, LANES), F32),
                        pltpu.VMEM((TM, D_MODEL), F32), pltpu.SemaphoreType.DMA((2,))],
    )
    return pl.pallas_call(
        functools.partial(_combine_kernel, tiles=tiles, final_norm=final_norm),
        grid_spec=grid_spec,
        out_shape=(jax.ShapeDtypeStruct(xp.shape, F32), jax.ShapeDtypeStruct(xs.shape, F32)),
        compiler_params=_cparams(("arbitrary",)),
        name="combine",
    )(slots, y_sorted, mw, xp, xs, mod_p, mod_s, gfin)


def _moe_layer(tiles, xp, xs, mod_p, mod_s, g_ffn, w_router, b_router, w_gu, b_gu, w_down, b_down, gfin,
               layer, final_norm):
    n_tok = tiles.n_tok
    w_pad = jnp.pad(w_router, ((0, 0), (0, LANES - N_EXPERTS)))
    w_hi, w_lo = _split_bf16(w_pad)
    b_pad = jnp.pad(b_router, (0, LANES - N_EXPERTS), constant_values=NEG_INF).reshape(1, LANES)
    mi, mw, counts = _router(tiles, xp, xs, mod_p, mod_s, g_ffn, w_hi, w_lo, b_pad)

    cnt = counts[0, :N_EXPERTS].astype(I32)
    tiles_e = (cnt + TM - 1) // TM
    tile_end = jnp.cumsum(tiles_e)
    tile_start = tile_end - tiles_e
    first_slot = jnp.sum(jnp.where((mi & (N_EXPERTS - 1))[..., None] == jnp.arange(N_EXPERTS, dtype=I32),
                                   (tile_start * TM)[None, None, :], 0), axis=-1)
    slots = (first_slot + (mi >> int(math.log2(N_EXPERTS)))).reshape(n_tok * TOP_K)
    max_tiles = (n_tok * TOP_K) // TM + N_EXPERTS
    n_tiles = tile_end[-1:].astype(I32)
    tile_ids = jnp.arange(max_tiles, dtype=I32)
    tile_expert = jnp.minimum(jnp.sum(tile_ids[:, None] >= tile_end[None, :], axis=-1), N_EXPERTS - 1).astype(I32)
    group_end = jnp.sum(jnp.where(tile_expert[:, None] == jnp.arange(N_EXPERTS, dtype=I32), tile_end[None, :], 0),
                        axis=-1)
    after = jnp.minimum(jnp.sum(group_end[:, None] >= tile_end[None, :], axis=-1), N_EXPERTS - 1).astype(I32)
    next_expert = jnp.where(group_end < n_tiles[0], after, -1).astype(I32)
    pad_start = (tile_start * TM + cnt).astype(I32)
    pad_cnt = (tiles_e * TM - cnt).astype(I32)

    hs = _dispatch(tiles, max_tiles * TM, slots, pad_start, pad_cnt, xp, xs, mod_p, mod_s, g_ffn)
    ys = _experts(max_tiles, tile_expert, next_expert, n_tiles, hs, w_gu, b_gu, w_down, b_down, layer)
    return _combine(tiles, slots, ys, mw, xp, xs, mod_p, mod_s, gfin, final_norm)


def _slopes():
    return [2.0 ** (-8.0 * (h + 1) / N_HEADS) for h in range(N_HEADS)]


def _attn_prompt_kernel(sink_ref, x_ref, shkv_ref, sckv_ref, sh_ref, sc_ref, g1_ref, gkv_ref, gmix_ref,
                        wkv_ref, bkv_ref, wq_ref, bq_ref, wo_ref,
                        o_ref, kl_ref, vl_ref, kk_ref, vv_ref, oh_ref):
    j = pl.program_id(1)
    blk = WINDOW

    @pl.when(j == 0)
    def _():
        kk_ref[...] = jnp.zeros_like(kk_ref)
        vv_ref[...] = jnp.zeros_like(vv_ref)

    @pl.when(j > 0)
    def _():
        kk_ref[0:blk, :] = kk_ref[blk:2 * blk, :]
        vv_ref[0:blk, :] = vv_ref[blk:2 * blk, :]

    x = x_ref[...]
    hkv = _rms(x, gkv_ref[...]) * (1.0 + sckv_ref[...]) + shkv_ref[...]
    kv = _bdot(hkv, wkv_ref[...]) + bkv_ref[...]
    kl_ref[...] = kv[:, :KV_DIM]
    vl_ref[...] = kv[:, KV_DIM:]
    kk_ref[blk:2 * blk, :] = kv[:, :KV_DIM].astype(BF16)
    vv_ref[blk:2 * blk, :] = kv[:, KV_DIM:].astype(BF16)

    h = _rms(x, gmix_ref[...]) * (1.0 + sc_ref[...]) + sh_ref[...]
    q = _bdot(h, wq_ref[...]) + bq_ref[...]

    r = lax.broadcasted_iota(I32, (blk, 2 * blk), 0)
    c = lax.broadcasted_iota(I32, (blk, 2 * blk), 1)
    dist_i = r + blk - c
    valid = (dist_i >= 0) & (dist_i < WINDOW) & jnp.logical_not((j == 0) & (c < blk))
    dist = dist_i.astype(F32)
    slopes = _slopes()
    for g in range(N_KV_HEADS):
        kg = kk_ref[:, g * HEAD_DIM:(g + 1) * HEAD_DIM]
        vg = vv_ref[:, g * HEAD_DIM:(g + 1) * HEAD_DIM]
        for qh in range(Q_PER_KV):
            hd = g * Q_PER_KV + qh
            qd = q[:, hd * HEAD_DIM:(hd + 1) * HEAD_DIM].astype(BF16)
            s = lax.dot_general(qd, kg, (((1,), (1,)), ((), ())), preferred_element_type=F32)
            s = s * (HEAD_DIM ** -0.5) - slopes[hd] * dist
            s = jnp.where(valid, s, NEG_INF)
            sink = sink_ref[hd]
            m = jnp.maximum(jnp.max(s, axis=-1, keepdims=True), sink)
            e = jnp.exp(s - m)
            den = jnp.sum(e, axis=-1, keepdims=True) + jnp.exp(sink - m)
            p = (e / den).astype(BF16)
            oh_ref[:, hd * HEAD_DIM:(hd + 1) * HEAD_DIM] = jnp.dot(p, vg, preferred_element_type=F32)
    mix = _bdot(oh_ref[...], wo_ref[...])
    o_ref[...] = x + g1_ref[...] * mix


SHKV, SCKV = 0, 1


def _attn_prompt(x, mod_kv, mod, gkv, gmix, wkv, bkv, wq, bq, wo, sinks):
    b, t, _ = x.shape
    blk = WINDOW
    qd = N_HEADS * HEAD_DIM
    full = lambda shape: pl.BlockSpec(shape, lambda bi, ji, *_: (0,) * len(shape))
    mods = [pl.BlockSpec((None, 1, D_MODEL), lambda bi, ji, *_, k=k: (bi, 0, k))
            for k in (SHKV, SCKV, SH1, SC1, G1)]
    rows = pl.BlockSpec((None, blk, D_MODEL), lambda bi, ji, *_: (bi, ji, 0))
    last = pl.BlockSpec((None, blk, KV_DIM), lambda bi, ji, *_: (bi, 0, 0))
    grid_spec = pltpu.PrefetchScalarGridSpec(
        num_scalar_prefetch=1,
        grid=(b, t // blk),
        in_specs=[rows] + mods + [full((1, D_MODEL)), full((1, D_MODEL)),
                  full((D_MODEL, 2 * KV_DIM)), full((1, 2 * KV_DIM)), full((D_MODEL, qd)), full((1, qd)),
                  full((qd, D_MODEL))],
        out_specs=[rows, last, last],
        scratch_shapes=[pltpu.VMEM((2 * blk, KV_DIM), BF16), pltpu.VMEM((2 * blk, KV_DIM), BF16),
                        pltpu.VMEM((blk, qd), F32)],
    )
    return pl.pallas_call(
        _attn_prompt_kernel,
        grid_spec=grid_spec,
        out_shape=(jax.ShapeDtypeStruct((b, t, D_MODEL), F32),
                   jax.ShapeDtypeStruct((b, blk, KV_DIM), F32),
                   jax.ShapeDtypeStruct((b, blk, KV_DIM), F32)),
        compiler_params=_cparams(("arbitrary", "arbitrary")),
        name="attn_prompt",
    )(sinks, x, mod_kv, mod_kv, mod, mod, mod, gkv, gmix, wkv, bkv, wq, bq, wo)


def _qkv_sample_kernel(x_ref, mkv_ref, m_ref, gkv_ref, gmix_ref,
                       wkv_ref, bkv_ref, wq_ref, bq_ref, q_ref, k_ref, v_ref):
    x = x_ref[...]
    n = x.shape[0]
    hkv = _rms(x, gkv_ref[...]) * (1.0 + _mod_cols(mkv_ref, SCKV, n)) + _mod_cols(mkv_ref, SHKV, n)
    kv = _bdot(hkv, wkv_ref[...]) + bkv_ref[...]
    k_ref[...] = kv[:, :KV_DIM]
    v_ref[...] = kv[:, KV_DIM:]
    h = _rms(x, gmix_ref[...]) * (1.0 + _mod_cols(m_ref, SC1, n)) + _mod_cols(m_ref, SH1, n)
    q_ref[...] = _bdot(h, wq_ref[...]) + bq_ref[...]


def _qkv_sample(x, mod_kv, mod, gkv, gmix, wkv, bkv, wq, bq):
    n = x.shape[0]
    return pl.pallas_call(
        _qkv_sample_kernel,
        out_shape=(jax.ShapeDtypeStruct((n, N_HEADS * HEAD_DIM), F32),
                   jax.ShapeDtypeStruct((n, KV_DIM), F32), jax.ShapeDtypeStruct((n, KV_DIM), F32)),
        compiler_params=_cparams(),
        name="qkv_sample",
    )(x, mod_kv, mod, gkv, gmix, wkv, bkv, wq, bq)


def _attn_sample_kernel(q_ref, ck_ref, cv_ref, nk_ref, nv_ref, slope_ref, sink_ref, o_ref, *, steps):
    rows = steps * Q_PER_KV
    sb = q_ref.shape[0]
    r_c = lax.broadcasted_iota(I32, (sb, rows, WINDOW), 1) // Q_PER_KV
    j_c = lax.broadcasted_iota(I32, (sb, rows, WINDOW), 2)
    dist_c = r_c + WINDOW - j_c
    valid_c = (dist_c >= 0) & (dist_c < WINDOW)
    r_n = lax.broadcasted_iota(I32, (sb, rows, steps), 1) // Q_PER_KV
    j_n = lax.broadcasted_iota(I32, (sb, rows, steps), 2)
    dist_n = r_n - j_n
    valid_n = (dist_n >= 0) & (dist_n < WINDOW)
    scale = HEAD_DIM ** -0.5
    for g in range(N_KV_HEADS):
        lanes = slice(g * HEAD_DIM, (g + 1) * HEAD_DIM)
        qg = q_ref[:, g].astype(BF16)
        slope = slope_ref[g][None]
        sink = sink_ref[g][None]
        kc, vc = ck_ref[:, :, lanes].astype(BF16), cv_ref[:, :, lanes].astype(BF16)
        kn, vn = nk_ref[:, :, lanes].astype(BF16), nv_ref[:, :, lanes].astype(BF16)
        s_c = jnp.einsum('nrd,njd->nrj', qg, kc, preferred_element_type=F32) * scale
        s_n = jnp.einsum('nrd,njd->nrj', qg, kn, preferred_element_type=F32) * scale
        s_c = jnp.where(valid_c, s_c - slope * dist_c.astype(F32), NEG_INF)
        s_n = jnp.where(valid_n, s_n - slope * dist_n.astype(F32), NEG_INF)
        m = jnp.maximum(jnp.maximum(jnp.max(s_c, axis=-1, keepdims=True),
                                    jnp.max(s_n, axis=-1, keepdims=True)), sink)
        e_c = jnp.exp(s_c - m)
        e_n = jnp.exp(s_n - m)
        den = (jnp.sum(e_c, axis=-1, keepdims=True) + jnp.sum(e_n, axis=-1, keepdims=True)
               + jnp.exp(sink - m))
        o = (jnp.einsum('nrj,njd->nrd', (e_c / den).astype(BF16), vc, preferred_element_type=F32)
             + jnp.einsum('nrj,njd->nrd', (e_n / den).astype(BF16), vn, preferred_element_type=F32))
        o_ref[:, g] = o


def _attn_sample(qg, cache_k, cache_v, k_new, v_new, slope_rows, sink_rows):
    n, _, rows, _ = qg.shape
    steps = rows // Q_PER_KV
    sb = math.gcd(n, 16)
    blk = lambda shape: pl.BlockSpec((sb,) + shape, lambda i: (i,) + (0,) * len(shape))
    full = lambda shape: pl.BlockSpec(shape, lambda i: (0,) * len(shape))
    return pl.pallas_call(
        functools.partial(_attn_sample_kernel, steps=steps),
        grid=(n // sb,),
        in_specs=[blk((N_KV_HEADS, rows, HEAD_DIM)), blk((WINDOW, KV_DIM)), blk((WINDOW, KV_DIM)),
                  blk((steps, KV_DIM)), blk((steps, KV_DIM)),
                  full((N_KV_HEADS, rows, 1)), full((N_KV_HEADS, rows, 1))],
        out_specs=blk((N_KV_HEADS, rows, HEAD_DIM)),
        out_shape=jax.ShapeDtypeStruct(qg.shape, F32),
        compiler_params=_cparams(("arbitrary",)),
        name="attn_sample",
    )(qg, cache_k, cache_v, k_new, v_new, slope_rows, sink_rows)


def _oproj_kernel(o_ref, x_ref, m_ref, wo_ref, y_ref):
    y_ref[...] = x_ref[...] + _mod_cols(m_ref, G1, x_ref.shape[0]) * _bdot(o_ref[...], wo_ref[...])


def _oproj(o, x, mod, wo):
    return pl.pallas_call(
        _oproj_kernel,
        out_shape=jax.ShapeDtypeStruct(x.shape, F32),
        compiler_params=_cparams(),
        name="oproj_sample",
    )(o, x, mod, wo)


def kernel(x_prompt, x_sample, state_ssm_re, state_ssm_im, cache_k, cache_v, c_prompt, c_sample, g_mix, g_ffn, w_ada, b_ada, ssm_a_re, ssm_a_im, ssm_log_dt, ssm_b_re, ssm_b_im, ssm_c_re, ssm_c_im, ssm_d, w_glu_a, w_glu_b, g_kv, w_ada_kv, b_ada_kv, w_kv, b_kv, w_q, b_q, w_o, attn_sinks, w_router, b_router, w_gu, b_gu, w_down, b_down, g_final):
    bsz, seq, d = x_prompt.shape
    n_seq, steps, _ = x_sample.shape
    assert d == D_MODEL and seq % S5_CHUNK == 0 and seq % TM == 0 and (n_seq * steps) % TM == 0
    assert g_mix.shape[0] == 2 and ssm_a_re.shape[0] == 1 and w_q.shape[0] == 1
    n_p, n_s = bsz * seq, n_seq * steps
    tiles = _Tiles(n_p, n_s, seq, n_seq)
    row = lambda v: v.reshape(1, -1)

    c_all = jnp.concatenate([c_prompt, c_sample], axis=0)
    c_rows = -(-c_all.shape[0] // SUBLANES) * SUBLANES
    c_all = jnp.pad(c_all, ((0, c_rows - c_all.shape[0]), (0, 0)))
    mods = [_ada(c_all, w_ada, b_ada, l) for l in range(2)]
    mod_kv = _ada(c_all, w_ada_kv[None], b_ada_kv[None], 0)

    def split(m):
        return m[:bsz, None, :], m[bsz:bsz + n_seq]

    mod_p, mod_s = split(mods[0])
    abr, abi, apr, api, brh, brl, bih, bil, c_re_x, c_im_x = _ssm_prep(
        ssm_a_re[0], ssm_a_im[0], ssm_log_dt[0], ssm_b_re[0], ssm_b_im[0], ssm_c_re[0], ssm_c_im[0],
        S5_CHUNK // SUBLANES)
    cexp = (c_re_x, c_im_x)
    wa, wb = w_glu_a[0].astype(BF16), w_glu_b[0].astype(BF16)
    xp, sre_p, sim_p = _s5_prompt(x_prompt, mod_p, row(g_mix[0]), (abr, abi, apr, api), (brh, bih), cexp,
                                  row(ssm_d[0]), wa, wb)
    xs_t, sre_s, sim_s = _s5_sample(
        x_sample.transpose(1, 0, 2), mod_s, row(g_mix[0]),
        state_ssm_re[0].reshape(n_seq, SSM_COLS), state_ssm_im[0].reshape(n_seq, SSM_COLS), abr, abi,
        ((brh, brl), (bih, bil)), cexp, row(ssm_d[0]), wa, wb)
    xp = xp.reshape(n_p, D_MODEL)
    xs = xs_t.reshape(n_s, D_MODEL)
    xp, xs = _moe_layer(tiles, xp, xs, mod_p, mod_s, row(g_ffn[0]), w_router[0], b_router[0], w_gu, b_gu,
                        w_down, b_down, row(g_final), 0, False)

    mod_p, mod_s = split(mods[1])
    modkv_p, modkv_s = split(mod_kv)
    wkv, wq, wo = w_kv.astype(BF16), w_q[0].astype(BF16), w_o[0].astype(BF16)
    xp3, k_p, v_p = _attn_prompt(xp.reshape(bsz, seq, D_MODEL), modkv_p, mod_p, row(g_kv), row(g_mix[1]), wkv,
                                 row(b_kv), wq, row(b_q[0]), wo, attn_sinks[0])
    xp = xp3.reshape(n_p, D_MODEL)

    q_s, k_s, v_s = _qkv_sample(xs, modkv_s, mod_s, row(g_kv), row(g_mix[1]), wkv, row(b_kv), wq, row(b_q[0]))
    rows = steps * Q_PER_KV
    qg = q_s.reshape(steps, n_seq, N_KV_HEADS, Q_PER_KV, HEAD_DIM).transpose(1, 2, 0, 3, 4).reshape(
        n_seq, N_KV_HEADS, rows, HEAD_DIM)
    head_of_row = (np.arange(N_KV_HEADS)[:, None] * Q_PER_KV + np.arange(rows)[None, :] % Q_PER_KV)
    slope_rows = jnp.asarray(np.asarray(_slopes(), np.float32)[head_of_row][..., None])
    sink_rows = attn_sinks[0][head_of_row][..., None]
    k_new = k_s.reshape(steps, n_seq, KV_DIM).transpose(1, 0, 2)
    v_new = v_s.reshape(steps, n_seq, KV_DIM).transpose(1, 0, 2)
    ck = cache_k.reshape(n_seq, WINDOW, KV_DIM)
    cv = cache_v.reshape(n_seq, WINDOW, KV_DIM)
    og = _attn_sample(qg, ck, cv, k_new, v_new, slope_rows, sink_rows)
    o_s = og.reshape(n_seq, N_KV_HEADS, steps, Q_PER_KV, HEAD_DIM).transpose(2, 0, 1, 3, 4).reshape(
        n_s, N_HEADS * HEAD_DIM)
    xs = _oproj(o_s, xs, mod_s, wo)

    yp, ys = _moe_layer(tiles, xp, xs, mod_p, mod_s, row(g_ffn[1]), w_router[1], b_router[1], w_gu, b_gu,
                        w_down, b_down, row(g_final), 1, True)

    shape4 = lambda a: a.reshape(a.shape[0], a.shape[1], N_KV_HEADS, HEAD_DIM)
    state = lambda a, n: a.reshape(1, n, SSM_GROUPS, SSM_STATE)
    return (yp.reshape(bsz, seq, D_MODEL), ys.reshape(steps, n_seq, D_MODEL).transpose(1, 0, 2),
            state(sre_p, bsz), state(sim_p, bsz), shape4(k_p), shape4(v_p),
            state(sre_s, n_seq), state(sim_s, n_seq),
            jnp.concatenate([cache_k[:, steps:], shape4(k_new)], axis=1),
            jnp.concatenate([cache_v[:, steps:], shape4(v_new)], axis=1))
```

```python
import functools
import math

import numpy as np
import jax
import jax.numpy as jnp
from jax import lax
from jax.experimental import pallas as pl
from jax.experimental.pallas import tpu as pltpu

F32 = jnp.float32
BF16 = jnp.bfloat16
I32 = jnp.int32

D_MODEL = 1024
SSM_GROUP = 16
SSM_GROUPS = D_MODEL // SSM_GROUP
SSM_STATE = 64
SSM_COLS = SSM_GROUPS * SSM_STATE
N_HEADS = 16
HEAD_DIM = 64
N_KV_HEADS = 4
Q_PER_KV = N_HEADS // N_KV_HEADS
KV_DIM = N_KV_HEADS * HEAD_DIM
WINDOW = 128
N_EXPERTS = 32
TOP_K = 4
D_FF = D_MODEL
SWIGLU_LIMIT = 7.0
SWIGLU_ALPHA = 1.702
RMS_EPS = 1e-5
NEG_INF = -1e30

LANES = 128
SUBLANES = 8
ROW_TILES = D_MODEL // LANES
TM = 512
S5_CHUNK = 256
S5_COLS = 512
MXU_K = 256
GROUPS_PER_K = MXU_K // SSM_GROUP
N_KT = D_MODEL // MXU_K
VMEM_LIMIT = 56 * 1024 * 1024


def _cparams(sem=None):
    return pltpu.CompilerParams(dimension_semantics=sem, vmem_limit_bytes=VMEM_LIMIT)


def _sigmoid(x):
    return 1.0 / (1.0 + jnp.exp(-x))


def _rms(x, g):
    return x * lax.rsqrt(jnp.mean(x * x, axis=-1, keepdims=True) + RMS_EPS) * g


def _bdot(a, b):
    return jnp.dot(a.astype(BF16), b.astype(BF16), preferred_element_type=F32)


def _gelu_tanh(x):
    return 0.5 * x * (1.0 + jnp.tanh(math.sqrt(2.0 / math.pi) * (x + 0.044715 * (x * x * x))))


def _split_bf16(w):
    hi = w.astype(BF16)
    lo = (w - hi.astype(F32)).astype(BF16)
    return hi, lo


def _dot3(a, b_hi, b_lo):
    a_hi = a.astype(BF16)
    a_lo = (a - a_hi.astype(F32)).astype(BF16)
    return (jnp.dot(a_hi, b_hi, preferred_element_type=F32)
            + jnp.dot(a_lo, b_hi, preferred_element_type=F32)
            + jnp.dot(a_hi, b_lo, preferred_element_type=F32))


def _ada_kernel(c_ref, w_ref, b_ref, o_ref):
    c = c_ref[...]
    o_ref[...] = _bdot(c * _sigmoid(c), w_ref[...]) + b_ref[...]


def _ada(c, w, b, layer):
    rows, cols = c.shape[0], w.shape[2]
    tn = 1024
    return pl.pallas_call(
        _ada_kernel,
        grid=(cols // tn,),
        in_specs=[pl.BlockSpec((rows, D_MODEL), lambda j: (0, 0)),
                  pl.BlockSpec((None, D_MODEL, tn), lambda j: (layer, 0, j)),
                  pl.BlockSpec((None, 1, tn), lambda j: (layer, 0, j))],
        out_specs=pl.BlockSpec((rows, tn), lambda j: (0, j)),
        out_shape=jax.ShapeDtypeStruct((rows, cols), F32),
        compiler_params=_cparams(("arbitrary",)),
        name="ada",
    )(c, w, b.reshape(b.shape[0], 1, cols))


def _ssm_prep_kernel(are_ref, aim_ref, ldt_ref, bre_ref, bim_ref, cre_ref, cim_ref,
                     abr_ref, abi_ref, apr_ref, api_ref, brh_ref, brl_ref, bih_ref, bil_ref, cr_ref, ci_ref,
                     *, sub_len):
    kcols = GROUPS_PER_K * SSM_STATE
    a_re, a_im = are_ref[...], aim_ref[...]
    dt = jnp.exp(ldt_ref[...])
    mag = jnp.exp(a_re * dt)
    ab_re = mag * jnp.cos(a_im * dt)
    ab_im = mag * jnp.sin(a_im * dt)
    abr_ref[...] = ab_re
    abi_ref[...] = ab_im
    den = a_re * a_re + a_im * a_im
    x, y = ab_re - 1.0, ab_im
    f_re = (x * a_re + y * a_im) / den
    f_im = (y * a_re - x * a_im) / den
    b_re, b_im = bre_ref[...], bim_ref[...]
    bb_re = f_re * b_re - f_im * b_im
    bb_im = f_re * b_im + f_im * b_re
    same_b = (lax.broadcasted_iota(I32, (MXU_K, kcols), 0) // SSM_GROUP
              == lax.broadcasted_iota(I32, (MXU_K, kcols), 1) // SSM_STATE)
    same_c = (lax.broadcasted_iota(I32, (kcols, MXU_K), 0) // SSM_STATE
              == lax.broadcasted_iota(I32, (kcols, MXU_K), 1) // SSM_GROUP)
    for kt in range(N_KT):
        for bb, hi_ref, lo_ref in ((bb_re, brh_ref, brl_ref), (bb_im, bih_ref, bil_ref)):
            full = jnp.where(same_b, jnp.concatenate([bb[:, kt * kcols:(kt + 1) * kcols]] * GROUPS_PER_K, axis=0), 0.0)
            hi = full.astype(BF16)
            hi_ref[kt] = hi
            lo_ref[kt] = (full - hi.astype(F32)).astype(BF16)
        cr_ref[kt] = jnp.where(same_c, cre_ref[kt * kcols:(kt + 1) * kcols, :], 0.0).astype(BF16)
        ci_ref[kt] = jnp.where(same_c, -cim_ref[kt * kcols:(kt + 1) * kcols, :], 0.0).astype(BF16)
    pr, pi = ab_re, ab_im
    for _ in range(int(math.log2(sub_len))):
        pr, pi = pr * pr - pi * pi, 2.0 * pr * pi
    cr, ci = jnp.ones_like(pr), jnp.zeros_like(pi)
    for j in range(SUBLANES + 1):
        apr_ref[j:j + 1, :] = cr
        api_ref[j:j + 1, :] = ci
        cr, ci = cr * pr - ci * pi, cr * pi + ci * pr


def _ssm_prep(a_re, a_im, log_dt, b_re, b_im, c_re, c_im, sub_len):
    gp = SSM_COLS
    kcols = GROUPS_PER_K * SSM_STATE
    ldt = jnp.broadcast_to(log_dt[:, None], (SSM_GROUPS, SSM_STATE)).reshape(1, gp)
    b_re_t = b_re.reshape(gp, SSM_GROUP).T
    b_im_t = b_im.reshape(gp, SSM_GROUP).T
    c_rows = lambda c: jnp.tile(c.transpose(0, 2, 1).reshape(gp, SSM_GROUP), (1, GROUPS_PER_K))
    row = jax.ShapeDtypeStruct((1, gp), F32)
    pw = jax.ShapeDtypeStruct((SUBLANES + 1, gp), F32)
    bx = jax.ShapeDtypeStruct((N_KT, MXU_K, kcols), BF16)
    cx = jax.ShapeDtypeStruct((N_KT, kcols, MXU_K), BF16)
    return pl.pallas_call(
        functools.partial(_ssm_prep_kernel, sub_len=sub_len),
        out_shape=(row, row, pw, pw, bx, bx, bx, bx, cx, cx),
        compiler_params=_cparams(),
        name="ssm_prep",
    )(a_re.reshape(1, gp), a_im.reshape(1, gp), ldt, b_re_t, b_im_t, c_rows(c_re), c_rows(c_im))


def _cmul(ar, ai, br, bi):
    return ar * br - ai * bi, ar * bi + ai * br


def _s5_prompt_kernel(x_ref, sh_ref, sc_ref, g1_ref, gmix_ref, abr_ref, abi_ref, apr_ref, api_ref,
                      bre_ref, bim_ref, cre_ref, cim_ref, dsk_ref, wa_ref, wb_ref, perm_ref, unperm_ref,
                      o_ref, sre_ref, sim_ref,
                      u_ref, hr_ref, hi_ref, hmr_ref, hmi_ref, car_ref, cai_ref, *, L):
    ls = L // SUBLANES
    kcols = GROUPS_PER_K * SSM_STATE

    @pl.when(pl.program_id(1) == 0)
    def _():
        car_ref[...] = jnp.zeros_like(car_ref)
        cai_ref[...] = jnp.zeros_like(cai_ref)

    x = x_ref[...]
    u = _rms(x, gmix_ref[...]) * (1.0 + sc_ref[...]) + sh_ref[...]
    u_ref[...] = u
    ub = jnp.dot(perm_ref[...], u.astype(BF16), preferred_element_type=F32).astype(BF16)
    for kt in range(N_KT):
        uk = ub[:, kt * MXU_K:(kt + 1) * MXU_K]
        hr_ref[:, kt * kcols:(kt + 1) * kcols] = jnp.dot(uk, bre_ref[kt], preferred_element_type=F32)
        hi_ref[:, kt * kcols:(kt + 1) * kcols] = jnp.dot(uk, bim_ref[kt], preferred_element_type=F32)

    for cb in range(SSM_COLS // S5_COLS):
        cols = slice(cb * S5_COLS, (cb + 1) * S5_COLS)
        ar = jnp.broadcast_to(abr_ref[:, cols], (SUBLANES, S5_COLS))
        ai = jnp.broadcast_to(abi_ref[:, cols], (SUBLANES, S5_COLS))

        sr = si = jnp.zeros((SUBLANES, S5_COLS), F32)
        for i in range(ls):
            rows = slice(i * SUBLANES, (i + 1) * SUBLANES)
            tr, ti = _cmul(ar, ai, sr, si)
            sr = tr + hr_ref[rows, cols]
            si = ti + hi_ref[rows, cols]
            hr_ref[rows, cols] = sr
            hi_ref[rows, cols] = si

    fr = hr_ref[L - SUBLANES:L, :]
    fi = hi_ref[L - SUBLANES:L, :]
    row = lax.broadcasted_iota(I32, (SUBLANES, SSM_COLS), 0)
    gr, gi = fr, fi
    pr, pi = apr_ref[1:2, :], api_ref[1:2, :]
    for s in (1, 2, 4):
        sr = jnp.where(row >= s, pltpu.roll(gr, s, axis=0), 0.0)
        si = jnp.where(row >= s, pltpu.roll(gi, s, axis=0), 0.0)
        tr, ti = _cmul(pr, pi, sr, si)
        gr, gi = gr + tr, gi + ti
        pr, pi = _cmul(pr, pi, pr, pi)
    c0r, c0i = car_ref[...], cai_ref[...]
    tr, ti = _cmul(apr_ref[0:SUBLANES, :], api_ref[0:SUBLANES, :], c0r, c0i)
    hmr = tr + jnp.where(row >= 1, pltpu.roll(gr, 1, axis=0), 0.0)
    hmi = ti + jnp.where(row >= 1, pltpu.roll(gi, 1, axis=0), 0.0)
    hmr_ref[...] = hmr
    hmi_ref[...] = hmi
    tr, ti = _cmul(apr_ref[SUBLANES:SUBLANES + 1, :], api_ref[SUBLANES:SUBLANES + 1, :], c0r, c0i)
    ncr = tr + gr[SUBLANES - 1:SUBLANES, :]
    nci = ti + gi[SUBLANES - 1:SUBLANES, :]
    car_ref[...] = ncr
    cai_ref[...] = nci
    sre_ref[...] = ncr
    sim_ref[...] = nci

    for cb in range(SSM_COLS // S5_COLS):
        cols = slice(cb * S5_COLS, (cb + 1) * S5_COLS)
        ar = jnp.broadcast_to(abr_ref[:, cols], (SUBLANES, S5_COLS))
        ai = jnp.broadcast_to(abi_ref[:, cols], (SUBLANES, S5_COLS))

        dr, di = hmr_ref[:, cols], hmi_ref[:, cols]
        for i in range(ls):
            rows = slice(i * SUBLANES, (i + 1) * SUBLANES)
            dr, di = _cmul(ar, ai, dr, di)
            hr_ref[rows, cols] = hr_ref[rows, cols] + dr
            hi_ref[rows, cols] = hi_ref[rows, cols] + di

    ys = []
    for nt in range(N_KT):
        hr = hr_ref[:, nt * kcols:(nt + 1) * kcols].astype(BF16)
        hi = hi_ref[:, nt * kcols:(nt + 1) * kcols].astype(BF16)
        ys.append(jnp.dot(hr, cre_ref[nt], preferred_element_type=F32)
                  + jnp.dot(hi, cim_ref[nt], preferred_element_type=F32))
    yp = jnp.concatenate(ys, axis=-1)
    yp_hi = yp.astype(BF16)
    yp_lo = (yp - yp_hi.astype(F32)).astype(BF16)
    unperm = unperm_ref[...]
    y = (jnp.dot(unperm, yp_hi, preferred_element_type=F32)
         + jnp.dot(unperm, yp_lo, preferred_element_type=F32)) + dsk_ref[...] * u_ref[...]
    z = _gelu_tanh(y).astype(BF16)
    mix = jnp.dot(z, wa_ref[...], preferred_element_type=F32) * _sigmoid(
        jnp.dot(z, wb_ref[...], preferred_element_type=F32))
    o_ref[...] = x_ref[...] + g1_ref[...] * mix


SH1, SC1, G1 = 0, 1, 2


def _s5_prompt(x, mod, gmix, prep, bexp, cexp, dsk, wa, wb):
    b, t, _ = x.shape
    L = S5_CHUNK
    abr, abi, apr, api = prep
    bre, bim = bexp
    cre, cim = cexp
    kcols = GROUPS_PER_K * SSM_STATE
    full = lambda shape: pl.BlockSpec(shape, lambda bi, ci: (0,) * len(shape))
    mods = [pl.BlockSpec((None, 1, D_MODEL), lambda bi, ci, k=k: (bi, 0, k)) for k in (SH1, SC1, G1)]
    r = np.arange(L)
    perm_np = np.zeros((L, L), np.float32)
    perm_np[r, (r % SUBLANES) * (L // SUBLANES) + r // SUBLANES] = 1.0
    perm, unperm = jnp.asarray(perm_np, BF16), jnp.asarray(perm_np.T, BF16)
    return pl.pallas_call(
        functools.partial(_s5_prompt_kernel, L=L),
        grid=(b, t // L),
        in_specs=[pl.BlockSpec((None, L, D_MODEL), lambda bi, ci: (bi, ci, 0))] + mods + [
                  full((1, D_MODEL)), full((1, SSM_COLS)), full((1, SSM_COLS)),
                  full((SUBLANES + 1, SSM_COLS)), full((SUBLANES + 1, SSM_COLS)),
                  full((N_KT, MXU_K, kcols)), full((N_KT, MXU_K, kcols)),
                  full((N_KT, kcols, MXU_K)), full((N_KT, kcols, MXU_K)),
                  full((1, D_MODEL)), full((D_MODEL, D_MODEL)), full((D_MODEL, D_MODEL)),
                  full((L, L)), full((L, L))],
        out_specs=[pl.BlockSpec((None, L, D_MODEL), lambda bi, ci: (bi, ci, 0)),
                   pl.BlockSpec((None, 1, SSM_COLS), lambda bi, ci: (bi, 0, 0)),
                   pl.BlockSpec((None, 1, SSM_COLS), lambda bi, ci: (bi, 0, 0))],
        out_shape=(jax.ShapeDtypeStruct((b, t, D_MODEL), F32),
                   jax.ShapeDtypeStruct((b, 1, SSM_COLS), F32),
                   jax.ShapeDtypeStruct((b, 1, SSM_COLS), F32)),
        scratch_shapes=[pltpu.VMEM((L, D_MODEL), F32),
                        pltpu.VMEM((L, SSM_COLS), F32), pltpu.VMEM((L, SSM_COLS), F32),
                        pltpu.VMEM((SUBLANES, SSM_COLS), F32), pltpu.VMEM((SUBLANES, SSM_COLS), F32),
                        pltpu.VMEM((1, SSM_COLS), F32), pltpu.VMEM((1, SSM_COLS), F32)],
        compiler_params=_cparams(("arbitrary", "arbitrary")),
        name="s5_prompt",
    )(x, mod, mod, mod, gmix, abr, abi, apr, api, bre, bim, cre, cim, dsk, wa, wb, perm, unperm)


def _tile_rows(v, rows):
    reps = rows // v.shape[0]
    return v if reps == 1 else jnp.concatenate([v] * reps, axis=0)


def _mod_cols(m_ref, k, rows=None):
    v = m_ref[:, k * D_MODEL:(k + 1) * D_MODEL]
    return v if rows is None else _tile_rows(v, rows)


def _s5_sample_kernel(x_ref, m_ref, gmix_ref, h0r_ref, h0i_ref, abr_ref, abi_ref,
                      brh_ref, brl_ref, bih_ref, bil_ref, cre_ref, cim_ref, dsk_ref, wa_ref, wb_ref,
                      o_ref, sre_ref, sim_ref, *, steps):
    kcols = GROUPS_PER_K * SSM_STATE
    ar, ai = abr_ref[...], abi_ref[...]
    sr, si = h0r_ref[...], h0i_ref[...]
    sh, sc, g1 = _mod_cols(m_ref, SH1), _mod_cols(m_ref, SC1), _mod_cols(m_ref, G1)
    for t in range(steps):
        x = x_ref[t]
        u = _rms(x, gmix_ref[...]) * (1.0 + sc) + sh
        bur, bui = [], []
        for kt in range(N_KT):
            uk = u[:, kt * MXU_K:(kt + 1) * MXU_K]
            bur.append(_dot3(uk, brh_ref[kt], brl_ref[kt]))
            bui.append(_dot3(uk, bih_ref[kt], bil_ref[kt]))
        tr, ti = _cmul(ar, ai, sr, si)
        sr = tr + jnp.concatenate(bur, axis=-1)
        si = ti + jnp.concatenate(bui, axis=-1)
        ys = []
        for nt in range(N_KT):
            ys.append(_bdot(sr[:, nt * kcols:(nt + 1) * kcols], cre_ref[nt])
                      + _bdot(si[:, nt * kcols:(nt + 1) * kcols], cim_ref[nt]))
        y = jnp.concatenate(ys, axis=-1) + dsk_ref[...] * u
        z = _gelu_tanh(y).astype(BF16)
        mix = jnp.dot(z, wa_ref[...], preferred_element_type=F32) * _sigmoid(
            jnp.dot(z, wb_ref[...], preferred_element_type=F32))
        o_ref[t] = x + g1 * mix
    sre_ref[...] = sr
    sim_ref[...] = si


def _s5_sample(x_t, mod, gmix, h0r, h0i, abr, abi, bsplit, cexp, dsk, wa, wb):
    steps, n, _ = x_t.shape
    (brh, brl), (bih, bil) = bsplit
    cre, cim = cexp
    return pl.pallas_call(
        functools.partial(_s5_sample_kernel, steps=steps),
        out_shape=(jax.ShapeDtypeStruct((steps, n, D_MODEL), F32),
                   jax.ShapeDtypeStruct((n, SSM_COLS), F32),
                   jax.ShapeDtypeStruct((n, SSM_COLS), F32)),
        compiler_params=_cparams(),
        name="s5_sample",
    )(x_t, mod, gmix, h0r, h0i, abr, abi, brh, brl, bih, bil, cre, cim, dsk, wa, wb)


class _Tiles:
    def __init__(self, n_prompt, n_sample, seq_len, n_seq):
        assert TM % n_seq == 0
        self.n_seq = n_seq
        self.ntp = n_prompt // TM
        self.nts = n_sample // TM
        self.nt = self.ntp + self.nts
        self.tiles_per_seq = seq_len // TM
        self.n_batch = n_prompt // seq_len
        self.n_tok = n_prompt + n_sample

    def p_rows(self):
        return pl.BlockSpec((TM, D_MODEL), lambda t, *_: (jnp.minimum(t, self.ntp - 1), 0))

    def s_rows(self):
        return pl.BlockSpec((TM, D_MODEL), lambda t, *_: (jnp.maximum(t - self.ntp, 0), 0))

    def s_mod(self, chunk):
        return pl.BlockSpec((self.n_seq, D_MODEL), lambda t, *_: (0, chunk))

    def p_mod(self, chunk):
        return pl.BlockSpec((None, 1, D_MODEL),
                            lambda t, *_: (jnp.minimum(t // self.tiles_per_seq, self.n_batch - 1), 0, chunk))

    def all_rows(self, width):
        return pl.BlockSpec((TM, width), lambda t, *_: (t, 0))


def _const_spec(shape):
    return pl.BlockSpec(shape, lambda t, *_: (0,) * len(shape))


def _by_kind(tiles, body, prompt_refs, sample_refs):
    t = pl.program_id(0)

    @pl.when(t < tiles.ntp)
    def _():
        body(*prompt_refs)

    @pl.when(t >= tiles.ntp)
    def _():
        body(*sample_refs)


def _mod_rows(m_ref):
    v = m_ref[...]
    return v if v.shape[0] == 1 else _tile_rows(v, TM)


def _moe_input(x_ref, sh_ref, sc_ref, g_ref):
    return _rms(x_ref[...], g_ref[...]) * (1.0 + _mod_rows(sc_ref)) + _mod_rows(sh_ref)


def _router_kernel(xp_ref, xs_ref, shp_ref, shs_ref, scp_ref, scs_ref, g_ref, wh_ref, wl_ref, b_ref, tri_ref,
                   mi_ref, mw_ref, cnt_ref, run_ref, *, tiles):
    @pl.when(pl.program_id(0) == 0)
    def _():
        run_ref[...] = jnp.zeros_like(run_ref)

    def route(x_ref, sh_ref, sc_ref):
        h = _moe_input(x_ref, sh_ref, sc_ref, g_ref)
        logits = _dot3(h, wh_ref[...], wl_ref[...]) + b_ref[...]
        lane = lax.broadcasted_iota(I32, (TM, LANES), 1)
        lane_f = lane.astype(F32)
        vals, firsts, hots = [], [], []
        work = logits
        for _ in range(TOP_K):
            m = jnp.max(work, axis=-1, keepdims=True)
            first = jnp.min(jnp.where(work == m, lane_f, float(LANES)), axis=-1, keepdims=True)
            hot = lane_f == first
            vals.append(m)
            firsts.append(first)
            hots.append(hot)
            work = jnp.where(hot, -jnp.inf, work)
        exps = [jnp.exp(v - vals[0]) for v in vals]
        den = exps[0] + exps[1] + exps[2] + exps[3]

        chosen = jnp.zeros((TM, LANES), F32)
        for hot in hots:
            chosen = jnp.where(hot, 1.0, chosen)
        rank_all = jnp.dot(tri_ref[...], chosen.astype(BF16), preferred_element_type=F32) + run_ref[...]
        run_ref[...] = run_ref[...] + jnp.sum(chosen, axis=0, keepdims=True)
        cnt_ref[...] = run_ref[...]

        code = jnp.zeros((TM, LANES), F32)
        mw = jnp.zeros((TM, LANES), F32)
        for k in range(TOP_K):
            r_k = jnp.sum(jnp.where(hots[k], rank_all, 0.0), axis=-1, keepdims=True)
            code = jnp.where(lane == k, r_k * float(N_EXPERTS) + firsts[k], code)
            mw = jnp.where(lane == k, exps[k] / den, mw)
        mw_ref[...] = mw
        code_t = code.T
        mi_ref[...] = jnp.concatenate(
            [code_t[k:k + 1, c * LANES:(c + 1) * LANES] for k in range(TOP_K) for c in range(TM // LANES)],
            axis=0).astype(I32)

    _by_kind(tiles, route, (xp_ref, shp_ref, scp_ref), (xs_ref, shs_ref, scs_ref))


SH2, SC2, G2 = 3, 4, 5


def _router(tiles, xp, xs, mod_p, mod_s, g, w_hi, w_lo, b_pad):
    tri = jnp.asarray(np.tril(np.ones((TM, TM), np.float32), -1), BF16)
    return pl.pallas_call(
        functools.partial(_router_kernel, tiles=tiles),
        grid=(tiles.nt,),
        in_specs=[tiles.p_rows(), tiles.s_rows(), tiles.p_mod(SH2), tiles.s_mod(SH2), tiles.p_mod(SC2),
                  tiles.s_mod(SC2), _const_spec((1, D_MODEL)), _const_spec((D_MODEL, LANES)),
                  _const_spec((D_MODEL, LANES)), _const_spec((1, LANES)), _const_spec((TM, TM))],
        out_specs=[pl.BlockSpec((TOP_K * TM // LANES, LANES), lambda t: (t, 0)), tiles.all_rows(LANES),
                   _const_spec((1, LANES))],
        out_shape=(jax.ShapeDtypeStruct((tiles.n_tok * TOP_K // LANES, LANES), I32),
                   jax.ShapeDtypeStruct((tiles.n_tok, LANES), F32),
                   jax.ShapeDtypeStruct((1, LANES), F32)),
        scratch_shapes=[pltpu.VMEM((1, LANES), F32)],
        compiler_params=_cparams(("arbitrary",)),
        name="router",
    )(xp, xs, mod_p, mod_s, mod_p, mod_s, g, w_hi, w_lo, b_pad, tri)


PAD_ARMS = tuple(1 << i for i in reversed(range(int(math.log2(TM)))))


def _dispatch_kernel(slot_ref, pstart_ref, pcnt_ref,
                     xp_ref, xs_ref, shp_ref, shs_ref, scp_ref, scs_ref, g_ref,
                     o_hbm, rows_ref, zero_ref, sems, zsem, *, tiles):
    t = pl.program_id(0)
    buf = t % 2
    tile_rows = TM * ROW_TILES
    base = pl.multiple_of(buf * tile_rows, tile_rows)

    def tile_wait(b):
        b0 = pl.multiple_of(b * tile_rows, tile_rows)
        for _ in range(TOP_K):
            pltpu.make_async_copy(rows_ref.at[pl.ds(b0, tile_rows), :],
                                  o_hbm.at[pl.ds(0, tile_rows), :], sems.at[b]).wait()

    def pad_copies(act):
        def body(e, carry):
            start, cnt = pstart_ref[e], pcnt_ref[e]
            for p in PAD_ARMS:
                @pl.when((cnt & p) != 0)
                def _(start=start, p=p):
                    cp = pltpu.make_async_copy(
                        zero_ref.at[pl.ds(0, p * ROW_TILES), :],
                        o_hbm.at[pl.ds(pl.multiple_of(start * ROW_TILES, ROW_TILES), p * ROW_TILES), :],
                        zsem)
                    cp.start() if act == "start" else cp.wait()
                start = start + jnp.where((cnt & p) != 0, p, 0)
            return carry
        lax.fori_loop(0, N_EXPERTS, body, 0)

    @pl.when(t == 0)
    def _():
        zero_ref[...] = jnp.zeros_like(zero_ref)
        pad_copies("start")

    @pl.when(t >= 2)
    def _():
        tile_wait(buf)

    def stage(x_ref, sh_ref, sc_ref):
        h = _moe_input(x_ref, sh_ref, sc_ref, g_ref)
        for c in range(ROW_TILES):
            rows_ref[pl.ds(base + c, TM, stride=ROW_TILES), :] = h[:, c * LANES:(c + 1) * LANES]

    _by_kind(tiles, stage, (xp_ref, shp_ref, scp_ref), (xs_ref, shs_ref, scs_ref))

    def issue(n, carry):
        src = rows_ref.at[pl.ds(pl.multiple_of(base + n * ROW_TILES, ROW_TILES), ROW_TILES), :]
        for k in range(TOP_K):
            slot = slot_ref[(t * TOP_K + k) * TM + n]
            dst = o_hbm.at[pl.ds(pl.multiple_of(slot * ROW_TILES, ROW_TILES), ROW_TILES), :]
            pltpu.make_async_copy(src, dst, sems.at[buf]).start()
        return carry
    lax.fori_loop(0, TM, issue, 0)

    @pl.when(t == tiles.nt - 1)
    def _():
        if tiles.nt >= 2:
            tile_wait(1 - buf)
        tile_wait(buf)
        pad_copies("wait")


def _dispatch(tiles, n_slots, slots, pad_start, pad_cnt, xp, xs, mod_p, mod_s, g):
    grid_spec = pltpu.PrefetchScalarGridSpec(
        num_scalar_prefetch=3,
        grid=(tiles.nt,),
        in_specs=[tiles.p_rows(), tiles.s_rows(), tiles.p_mod(SH2), tiles.s_mod(SH2), tiles.p_mod(SC2),
                  tiles.s_mod(SC2), _const_spec((1, D_MODEL))],
        out_specs=pl.BlockSpec(memory_space=pl.ANY),
        scratch_shapes=[pltpu.VMEM((2 * TM * ROW_TILES, LANES), F32),
                        pltpu.VMEM((PAD_ARMS[0] * ROW_TILES, LANES), F32),
                        pltpu.SemaphoreType.DMA((2,)), pltpu.SemaphoreType.DMA(())],
    )
    return pl.pallas_call(
        functools.partial(_dispatch_kernel, tiles=tiles),
        grid_spec=grid_spec,
        out_shape=jax.ShapeDtypeStruct((n_slots * ROW_TILES, LANES), F32),
        compiler_params=_cparams(("arbitrary",)),
        name="dispatch",
    )(slots, pad_start, pad_cnt, xp, xs, mod_p, mod_s, mod_p, mod_s, g)


def _expert_kernel(te_ref, nx_ref, nt_ref, hs_ref, wgu_hbm, bgu_ref, wd_hbm, bd_ref, o_ref,
                   wgu_f32, wd_f32, wgu_bf, wd_bf, lhs_ref, wsem, *, layer):
    i = pl.program_id(0)

    def fetch(expert):
        return (pltpu.make_async_copy(wgu_hbm.at[layer, expert], wgu_f32, wsem.at[0]),
                pltpu.make_async_copy(wd_hbm.at[layer, expert], wd_f32, wsem.at[1]))

    @pl.when(i < nt_ref[0])
    def _():
        expert = te_ref[i]

        @pl.when(i == 0)
        def _():
            for cp in fetch(expert):
                cp.start()

        @pl.when((i == 0) | (expert != te_ref[jnp.maximum(i - 1, 0)]))
        def _():
            for cp in fetch(expert):
                cp.wait()
            wgu_bf[...] = wgu_f32[...].astype(BF16)
            wd_bf[...] = wd_f32[...].astype(BF16)
            upcoming = nx_ref[i]

            @pl.when(upcoming >= 0)
            def _():
                for cp in fetch(upcoming):
                    cp.start()

        for c in range(ROW_TILES):
            lhs_ref[:, c * LANES:(c + 1) * LANES] = hs_ref[pl.ds(c, TM, stride=ROW_TILES), :].astype(BF16)
        gu = jnp.dot(lhs_ref[...], wgu_bf[...], preferred_element_type=F32) + bgu_ref[...]
        gate = jnp.minimum(gu[:, :D_FF], SWIGLU_LIMIT)
        up = jnp.clip(gu[:, D_FF:], -SWIGLU_LIMIT, SWIGLU_LIMIT)
        act = ((up + 1.0) * gate * _sigmoid(SWIGLU_ALPHA * gate)).astype(BF16)
        y = jnp.dot(act, wd_bf[...], preferred_element_type=F32) + bd_ref[...]
        for c in range(ROW_TILES):
            o_ref[pl.ds(c, TM, stride=ROW_TILES), :] = y[:, c * LANES:(c + 1) * LANES]


def _experts(max_tiles, tile_expert, next_expert, n_tiles, hs, w_gu, b_gu, w_down, b_down, layer):
    tile_rows = TM * ROW_TILES
    row_map = lambda i, te, nx, nt: (jnp.minimum(i, nt[0] - 1), 0)
    b_map = lambda i, te, nx, nt: (layer, te[i], 0, 0)
    grid_spec = pltpu.PrefetchScalarGridSpec(
        num_scalar_prefetch=3,
        grid=(max_tiles,),
        in_specs=[pl.BlockSpec((tile_rows, LANES), row_map),
                  pl.BlockSpec(memory_space=pl.ANY),
                  pl.BlockSpec((None, None, 1, 2 * D_FF), b_map),
                  pl.BlockSpec(memory_space=pl.ANY),
                  pl.BlockSpec((None, None, 1, D_MODEL), b_map)],
        out_specs=pl.BlockSpec((tile_rows, LANES), row_map),
        scratch_shapes=[pltpu.VMEM((D_MODEL, 2 * D_FF), F32), pltpu.VMEM((D_FF, D_MODEL), F32),
                        pltpu.VMEM((D_MODEL, 2 * D_FF), BF16), pltpu.VMEM((D_FF, D_MODEL), BF16),
                        pltpu.VMEM((TM, D_MODEL), BF16), pltpu.SemaphoreType.DMA((2,))],
    )
    return pl.pallas_call(
        functools.partial(_expert_kernel, layer=layer),
        grid_spec=grid_spec,
        out_shape=jax.ShapeDtypeStruct(hs.shape, F32),
        compiler_params=_cparams(("arbitrary",)),
        name="experts",
    )(tile_expert, next_expert, n_tiles, hs, w_gu, b_gu[:, :, None, :], w_down, b_down[:, :, None, :])


def _combine_kernel(slot_ref, y_hbm, mw_ref, xp_ref, xs_ref, g2p_ref, g2s_ref, gfin_ref,
                    op_ref, os_ref, gbuf, acc_ref, sems, *, tiles, final_norm):
    t = pl.program_id(0)
    tile_rows = TM * ROW_TILES
    buf_rows = TOP_K * tile_rows

    def issue_tile(tt, b):
        b0 = pl.multiple_of(b * buf_rows, buf_rows)

        for k in range(TOP_K):
            def issue(it, carry, k=k):
                for u in range(TOP_K):
                    n = it * TOP_K + u
                    slot = slot_ref[(tt * TOP_K + k) * TM + n]
                    src = y_hbm.at[pl.ds(pl.multiple_of(slot * ROW_TILES, ROW_TILES), ROW_TILES), :]
                    dst = gbuf.at[pl.ds(pl.multiple_of(b0 + k * tile_rows + n * ROW_TILES, ROW_TILES),
                                        ROW_TILES), :]
                    pltpu.make_async_copy(src, dst, sems.at[b]).start()
                return carry
            lax.fori_loop(0, TM // TOP_K, issue, 0)

    @pl.when(t == 0)
    def _():
        issue_tile(0, 0)

    @pl.when(t + 1 < tiles.nt)
    def _():
        issue_tile(t + 1, (t + 1) % 2)

    buf = t % 2
    b0 = pl.multiple_of(buf * buf_rows, buf_rows)
    for _ in range(TOP_K):
        pltpu.make_async_copy(y_hbm.at[pl.ds(0, tile_rows), :],
                              gbuf.at[pl.ds(b0, tile_rows), :], sems.at[buf]).wait()

    w = mw_ref[...]
    for c in range(ROW_TILES):
        acc = None
        for k in range(TOP_K):
            rows = gbuf[pl.ds(b0 + k * tile_rows + c, TM, stride=ROW_TILES), :]
            term = w[:, k:k + 1] * rows
            acc = term if acc is None else acc + term
        acc_ref[:, c * LANES:(c + 1) * LANES] = acc

    def finish(x_ref, g2_ref, o_ref):
        res = x_ref[...] + _mod_rows(g2_ref) * acc_ref[...]
        if final_norm:
            res = _rms(res, gfin_ref[...])
        o_ref[...] = res

    _by_kind(tiles, finish, (xp_ref, g2p_ref, op_ref), (xs_ref, g2s_ref, os_ref))


def _combine(tiles, slots, y_sorted, mw, xp, xs, mod_p, mod_s, gfin, final_norm):
    grid_spec = pltpu.PrefetchScalarGridSpec(
        num_scalar_prefetch=1,
        grid=(tiles.nt,),
        in_specs=[pl.BlockSpec(memory_space=pl.ANY), tiles.all_rows(LANES), tiles.p_rows(), tiles.s_rows(),
                  tiles.p_mod(G2), tiles.s_mod(G2), _const_spec((1, D_MODEL))],
        out_specs=[tiles.p_rows(), tiles.s_rows()],
        scratch_shapes=[pltpu.VMEM((2 * TOP_K * TM * ROW_TILES, LANES), F32),
                        pltpu.VMEM((TM, D_MODEL), F32), pltpu.SemaphoreType.DMA((2,))],
    )
    return pl.pallas_call(
        functools.partial(_combine_kernel, tiles=tiles, final_norm=final_norm),
        grid_spec=grid_spec,
        out_shape=(jax.ShapeDtypeStruct(xp.shape, F32), jax.ShapeDtypeStruct(xs.shape, F32)),
        compiler_params=_cparams(("arbitrary",)),
        name="combine",
    )(slots, y_sorted, mw, xp, xs, mod_p, mod_s, gfin)


def _moe_layer(tiles, xp, xs, mod_p, mod_s, g_ffn, w_router, b_router, w_gu, b_gu, w_down, b_down, gfin,
               layer, final_norm):
    n_tok = tiles.n_tok
    w_pad = jnp.pad(w_router, ((0, 0), (0, LANES - N_EXPERTS)))
    w_hi, w_lo = _split_bf16(w_pad)
    b_pad = jnp.pad(b_router, (0, LANES - N_EXPERTS), constant_values=NEG_INF).reshape(1, LANES)
    mi, mw, counts = _router(tiles, xp, xs, mod_p, mod_s, g_ffn, w_hi, w_lo, b_pad)

    cnt = counts[0, :N_EXPERTS].astype(I32)
    tiles_e = (cnt + TM - 1) // TM
    tile_end = jnp.cumsum(tiles_e)
    tile_start = tile_end - tiles_e
    first_slot = jnp.sum(jnp.where((mi & (N_EXPERTS - 1))[..., None] == jnp.arange(N_EXPERTS, dtype=I32),
                                   (tile_start * TM)[None, None, :], 0), axis=-1)
    slots = (first_slot + (mi >> int(math.log2(N_EXPERTS)))).reshape(n_tok * TOP_K)
    max_tiles = (n_tok * TOP_K) // TM + N_EXPERTS
    n_tiles = tile_end[-1:].astype(I32)
    tile_ids = jnp.arange(max_tiles, dtype=I32)
    tile_expert = jnp.minimum(jnp.sum(tile_ids[:, None] >= tile_end[None, :], axis=-1), N_EXPERTS - 1).astype(I32)
    group_end = jnp.sum(jnp.where(tile_expert[:, None] == jnp.arange(N_EXPERTS, dtype=I32), tile_end[None, :], 0),
                        axis=-1)
    after = jnp.minimum(jnp.sum(group_end[:, None] >= tile_end[None, :], axis=-1), N_EXPERTS - 1).astype(I32)
    next_expert = jnp.where(group_end < n_tiles[0], after, -1).astype(I32)
    pad_start = (tile_start * TM + cnt).astype(I32)
    pad_cnt = (tiles_e * TM - cnt).astype(I32)

    hs = _dispatch(tiles, max_tiles * TM, slots, pad_start, pad_cnt, xp, xs, mod_p, mod_s, g_ffn)
    ys = _experts(max_tiles, tile_expert, next_expert, n_tiles, hs, w_gu, b_gu, w_down, b_down, layer)
    return _combine(tiles, slots, ys, mw, xp, xs, mod_p, mod_s, gfin, final_norm)


ATTN_STEP_BLOCKS = 2


def _slopes():
    return [2.0 ** (-8.0 * (h + 1) / N_HEADS) for h in range(N_HEADS)]


def _attn_prompt_kernel(sink_ref, x_ref, shkv_ref, sckv_ref, sh_ref, sc_ref, g1_ref, gkv_ref, gmix_ref,
                        wkv_ref, bkv_ref, wq_ref, bq_ref, wo_ref,
                        o_ref, kl_ref, vl_ref, kk_ref, vv_ref, oh_ref):
    j = pl.program_id(1)
    blk = WINDOW
    nb = ATTN_STEP_BLOCKS

    @pl.when(j == 0)
    def _():
        kk_ref[...] = jnp.zeros_like(kk_ref)
        vv_ref[...] = jnp.zeros_like(vv_ref)

    @pl.when(j > 0)
    def _():
        kk_ref[0:blk, :] = kk_ref[nb * blk:(nb + 1) * blk, :]
        vv_ref[0:blk, :] = vv_ref[nb * blk:(nb + 1) * blk, :]

    x = x_ref[...]
    hkv = _rms(x, gkv_ref[...]) * (1.0 + sckv_ref[...]) + shkv_ref[...]
    kv = _bdot(hkv, wkv_ref[...]) + bkv_ref[...]
    kl_ref[...] = kv[(nb - 1) * blk:, :KV_DIM]
    vl_ref[...] = kv[(nb - 1) * blk:, KV_DIM:]
    kk_ref[blk:(nb + 1) * blk, :] = kv[:, :KV_DIM].astype(BF16)
    vv_ref[blk:(nb + 1) * blk, :] = kv[:, KV_DIM:].astype(BF16)

    h = _rms(x, gmix_ref[...]) * (1.0 + sc_ref[...]) + sh_ref[...]
    q = _bdot(h, wq_ref[...]) + bq_ref[...]

    r = lax.broadcasted_iota(I32, (blk, 2 * blk), 0)
    c = lax.broadcasted_iota(I32, (blk, 2 * blk), 1)
    dist_i = r + blk - c
    visible = (dist_i >= 0) & (dist_i < WINDOW)
    dist = dist_i.astype(F32)
    slopes = _slopes()
    for qb in range(nb):
        valid = visible & jnp.logical_not((j == 0) & (c < blk)) if qb == 0 else visible
        q_rows = slice(qb * blk, (qb + 1) * blk)
        k_rows = slice(qb * blk, (qb + 2) * blk)
        for g in range(N_KV_HEADS):
            kg = kk_ref[k_rows, g * HEAD_DIM:(g + 1) * HEAD_DIM]
            vg = vv_ref[k_rows, g * HEAD_DIM:(g + 1) * HEAD_DIM]
            for qh in range(Q_PER_KV):
                hd = g * Q_PER_KV + qh
                qd = q[q_rows, hd * HEAD_DIM:(hd + 1) * HEAD_DIM].astype(BF16)
                s = lax.dot_general(qd, kg, (((1,), (1,)), ((), ())), preferred_element_type=F32)
                s = s * (HEAD_DIM ** -0.5) - slopes[hd] * dist
                s = jnp.where(valid, s, NEG_INF)
                sink = sink_ref[hd]
                m = jnp.maximum(jnp.max(s, axis=-1, keepdims=True), sink)
                e = jnp.exp(s - m)
                den = jnp.sum(e, axis=-1, keepdims=True) + jnp.exp(sink - m)
                p = (e / den).astype(BF16)
                oh_ref[q_rows, hd * HEAD_DIM:(hd + 1) * HEAD_DIM] = jnp.dot(p, vg, preferred_element_type=F32)
    mix = _bdot(oh_ref[...], wo_ref[...])
    o_ref[...] = x + g1_ref[...] * mix


SHKV, SCKV = 0, 1


def _attn_prompt(x, mod_kv, mod, gkv, gmix, wkv, bkv, wq, bq, wo, sinks):
    b, t, _ = x.shape
    blk = WINDOW
    qd = N_HEADS * HEAD_DIM
    full = lambda shape: pl.BlockSpec(shape, lambda bi, ji, *_: (0,) * len(shape))
    mods = [pl.BlockSpec((None, 1, D_MODEL), lambda bi, ji, *_, k=k: (bi, 0, k))
            for k in (SHKV, SCKV, SH1, SC1, G1)]
    nb = ATTN_STEP_BLOCKS
    assert t % (nb * blk) == 0
    rows = pl.BlockSpec((None, nb * blk, D_MODEL), lambda bi, ji, *_: (bi, ji, 0))
    last = pl.BlockSpec((None, blk, KV_DIM), lambda bi, ji, *_: (bi, 0, 0))
    grid_spec = pltpu.PrefetchScalarGridSpec(
        num_scalar_prefetch=1,
        grid=(b, t // (nb * blk)),
        in_specs=[rows] + mods + [full((1, D_MODEL)), full((1, D_MODEL)),
                  full((D_MODEL, 2 * KV_DIM)), full((1, 2 * KV_DIM)), full((D_MODEL, qd)), full((1, qd)),
                  full((qd, D_MODEL))],
        out_specs=[rows, last, last],
        scratch_shapes=[pltpu.VMEM(((nb + 1) * blk, KV_DIM), BF16), pltpu.VMEM(((nb + 1) * blk, KV_DIM), BF16),
                        pltpu.VMEM((nb * blk, qd), F32)],
    )
    return pl.pallas_call(
        _attn_prompt_kernel,
        grid_spec=grid_spec,
        out_shape=(jax.ShapeDtypeStruct((b, t, D_MODEL), F32),
                   jax.ShapeDtypeStruct((b, blk, KV_DIM), F32),
                   jax.ShapeDtypeStruct((b, blk, KV_DIM), F32)),
        compiler_params=_cparams(("arbitrary", "arbitrary")),
        name="attn_prompt",
    )(sinks, x, mod_kv, mod_kv, mod, mod, mod, gkv, gmix, wkv, bkv, wq, bq, wo)


def _qkv_sample_kernel(x_ref, mkv_ref, m_ref, gkv_ref, gmix_ref,
                       wkv_ref, bkv_ref, wq_ref, bq_ref, q_ref, k_ref, v_ref):
    x = x_ref[...]
    n = x.shape[0]
    hkv = _rms(x, gkv_ref[...]) * (1.0 + _mod_cols(mkv_ref, SCKV, n)) + _mod_cols(mkv_ref, SHKV, n)
    kv = _bdot(hkv, wkv_ref[...]) + bkv_ref[...]
    k_ref[...] = kv[:, :KV_DIM]
    v_ref[...] = kv[:, KV_DIM:]
    h = _rms(x, gmix_ref[...]) * (1.0 + _mod_cols(m_ref, SC1, n)) + _mod_cols(m_ref, SH1, n)
    q_ref[...] = _bdot(h, wq_ref[...]) + bq_ref[...]


def _qkv_sample(x, mod_kv, mod, gkv, gmix, wkv, bkv, wq, bq):
    n = x.shape[0]
    return pl.pallas_call(
        _qkv_sample_kernel,
        out_shape=(jax.ShapeDtypeStruct((n, N_HEADS * HEAD_DIM), F32),
                   jax.ShapeDtypeStruct((n, KV_DIM), F32), jax.ShapeDtypeStruct((n, KV_DIM), F32)),
        compiler_params=_cparams(),
        name="qkv_sample",
    )(x, mod_kv, mod, gkv, gmix, wkv, bkv, wq, bq)


def _attn_sample_kernel(q_ref, ck_ref, cv_ref, nk_ref, nv_ref, slope_ref, sink_ref, o_ref, *, steps):
    rows = steps * Q_PER_KV
    sb = q_ref.shape[0]
    r_c = lax.broadcasted_iota(I32, (sb, rows, WINDOW), 1) // Q_PER_KV
    j_c = lax.broadcasted_iota(I32, (sb, rows, WINDOW), 2)
    dist_c = r_c + WINDOW - j_c
    valid_c = (dist_c >= 0) & (dist_c < WINDOW)
    r_n = lax.broadcasted_iota(I32, (sb, rows, steps), 1) // Q_PER_KV
    j_n = lax.broadcasted_iota(I32, (sb, rows, steps), 2)
    dist_n = r_n - j_n
    valid_n = (dist_n >= 0) & (dist_n < WINDOW)
    scale = HEAD_DIM ** -0.5
    for g in range(N_KV_HEADS):
        lanes = slice(g * HEAD_DIM, (g + 1) * HEAD_DIM)
        qg = q_ref[:, g].astype(BF16)
        slope = slope_ref[g][None]
        sink = sink_ref[g][None]
        kc, vc = ck_ref[:, :, lanes].astype(BF16), cv_ref[:, :, lanes].astype(BF16)
        kn, vn = nk_ref[:, :, lanes].astype(BF16), nv_ref[:, :, lanes].astype(BF16)
        s_c = jnp.einsum('nrd,njd->nrj', qg, kc, preferred_element_type=F32) * scale
        s_n = jnp.einsum('nrd,njd->nrj', qg, kn, preferred_element_type=F32) * scale
        s_c = jnp.where(valid_c, s_c - slope * dist_c.astype(F32), NEG_INF)
        s_n = jnp.where(valid_n, s_n - slope * dist_n.astype(F32), NEG_INF)
        m = jnp.maximum(jnp.maximum(jnp.max(s_c, axis=-1, keepdims=True),
                                    jnp.max(s_n, axis=-1, keepdims=True)), sink)
        e_c = jnp.exp(s_c - m)
        e_n = jnp.exp(s_n - m)
        den = (jnp.sum(e_c, axis=-1, keepdims=True) + jnp.sum(e_n, axis=-1, keepdims=True)
               + jnp.exp(sink - m))
        o = (jnp.einsum('nrj,njd->nrd', (e_c / den).astype(BF16), vc, preferred_element_type=F32)
             + jnp.einsum('nrj,njd->nrd', (e_n / den).astype(BF16), vn, preferred_element_type=F32))
        o_ref[:, g] = o


def _attn_sample(qg, cache_k, cache_v, k_new, v_new, slope_rows, sink_rows):
    n, _, rows, _ = qg.shape
    steps = rows // Q_PER_KV
    sb = math.gcd(n, 16)
    blk = lambda shape: pl.BlockSpec((sb,) + shape, lambda i: (i,) + (0,) * len(shape))
    full = lambda shape: pl.BlockSpec(shape, lambda i: (0,) * len(shape))
    return pl.pallas_call(
        functools.partial(_attn_sample_kernel, steps=steps),
        grid=(n // sb,),
        in_specs=[blk((N_KV_HEADS, rows, HEAD_DIM)), blk((WINDOW, KV_DIM)), blk((WINDOW, KV_DIM)),
                  blk((steps, KV_DIM)), blk((steps, KV_DIM)),
                  full((N_KV_HEADS, rows, 1)), full((N_KV_HEADS, rows, 1))],
        out_specs=blk((N_KV_HEADS, rows, HEAD_DIM)),
        out_shape=jax.ShapeDtypeStruct(qg.shape, F32),
        compiler_params=_cparams(("arbitrary",)),
        name="attn_sample",
    )(qg, cache_k, cache_v, k_new, v_new, slope_rows, sink_rows)


def _oproj_kernel(o_ref, x_ref, m_ref, wo_ref, y_ref):
    y_ref[...] = x_ref[...] + _mod_cols(m_ref, G1, x_ref.shape[0]) * _bdot(o_ref[...], wo_ref[...])


def _oproj(o, x, mod, wo):
    return pl.pallas_call(
        _oproj_kernel,
        out_shape=jax.ShapeDtypeStruct(x.shape, F32),
        compiler_params=_cparams(),
        name="oproj_sample",
    )(o, x, mod, wo)


def kernel(x_prompt, x_sample, state_ssm_re, state_ssm_im, cache_k, cache_v, c_prompt, c_sample, g_mix, g_ffn, w_ada, b_ada, ssm_a_re, ssm_a_im, ssm_log_dt, ssm_b_re, ssm_b_im, ssm_c_re, ssm_c_im, ssm_d, w_glu_a, w_glu_b, g_kv, w_ada_kv, b_ada_kv, w_kv, b_kv, w_q, b_q, w_o, attn_sinks, w_router, b_router, w_gu, b_gu, w_down, b_down, g_final):
    bsz, seq, d = x_prompt.shape
    n_seq, steps, _ = x_sample.shape
    assert d == D_MODEL and seq % S5_CHUNK == 0 and seq % TM == 0 and (n_seq * steps) % TM == 0
    assert g_mix.shape[0] == 2 and ssm_a_re.shape[0] == 1 and w_q.shape[0] == 1
    n_p, n_s = bsz * seq, n_seq * steps
    tiles = _Tiles(n_p, n_s, seq, n_seq)
    row = lambda v: v.reshape(1, -1)

    c_all = jnp.concatenate([c_prompt, c_sample], axis=0)
    c_rows = -(-c_all.shape[0] // SUBLANES) * SUBLANES
    c_all = jnp.pad(c_all, ((0, c_rows - c_all.shape[0]), (0, 0)))
    mods = [_ada(c_all, w_ada, b_ada, l) for l in range(2)]
    mod_kv = _ada(c_all, w_ada_kv[None], b_ada_kv[None], 0)

    def split(m):
        return m[:bsz, None, :], m[bsz:bsz + n_seq]

    mod_p, mod_s = split(mods[0])
    abr, abi, apr, api, brh, brl, bih, bil, c_re_x, c_im_x = _ssm_prep(
        ssm_a_re[0], ssm_a_im[0], ssm_log_dt[0], ssm_b_re[0], ssm_b_im[0], ssm_c_re[0], ssm_c_im[0],
        S5_CHUNK // SUBLANES)
    cexp = (c_re_x, c_im_x)
    wa, wb = w_glu_a[0].astype(BF16), w_glu_b[0].astype(BF16)
    xp, sre_p, sim_p = _s5_prompt(x_prompt, mod_p, row(g_mix[0]), (abr, abi, apr, api), (brh, bih), cexp,
                                  row(ssm_d[0]), wa, wb)
    xs_t, sre_s, sim_s = _s5_sample(
        x_sample.transpose(1, 0, 2), mod_s, row(g_mix[0]),
        state_ssm_re[0].reshape(n_seq, SSM_COLS), state_ssm_im[0].reshape(n_seq, SSM_COLS), abr, abi,
        ((brh, brl), (bih, bil)), cexp, row(ssm_d[0]), wa, wb)
    xp = xp.reshape(n_p, D_MODEL)
    xs = xs_t.reshape(n_s, D_MODEL)
    xp, xs = _moe_layer(tiles, xp, xs, mod_p, mod_s, row(g_ffn[0]), w_router[0], b_router[0], w_gu, b_gu,
                        w_down, b_down, row(g_final), 0, False)

    mod_p, mod_s = split(mods[1])
    modkv_p, modkv_s = split(mod_kv)
    wkv, wq, wo = w_kv.astype(BF16), w_q[0].astype(BF16), w_o[0].astype(BF16)
    xp3, k_p, v_p = _attn_prompt(xp.reshape(bsz, seq, D_MODEL), modkv_p, mod_p, row(g_kv), row(g_mix[1]), wkv,
                                 row(b_kv), wq, row(b_q[0]), wo, attn_sinks[0])
    xp = xp3.reshape(n_p, D_MODEL)

    q_s, k_s, v_s = _qkv_sample(xs, modkv_s, mod_s, row(g_kv), row(g_mix[1]), wkv, row(b_kv), wq, row(b_q[0]))
    rows = steps * Q_PER_KV
    qg = q_s.reshape(steps, n_seq, N_KV_HEADS, Q_PER_KV, HEAD_DIM).transpose(1, 2, 0, 3, 4).reshape(
        n_seq, N_KV_HEADS, rows, HEAD_DIM)
    head_of_row = (np.arange(N_KV_HEADS)[:, None] * Q_PER_KV + np.arange(rows)[None, :] % Q_PER_KV)
    slope_rows = jnp.asarray(np.asarray(_slopes(), np.float32)[head_of_row][..., None])
    sink_rows = attn_sinks[0][head_of_row][..., None]
    k_new = k_s.reshape(steps, n_seq, KV_DIM).transpose(1, 0, 2)
    v_new = v_s.reshape(steps, n_seq, KV_DIM).transpose(1, 0, 2)
    ck = cache_k.reshape(n_seq, WINDOW, KV_DIM)
    cv = cache_v.reshape(n_seq, WINDOW, KV_DIM)
    og = _attn_sample(qg, ck, cv, k_new, v_new, slope_rows, sink_rows)
    o_s = og.reshape(n_seq, N_KV_HEADS, steps, Q_PER_KV, HEAD_DIM).transpose(2, 0, 1, 3, 4).reshape(
        n_s, N_HEADS * HEAD_DIM)
    xs = _oproj(o_s, xs, mod_s, wo)

    yp, ys = _moe_layer(tiles, xp, xs, mod_p, mod_s, row(g_ffn[1]), w_router[1], b_router[1], w_gu, b_gu,
                        w_down, b_down, row(g_final), 1, True)

    shape4 = lambda a: a.reshape(a.shape[0], a.shape[1], N_KV_HEADS, HEAD_DIM)
    state = lambda a, n: a.reshape(1, n, SSM_GROUPS, SSM_STATE)
    return (yp.reshape(bsz, seq, D_MODEL), ys.reshape(steps, n_seq, D_MODEL).transpose(1, 0, 2),
            state(sre_p, bsz), state(sim_p, bsz), shape4(k_p), shape4(v_p),
            state(sre_s, n_seq), state(sim_s, n_seq),
            jnp.concatenate([cache_k[:, steps:], shape4(k_new)], axis=1),
            jnp.concatenate([cache_v[:, steps:], shape4(v_new)], axis=1))
```

```python
import functools
import math

import numpy as np
import jax
import jax.numpy as jnp
from jax import lax
from jax.experimental import pallas as pl
from jax.experimental.pallas import tpu as pltpu

F32 = jnp.float32
BF16 = jnp.bfloat16
I32 = jnp.int32

D_MODEL = 1024
SSM_GROUP = 16
SSM_GROUPS = D_MODEL // SSM_GROUP
SSM_STATE = 64
SSM_COLS = SSM_GROUPS * SSM_STATE
N_HEADS = 16
HEAD_DIM = 64
N_KV_HEADS = 4
Q_PER_KV = N_HEADS // N_KV_HEADS
KV_DIM = N_KV_HEADS * HEAD_DIM
WINDOW = 128
N_EXPERTS = 32
TOP_K = 4
D_FF = D_MODEL
SWIGLU_LIMIT = 7.0
SWIGLU_ALPHA = 1.702
RMS_EPS = 1e-5
NEG_INF = -1e30

LANES = 128
SUBLANES = 8
ROW_TILES = D_MODEL // LANES
TM = 512
S5_CHUNK = 512
S5_COLS = 512
MXU_K = 256
GROUPS_PER_K = MXU_K // SSM_GROUP
N_KT = D_MODEL // MXU_K
VMEM_LIMIT = 56 * 1024 * 1024


def _cparams(sem=None):
    return pltpu.CompilerParams(dimension_semantics=sem, vmem_limit_bytes=VMEM_LIMIT)


def _sigmoid(x):
    return 1.0 / (1.0 + jnp.exp(-x))


def _rms(x, g):
    return x * lax.rsqrt(jnp.mean(x * x, axis=-1, keepdims=True) + RMS_EPS) * g


def _bdot(a, b):
    return jnp.dot(a.astype(BF16), b.astype(BF16), preferred_element_type=F32)


def _gelu_tanh(x):
    return 0.5 * x * (1.0 + jnp.tanh(math.sqrt(2.0 / math.pi) * (x + 0.044715 * (x * x * x))))


def _split_bf16(w):
    hi = w.astype(BF16)
    lo = (w - hi.astype(F32)).astype(BF16)
    return hi, lo


def _dot3(a, b_hi, b_lo):
    a_hi = a.astype(BF16)
    a_lo = (a - a_hi.astype(F32)).astype(BF16)
    return (jnp.dot(a_hi, b_hi, preferred_element_type=F32)
            + jnp.dot(a_lo, b_hi, preferred_element_type=F32)
            + jnp.dot(a_hi, b_lo, preferred_element_type=F32))


def _ada_kernel(c_ref, w_ref, b_ref, o_ref):
    c = c_ref[...]
    o_ref[...] = _bdot(c * _sigmoid(c), w_ref[...]) + b_ref[...]


def _ada(c, w, b, layer):
    rows, cols = c.shape[0], w.shape[2]
    tn = 1024
    return pl.pallas_call(
        _ada_kernel,
        grid=(cols // tn,),
        in_specs=[pl.BlockSpec((rows, D_MODEL), lambda j: (0, 0)),
                  pl.BlockSpec((None, D_MODEL, tn), lambda j: (layer, 0, j)),
                  pl.BlockSpec((None, 1, tn), lambda j: (layer, 0, j))],
        out_specs=pl.BlockSpec((rows, tn), lambda j: (0, j)),
        out_shape=jax.ShapeDtypeStruct((rows, cols), F32),
        compiler_params=_cparams(("arbitrary",)),
        name="ada",
    )(c, w, b.reshape(b.shape[0], 1, cols))


def _ssm_prep_kernel(are_ref, aim_ref, ldt_ref, bre_ref, bim_ref, cre_ref, cim_ref,
                     abr_ref, abi_ref, apr_ref, api_ref, brh_ref, brl_ref, bih_ref, bil_ref, cr_ref, ci_ref,
                     *, sub_len):
    kcols = GROUPS_PER_K * SSM_STATE
    a_re, a_im = are_ref[...], aim_ref[...]
    dt = jnp.exp(ldt_ref[...])
    mag = jnp.exp(a_re * dt)
    ab_re = mag * jnp.cos(a_im * dt)
    ab_im = mag * jnp.sin(a_im * dt)
    abr_ref[...] = ab_re
    abi_ref[...] = ab_im
    den = a_re * a_re + a_im * a_im
    x, y = ab_re - 1.0, ab_im
    f_re = (x * a_re + y * a_im) / den
    f_im = (y * a_re - x * a_im) / den
    b_re, b_im = bre_ref[...], bim_ref[...]
    bb_re = f_re * b_re - f_im * b_im
    bb_im = f_re * b_im + f_im * b_re
    same_b = (lax.broadcasted_iota(I32, (MXU_K, kcols), 0) // SSM_GROUP
              == lax.broadcasted_iota(I32, (MXU_K, kcols), 1) // SSM_STATE)
    same_c = (lax.broadcasted_iota(I32, (kcols, MXU_K), 0) // SSM_STATE
              == lax.broadcasted_iota(I32, (kcols, MXU_K), 1) // SSM_GROUP)
    for kt in range(N_KT):
        for bb, hi_ref, lo_ref in ((bb_re, brh_ref, brl_ref), (bb_im, bih_ref, bil_ref)):
            full = jnp.where(same_b, jnp.concatenate([bb[:, kt * kcols:(kt + 1) * kcols]] * GROUPS_PER_K, axis=0), 0.0)
            hi = full.astype(BF16)
            hi_ref[kt] = hi
            lo_ref[kt] = (full - hi.astype(F32)).astype(BF16)
        cr_ref[kt] = jnp.where(same_c, cre_ref[kt * kcols:(kt + 1) * kcols, :], 0.0).astype(BF16)
        ci_ref[kt] = jnp.where(same_c, -cim_ref[kt * kcols:(kt + 1) * kcols, :], 0.0).astype(BF16)
    pr, pi = ab_re, ab_im
    for _ in range(int(math.log2(sub_len))):
        pr, pi = pr * pr - pi * pi, 2.0 * pr * pi
    cr, ci = jnp.ones_like(pr), jnp.zeros_like(pi)
    for j in range(SUBLANES + 1):
        apr_ref[j:j + 1, :] = cr
        api_ref[j:j + 1, :] = ci
        cr, ci = cr * pr - ci * pi, cr * pi + ci * pr


def _ssm_prep(a_re, a_im, log_dt, b_re, b_im, c_re, c_im, sub_len):
    gp = SSM_COLS
    kcols = GROUPS_PER_K * SSM_STATE
    ldt = jnp.broadcast_to(log_dt[:, None], (SSM_GROUPS, SSM_STATE)).reshape(1, gp)
    b_re_t = b_re.reshape(gp, SSM_GROUP).T
    b_im_t = b_im.reshape(gp, SSM_GROUP).T
    c_rows = lambda c: jnp.tile(c.transpose(0, 2, 1).reshape(gp, SSM_GROUP), (1, GROUPS_PER_K))
    row = jax.ShapeDtypeStruct((1, gp), F32)
    pw = jax.ShapeDtypeStruct((SUBLANES + 1, gp), F32)
    bx = jax.ShapeDtypeStruct((N_KT, MXU_K, kcols), BF16)
    cx = jax.ShapeDtypeStruct((N_KT, kcols, MXU_K), BF16)
    return pl.pallas_call(
        functools.partial(_ssm_prep_kernel, sub_len=sub_len),
        out_shape=(row, row, pw, pw, bx, bx, bx, bx, cx, cx),
        compiler_params=_cparams(),
        name="ssm_prep",
    )(a_re.reshape(1, gp), a_im.reshape(1, gp), ldt, b_re_t, b_im_t, c_rows(c_re), c_rows(c_im))


def _cmul(ar, ai, br, bi):
    return ar * br - ai * bi, ar * bi + ai * br


def _s5_prompt_kernel(x_ref, sh_ref, sc_ref, g1_ref, gmix_ref, abr_ref, abi_ref, apr_ref, api_ref,
                      bre_ref, bim_ref, cre_ref, cim_ref, dsk_ref, wa_ref, wb_ref, perm_ref, unperm_ref,
                      o_ref, sre_ref, sim_ref,
                      u_ref, hr_ref, hi_ref, hmr_ref, hmi_ref, car_ref, cai_ref, *, L):
    ls = L // SUBLANES
    kcols = GROUPS_PER_K * SSM_STATE

    @pl.when(pl.program_id(1) == 0)
    def _():
        car_ref[...] = jnp.zeros_like(car_ref)
        cai_ref[...] = jnp.zeros_like(cai_ref)

    x = x_ref[...]
    u = _rms(x, gmix_ref[...]) * (1.0 + sc_ref[...]) + sh_ref[...]
    u_ref[...] = u
    ub = jnp.dot(perm_ref[...], u.astype(BF16), preferred_element_type=F32).astype(BF16)
    for kt in range(N_KT):
        uk = ub[:, kt * MXU_K:(kt + 1) * MXU_K]
        hr_ref[:, kt * kcols:(kt + 1) * kcols] = jnp.dot(uk, bre_ref[kt], preferred_element_type=F32)
        hi_ref[:, kt * kcols:(kt + 1) * kcols] = jnp.dot(uk, bim_ref[kt], preferred_element_type=F32)

    for cb in range(SSM_COLS // S5_COLS):
        cols = slice(cb * S5_COLS, (cb + 1) * S5_COLS)
        ar = jnp.broadcast_to(abr_ref[:, cols], (SUBLANES, S5_COLS))
        ai = jnp.broadcast_to(abi_ref[:, cols], (SUBLANES, S5_COLS))

        sr = si = jnp.zeros((SUBLANES, S5_COLS), F32)
        for i in range(ls):
            rows = slice(i * SUBLANES, (i + 1) * SUBLANES)
            tr, ti = _cmul(ar, ai, sr, si)
            sr = tr + hr_ref[rows, cols]
            si = ti + hi_ref[rows, cols]
            hr_ref[rows, cols] = sr
            hi_ref[rows, cols] = si

    fr = hr_ref[L - SUBLANES:L, :]
    fi = hi_ref[L - SUBLANES:L, :]
    row = lax.broadcasted_iota(I32, (SUBLANES, SSM_COLS), 0)
    gr, gi = fr, fi
    pr, pi = apr_ref[1:2, :], api_ref[1:2, :]
    for s in (1, 2, 4):
        sr = jnp.where(row >= s, pltpu.roll(gr, s, axis=0), 0.0)
        si = jnp.where(row >= s, pltpu.roll(gi, s, axis=0), 0.0)
        tr, ti = _cmul(pr, pi, sr, si)
        gr, gi = gr + tr, gi + ti
        pr, pi = _cmul(pr, pi, pr, pi)
    c0r, c0i = car_ref[...], cai_ref[...]
    tr, ti = _cmul(apr_ref[0:SUBLANES, :], api_ref[0:SUBLANES, :], c0r, c0i)
    hmr = tr + jnp.where(row >= 1, pltpu.roll(gr, 1, axis=0), 0.0)
    hmi = ti + jnp.where(row >= 1, pltpu.roll(gi, 1, axis=0), 0.0)
    hmr_ref[...] = hmr
    hmi_ref[...] = hmi
    tr, ti = _cmul(apr_ref[SUBLANES:SUBLANES + 1, :], api_ref[SUBLANES:SUBLANES + 1, :], c0r, c0i)
    ncr = tr + gr[SUBLANES - 1:SUBLANES, :]
    nci = ti + gi[SUBLANES - 1:SUBLANES, :]
    car_ref[...] = ncr
    cai_ref[...] = nci
    sre_ref[...] = ncr
    sim_ref[...] = nci

    for cb in range(SSM_COLS // S5_COLS):
        cols = slice(cb * S5_COLS, (cb + 1) * S5_COLS)
        ar = jnp.broadcast_to(abr_ref[:, cols], (SUBLANES, S5_COLS))
        ai = jnp.broadcast_to(abi_ref[:, cols], (SUBLANES, S5_COLS))

        dr, di = hmr_ref[:, cols], hmi_ref[:, cols]
        for i in range(ls):
            rows = slice(i * SUBLANES, (i + 1) * SUBLANES)
            dr, di = _cmul(ar, ai, dr, di)
            hr_ref[rows, cols] = hr_ref[rows, cols] + dr
            hi_ref[rows, cols] = hi_ref[rows, cols] + di

    ys = []
    for nt in range(N_KT):
        hr = hr_ref[:, nt * kcols:(nt + 1) * kcols].astype(BF16)
        hi = hi_ref[:, nt * kcols:(nt + 1) * kcols].astype(BF16)
        ys.append(jnp.dot(hr, cre_ref[nt], preferred_element_type=F32)
                  + jnp.dot(hi, cim_ref[nt], preferred_element_type=F32))
    yp = jnp.concatenate(ys, axis=-1)
    yp_hi = yp.astype(BF16)
    yp_lo = (yp - yp_hi.astype(F32)).astype(BF16)
    unperm = unperm_ref[...]
    y = (jnp.dot(unperm, yp_hi, preferred_element_type=F32)
         + jnp.dot(unperm, yp_lo, preferred_element_type=F32)) + dsk_ref[...] * u_ref[...]
    z = _gelu_tanh(y).astype(BF16)
    mix = jnp.dot(z, wa_ref[...], preferred_element_type=F32) * _sigmoid(
        jnp.dot(z, wb_ref[...], preferred_element_type=F32))
    o_ref[...] = x_ref[...] + g1_ref[...] * mix


SH1, SC1, G1 = 0, 1, 2


def _s5_prompt(x, mod, gmix, prep, bexp, cexp, dsk, wa, wb):
    b, t, _ = x.shape
    L = S5_CHUNK
    abr, abi, apr, api = prep
    bre, bim = bexp
    cre, cim = cexp
    kcols = GROUPS_PER_K * SSM_STATE
    full = lambda shape: pl.BlockSpec(shape, lambda bi, ci: (0,) * len(shape))
    mods = [pl.BlockSpec((None, 1, D_MODEL), lambda bi, ci, k=k: (bi, 0, k)) for k in (SH1, SC1, G1)]
    r = np.arange(L)
    perm_np = np.zeros((L, L), np.float32)
    perm_np[r, (r % SUBLANES) * (L // SUBLANES) + r // SUBLANES] = 1.0
    perm, unperm = jnp.asarray(perm_np, BF16), jnp.asarray(perm_np.T, BF16)
    return pl.pallas_call(
        functools.partial(_s5_prompt_kernel, L=L),
        grid=(b, t // L),
        in_specs=[pl.BlockSpec((None, L, D_MODEL), lambda bi, ci: (bi, ci, 0))] + mods + [
                  full((1, D_MODEL)), full((1, SSM_COLS)), full((1, SSM_COLS)),
                  full((SUBLANES + 1, SSM_COLS)), full((SUBLANES + 1, SSM_COLS)),
                  full((N_KT, MXU_K, kcols)), full((N_KT, MXU_K, kcols)),
                  full((N_KT, kcols, MXU_K)), full((N_KT, kcols, MXU_K)),
                  full((1, D_MODEL)), full((D_MODEL, D_MODEL)), full((D_MODEL, D_MODEL)),
                  full((L, L)), full((L, L))],
        out_specs=[pl.BlockSpec((None, L, D_MODEL), lambda bi, ci: (bi, ci, 0)),
                   pl.BlockSpec((None, 1, SSM_COLS), lambda bi, ci: (bi, 0, 0)),
                   pl.BlockSpec((None, 1, SSM_COLS), lambda bi, ci: (bi, 0, 0))],
        out_shape=(jax.ShapeDtypeStruct((b, t, D_MODEL), F32),
                   jax.ShapeDtypeStruct((b, 1, SSM_COLS), F32),
                   jax.ShapeDtypeStruct((b, 1, SSM_COLS), F32)),
        scratch_shapes=[pltpu.VMEM((L, D_MODEL), F32),
                        pltpu.VMEM((L, SSM_COLS), F32), pltpu.VMEM((L, SSM_COLS), F32),
                        pltpu.VMEM((SUBLANES, SSM_COLS), F32), pltpu.VMEM((SUBLANES, SSM_COLS), F32),
                        pltpu.VMEM((1, SSM_COLS), F32), pltpu.VMEM((1, SSM_COLS), F32)],
        compiler_params=_cparams(("arbitrary", "arbitrary")),
        name="s5_prompt",
    )(x, mod, mod, mod, gmix, abr, abi, apr, api, bre, bim, cre, cim, dsk, wa, wb, perm, unperm)


def _tile_rows(v, rows):
    reps = rows // v.shape[0]
    return v if reps == 1 else jnp.concatenate([v] * reps, axis=0)


def _mod_cols(m_ref, k, rows=None):
    v = m_ref[:, k * D_MODEL:(k + 1) * D_MODEL]
    return v if rows is None else _tile_rows(v, rows)


def _s5_sample_kernel(x_ref, m_ref, gmix_ref, h0r_ref, h0i_ref, abr_ref, abi_ref,
                      brh_ref, brl_ref, bih_ref, bil_ref, cre_ref, cim_ref, dsk_ref, wa_ref, wb_ref,
                      o_ref, sre_ref, sim_ref, *, steps):
    kcols = GROUPS_PER_K * SSM_STATE
    ar, ai = abr_ref[...], abi_ref[...]
    sr, si = h0r_ref[...], h0i_ref[...]
    sh, sc, g1 = _mod_cols(m_ref, SH1), _mod_cols(m_ref, SC1), _mod_cols(m_ref, G1)
    for t in range(steps):
        x = x_ref[t]
        u = _rms(x, gmix_ref[...]) * (1.0 + sc) + sh
        bur, bui = [], []
        for kt in range(N_KT):
            uk = u[:, kt * MXU_K:(kt + 1) * MXU_K]
            bur.append(_dot3(uk, brh_ref[kt], brl_ref[kt]))
            bui.append(_dot3(uk, bih_ref[kt], bil_ref[kt]))
        tr, ti = _cmul(ar, ai, sr, si)
        sr = tr + jnp.concatenate(bur, axis=-1)
        si = ti + jnp.concatenate(bui, axis=-1)
        ys = []
        for nt in range(N_KT):
            ys.append(_bdot(sr[:, nt * kcols:(nt + 1) * kcols], cre_ref[nt])
                      + _bdot(si[:, nt * kcols:(nt + 1) * kcols], cim_ref[nt]))
        y = jnp.concatenate(ys, axis=-1) + dsk_ref[...] * u
        z = _gelu_tanh(y).astype(BF16)
        mix = jnp.dot(z, wa_ref[...], preferred_element_type=F32) * _sigmoid(
            jnp.dot(z, wb_ref[...], preferred_element_type=F32))
        o_ref[t] = x + g1 * mix
    sre_ref[...] = sr
    sim_ref[...] = si


def _s5_sample(x_t, mod, gmix, h0r, h0i, abr, abi, bsplit, cexp, dsk, wa, wb):
    steps, n, _ = x_t.shape
    (brh, brl), (bih, bil) = bsplit
    cre, cim = cexp
    return pl.pallas_call(
        functools.partial(_s5_sample_kernel, steps=steps),
        out_shape=(jax.ShapeDtypeStruct((steps, n, D_MODEL), F32),
                   jax.ShapeDtypeStruct((n, SSM_COLS), F32),
                   jax.ShapeDtypeStruct((n, SSM_COLS), F32)),
        compiler_params=_cparams(),
        name="s5_sample",
    )(x_t, mod, gmix, h0r, h0i, abr, abi, brh, brl, bih, bil, cre, cim, dsk, wa, wb)


class _Tiles:
    def __init__(self, n_prompt, n_sample, seq_len, n_seq):
        assert TM % n_seq == 0
        self.n_seq = n_seq
        self.ntp = n_prompt // TM
        self.nts = n_sample // TM
        self.nt = self.ntp + self.nts
        self.tiles_per_seq = seq_len // TM
        self.n_batch = n_prompt // seq_len
        self.n_tok = n_prompt + n_sample

    def p_rows(self):
        return pl.BlockSpec((TM, D_MODEL), lambda t, *_: (jnp.minimum(t, self.ntp - 1), 0))

    def s_rows(self):
        return pl.BlockSpec((TM, D_MODEL), lambda t, *_: (jnp.maximum(t - self.ntp, 0), 0))

    def s_mod(self, chunk):
        return pl.BlockSpec((self.n_seq, D_MODEL), lambda t, *_: (0, chunk))

    def p_mod(self, chunk):
        return pl.BlockSpec((None, 1, D_MODEL),
                            lambda t, *_: (jnp.minimum(t // self.tiles_per_seq, self.n_batch - 1), 0, chunk))

    def all_rows(self, width):
        return pl.BlockSpec((TM, width), lambda t, *_: (t, 0))


def _const_spec(shape):
    return pl.BlockSpec(shape, lambda t, *_: (0,) * len(shape))


def _by_kind(tiles, body, prompt_refs, sample_refs):
    t = pl.program_id(0)

    @pl.when(t < tiles.ntp)
    def _():
        body(*prompt_refs)

    @pl.when(t >= tiles.ntp)
    def _():
        body(*sample_refs)


def _mod_rows(m_ref):
    v = m_ref[...]
    return v if v.shape[0] == 1 else _tile_rows(v, TM)


def _moe_input(x_ref, sh_ref, sc_ref, g_ref):
    return _rms(x_ref[...], g_ref[...]) * (1.0 + _mod_rows(sc_ref)) + _mod_rows(sh_ref)


def _router_kernel(xp_ref, xs_ref, shp_ref, shs_ref, scp_ref, scs_ref, g_ref, wh_ref, wl_ref, b_ref, tri_ref,
                   mi_ref, mw_ref, cnt_ref, run_ref, *, tiles):
    @pl.when(pl.program_id(0) == 0)
    def _():
        run_ref[...] = jnp.zeros_like(run_ref)

    def route(x_ref, sh_ref, sc_ref):
        h = _moe_input(x_ref, sh_ref, sc_ref, g_ref)
        logits = _dot3(h, wh_ref[...], wl_ref[...]) + b_ref[...]
        lane = lax.broadcasted_iota(I32, (TM, LANES), 1)
        lane_f = lane.astype(F32)
        vals, firsts, hots = [], [], []
        work = logits
        for _ in range(TOP_K):
            m = jnp.max(work, axis=-1, keepdims=True)
            first = jnp.min(jnp.where(work == m, lane_f, float(LANES)), axis=-1, keepdims=True)
            hot = lane_f == first
            vals.append(m)
            firsts.append(first)
            hots.append(hot)
            work = jnp.where(hot, -jnp.inf, work)
        exps = [jnp.exp(v - vals[0]) for v in vals]
        den = exps[0] + exps[1] + exps[2] + exps[3]

        chosen = jnp.zeros((TM, LANES), F32)
        for hot in hots:
            chosen = jnp.where(hot, 1.0, chosen)
        rank_all = jnp.dot(tri_ref[...], chosen.astype(BF16), preferred_element_type=F32) + run_ref[...]
        run_ref[...] = run_ref[...] + jnp.sum(chosen, axis=0, keepdims=True)
        cnt_ref[...] = run_ref[...]

        code = jnp.zeros((TM, LANES), F32)
        mw = jnp.zeros((TM, LANES), F32)
        for k in range(TOP_K):
            r_k = jnp.sum(jnp.where(hots[k], rank_all, 0.0), axis=-1, keepdims=True)
            code = jnp.where(lane == k, r_k * float(N_EXPERTS) + firsts[k], code)
            mw = jnp.where(lane == k, exps[k] / den, mw)
        mw_ref[...] = mw
        code_t = code.T
        mi_ref[...] = jnp.concatenate(
            [code_t[k:k + 1, c * LANES:(c + 1) * LANES] for k in range(TOP_K) for c in range(TM // LANES)],
            axis=0).astype(I32)

    _by_kind(tiles, route, (xp_ref, shp_ref, scp_ref), (xs_ref, shs_ref, scs_ref))


SH2, SC2, G2 = 3, 4, 5


def _router(tiles, xp, xs, mod_p, mod_s, g, w_hi, w_lo, b_pad):
    tri = jnp.asarray(np.tril(np.ones((TM, TM), np.float32), -1), BF16)
    return pl.pallas_call(
        functools.partial(_router_kernel, tiles=tiles),
        grid=(tiles.nt,),
        in_specs=[tiles.p_rows(), tiles.s_rows(), tiles.p_mod(SH2), tiles.s_mod(SH2), tiles.p_mod(SC2),
                  tiles.s_mod(SC2), _const_spec((1, D_MODEL)), _const_spec((D_MODEL, LANES)),
                  _const_spec((D_MODEL, LANES)), _const_spec((1, LANES)), _const_spec((TM, TM))],
        out_specs=[pl.BlockSpec((TOP_K * TM // LANES, LANES), lambda t: (t, 0)), tiles.all_rows(LANES),
                   _const_spec((1, LANES))],
        out_shape=(jax.ShapeDtypeStruct((tiles.n_tok * TOP_K // LANES, LANES), I32),
                   jax.ShapeDtypeStruct((tiles.n_tok, LANES), F32),
                   jax.ShapeDtypeStruct((1, LANES), F32)),
        scratch_shapes=[pltpu.VMEM((1, LANES), F32)],
        compiler_params=_cparams(("arbitrary",)),
        name="router",
    )(xp, xs, mod_p, mod_s, mod_p, mod_s, g, w_hi, w_lo, b_pad, tri)


PAD_ARMS = tuple(1 << i for i in reversed(range(int(math.log2(TM)))))


def _dispatch_kernel(slot_ref, pstart_ref, pcnt_ref,
                     xp_ref, xs_ref, shp_ref, shs_ref, scp_ref, scs_ref, g_ref,
                     o_hbm, rows_ref, zero_ref, sems, zsem, *, tiles):
    t = pl.program_id(0)
    buf = t % 2
    tile_rows = TM * ROW_TILES
    base = pl.multiple_of(buf * tile_rows, tile_rows)

    def tile_wait(b):
        b0 = pl.multiple_of(b * tile_rows, tile_rows)
        for _ in range(TOP_K):
            pltpu.make_async_copy(rows_ref.at[pl.ds(b0, tile_rows), :],
                                  o_hbm.at[pl.ds(0, tile_rows), :], sems.at[b]).wait()

    def pad_copies(act):
        def body(e, carry):
            start, cnt = pstart_ref[e], pcnt_ref[e]
            for p in PAD_ARMS:
                @pl.when((cnt & p) != 0)
                def _(start=start, p=p):
                    cp = pltpu.make_async_copy(
                        zero_ref.at[pl.ds(0, p * ROW_TILES), :],
                        o_hbm.at[pl.ds(pl.multiple_of(start * ROW_TILES, ROW_TILES), p * ROW_TILES), :],
                        zsem)
                    cp.start() if act == "start" else cp.wait()
                start = start + jnp.where((cnt & p) != 0, p, 0)
            return carry
        lax.fori_loop(0, N_EXPERTS, body, 0)

    @pl.when(t == 0)
    def _():
        zero_ref[...] = jnp.zeros_like(zero_ref)
        pad_copies("start")

    @pl.when(t >= 2)
    def _():
        tile_wait(buf)

    def stage(x_ref, sh_ref, sc_ref):
        h = _moe_input(x_ref, sh_ref, sc_ref, g_ref)
        for c in range(ROW_TILES):
            rows_ref[pl.ds(base + c, TM, stride=ROW_TILES), :] = h[:, c * LANES:(c + 1) * LANES]

    _by_kind(tiles, stage, (xp_ref, shp_ref, scp_ref), (xs_ref, shs_ref, scs_ref))

    def issue(n, carry):
        src = rows_ref.at[pl.ds(pl.multiple_of(base + n * ROW_TILES, ROW_TILES), ROW_TILES), :]
        for k in range(TOP_K):
            slot = slot_ref[(t * TOP_K + k) * TM + n]
            dst = o_hbm.at[pl.ds(pl.multiple_of(slot * ROW_TILES, ROW_TILES), ROW_TILES), :]
            pltpu.make_async_copy(src, dst, sems.at[buf]).start()
        return carry
    lax.fori_loop(0, TM, issue, 0)

    @pl.when(t == tiles.nt - 1)
    def _():
        if tiles.nt >= 2:
            tile_wait(1 - buf)
        tile_wait(buf)
        pad_copies("wait")


def _dispatch(tiles, n_slots, slots, pad_start, pad_cnt, xp, xs, mod_p, mod_s, g):
    grid_spec = pltpu.PrefetchScalarGridSpec(
        num_scalar_prefetch=3,
        grid=(tiles.nt,),
        in_specs=[tiles.p_rows(), tiles.s_rows(), tiles.p_mod(SH2), tiles.s_mod(SH2), tiles.p_mod(SC2),
                  tiles.s_mod(SC2), _const_spec((1, D_MODEL))],
        out_specs=pl.BlockSpec(memory_space=pl.ANY),
        scratch_shapes=[pltpu.VMEM((2 * TM * ROW_TILES, LANES), F32),
                        pltpu.VMEM((PAD_ARMS[0] * ROW_TILES, LANES), F32),
                        pltpu.SemaphoreType.DMA((2,)), pltpu.SemaphoreType.DMA(())],
    )
    return pl.pallas_call(
        functools.partial(_dispatch_kernel, tiles=tiles),
        grid_spec=grid_spec,
        out_shape=jax.ShapeDtypeStruct((n_slots * ROW_TILES, LANES), F32),
        compiler_params=_cparams(("arbitrary",)),
        name="dispatch",
    )(slots, pad_start, pad_cnt, xp, xs, mod_p, mod_s, mod_p, mod_s, g)


def _expert_kernel(te_ref, nx_ref, nt_ref, hs_ref, wgu_hbm, bgu_ref, wd_hbm, bd_ref, o_ref,
                   wgu_f32, wd_f32, wgu_bf, wd_bf, lhs_ref, wsem, *, layer):
    i = pl.program_id(0)

    def fetch(expert):
        return (pltpu.make_async_copy(wgu_hbm.at[layer, expert], wgu_f32, wsem.at[0]),
                pltpu.make_async_copy(wd_hbm.at[layer, expert], wd_f32, wsem.at[1]))

    @pl.when(i < nt_ref[0])
    def _():
        expert = te_ref[i]

        @pl.when(i == 0)
        def _():
            for cp in fetch(expert):
                cp.start()

        @pl.when((i == 0) | (expert != te_ref[jnp.maximum(i - 1, 0)]))
        def _():
            for cp in fetch(expert):
                cp.wait()
            wgu_bf[...] = wgu_f32[...].astype(BF16)
            wd_bf[...] = wd_f32[...].astype(BF16)
            upcoming = nx_ref[i]

            @pl.when(upcoming >= 0)
            def _():
                for cp in fetch(upcoming):
                    cp.start()

        for c in range(ROW_TILES):
            lhs_ref[:, c * LANES:(c + 1) * LANES] = hs_ref[pl.ds(c, TM, stride=ROW_TILES), :].astype(BF16)
        gu = jnp.dot(lhs_ref[...], wgu_bf[...], preferred_element_type=F32) + bgu_ref[...]
        gate = jnp.minimum(gu[:, :D_FF], SWIGLU_LIMIT)
        up = jnp.clip(gu[:, D_FF:], -SWIGLU_LIMIT, SWIGLU_LIMIT)
        act = ((up + 1.0) * gate * _sigmoid(SWIGLU_ALPHA * gate)).astype(BF16)
        y = jnp.dot(act, wd_bf[...], preferred_element_type=F32) + bd_ref[...]
        for c in range(ROW_TILES):
            o_ref[pl.ds(c, TM, stride=ROW_TILES), :] = y[:, c * LANES:(c + 1) * LANES]


def _experts(max_tiles, tile_expert, next_expert, n_tiles, hs, w_gu, b_gu, w_down, b_down, layer):
    tile_rows = TM * ROW_TILES
    row_map = lambda i, te, nx, nt: (jnp.minimum(i, nt[0] - 1), 0)
    b_map = lambda i, te, nx, nt: (layer, te[i], 0, 0)
    grid_spec = pltpu.PrefetchScalarGridSpec(
        num_scalar_prefetch=3,
        grid=(max_tiles,),
        in_specs=[pl.BlockSpec((tile_rows, LANES), row_map),
                  pl.BlockSpec(memory_space=pl.ANY),
                  pl.BlockSpec((None, None, 1, 2 * D_FF), b_map),
                  pl.BlockSpec(memory_space=pl.ANY),
                  pl.BlockSpec((None, None, 1, D_MODEL), b_map)],
        out_specs=pl.BlockSpec((tile_rows, LANES), row_map),
        scratch_shapes=[pltpu.VMEM((D_MODEL, 2 * D_FF), F32), pltpu.VMEM((D_FF, D_MODEL), F32),
                        pltpu.VMEM((D_MODEL, 2 * D_FF), BF16), pltpu.VMEM((D_FF, D_MODEL), BF16),
                        pltpu.VMEM((TM, D_MODEL), BF16), pltpu.SemaphoreType.DMA((2,))],
    )
    return pl.pallas_call(
        functools.partial(_expert_kernel, layer=layer),
        grid_spec=grid_spec,
        out_shape=jax.ShapeDtypeStruct(hs.shape, F32),
        compiler_params=_cparams(("arbitrary",)),
        name="experts",
    )(tile_expert, next_expert, n_tiles, hs, w_gu, b_gu[:, :, None, :], w_down, b_down[:, :, None, :])


def _combine_kernel(slot_ref, y_hbm, mw_ref, xp_ref, xs_ref, g2p_ref, g2s_ref, gfin_ref,
                    op_ref, os_ref, gbuf, acc_ref, sems, *, tiles, final_norm):
    t = pl.program_id(0)
    tile_rows = TM * ROW_TILES
    buf_rows = TOP_K * tile_rows

    def issue_tile(tt, b):
        b0 = pl.multiple_of(b * buf_rows, buf_rows)

        for k in range(TOP_K):
            def issue(it, carry, k=k):
                for u in range(TOP_K):
                    n = it * TOP_K + u
                    slot = slot_ref[(tt * TOP_K + k) * TM + n]
                    src = y_hbm.at[pl.ds(pl.multiple_of(slot * ROW_TILES, ROW_TILES), ROW_TILES), :]
                    dst = gbuf.at[pl.ds(pl.multiple_of(b0 + k * tile_rows + n * ROW_TILES, ROW_TILES),
                                        ROW_TILES), :]
                    pltpu.make_async_copy(src, dst, sems.at[b]).start()
                return carry
            lax.fori_loop(0, TM // TOP_K, issue, 0)

    @pl.when(t == 0)
    def _():
        issue_tile(0, 0)

    @pl.when(t + 1 < tiles.nt)
    def _():
        issue_tile(t + 1, (t + 1) % 2)

    buf = t % 2
    b0 = pl.multiple_of(buf * buf_rows, buf_rows)
    for _ in range(TOP_K):
        pltpu.make_async_copy(y_hbm.at[pl.ds(0, tile_rows), :],
                              gbuf.at[pl.ds(b0, tile_rows), :], sems.at[buf]).wait()

    w = mw_ref[...]
    for c in range(ROW_TILES):
        acc = None
        for k in range(TOP_K):
            rows = gbuf[pl.ds(b0 + k * tile_rows + c, TM, stride=ROW_TILES), :]
            term = w[:, k:k + 1] * rows
            acc = term if acc is None else acc + term
        acc_ref[:, c * LANES:(c + 1) * LANES] = acc

    def finish(x_ref, g2_ref, o_ref):
        res = x_ref[...] + _mod_rows(g2_ref) * acc_ref[...]
        if final_norm:
            res = _rms(res, gfin_ref[...])
        o_ref[...] = res

    _by_kind(tiles, finish, (xp_ref, g2p_ref, op_ref), (xs_ref, g2s_ref, os_ref))


def _combine(tiles, slots, y_sorted, mw, xp, xs, mod_p, mod_s, gfin, final_norm):
    grid_spec = pltpu.PrefetchScalarGridSpec(
        num_scalar_prefetch=1,
        grid=(tiles.nt,),
        in_specs=[pl.BlockSpec(memory_space=pl.ANY), tiles.all_rows(LANES), tiles.p_rows(), tiles.s_rows(),
                  tiles.p_mod(G2), tiles.s_mod(G2), _const_spec((1, D_MODEL))],
        out_specs=[tiles.p_rows(), tiles.s_rows()],
        scratch_shapes=[pltpu.VMEM((2 * TOP_K * TM * ROW_TILES, LANES), F32),
                        pltpu.VMEM((TM, D_MODEL), F32), pltpu.SemaphoreType.DMA((2,))],
    )
    return pl.pallas_call(
        functools.partial(_combine_kernel, tiles=tiles, final_norm=final_norm),
        grid_spec=grid_spec,
        out_shape=(jax.ShapeDtypeStruct(xp.shape, F32), jax.ShapeDtypeStruct(xs.shape, F32)),
        compiler_params=_cparams(("arbitrary",)),
        name="combine",
    )(slots, y_sorted, mw, xp, xs, mod_p, mod_s, gfin)


def _moe_layer(tiles, xp, xs, mod_p, mod_s, g_ffn, w_router, b_router, w_gu, b_gu, w_down, b_down, gfin,
               layer, final_norm):
    n_tok = tiles.n_tok
    w_pad = jnp.pad(w_router, ((0, 0), (0, LANES - N_EXPERTS)))
    w_hi, w_lo = _split_bf16(w_pad)
    b_pad = jnp.pad(b_router, (0, LANES - N_EXPERTS), constant_values=NEG_INF).reshape(1, LANES)
    mi, mw, counts = _router(tiles, xp, xs, mod_p, mod_s, g_ffn, w_hi, w_lo, b_pad)

    cnt = counts[0, :N_EXPERTS].astype(I32)
    tiles_e = (cnt + TM - 1) // TM
    tile_end = jnp.cumsum(tiles_e)
    tile_start = tile_end - tiles_e
    first_slot = jnp.sum(jnp.where((mi & (N_EXPERTS - 1))[..., None] == jnp.arange(N_EXPERTS, dtype=I32),
                                   (tile_start * TM)[None, None, :], 0), axis=-1)
    slots = (first_slot + (mi >> int(math.log2(N_EXPERTS)))).reshape(n_tok * TOP_K)
    max_tiles = (n_tok * TOP_K) // TM + N_EXPERTS
    n_tiles = tile_end[-1:].astype(I32)
    tile_ids = jnp.arange(max_tiles, dtype=I32)
    tile_expert = jnp.minimum(jnp.sum(tile_ids[:, None] >= tile_end[None, :], axis=-1), N_EXPERTS - 1).astype(I32)
    group_end = jnp.sum(jnp.where(tile_expert[:, None] == jnp.arange(N_EXPERTS, dtype=I32), tile_end[None, :], 0),
                        axis=-1)
    after = jnp.minimum(jnp.sum(group_end[:, None] >= tile_end[None, :], axis=-1), N_EXPERTS - 1).astype(I32)
    next_expert = jnp.where(group_end < n_tiles[0], after, -1).astype(I32)
    pad_start = (tile_start * TM + cnt).astype(I32)
    pad_cnt = (tiles_e * TM - cnt).astype(I32)

    hs = _dispatch(tiles, max_tiles * TM, slots, pad_start, pad_cnt, xp, xs, mod_p, mod_s, g_ffn)
    ys = _experts(max_tiles, tile_expert, next_expert, n_tiles, hs, w_gu, b_gu, w_down, b_down, layer)
    return _combine(tiles, slots, ys, mw, xp, xs, mod_p, mod_s, gfin, final_norm)


ATTN_STEP_BLOCKS = 2


def _slopes():
    return [2.0 ** (-8.0 * (h + 1) / N_HEADS) for h in range(N_HEADS)]


def _attn_prompt_kernel(sink_ref, x_ref, shkv_ref, sckv_ref, sh_ref, sc_ref, g1_ref, gkv_ref, gmix_ref,
                        wkv_ref, bkv_ref, wq_ref, bq_ref, wo_ref,
                        o_ref, kl_ref, vl_ref, kk_ref, vv_ref, oh_ref):
    j = pl.program_id(1)
    blk = WINDOW
    nb = ATTN_STEP_BLOCKS

    @pl.when(j == 0)
    def _():
        kk_ref[...] = jnp.zeros_like(kk_ref)
        vv_ref[...] = jnp.zeros_like(vv_ref)

    @pl.when(j > 0)
    def _():
        kk_ref[0:blk, :] = kk_ref[nb * blk:(nb + 1) * blk, :]
        vv_ref[0:blk, :] = vv_ref[nb * blk:(nb + 1) * blk, :]

    x = x_ref[...]
    hkv = _rms(x, gkv_ref[...]) * (1.0 + sckv_ref[...]) + shkv_ref[...]
    kv = _bdot(hkv, wkv_ref[...]) + bkv_ref[...]
    kl_ref[...] = kv[(nb - 1) * blk:, :KV_DIM]
    vl_ref[...] = kv[(nb - 1) * blk:, KV_DIM:]
    kk_ref[blk:(nb + 1) * blk, :] = kv[:, :KV_DIM].astype(BF16)
    vv_ref[blk:(nb + 1) * blk, :] = kv[:, KV_DIM:].astype(BF16)

    h = _rms(x, gmix_ref[...]) * (1.0 + sc_ref[...]) + sh_ref[...]
    q = _bdot(h, wq_ref[...]) + bq_ref[...]

    r = lax.broadcasted_iota(I32, (blk, 2 * blk), 0)
    c = lax.broadcasted_iota(I32, (blk, 2 * blk), 1)
    dist_i = r + blk - c
    visible = (dist_i >= 0) & (dist_i < WINDOW)
    dist = dist_i.astype(F32)
    slopes = _slopes()
    for qb in range(nb):
        valid = visible & jnp.logical_not((j == 0) & (c < blk)) if qb == 0 else visible
        q_rows = slice(qb * blk, (qb + 1) * blk)
        k_rows = slice(qb * blk, (qb + 2) * blk)
        for g in range(N_KV_HEADS):
            kg = kk_ref[k_rows, g * HEAD_DIM:(g + 1) * HEAD_DIM]
            vg = vv_ref[k_rows, g * HEAD_DIM:(g + 1) * HEAD_DIM]
            for qh in range(Q_PER_KV):
                hd = g * Q_PER_KV + qh
                qd = q[q_rows, hd * HEAD_DIM:(hd + 1) * HEAD_DIM].astype(BF16)
                s = lax.dot_general(qd, kg, (((1,), (1,)), ((), ())), preferred_element_type=F32)
                s = s * (HEAD_DIM ** -0.5) - slopes[hd] * dist
                s = jnp.where(valid, s, NEG_INF)
                sink = sink_ref[hd]
                m = jnp.maximum(jnp.max(s, axis=-1, keepdims=True), sink)
                e = jnp.exp(s - m)
                den = jnp.sum(e, axis=-1, keepdims=True) + jnp.exp(sink - m)
                p = (e / den).astype(BF16)
                oh_ref[q_rows, hd * HEAD_DIM:(hd + 1) * HEAD_DIM] = jnp.dot(p, vg, preferred_element_type=F32)
    mix = _bdot(oh_ref[...], wo_ref[...])
    o_ref[...] = x + g1_ref[...] * mix


SHKV, SCKV = 0, 1


def _attn_prompt(x, mod_kv, mod, gkv, gmix, wkv, bkv, wq, bq, wo, sinks):
    b, t, _ = x.shape
    blk = WINDOW
    qd = N_HEADS * HEAD_DIM
    full = lambda shape: pl.BlockSpec(shape, lambda bi, ji, *_: (0,) * len(shape))
    mods = [pl.BlockSpec((None, 1, D_MODEL), lambda bi, ji, *_, k=k: (bi, 0, k))
            for k in (SHKV, SCKV, SH1, SC1, G1)]
    nb = ATTN_STEP_BLOCKS
    assert t % (nb * blk) == 0
    rows = pl.BlockSpec((None, nb * blk, D_MODEL), lambda bi, ji, *_: (bi, ji, 0))
    last = pl.BlockSpec((None, blk, KV_DIM), lambda bi, ji, *_: (bi, 0, 0))
    grid_spec = pltpu.PrefetchScalarGridSpec(
        num_scalar_prefetch=1,
        grid=(b, t // (nb * blk)),
        in_specs=[rows] + mods + [full((1, D_MODEL)), full((1, D_MODEL)),
                  full((D_MODEL, 2 * KV_DIM)), full((1, 2 * KV_DIM)), full((D_MODEL, qd)), full((1, qd)),
                  full((qd, D_MODEL))],
        out_specs=[rows, last, last],
        scratch_shapes=[pltpu.VMEM(((nb + 1) * blk, KV_DIM), BF16), pltpu.VMEM(((nb + 1) * blk, KV_DIM), BF16),
                        pltpu.VMEM((nb * blk, qd), F32)],
    )
    return pl.pallas_call(
        _attn_prompt_kernel,
        grid_spec=grid_spec,
        out_shape=(jax.ShapeDtypeStruct((b, t, D_MODEL), F32),
                   jax.ShapeDtypeStruct((b, blk, KV_DIM), F32),
                   jax.ShapeDtypeStruct((b, blk, KV_DIM), F32)),
        compiler_params=_cparams(("arbitrary", "arbitrary")),
        name="attn_prompt",
    )(sinks, x, mod_kv, mod_kv, mod, mod, mod, gkv, gmix, wkv, bkv, wq, bq, wo)


def _qkv_sample_kernel(x_ref, mkv_ref, m_ref, gkv_ref, gmix_ref,
                       wkv_ref, bkv_ref, wq_ref, bq_ref, q_ref, k_ref, v_ref):
    x = x_ref[...]
    n = x.shape[0]
    hkv = _rms(x, gkv_ref[...]) * (1.0 + _mod_cols(mkv_ref, SCKV, n)) + _mod_cols(mkv_ref, SHKV, n)
    kv = _bdot(hkv, wkv_ref[...]) + bkv_ref[...]
    k_ref[...] = kv[:, :KV_DIM]
    v_ref[...] = kv[:, KV_DIM:]
    h = _rms(x, gmix_ref[...]) * (1.0 + _mod_cols(m_ref, SC1, n)) + _mod_cols(m_ref, SH1, n)
    q_ref[...] = _bdot(h, wq_ref[...]) + bq_ref[...]


def _qkv_sample(x, mod_kv, mod, gkv, gmix, wkv, bkv, wq, bq):
    n = x.shape[0]
    return pl.pallas_call(
        _qkv_sample_kernel,
        out_shape=(jax.ShapeDtypeStruct((n, N_HEADS * HEAD_DIM), F32),
                   jax.ShapeDtypeStruct((n, KV_DIM), F32), jax.ShapeDtypeStruct((n, KV_DIM), F32)),
        compiler_params=_cparams(),
        name="qkv_sample",
    )(x, mod_kv, mod, gkv, gmix, wkv, bkv, wq, bq)


def _attn_sample_kernel(q_ref, ck_ref, cv_ref, nk_ref, nv_ref, slope_ref, sink_ref, o_ref, *, steps):
    rows = steps * Q_PER_KV
    sb = q_ref.shape[0]
    r_c = lax.broadcasted_iota(I32, (sb, rows, WINDOW), 1) // Q_PER_KV
    j_c = lax.broadcasted_iota(I32, (sb, rows, WINDOW), 2)
    dist_c = r_c + WINDOW - j_c
    valid_c = (dist_c >= 0) & (dist_c < WINDOW)
    r_n = lax.broadcasted_iota(I32, (sb, rows, steps), 1) // Q_PER_KV
    j_n = lax.broadcasted_iota(I32, (sb, rows, steps), 2)
    dist_n = r_n - j_n
    valid_n = (dist_n >= 0) & (dist_n < WINDOW)
    scale = HEAD_DIM ** -0.5
    for g in range(N_KV_HEADS):
        lanes = slice(g * HEAD_DIM, (g + 1) * HEAD_DIM)
        qg = q_ref[:, g].astype(BF16)
        slope = slope_ref[g][None]
        sink = sink_ref[g][None]
        kc, vc = ck_ref[:, :, lanes].astype(BF16), cv_ref[:, :, lanes].astype(BF16)
        kn, vn = nk_ref[:, :, lanes].astype(BF16), nv_ref[:, :, lanes].astype(BF16)
        s_c = jnp.einsum('nrd,njd->nrj', qg, kc, preferred_element_type=F32) * scale
        s_n = jnp.einsum('nrd,njd->nrj', qg, kn, preferred_element_type=F32) * scale
        s_c = jnp.where(valid_c, s_c - slope * dist_c.astype(F32), NEG_INF)
        s_n = jnp.where(valid_n, s_n - slope * dist_n.astype(F32), NEG_INF)
        m = jnp.maximum(jnp.maximum(jnp.max(s_c, axis=-1, keepdims=True),
                                    jnp.max(s_n, axis=-1, keepdims=True)), sink)
        e_c = jnp.exp(s_c - m)
        e_n = jnp.exp(s_n - m)
        den = (jnp.sum(e_c, axis=-1, keepdims=True) + jnp.sum(e_n, axis=-1, keepdims=True)
               + jnp.exp(sink - m))
        o = (jnp.einsum('nrj,njd->nrd', (e_c / den).astype(BF16), vc, preferred_element_type=F32)
             + jnp.einsum('nrj,njd->nrd', (e_n / den).astype(BF16), vn, preferred_element_type=F32))
        o_ref[:, g] = o


def _attn_sample(qg, cache_k, cache_v, k_new, v_new, slope_rows, sink_rows):
    n, _, rows, _ = qg.shape
    steps = rows // Q_PER_KV
    sb = math.gcd(n, 16)
    blk = lambda shape: pl.BlockSpec((sb,) + shape, lambda i: (i,) + (0,) * len(shape))
    full = lambda shape: pl.BlockSpec(shape, lambda i: (0,) * len(shape))
    return pl.pallas_call(
        functools.partial(_attn_sample_kernel, steps=steps),
        grid=(n // sb,),
        in_specs=[blk((N_KV_HEADS, rows, HEAD_DIM)), blk((WINDOW, KV_DIM)), blk((WINDOW, KV_DIM)),
                  blk((steps, KV_DIM)), blk((steps, KV_DIM)),
                  full((N_KV_HEADS, rows, 1)), full((N_KV_HEADS, rows, 1))],
        out_specs=blk((N_KV_HEADS, rows, HEAD_DIM)),
        out_shape=jax.ShapeDtypeStruct(qg.shape, F32),
        compiler_params=_cparams(("arbitrary",)),
        name="attn_sample",
    )(qg, cache_k, cache_v, k_new, v_new, slope_rows, sink_rows)


def _oproj_kernel(o_ref, x_ref, m_ref, wo_ref, y_ref):
    y_ref[...] = x_ref[...] + _mod_cols(m_ref, G1, x_ref.shape[0]) * _bdot(o_ref[...], wo_ref[...])


def _oproj(o, x, mod, wo):
    return pl.pallas_call(
        _oproj_kernel,
        out_shape=jax.ShapeDtypeStruct(x.shape, F32),
        compiler_params=_cparams(),
        name="oproj_sample",
    )(o, x, mod, wo)


def kernel(x_prompt, x_sample, state_ssm_re, state_ssm_im, cache_k, cache_v, c_prompt, c_sample, g_mix, g_ffn, w_ada, b_ada, ssm_a_re, ssm_a_im, ssm_log_dt, ssm_b_re, ssm_b_im, ssm_c_re, ssm_c_im, ssm_d, w_glu_a, w_glu_b, g_kv, w_ada_kv, b_ada_kv, w_kv, b_kv, w_q, b_q, w_o, attn_sinks, w_router, b_router, w_gu, b_gu, w_down, b_down, g_final):
    bsz, seq, d = x_prompt.shape
    n_seq, steps, _ = x_sample.shape
    assert d == D_MODEL and seq % S5_CHUNK == 0 and seq % TM == 0 and (n_seq * steps) % TM == 0
    assert g_mix.shape[0] == 2 and ssm_a_re.shape[0] == 1 and w_q.shape[0] == 1
    n_p, n_s = bsz * seq, n_seq * steps
    tiles = _Tiles(n_p, n_s, seq, n_seq)
    row = lambda v: v.reshape(1, -1)

    c_all = jnp.concatenate([c_prompt, c_sample], axis=0)
    c_rows = -(-c_all.shape[0] // SUBLANES) * SUBLANES
    c_all = jnp.pad(c_all, ((0, c_rows - c_all.shape[0]), (0, 0)))
    mods = [_ada(c_all, w_ada, b_ada, l) for l in range(2)]
    mod_kv = _ada(c_all, w_ada_kv[None], b_ada_kv[None], 0)

    def split(m):
        return m[:bsz, None, :], m[bsz:bsz + n_seq]

    mod_p, mod_s = split(mods[0])
    abr, abi, apr, api, brh, brl, bih, bil, c_re_x, c_im_x = _ssm_prep(
        ssm_a_re[0], ssm_a_im[0], ssm_log_dt[0], ssm_b_re[0], ssm_b_im[0], ssm_c_re[0], ssm_c_im[0],
        S5_CHUNK // SUBLANES)
    cexp = (c_re_x, c_im_x)
    wa, wb = w_glu_a[0].astype(BF16), w_glu_b[0].astype(BF16)
    xp, sre_p, sim_p = _s5_prompt(x_prompt, mod_p, row(g_mix[0]), (abr, abi, apr, api), (brh, bih), cexp,
                                  row(ssm_d[0]), wa, wb)
    xs_t, sre_s, sim_s = _s5_sample(
        x_sample.transpose(1, 0, 2), mod_s, row(g_mix[0]),
        state_ssm_re[0].reshape(n_seq, SSM_COLS), state_ssm_im[0].reshape(n_seq, SSM_COLS), abr, abi,
        ((brh, brl), (bih, bil)), cexp, row(ssm_d[0]), wa, wb)
    xp = xp.reshape(n_p, D_MODEL)
    xs = xs_t.reshape(n_s, D_MODEL)
    xp, xs = _moe_layer(tiles, xp, xs, mod_p, mod_s, row(g_ffn[0]), w_router[0], b_router[0], w_gu, b_gu,
                        w_down, b_down, row(g_final), 0, False)

    mod_p, mod_s = split(mods[1])
    modkv_p, modkv_s = split(mod_kv)
    wkv, wq, wo = w_kv.astype(BF16), w_q[0].astype(BF16), w_o[0].astype(BF16)
    xp3, k_p, v_p = _attn_prompt(xp.reshape(bsz, seq, D_MODEL), modkv_p, mod_p, row(g_kv), row(g_mix[1]), wkv,
                                 row(b_kv), wq, row(b_q[0]), wo, attn_sinks[0])
    xp = xp3.reshape(n_p, D_MODEL)

    q_s, k_s, v_s = _qkv_sample(xs, modkv_s, mod_s, row(g_kv), row(g_mix[1]), wkv, row(b_kv), wq, row(b_q[0]))
    rows = steps * Q_PER_KV
    qg = q_s.reshape(steps, n_seq, N_KV_HEADS, Q_PER_KV, HEAD_DIM).transpose(1, 2, 0, 3, 4).reshape(
        n_seq, N_KV_HEADS, rows, HEAD_DIM)
    head_of_row = (np.arange(N_KV_HEADS)[:, None] * Q_PER_KV + np.arange(rows)[None, :] % Q_PER_KV)
    slope_rows = jnp.asarray(np.asarray(_slopes(), np.float32)[head_of_row][..., None])
    sink_rows = attn_sinks[0][head_of_row][..., None]
    k_new = k_s.reshape(steps, n_seq, KV_DIM).transpose(1, 0, 2)
    v_new = v_s.reshape(steps, n_seq, KV_DIM).transpose(1, 0, 2)
    ck = cache_k.reshape(n_seq, WINDOW, KV_DIM)
    cv = cache_v.reshape(n_seq, WINDOW, KV_DIM)
    og = _attn_sample(qg, ck, cv, k_new, v_new, slope_rows, sink_rows)
    o_s = og.reshape(n_seq, N_KV_HEADS, steps, Q_PER_KV, HEAD_DIM).transpose(2, 0, 1, 3, 4).reshape(
        n_s, N_HEADS * HEAD_DIM)
    xs = _oproj(o_s, xs, mod_s, wo)

    yp, ys = _moe_layer(tiles, xp, xs, mod_p, mod_s, row(g_ffn[1]), w_router[1], b_router[1], w_gu, b_gu,
                        w_down, b_down, row(g_final), 1, True)

    shape4 = lambda a: a.reshape(a.shape[0], a.shape[1], N_KV_HEADS, HEAD_DIM)
    state = lambda a, n: a.reshape(1, n, SSM_GROUPS, SSM_STATE)
    return (yp.reshape(bsz, seq, D_MODEL), ys.reshape(steps, n_seq, D_MODEL).transpose(1, 0, 2),
            state(sre_p, bsz), state(sim_p, bsz), shape4(k_p), shape4(v_p),
            state(sre_s, n_seq), state(sim_s, n_seq),
            jnp.concatenate([cache_k[:, steps:], shape4(k_new)], axis=1),
            jnp.concatenate([cache_v[:, steps:], shape4(v_new)], axis=1))
```

```python
import functools
import math

import numpy as np
import jax
import jax.numpy as jnp
from jax import lax
from jax.experimental import pallas as pl
from jax.experimental.pallas import tpu as pltpu

F32 = jnp.float32
BF16 = jnp.bfloat16
I32 = jnp.int32

D_MODEL = 1024
SSM_GROUP = 16
SSM_GROUPS = D_MODEL // SSM_GROUP
SSM_STATE = 64
SSM_COLS = SSM_GROUPS * SSM_STATE
N_HEADS = 16
HEAD_DIM = 64
N_KV_HEADS = 4
Q_PER_KV = N_HEADS // N_KV_HEADS
KV_DIM = N_KV_HEADS * HEAD_DIM
WINDOW = 128
N_EXPERTS = 32
TOP_K = 4
D_FF = D_MODEL
SWIGLU_LIMIT = 7.0
SWIGLU_ALPHA = 1.702
RMS_EPS = 1e-5
NEG_INF = -1e30

LANES = 128
SUBLANES = 8
ROW_TILES = D_MODEL // LANES
TM = 512
S5_CHUNK = 512
S5_COLS = 512
MXU_K = 256
GROUPS_PER_K = MXU_K // SSM_GROUP
N_KT = D_MODEL // MXU_K
VMEM_LIMIT = 56 * 1024 * 1024


def _cparams(sem=None):
    return pltpu.CompilerParams(dimension_semantics=sem, vmem_limit_bytes=VMEM_LIMIT)


def _sigmoid(x):
    return 1.0 / (1.0 + jnp.exp(-x))


def _rms(x, g):
    return x * lax.rsqrt(jnp.mean(x * x, axis=-1, keepdims=True) + RMS_EPS) * g


def _bdot(a, b):
    return jnp.dot(a.astype(BF16), b.astype(BF16), preferred_element_type=F32)


def _gelu_tanh(x):
    return 0.5 * x * (1.0 + jnp.tanh(math.sqrt(2.0 / math.pi) * (x + 0.044715 * (x * x * x))))


def _split_bf16(w):
    hi = w.astype(BF16)
    lo = (w - hi.astype(F32)).astype(BF16)
    return hi, lo


def _dot3(a, b_hi, b_lo):
    a_hi = a.astype(BF16)
    a_lo = (a - a_hi.astype(F32)).astype(BF16)
    return (jnp.dot(a_hi, b_hi, preferred_element_type=F32)
            + jnp.dot(a_lo, b_hi, preferred_element_type=F32)
            + jnp.dot(a_hi, b_lo, preferred_element_type=F32))


def _ada_kernel(c_ref, w_ref, b_ref, o_ref):
    c = c_ref[...]
    o_ref[...] = _bdot(c * _sigmoid(c), w_ref[...]) + b_ref[...]


def _ada(c, w, b, layer):
    rows, cols = c.shape[0], w.shape[2]
    tn = 1024
    return pl.pallas_call(
        _ada_kernel,
        grid=(cols // tn,),
        in_specs=[pl.BlockSpec((rows, D_MODEL), lambda j: (0, 0)),
                  pl.BlockSpec((None, D_MODEL, tn), lambda j: (layer, 0, j)),
                  pl.BlockSpec((None, 1, tn), lambda j: (layer, 0, j))],
        out_specs=pl.BlockSpec((rows, tn), lambda j: (0, j)),
        out_shape=jax.ShapeDtypeStruct((rows, cols), F32),
        compiler_params=_cparams(("arbitrary",)),
        name="ada",
    )(c, w, b.reshape(b.shape[0], 1, cols))


def _ssm_prep_kernel(are_ref, aim_ref, ldt_ref, bre_ref, bim_ref, cre_ref, cim_ref,
                     abr_ref, abi_ref, apr_ref, api_ref, brh_ref, brl_ref, bih_ref, bil_ref, cr_ref, ci_ref,
                     *, sub_len):
    kcols = GROUPS_PER_K * SSM_STATE
    a_re, a_im = are_ref[...], aim_ref[...]
    dt = jnp.exp(ldt_ref[...])
    mag = jnp.exp(a_re * dt)
    ab_re = mag * jnp.cos(a_im * dt)
    ab_im = mag * jnp.sin(a_im * dt)
    abr_ref[...] = ab_re
    abi_ref[...] = ab_im
    den = a_re * a_re + a_im * a_im
    x, y = ab_re - 1.0, ab_im
    f_re = (x * a_re + y * a_im) / den
    f_im = (y * a_re - x * a_im) / den
    b_re, b_im = bre_ref[...], bim_ref[...]
    bb_re = f_re * b_re - f_im * b_im
    bb_im = f_re * b_im + f_im * b_re
    same_b = (lax.broadcasted_iota(I32, (MXU_K, kcols), 0) // SSM_GROUP
              == lax.broadcasted_iota(I32, (MXU_K, kcols), 1) // SSM_STATE)
    same_c = (lax.broadcasted_iota(I32, (kcols, MXU_K), 0) // SSM_STATE
              == lax.broadcasted_iota(I32, (kcols, MXU_K), 1) // SSM_GROUP)
    for kt in range(N_KT):
        for bb, hi_ref, lo_ref in ((bb_re, brh_ref, brl_ref), (bb_im, bih_ref, bil_ref)):
            full = jnp.where(same_b, jnp.concatenate([bb[:, kt * kcols:(kt + 1) * kcols]] * GROUPS_PER_K, axis=0), 0.0)
            hi = full.astype(BF16)
            hi_ref[kt] = hi
            lo_ref[kt] = (full - hi.astype(F32)).astype(BF16)
        cr_ref[kt] = jnp.where(same_c, cre_ref[kt * kcols:(kt + 1) * kcols, :], 0.0).astype(BF16)
        ci_ref[kt] = jnp.where(same_c, -cim_ref[kt * kcols:(kt + 1) * kcols, :], 0.0).astype(BF16)
    pr, pi = ab_re, ab_im
    for _ in range(int(math.log2(sub_len))):
        pr, pi = pr * pr - pi * pi, 2.0 * pr * pi
    cr, ci = jnp.ones_like(pr), jnp.zeros_like(pi)
    for j in range(SUBLANES + 1):
        apr_ref[j:j + 1, :] = cr
        api_ref[j:j + 1, :] = ci
        cr, ci = cr * pr - ci * pi, cr * pi + ci * pr


def _ssm_prep(a_re, a_im, log_dt, b_re, b_im, c_re, c_im, sub_len):
    gp = SSM_COLS
    kcols = GROUPS_PER_K * SSM_STATE
    ldt = jnp.broadcast_to(log_dt[:, None], (SSM_GROUPS, SSM_STATE)).reshape(1, gp)
    b_re_t = b_re.reshape(gp, SSM_GROUP).T
    b_im_t = b_im.reshape(gp, SSM_GROUP).T
    c_rows = lambda c: jnp.tile(c.transpose(0, 2, 1).reshape(gp, SSM_GROUP), (1, GROUPS_PER_K))
    row = jax.ShapeDtypeStruct((1, gp), F32)
    pw = jax.ShapeDtypeStruct((SUBLANES + 1, gp), F32)
    bx = jax.ShapeDtypeStruct((N_KT, MXU_K, kcols), BF16)
    cx = jax.ShapeDtypeStruct((N_KT, kcols, MXU_K), BF16)
    return pl.pallas_call(
        functools.partial(_ssm_prep_kernel, sub_len=sub_len),
        out_shape=(row, row, pw, pw, bx, bx, bx, bx, cx, cx),
        compiler_params=_cparams(),
        name="ssm_prep",
    )(a_re.reshape(1, gp), a_im.reshape(1, gp), ldt, b_re_t, b_im_t, c_rows(c_re), c_rows(c_im))


def _cmul(ar, ai, br, bi):
    return ar * br - ai * bi, ar * bi + ai * br


def _s5_prompt_kernel(x_ref, sh_ref, sc_ref, g1_ref, gmix_ref, abr_ref, abi_ref, apr_ref, api_ref,
                      bre_ref, bim_ref, cre_ref, cim_ref, dsk_ref, wa_ref, wb_ref, perm_ref, unperm_ref,
                      o_ref, sre_ref, sim_ref,
                      u_ref, hr_ref, hi_ref, hmr_ref, hmi_ref, car_ref, cai_ref, *, L):
    ls = L // SUBLANES
    kcols = GROUPS_PER_K * SSM_STATE

    @pl.when(pl.program_id(1) == 0)
    def _():
        car_ref[...] = jnp.zeros_like(car_ref)
        cai_ref[...] = jnp.zeros_like(cai_ref)

    x = x_ref[...]
    u = _rms(x, gmix_ref[...]) * (1.0 + sc_ref[...]) + sh_ref[...]
    u_ref[...] = u
    ub = jnp.dot(perm_ref[...], u.astype(BF16), preferred_element_type=F32).astype(BF16)
    for kt in range(N_KT):
        uk = ub[:, kt * MXU_K:(kt + 1) * MXU_K]
        hr_ref[:, kt * kcols:(kt + 1) * kcols] = jnp.dot(uk, bre_ref[kt], preferred_element_type=F32)
        hi_ref[:, kt * kcols:(kt + 1) * kcols] = jnp.dot(uk, bim_ref[kt], preferred_element_type=F32)

    for cb in range(SSM_COLS // S5_COLS):
        cols = slice(cb * S5_COLS, (cb + 1) * S5_COLS)
        ar = jnp.broadcast_to(abr_ref[:, cols], (SUBLANES, S5_COLS))
        ai = jnp.broadcast_to(abi_ref[:, cols], (SUBLANES, S5_COLS))

        sr = si = jnp.zeros((SUBLANES, S5_COLS), F32)
        for i in range(ls):
            rows = slice(i * SUBLANES, (i + 1) * SUBLANES)
            tr, ti = _cmul(ar, ai, sr, si)
            sr = tr + hr_ref[rows, cols]
            si = ti + hi_ref[rows, cols]
            hr_ref[rows, cols] = sr
            hi_ref[rows, cols] = si

    fr = hr_ref[L - SUBLANES:L, :]
    fi = hi_ref[L - SUBLANES:L, :]
    row = lax.broadcasted_iota(I32, (SUBLANES, SSM_COLS), 0)
    gr, gi = fr, fi
    pr, pi = apr_ref[1:2, :], api_ref[1:2, :]
    for s in (1, 2, 4):
        sr = jnp.where(row >= s, pltpu.roll(gr, s, axis=0), 0.0)
        si = jnp.where(row >= s, pltpu.roll(gi, s, axis=0), 0.0)
        tr, ti = _cmul(pr, pi, sr, si)
        gr, gi = gr + tr, gi + ti
        pr, pi = _cmul(pr, pi, pr, pi)
    c0r, c0i = car_ref[...], cai_ref[...]
    tr, ti = _cmul(apr_ref[0:SUBLANES, :], api_ref[0:SUBLANES, :], c0r, c0i)
    hmr = tr + jnp.where(row >= 1, pltpu.roll(gr, 1, axis=0), 0.0)
    hmi = ti + jnp.where(row >= 1, pltpu.roll(gi, 1, axis=0), 0.0)
    hmr_ref[...] = hmr
    hmi_ref[...] = hmi
    tr, ti = _cmul(apr_ref[SUBLANES:SUBLANES + 1, :], api_ref[SUBLANES:SUBLANES + 1, :], c0r, c0i)
    ncr = tr + gr[SUBLANES - 1:SUBLANES, :]
    nci = ti + gi[SUBLANES - 1:SUBLANES, :]
    car_ref[...] = ncr
    cai_ref[...] = nci
    sre_ref[...] = ncr
    sim_ref[...] = nci

    for cb in range(SSM_COLS // S5_COLS):
        cols = slice(cb * S5_COLS, (cb + 1) * S5_COLS)
        ar = jnp.broadcast_to(abr_ref[:, cols], (SUBLANES, S5_COLS))
        ai = jnp.broadcast_to(abi_ref[:, cols], (SUBLANES, S5_COLS))

        dr, di = hmr_ref[:, cols], hmi_ref[:, cols]
        for i in range(ls):
            rows = slice(i * SUBLANES, (i + 1) * SUBLANES)
            dr, di = _cmul(ar, ai, dr, di)
            hr_ref[rows, cols] = hr_ref[rows, cols] + dr
            hi_ref[rows, cols] = hi_ref[rows, cols] + di

    ys = []
    for nt in range(N_KT):
        hr = hr_ref[:, nt * kcols:(nt + 1) * kcols].astype(BF16)
        hi = hi_ref[:, nt * kcols:(nt + 1) * kcols].astype(BF16)
        ys.append(jnp.dot(hr, cre_ref[nt], preferred_element_type=F32)
                  + jnp.dot(hi, cim_ref[nt], preferred_element_type=F32))
    yp = jnp.concatenate(ys, axis=-1)
    yp_hi = yp.astype(BF16)
    yp_lo = (yp - yp_hi.astype(F32)).astype(BF16)
    unperm = unperm_ref[...]
    y = (jnp.dot(unperm, yp_hi, preferred_element_type=F32)
         + jnp.dot(unperm, yp_lo, preferred_element_type=F32)) + dsk_ref[...] * u_ref[...]
    z = _gelu_tanh(y).astype(BF16)
    mix = jnp.dot(z, wa_ref[...], preferred_element_type=F32) * _sigmoid(
        jnp.dot(z, wb_ref[...], preferred_element_type=F32))
    o_ref[...] = x_ref[...] + g1_ref[...] * mix


SH1, SC1, G1 = 0, 1, 2


def _s5_prompt(x, mod, gmix, prep, bexp, cexp, dsk, wa, wb):
    b, t, _ = x.shape
    L = S5_CHUNK
    abr, abi, apr, api = prep
    bre, bim = bexp
    cre, cim = cexp
    kcols = GROUPS_PER_K * SSM_STATE
    full = lambda shape: pl.BlockSpec(shape, lambda bi, ci: (0,) * len(shape))
    mods = [pl.BlockSpec((None, 1, D_MODEL), lambda bi, ci, k=k: (bi, 0, k)) for k in (SH1, SC1, G1)]
    r = np.arange(L)
    perm_np = np.zeros((L, L), np.float32)
    perm_np[r, (r % SUBLANES) * (L // SUBLANES) + r // SUBLANES] = 1.0
    perm, unperm = jnp.asarray(perm_np, BF16), jnp.asarray(perm_np.T, BF16)
    return pl.pallas_call(
        functools.partial(_s5_prompt_kernel, L=L),
        grid=(b, t // L),
        in_specs=[pl.BlockSpec((None, L, D_MODEL), lambda bi, ci: (bi, ci, 0))] + mods + [
                  full((1, D_MODEL)), full((1, SSM_COLS)), full((1, SSM_COLS)),
                  full((SUBLANES + 1, SSM_COLS)), full((SUBLANES + 1, SSM_COLS)),
                  full((N_KT, MXU_K, kcols)), full((N_KT, MXU_K, kcols)),
                  full((N_KT, kcols, MXU_K)), full((N_KT, kcols, MXU_K)),
                  full((1, D_MODEL)), full((D_MODEL, D_MODEL)), full((D_MODEL, D_MODEL)),
                  full((L, L)), full((L, L))],
        out_specs=[pl.BlockSpec((None, L, D_MODEL), lambda bi, ci: (bi, ci, 0)),
                   pl.BlockSpec((None, 1, SSM_COLS), lambda bi, ci: (bi, 0, 0)),
                   pl.BlockSpec((None, 1, SSM_COLS), lambda bi, ci: (bi, 0, 0))],
        out_shape=(jax.ShapeDtypeStruct((b, t, D_MODEL), F32),
                   jax.ShapeDtypeStruct((b, 1, SSM_COLS), F32),
                   jax.ShapeDtypeStruct((b, 1, SSM_COLS), F32)),
        scratch_shapes=[pltpu.VMEM((L, D_MODEL), F32),
                        pltpu.VMEM((L, SSM_COLS), F32), pltpu.VMEM((L, SSM_COLS), F32),
                        pltpu.VMEM((SUBLANES, SSM_COLS), F32), pltpu.VMEM((SUBLANES, SSM_COLS), F32),
                        pltpu.VMEM((1, SSM_COLS), F32), pltpu.VMEM((1, SSM_COLS), F32)],
        compiler_params=_cparams(("arbitrary", "arbitrary")),
        name="s5_prompt",
    )(x, mod, mod, mod, gmix, abr, abi, apr, api, bre, bim, cre, cim, dsk, wa, wb, perm, unperm)


def _tile_rows(v, rows):
    reps = rows // v.shape[0]
    return v if reps == 1 else jnp.concatenate([v] * reps, axis=0)


def _mod_cols(m_ref, k, rows=None):
    v = m_ref[:, k * D_MODEL:(k + 1) * D_MODEL]
    return v if rows is None else _tile_rows(v, rows)


def _s5_sample_kernel(x_ref, m_ref, gmix_ref, h0r_ref, h0i_ref, abr_ref, abi_ref,
                      brh_ref, brl_ref, bih_ref, bil_ref, cre_ref, cim_ref, dsk_ref, wa_ref, wb_ref,
                      o_ref, sre_ref, sim_ref, *, steps):
    kcols = GROUPS_PER_K * SSM_STATE
    ar, ai = abr_ref[...], abi_ref[...]
    sr, si = h0r_ref[...], h0i_ref[...]
    sh, sc, g1 = _mod_cols(m_ref, SH1), _mod_cols(m_ref, SC1), _mod_cols(m_ref, G1)
    for t in range(steps):
        x = x_ref[t]
        u = _rms(x, gmix_ref[...]) * (1.0 + sc) + sh
        bur, bui = [], []
        for kt in range(N_KT):
            uk = u[:, kt * MXU_K:(kt + 1) * MXU_K]
            bur.append(_dot3(uk, brh_ref[kt], brl_ref[kt]))
            bui.append(_dot3(uk, bih_ref[kt], bil_ref[kt]))
        tr, ti = _cmul(ar, ai, sr, si)
        sr = tr + jnp.concatenate(bur, axis=-1)
        si = ti + jnp.concatenate(bui, axis=-1)
        ys = []
        for nt in range(N_KT):
            ys.append(_bdot(sr[:, nt * kcols:(nt + 1) * kcols], cre_ref[nt])
                      + _bdot(si[:, nt * kcols:(nt + 1) * kcols], cim_ref[nt]))
        y = jnp.concatenate(ys, axis=-1) + dsk_ref[...] * u
        z = _gelu_tanh(y).astype(BF16)
        mix = jnp.dot(z, wa_ref[...], preferred_element_type=F32) * _sigmoid(
            jnp.dot(z, wb_ref[...], preferred_element_type=F32))
        o_ref[t] = x + g1 * mix
    sre_ref[...] = sr
    sim_ref[...] = si


def _s5_sample(x_t, mod, gmix, h0r, h0i, abr, abi, bsplit, cexp, dsk, wa, wb):
    steps, n, _ = x_t.shape
    (brh, brl), (bih, bil) = bsplit
    cre, cim = cexp
    return pl.pallas_call(
        functools.partial(_s5_sample_kernel, steps=steps),
        out_shape=(jax.ShapeDtypeStruct((steps, n, D_MODEL), F32),
                   jax.ShapeDtypeStruct((n, SSM_COLS), F32),
                   jax.ShapeDtypeStruct((n, SSM_COLS), F32)),
        compiler_params=_cparams(),
        name="s5_sample",
    )(x_t, mod, gmix, h0r, h0i, abr, abi, brh, brl, bih, bil, cre, cim, dsk, wa, wb)


class _Tiles:
    def __init__(self, n_prompt, n_sample, seq_len, n_seq):
        assert TM % n_seq == 0
        self.n_seq = n_seq
        self.ntp = n_prompt // TM
        self.nts = n_sample // TM
        self.nt = self.ntp + self.nts
        self.tiles_per_seq = seq_len // TM
        self.n_batch = n_prompt // seq_len
        self.n_tok = n_prompt + n_sample

    def p_rows(self):
        return pl.BlockSpec((TM, D_MODEL), lambda t, *_: (jnp.minimum(t, self.ntp - 1), 0))

    def s_rows(self):
        return pl.BlockSpec((TM, D_MODEL), lambda t, *_: (jnp.maximum(t - self.ntp, 0), 0))

    def s_mod(self, chunk):
        return pl.BlockSpec((self.n_seq, D_MODEL), lambda t, *_: (0, chunk))

    def p_mod(self, chunk):
        return pl.BlockSpec((None, 1, D_MODEL),
                            lambda t, *_: (jnp.minimum(t // self.tiles_per_seq, self.n_batch - 1), 0, chunk))

    def all_rows(self, width):
        return pl.BlockSpec((TM, width), lambda t, *_: (t, 0))


def _const_spec(shape):
    return pl.BlockSpec(shape, lambda t, *_: (0,) * len(shape))


def _by_kind(tiles, body, prompt_refs, sample_refs):
    t = pl.program_id(0)

    @pl.when(t < tiles.ntp)
    def _():
        body(*prompt_refs)

    @pl.when(t >= tiles.ntp)
    def _():
        body(*sample_refs)


def _mod_rows(m_ref):
    v = m_ref[...]
    return v if v.shape[0] == 1 else _tile_rows(v, TM)


def _moe_input(x_ref, sh_ref, sc_ref, g_ref):
    return _rms(x_ref[...], g_ref[...]) * (1.0 + _mod_rows(sc_ref)) + _mod_rows(sh_ref)


def _router_kernel(xp_ref, xs_ref, shp_ref, shs_ref, scp_ref, scs_ref, g_ref, wh_ref, wl_ref, b_ref, tri_ref,
                   mi_ref, mw_ref, cnt_ref, run_ref, *, tiles):
    @pl.when(pl.program_id(0) == 0)
    def _():
        run_ref[...] = jnp.zeros_like(run_ref)

    def route(x_ref, sh_ref, sc_ref):
        h = _moe_input(x_ref, sh_ref, sc_ref, g_ref)
        logits = _dot3(h, wh_ref[...], wl_ref[...]) + b_ref[...]
        lane = lax.broadcasted_iota(I32, (TM, LANES), 1)
        lane_f = lane.astype(F32)
        vals, firsts, hots = [], [], []
        work = logits
        for _ in range(TOP_K):
            m = jnp.max(work, axis=-1, keepdims=True)
            first = jnp.min(jnp.where(work == m, lane_f, float(LANES)), axis=-1, keepdims=True)
            hot = lane_f == first
            vals.append(m)
            firsts.append(first)
            hots.append(hot)
            work = jnp.where(hot, -jnp.inf, work)
        exps = [jnp.exp(v - vals[0]) for v in vals]
        den = exps[0] + exps[1] + exps[2] + exps[3]

        chosen = jnp.zeros((TM, LANES), F32)
        for hot in hots:
            chosen = jnp.where(hot, 1.0, chosen)
        rank_all = jnp.dot(tri_ref[...], chosen.astype(BF16), preferred_element_type=F32) + run_ref[...]
        run_ref[...] = run_ref[...] + jnp.sum(chosen, axis=0, keepdims=True)
        cnt_ref[...] = run_ref[...]

        code = jnp.zeros((TM, LANES), F32)
        mw = jnp.zeros((TM, LANES), F32)
        for k in range(TOP_K):
            r_k = jnp.sum(jnp.where(hots[k], rank_all, 0.0), axis=-1, keepdims=True)
            code = jnp.where(lane == k, r_k * float(N_EXPERTS) + firsts[k], code)
            mw = jnp.where(lane == k, exps[k] / den, mw)
        mw_ref[...] = mw
        code_t = code.T
        mi_ref[...] = jnp.concatenate(
            [code_t[k:k + 1, c * LANES:(c + 1) * LANES] for k in range(TOP_K) for c in range(TM // LANES)],
            axis=0).astype(I32)

    _by_kind(tiles, route, (xp_ref, shp_ref, scp_ref), (xs_ref, shs_ref, scs_ref))


SH2, SC2, G2 = 3, 4, 5


def _router(tiles, xp, xs, mod_p, mod_s, g, w_hi, w_lo, b_pad):
    tri = jnp.asarray(np.tril(np.ones((TM, TM), np.float32), -1), BF16)
    return pl.pallas_call(
        functools.partial(_router_kernel, tiles=tiles),
        grid=(tiles.nt,),
        in_specs=[tiles.p_rows(), tiles.s_rows(), tiles.p_mod(SH2), tiles.s_mod(SH2), tiles.p_mod(SC2),
                  tiles.s_mod(SC2), _const_spec((1, D_MODEL)), _const_spec((D_MODEL, LANES)),
                  _const_spec((D_MODEL, LANES)), _const_spec((1, LANES)), _const_spec((TM, TM))],
        out_specs=[pl.BlockSpec((TOP_K * TM // LANES, LANES), lambda t: (t, 0)), tiles.all_rows(LANES),
                   _const_spec((1, LANES))],
        out_shape=(jax.ShapeDtypeStruct((tiles.n_tok * TOP_K // LANES, LANES), I32),
                   jax.ShapeDtypeStruct((tiles.n_tok, LANES), F32),
                   jax.ShapeDtypeStruct((1, LANES), F32)),
        scratch_shapes=[pltpu.VMEM((1, LANES), F32)],
        compiler_params=_cparams(("arbitrary",)),
        name="router",
    )(xp, xs, mod_p, mod_s, mod_p, mod_s, g, w_hi, w_lo, b_pad, tri)


PAD_ARMS = tuple(1 << i for i in reversed(range(int(math.log2(TM)))))


def _dispatch_kernel(slot_ref, pstart_ref, pcnt_ref,
                     xp_ref, xs_ref, shp_ref, shs_ref, scp_ref, scs_ref, g_ref,
                     o_hbm, rows_ref, zero_ref, sems, zsem, *, tiles):
    t = pl.program_id(0)
    buf = t % 2
    tile_rows = TM * ROW_TILES
    base = pl.multiple_of(buf * tile_rows, tile_rows)

    def tile_wait(b):
        b0 = pl.multiple_of(b * tile_rows, tile_rows)
        for _ in range(TOP_K):
            pltpu.make_async_copy(rows_ref.at[pl.ds(b0, tile_rows), :],
                                  o_hbm.at[pl.ds(0, tile_rows), :], sems.at[b]).wait()

    def pad_copies(act):
        def body(e, carry):
            start, cnt = pstart_ref[e], pcnt_ref[e]
            for p in PAD_ARMS:
                @pl.when((cnt & p) != 0)
                def _(start=start, p=p):
                    cp = pltpu.make_async_copy(
                        zero_ref.at[pl.ds(0, p * ROW_TILES), :],
                        o_hbm.at[pl.ds(pl.multiple_of(start * ROW_TILES, ROW_TILES), p * ROW_TILES), :],
                        zsem)
                    cp.start() if act == "start" else cp.wait()
                start = start + jnp.where((cnt & p) != 0, p, 0)
            return carry
        lax.fori_loop(0, N_EXPERTS, body, 0)

        def tail(j, carry):
            row0 = pstart_ref[N_EXPERTS] + j * TM
            for half in range(TM // PAD_ARMS[0]):
                cp = pltpu.make_async_copy(
                    zero_ref,
                    o_hbm.at[pl.ds(pl.multiple_of((row0 + half * PAD_ARMS[0]) * ROW_TILES, ROW_TILES),
                                   PAD_ARMS[0] * ROW_TILES), :],
                    zsem)
                cp.start() if act == "start" else cp.wait()
            return carry
        lax.fori_loop(0, pcnt_ref[N_EXPERTS], tail, 0)

    @pl.when(t == 0)
    def _():
        zero_ref[...] = jnp.zeros_like(zero_ref)
        pad_copies("start")

    @pl.when(t >= 2)
    def _():
        tile_wait(buf)

    def stage(x_ref, sh_ref, sc_ref):
        h = _moe_input(x_ref, sh_ref, sc_ref, g_ref)
        for c in range(ROW_TILES):
            rows_ref[pl.ds(base + c, TM, stride=ROW_TILES), :] = h[:, c * LANES:(c + 1) * LANES]

    _by_kind(tiles, stage, (xp_ref, shp_ref, scp_ref), (xs_ref, shs_ref, scs_ref))

    def issue(n, carry):
        src = rows_ref.at[pl.ds(pl.multiple_of(base + n * ROW_TILES, ROW_TILES), ROW_TILES), :]
        for k in range(TOP_K):
            slot = slot_ref[(t * TOP_K + k) * TM + n]
            dst = o_hbm.at[pl.ds(pl.multiple_of(slot * ROW_TILES, ROW_TILES), ROW_TILES), :]
            pltpu.make_async_copy(src, dst, sems.at[buf]).start()
        return carry
    lax.fori_loop(0, TM, issue, 0)

    @pl.when(t == tiles.nt - 1)
    def _():
        if tiles.nt >= 2:
            tile_wait(1 - buf)
        tile_wait(buf)
        pad_copies("wait")


def _dispatch(tiles, n_slots, slots, pad_start, pad_cnt, xp, xs, mod_p, mod_s, g):
    grid_spec = pltpu.PrefetchScalarGridSpec(
        num_scalar_prefetch=3,
        grid=(tiles.nt,),
        in_specs=[tiles.p_rows(), tiles.s_rows(), tiles.p_mod(SH2), tiles.s_mod(SH2), tiles.p_mod(SC2),
                  tiles.s_mod(SC2), _const_spec((1, D_MODEL))],
        out_specs=pl.BlockSpec(memory_space=pl.ANY),
        scratch_shapes=[pltpu.VMEM((2 * TM * ROW_TILES, LANES), F32),
                        pltpu.VMEM((PAD_ARMS[0] * ROW_TILES, LANES), F32),
                        pltpu.SemaphoreType.DMA((2,)), pltpu.SemaphoreType.DMA(())],
    )
    return pl.pallas_call(
        functools.partial(_dispatch_kernel, tiles=tiles),
        grid_spec=grid_spec,
        out_shape=jax.ShapeDtypeStruct((n_slots * ROW_TILES, LANES), F32),
        compiler_params=_cparams(("arbitrary",)),
        name="dispatch",
    )(slots, pad_start, pad_cnt, xp, xs, mod_p, mod_s, mod_p, mod_s, g)


def _expert_kernel(te_ref, nx_ref, nt_ref, hs_ref, wgu_hbm, bgu_ref, wd_hbm, bd_ref, o_ref,
                   wgu_f32, wd_f32, wgu_bf, wd_bf, lhs_ref, wsem, *, layer):
    i = pl.program_id(0)

    def fetch(expert):
        return (pltpu.make_async_copy(wgu_hbm.at[layer, expert], wgu_f32, wsem.at[0]),
                pltpu.make_async_copy(wd_hbm.at[layer, expert], wd_f32, wsem.at[1]))

    @pl.when(i >= nt_ref[0])
    def _():
        o_ref[...] = jnp.zeros_like(o_ref)

    @pl.when(i < nt_ref[0])
    def _():
        expert = te_ref[i]

        @pl.when(i == 0)
        def _():
            for cp in fetch(expert):
                cp.start()

        @pl.when((i == 0) | (expert != te_ref[jnp.maximum(i - 1, 0)]))
        def _():
            for cp in fetch(expert):
                cp.wait()
            wgu_bf[...] = wgu_f32[...].astype(BF16)
            wd_bf[...] = wd_f32[...].astype(BF16)
            upcoming = nx_ref[i]

            @pl.when(upcoming >= 0)
            def _():
                for cp in fetch(upcoming):
                    cp.start()

        for c in range(ROW_TILES):
            lhs_ref[:, c * LANES:(c + 1) * LANES] = hs_ref[pl.ds(c, TM, stride=ROW_TILES), :].astype(BF16)
        gu = jnp.dot(lhs_ref[...], wgu_bf[...], preferred_element_type=F32) + bgu_ref[...]
        gate = jnp.minimum(gu[:, :D_FF], SWIGLU_LIMIT)
        up = jnp.clip(gu[:, D_FF:], -SWIGLU_LIMIT, SWIGLU_LIMIT)
        act = ((up + 1.0) * gate * _sigmoid(SWIGLU_ALPHA * gate)).astype(BF16)
        y = jnp.dot(act, wd_bf[...], preferred_element_type=F32) + bd_ref[...]
        for c in range(ROW_TILES):
            o_ref[pl.ds(c, TM, stride=ROW_TILES), :] = y[:, c * LANES:(c + 1) * LANES]


def _experts(max_tiles, tile_expert, next_expert, n_tiles, hs, w_gu, b_gu, w_down, b_down, layer):
    tile_rows = TM * ROW_TILES
    row_map = lambda i, te, nx, nt: (jnp.minimum(i, nt[0] - 1), 0)
    b_map = lambda i, te, nx, nt: (layer, te[i], 0, 0)
    grid_spec = pltpu.PrefetchScalarGridSpec(
        num_scalar_prefetch=3,
        grid=(max_tiles,),
        in_specs=[pl.BlockSpec((tile_rows, LANES), row_map),
                  pl.BlockSpec(memory_space=pl.ANY),
                  pl.BlockSpec((None, None, 1, 2 * D_FF), b_map),
                  pl.BlockSpec(memory_space=pl.ANY),
                  pl.BlockSpec((None, None, 1, D_MODEL), b_map)],
        out_specs=pl.BlockSpec((tile_rows, LANES), lambda i, te, nx, nt: (i, 0)),
        scratch_shapes=[pltpu.VMEM((D_MODEL, 2 * D_FF), F32), pltpu.VMEM((D_FF, D_MODEL), F32),
                        pltpu.VMEM((D_MODEL, 2 * D_FF), BF16), pltpu.VMEM((D_FF, D_MODEL), BF16),
                        pltpu.VMEM((TM, D_MODEL), BF16), pltpu.SemaphoreType.DMA((2,))],
    )
    return pl.pallas_call(
        functools.partial(_expert_kernel, layer=layer),
        grid_spec=grid_spec,
        out_shape=jax.ShapeDtypeStruct(hs.shape, F32),
        compiler_params=_cparams(("arbitrary",)),
        name="experts",
    )(tile_expert, next_expert, n_tiles, hs, w_gu, b_gu[:, :, None, :], w_down, b_down[:, :, None, :])


def _combine_kernel(slot_ref, y_hbm, mw_ref, xp_ref, xs_ref, g2p_ref, g2s_ref, gfin_ref,
                    op_ref, os_ref, gbuf, acc_ref, sems, *, tiles, final_norm):
    t = pl.program_id(0)
    tile_rows = TM * ROW_TILES
    buf_rows = TOP_K * tile_rows

    def issue_tile(tt, b):
        b0 = pl.multiple_of(b * buf_rows, buf_rows)

        for k in range(TOP_K):
            def issue(it, carry, k=k):
                for u in range(TOP_K):
                    n = it * TOP_K + u
                    slot = slot_ref[(tt * TOP_K + k) * TM + n]
                    src = y_hbm.at[pl.ds(pl.multiple_of(slot * ROW_TILES, ROW_TILES), ROW_TILES), :]
                    dst = gbuf.at[pl.ds(pl.multiple_of(b0 + k * tile_rows + n * ROW_TILES, ROW_TILES),
                                        ROW_TILES), :]
                    pltpu.make_async_copy(src, dst, sems.at[b]).start()
                return carry
            lax.fori_loop(0, TM // TOP_K, issue, 0)

    @pl.when(t == 0)
    def _():
        issue_tile(0, 0)

    @pl.when(t + 1 < tiles.nt)
    def _():
        issue_tile(t + 1, (t + 1) % 2)

    buf = t % 2
    b0 = pl.multiple_of(buf * buf_rows, buf_rows)
    for _ in range(TOP_K):
        pltpu.make_async_copy(y_hbm.at[pl.ds(0, tile_rows), :],
                              gbuf.at[pl.ds(b0, tile_rows), :], sems.at[buf]).wait()

    w = mw_ref[...]
    for c in range(ROW_TILES):
        acc = None
        for k in range(TOP_K):
            rows = gbuf[pl.ds(b0 + k * tile_rows + c, TM, stride=ROW_TILES), :]
            term = w[:, k:k + 1] * rows
            acc = term if acc is None else acc + term
        acc_ref[:, c * LANES:(c + 1) * LANES] = acc

    def finish(x_ref, g2_ref, o_ref):
        res = x_ref[...] + _mod_rows(g2_ref) * acc_ref[...]
        if final_norm:
            res = _rms(res, gfin_ref[...])
        o_ref[...] = res

    _by_kind(tiles, finish, (xp_ref, g2p_ref, op_ref), (xs_ref, g2s_ref, os_ref))


def _combine(tiles, slots, y_sorted, mw, xp, xs, mod_p, mod_s, gfin, final_norm):
    grid_spec = pltpu.PrefetchScalarGridSpec(
        num_scalar_prefetch=1,
        grid=(tiles.nt,),
        in_specs=[pl.BlockSpec(memory_space=pl.ANY), tiles.all_rows(LANES), tiles.p_rows(), tiles.s_rows(),
                  tiles.p_mod(G2), tiles.s_mod(G2), _const_spec((1, D_MODEL))],
        out_specs=[tiles.p_rows(), tiles.s_rows()],
        scratch_shapes=[pltpu.VMEM((2 * TOP_K * TM * ROW_TILES, LANES), F32),
                        pltpu.VMEM((TM, D_MODEL), F32), pltpu.SemaphoreType.DMA((2,))],
    )
    return pl.pallas_call(
        functools.partial(_combine_kernel, tiles=tiles, final_norm=final_norm),
        grid_spec=grid_spec,
        out_shape=(jax.ShapeDtypeStruct(xp.shape, F32), jax.ShapeDtypeStruct(xs.shape, F32)),
        compiler_params=_cparams(("arbitrary",)),
        name="combine",
    )(slots, y_sorted, mw, xp, xs, mod_p, mod_s, gfin)


def _moe_layer(tiles, xp, xs, mod_p, mod_s, g_ffn, w_router, b_router, w_gu, b_gu, w_down, b_down, gfin,
               layer, final_norm):
    n_tok = tiles.n_tok
    w_pad = jnp.pad(w_router, ((0, 0), (0, LANES - N_EXPERTS)))
    w_hi, w_lo = _split_bf16(w_pad)
    b_pad = jnp.pad(b_router, (0, LANES - N_EXPERTS), constant_values=NEG_INF).reshape(1, LANES)
    mi, mw, counts = _router(tiles, xp, xs, mod_p, mod_s, g_ffn, w_hi, w_lo, b_pad)

    cnt = counts[0, :N_EXPERTS].astype(I32)
    tiles_e = (cnt + TM - 1) // TM
    tile_end = jnp.cumsum(tiles_e)
    tile_start = tile_end - tiles_e
    first_slot = jnp.sum(jnp.where((mi & (N_EXPERTS - 1))[..., None] == jnp.arange(N_EXPERTS, dtype=I32),
                                   (tile_start * TM)[None, None, :], 0), axis=-1)
    slots = (first_slot + (mi >> int(math.log2(N_EXPERTS)))).reshape(n_tok * TOP_K)
    max_tiles = (n_tok * TOP_K) // TM + N_EXPERTS
    n_tiles = tile_end[-1:].astype(I32)
    tile_ids = jnp.arange(max_tiles, dtype=I32)
    tile_expert = jnp.minimum(jnp.sum(tile_ids[:, None] >= tile_end[None, :], axis=-1), N_EXPERTS - 1).astype(I32)
    group_end = jnp.sum(jnp.where(tile_expert[:, None] == jnp.arange(N_EXPERTS, dtype=I32), tile_end[None, :], 0),
                        axis=-1)
    after = jnp.minimum(jnp.sum(group_end[:, None] >= tile_end[None, :], axis=-1), N_EXPERTS - 1).astype(I32)
    next_expert = jnp.where(group_end < n_tiles[0], after, -1).astype(I32)
    pad_start = jnp.concatenate([tile_start * TM + cnt, n_tiles * TM]).astype(I32)
    pad_cnt = jnp.concatenate([tiles_e * TM - cnt, max_tiles - n_tiles]).astype(I32)

    hs = _dispatch(tiles, max_tiles * TM, slots, pad_start, pad_cnt, xp, xs, mod_p, mod_s, g_ffn)
    ys = _experts(max_tiles, tile_expert, next_expert, n_tiles, hs, w_gu, b_gu, w_down, b_down, layer)
    return _combine(tiles, slots, ys, mw, xp, xs, mod_p, mod_s, gfin, final_norm)


ATTN_STEP_BLOCKS = 2


def _slopes():
    return [2.0 ** (-8.0 * (h + 1) / N_HEADS) for h in range(N_HEADS)]


def _attn_prompt_kernel(sink_ref, x_ref, shkv_ref, sckv_ref, sh_ref, sc_ref, g1_ref, gkv_ref, gmix_ref,
                        wkv_ref, bkv_ref, wq_ref, bq_ref, wo_ref,
                        o_ref, kl_ref, vl_ref, kk_ref, vv_ref, oh_ref):
    j = pl.program_id(1)
    blk = WINDOW
    nb = ATTN_STEP_BLOCKS

    @pl.when(j == 0)
    def _():
        kk_ref[...] = jnp.zeros_like(kk_ref)
        vv_ref[...] = jnp.zeros_like(vv_ref)

    @pl.when(j > 0)
    def _():
        kk_ref[0:blk, :] = kk_ref[nb * blk:(nb + 1) * blk, :]
        vv_ref[0:blk, :] = vv_ref[nb * blk:(nb + 1) * blk, :]

    x = x_ref[...]
    hkv = _rms(x, gkv_ref[...]) * (1.0 + sckv_ref[...]) + shkv_ref[...]
    kv = _bdot(hkv, wkv_ref[...]) + bkv_ref[...]
    kl_ref[...] = kv[(nb - 1) * blk:, :KV_DIM]
    vl_ref[...] = kv[(nb - 1) * blk:, KV_DIM:]
    kk_ref[blk:(nb + 1) * blk, :] = kv[:, :KV_DIM].astype(BF16)
    vv_ref[blk:(nb + 1) * blk, :] = kv[:, KV_DIM:].astype(BF16)

    h = _rms(x, gmix_ref[...]) * (1.0 + sc_ref[...]) + sh_ref[...]
    q = _bdot(h, wq_ref[...]) + bq_ref[...]

    r = lax.broadcasted_iota(I32, (blk, 2 * blk), 0)
    c = lax.broadcasted_iota(I32, (blk, 2 * blk), 1)
    dist_i = r + blk - c
    visible = (dist_i >= 0) & (dist_i < WINDOW)
    dist = dist_i.astype(F32)
    slopes = _slopes()
    for qb in range(nb):
        valid = visible & jnp.logical_not((j == 0) & (c < blk)) if qb == 0 else visible
        q_rows = slice(qb * blk, (qb + 1) * blk)
        k_rows = slice(qb * blk, (qb + 2) * blk)
        for g in range(N_KV_HEADS):
            kg = kk_ref[k_rows, g * HEAD_DIM:(g + 1) * HEAD_DIM]
            vg = vv_ref[k_rows, g * HEAD_DIM:(g + 1) * HEAD_DIM]
            for qh in range(Q_PER_KV):
                hd = g * Q_PER_KV + qh
                qd = q[q_rows, hd * HEAD_DIM:(hd + 1) * HEAD_DIM].astype(BF16)
                s = lax.dot_general(qd, kg, (((1,), (1,)), ((), ())), preferred_element_type=F32)
                s = s * (HEAD_DIM ** -0.5) - slopes[hd] * dist
                s = jnp.where(valid, s, NEG_INF)
                sink = sink_ref[hd]
                m = jnp.maximum(jnp.max(s, axis=-1, keepdims=True), sink)
                e = jnp.exp(s - m)
                den = jnp.sum(e, axis=-1, keepdims=True) + jnp.exp(sink - m)
                p = (e / den).astype(BF16)
                oh_ref[q_rows, hd * HEAD_DIM:(hd + 1) * HEAD_DIM] = jnp.dot(p, vg, preferred_element_type=F32)
    mix = _bdot(oh_ref[...], wo_ref[...])
    o_ref[...] = x + g1_ref[...] * mix


SHKV, SCKV = 0, 1


def _attn_prompt(x, mod_kv, mod, gkv, gmix, wkv, bkv, wq, bq, wo, sinks):
    b, t, _ = x.shape
    blk = WINDOW
    qd = N_HEADS * HEAD_DIM
    full = lambda shape: pl.BlockSpec(shape, lambda bi, ji, *_: (0,) * len(shape))
    mods = [pl.BlockSpec((None, 1, D_MODEL), lambda bi, ji, *_, k=k: (bi, 0, k))
            for k in (SHKV, SCKV, SH1, SC1, G1)]
    nb = ATTN_STEP_BLOCKS
    assert t % (nb * blk) == 0
    rows = pl.BlockSpec((None, nb * blk, D_MODEL), lambda bi, ji, *_: (bi, ji, 0))
    last = pl.BlockSpec((None, blk, KV_DIM), lambda bi, ji, *_: (bi, 0, 0))
    grid_spec = pltpu.PrefetchScalarGridSpec(
        num_scalar_prefetch=1,
        grid=(b, t // (nb * blk)),
        in_specs=[rows] + mods + [full((1, D_MODEL)), full((1, D_MODEL)),
                  full((D_MODEL, 2 * KV_DIM)), full((1, 2 * KV_DIM)), full((D_MODEL, qd)), full((1, qd)),
                  full((qd, D_MODEL))],
        out_specs=[rows, last, last],
        scratch_shapes=[pltpu.VMEM(((nb + 1) * blk, KV_DIM), BF16), pltpu.VMEM(((nb + 1) * blk, KV_DIM), BF16),
                        pltpu.VMEM((nb * blk, qd), F32)],
    )
    return pl.pallas_call(
        _attn_prompt_kernel,
        grid_spec=grid_spec,
        out_shape=(jax.ShapeDtypeStruct((b, t, D_MODEL), F32),
                   jax.ShapeDtypeStruct((b, blk, KV_DIM), F32),
                   jax.ShapeDtypeStruct((b, blk, KV_DIM), F32)),
        compiler_params=_cparams(("arbitrary", "arbitrary")),
        name="attn_prompt",
    )(sinks, x, mod_kv, mod_kv, mod, mod, mod, gkv, gmix, wkv, bkv, wq, bq, wo)


def _qkv_sample_kernel(x_ref, mkv_ref, m_ref, gkv_ref, gmix_ref,
                       wkv_ref, bkv_ref, wq_ref, bq_ref, q_ref, k_ref, v_ref):
    x = x_ref[...]
    n = x.shape[0]
    hkv = _rms(x, gkv_ref[...]) * (1.0 + _mod_cols(mkv_ref, SCKV, n)) + _mod_cols(mkv_ref, SHKV, n)
    kv = _bdot(hkv, wkv_ref[...]) + bkv_ref[...]
    k_ref[...] = kv[:, :KV_DIM]
    v_ref[...] = kv[:, KV_DIM:]
    h = _rms(x, gmix_ref[...]) * (1.0 + _mod_cols(m_ref, SC1, n)) + _mod_cols(m_ref, SH1, n)
    q_ref[...] = _bdot(h, wq_ref[...]) + bq_ref[...]


def _qkv_sample(x, mod_kv, mod, gkv, gmix, wkv, bkv, wq, bq):
    n = x.shape[0]
    return pl.pallas_call(
        _qkv_sample_kernel,
        out_shape=(jax.ShapeDtypeStruct((n, N_HEADS * HEAD_DIM), F32),
                   jax.ShapeDtypeStruct((n, KV_DIM), F32), jax.ShapeDtypeStruct((n, KV_DIM), F32)),
        compiler_params=_cparams(),
        name="qkv_sample",
    )(x, mod_kv, mod, gkv, gmix, wkv, bkv, wq, bq)


def _attn_sample_kernel(q_ref, ck_ref, cv_ref, nk_ref, nv_ref, slope_ref, sink_ref, o_ref, *, steps):
    rows = steps * Q_PER_KV
    sb = q_ref.shape[0]
    r_c = lax.broadcasted_iota(I32, (sb, rows, WINDOW), 1) // Q_PER_KV
    j_c = lax.broadcasted_iota(I32, (sb, rows, WINDOW), 2)
    dist_c = r_c + WINDOW - j_c
    valid_c = (dist_c >= 0) & (dist_c < WINDOW)
    r_n = lax.broadcasted_iota(I32, (sb, rows, steps), 1) // Q_PER_KV
    j_n = lax.broadcasted_iota(I32, (sb, rows, steps), 2)
    dist_n = r_n - j_n
    valid_n = (dist_n >= 0) & (dist_n < WINDOW)
    scale = HEAD_DIM ** -0.5
    for g in range(N_KV_HEADS):
        lanes = slice(g * HEAD_DIM, (g + 1) * HEAD_DIM)
        qg = q_ref[:, g].astype(BF16)
        slope = slope_ref[g][None]
        sink = sink_ref[g][None]
        kc, vc = ck_ref[:, :, lanes].astype(BF16), cv_ref[:, :, lanes].astype(BF16)
        kn, vn = nk_ref[:, :, lanes].astype(BF16), nv_ref[:, :, lanes].astype(BF16)
        s_c = jnp.einsum('nrd,njd->nrj', qg, kc, preferred_element_type=F32) * scale
        s_n = jnp.einsum('nrd,njd->nrj', qg, kn, preferred_element_type=F32) * scale
        s_c = jnp.where(valid_c, s_c - slope * dist_c.astype(F32), NEG_INF)
        s_n = jnp.where(valid_n, s_n - slope * dist_n.astype(F32), NEG_INF)
        m = jnp.maximum(jnp.maximum(jnp.max(s_c, axis=-1, keepdims=True),
                                    jnp.max(s_n, axis=-1, keepdims=True)), sink)
        e_c = jnp.exp(s_c - m)
        e_n = jnp.exp(s_n - m)
        den = (jnp.sum(e_c, axis=-1, keepdims=True) + jnp.sum(e_n, axis=-1, keepdims=True)
               + jnp.exp(sink - m))
        o = (jnp.einsum('nrj,njd->nrd', (e_c / den).astype(BF16), vc, preferred_element_type=F32)
             + jnp.einsum('nrj,njd->nrd', (e_n / den).astype(BF16), vn, preferred_element_type=F32))
        o_ref[:, g] = o


def _attn_sample(qg, cache_k, cache_v, k_new, v_new, slope_rows, sink_rows):
    n, _, rows, _ = qg.shape
    steps = rows // Q_PER_KV
    sb = math.gcd(n, 16)
    blk = lambda shape: pl.BlockSpec((sb,) + shape, lambda i: (i,) + (0,) * len(shape))
    full = lambda shape: pl.BlockSpec(shape, lambda i: (0,) * len(shape))
    return pl.pallas_call(
        functools.partial(_attn_sample_kernel, steps=steps),
        grid=(n // sb,),
        in_specs=[blk((N_KV_HEADS, rows, HEAD_DIM)), blk((WINDOW, KV_DIM)), blk((WINDOW, KV_DIM)),
                  blk((steps, KV_DIM)), blk((steps, KV_DIM)),
                  full((N_KV_HEADS, rows, 1)), full((N_KV_HEADS, rows, 1))],
        out_specs=blk((N_KV_HEADS, rows, HEAD_DIM)),
        out_shape=jax.ShapeDtypeStruct(qg.shape, F32),
        compiler_params=_cparams(("arbitrary",)),
        name="attn_sample",
    )(qg, cache_k, cache_v, k_new, v_new, slope_rows, sink_rows)


def _oproj_kernel(o_ref, x_ref, m_ref, wo_ref, y_ref):
    y_ref[...] = x_ref[...] + _mod_cols(m_ref, G1, x_ref.shape[0]) * _bdot(o_ref[...], wo_ref[...])


def _oproj(o, x, mod, wo):
    return pl.pallas_call(
        _oproj_kernel,
        out_shape=jax.ShapeDtypeStruct(x.shape, F32),
        compiler_params=_cparams(),
        name="oproj_sample",
    )(o, x, mod, wo)


def kernel(x_prompt, x_sample, state_ssm_re, state_ssm_im, cache_k, cache_v, c_prompt, c_sample, g_mix, g_ffn, w_ada, b_ada, ssm_a_re, ssm_a_im, ssm_log_dt, ssm_b_re, ssm_b_im, ssm_c_re, ssm_c_im, ssm_d, w_glu_a, w_glu_b, g_kv, w_ada_kv, b_ada_kv, w_kv, b_kv, w_q, b_q, w_o, attn_sinks, w_router, b_router, w_gu, b_gu, w_down, b_down, g_final):
    bsz, seq, d = x_prompt.shape
    n_seq, steps, _ = x_sample.shape
    assert d == D_MODEL and seq % S5_CHUNK == 0 and seq % TM == 0 and (n_seq * steps) % TM == 0
    assert g_mix.shape[0] == 2 and ssm_a_re.shape[0] == 1 and w_q.shape[0] == 1
    n_p, n_s = bsz * seq, n_seq * steps
    tiles = _Tiles(n_p, n_s, seq, n_seq)
    row = lambda v: v.reshape(1, -1)

    c_all = jnp.concatenate([c_prompt, c_sample], axis=0)
    c_rows = -(-c_all.shape[0] // SUBLANES) * SUBLANES
    c_all = jnp.pad(c_all, ((0, c_rows - c_all.shape[0]), (0, 0)))
    mods = [_ada(c_all, w_ada, b_ada, l) for l in range(2)]
    mod_kv = _ada(c_all, w_ada_kv[None], b_ada_kv[None], 0)

    def split(m):
        return m[:bsz, None, :], m[bsz:bsz + n_seq]

    mod_p, mod_s = split(mods[0])
    abr, abi, apr, api, brh, brl, bih, bil, c_re_x, c_im_x = _ssm_prep(
        ssm_a_re[0], ssm_a_im[0], ssm_log_dt[0], ssm_b_re[0], ssm_b_im[0], ssm_c_re[0], ssm_c_im[0],
        S5_CHUNK // SUBLANES)
    cexp = (c_re_x, c_im_x)
    wa, wb = w_glu_a[0].astype(BF16), w_glu_b[0].astype(BF16)
    xp, sre_p, sim_p = _s5_prompt(x_prompt, mod_p, row(g_mix[0]), (abr, abi, apr, api), (brh, bih), cexp,
                                  row(ssm_d[0]), wa, wb)
    xs_t, sre_s, sim_s = _s5_sample(
        x_sample.transpose(1, 0, 2), mod_s, row(g_mix[0]),
        state_ssm_re[0].reshape(n_seq, SSM_COLS), state_ssm_im[0].reshape(n_seq, SSM_COLS), abr, abi,
        ((brh, brl), (bih, bil)), cexp, row(ssm_d[0]), wa, wb)
    xp = xp.reshape(n_p, D_MODEL)
    xs = xs_t.reshape(n_s, D_MODEL)
    xp, xs = _moe_layer(tiles, xp, xs, mod_p, mod_s, row(g_ffn[0]), w_router[0], b_router[0], w_gu, b_gu,
                        w_down, b_down, row(g_final), 0, False)

    mod_p, mod_s = split(mods[1])
    modkv_p, modkv_s = split(mod_kv)
    wkv, wq, wo = w_kv.astype(BF16), w_q[0].astype(BF16), w_o[0].astype(BF16)
    xp3, k_p, v_p = _attn_prompt(xp.reshape(bsz, seq, D_MODEL), modkv_p, mod_p, row(g_kv), row(g_mix[1]), wkv,
                                 row(b_kv), wq, row(b_q[0]), wo, attn_sinks[0])
    xp = xp3.reshape(n_p, D_MODEL)

    q_s, k_s, v_s = _qkv_sample(xs, modkv_s, mod_s, row(g_kv), row(g_mix[1]), wkv, row(b_kv), wq, row(b_q[0]))
    rows = steps * Q_PER_KV
    qg = q_s.reshape(steps, n_seq, N_KV_HEADS, Q_PER_KV, HEAD_DIM).transpose(1, 2, 0, 3, 4).reshape(
        n_seq, N_KV_HEADS, rows, HEAD_DIM)
    head_of_row = (np.arange(N_KV_HEADS)[:, None] * Q_PER_KV + np.arange(rows)[None, :] % Q_PER_KV)
    slope_rows = jnp.asarray(np.asarray(_slopes(), np.float32)[head_of_row][..., None])
    sink_rows = attn_sinks[0][head_of_row][..., None]
    k_new = k_s.reshape(steps, n_seq, KV_DIM).transpose(1, 0, 2)
    v_new = v_s.reshape(steps, n_seq, KV_DIM).transpose(1, 0, 2)
    ck = cache_k.reshape(n_seq, WINDOW, KV_DIM)
    cv = cache_v.reshape(n_seq, WINDOW, KV_DIM)
    og = _attn_sample(qg, ck, cv, k_new, v_new, slope_rows, sink_rows)
    o_s = og.reshape(n_seq, N_KV_HEADS, steps, Q_PER_KV, HEAD_DIM).transpose(2, 0, 1, 3, 4).reshape(
        n_s, N_HEADS * HEAD_DIM)
    xs = _oproj(o_s, xs, mod_s, wo)

    yp, ys = _moe_layer(tiles, xp, xs, mod_p, mod_s, row(g_ffn[1]), w_router[1], b_router[1], w_gu, b_gu,
                        w_down, b_down, row(g_final), 1, True)

    shape4 = lambda a: a.reshape(a.shape[0], a.shape[1], N_KV_HEADS, HEAD_DIM)
    state = lambda a, n: a.reshape(1, n, SSM_GROUPS, SSM_STATE)
    return (yp.reshape(bsz, seq, D_MODEL), ys.reshape(steps, n_seq, D_MODEL).transpose(1, 0, 2),
            state(sre_p, bsz), state(sim_p, bsz), shape4(k_p), shape4(v_p),
            state(sre_s, n_seq), state(sim_s, n_seq),
            jnp.concatenate([cache_k[:, steps:], shape4(k_new)], axis=1),
            jnp.concatenate([cache_v[:, steps:], shape4(v_new)], axis=1))
```
